```python
import math
import jax, jax.numpy as jnp
from jax import lax
import numpy as np

D_MODEL = 2048
BATCH = 4
SEQ = 2048
DEPTH = 1
DEC_BATCH = 128
DEC_SEQ = 1
PAST_LEN = 16384
PAGE_SIZE = 128

D_RNN = D_MODEL // 2
RNN_HEADS = 8
RNN_HD = D_RNN // RNN_HEADS
CONV_W = 4
LRU_C = 8.0
D_MLSTM = D_MODEL // 2
MLSTM_HEADS = 4
MLSTM_HD = D_MLSTM // MLSTM_HEADS
CHUNK = 128
N_MEM = 256
X_HEADS = 4
X_HD = D_MODEL // X_HEADS
D_FF = 4 * D_MODEL
EPS = 1e-6
OFF_RX = 0
OFF_RG = OFF_RX + D_RNN
OFF_MU = OFF_RG + D_RNN
OFF_MV = OFF_MU + D_MLSTM
OFF_MO = OFF_MV + D_MLSTM
OFF_MI = OFF_MO + D_MLSTM
OFF_MF = OFF_MI + MLSTM_HEADS
D_IN = OFF_MF + MLSTM_HEADS

kernel_name = "hymba_rglru_mlstm_memxattn_step"


def rmsnorm(x, g):
    xf = x.astype(jnp.float32)
    y = xf * lax.rsqrt(jnp.mean(xf * xf, axis=-1, keepdims=True) + EPS) * g.astype(jnp.float32)
    return y.astype(x.dtype)


def causal_conv(x, buf, w, b):
    T = x.shape[1]
    xp = jnp.concatenate([buf.astype(x.dtype), x], axis=1)
    y = b.astype(x.dtype)
    for j in range(CONV_W):
        y = y + w[j].astype(x.dtype) * xp[:, j:j + T]
    return y, xp[:, -(CONV_W - 1):]


def rglru(xc, h0, w_a, b_a, w_x, b_x, lam):
    B, T, _ = xc.shape
    xf = xc.astype(jnp.float32)
    xh = xf.reshape(B, T, RNN_HEADS, RNN_HD)
    r = jax.nn.sigmoid(jnp.einsum('bthi,hij->bthj', xh, w_a.astype(jnp.float32)) + b_a).reshape(B, T, D_RNN)
    i = jax.nn.sigmoid(jnp.einsum('bthi,hij->bthj', xh, w_x.astype(jnp.float32)) + b_x).reshape(B, T, D_RNN)
    log_a = -LRU_C * r * jax.nn.softplus(-lam.astype(jnp.float32))
    a = jnp.exp(log_a)
    u = jnp.sqrt(-jnp.expm1(2.0 * log_a)) * (i * xf)

    def step(h, au):
        a_t, u_t = au
        h = a_t * h + u_t
        return h, h

    hT, hs = lax.scan(step, h0.astype(jnp.float32), (a.swapaxes(0, 1), u.swapaxes(0, 1)))
    return hs.swapaxes(0, 1), hT


def to_chunks(a, nc, L):
    B, H = a.shape[:2]
    return jnp.moveaxis(a.reshape(B, H, nc, L, *a.shape[3:]), 2, 0)


def mlstm_chunkwise(q, k, v, ig, lf, C0, n0, m0):
    B, H, T, Dh = q.shape
    L = CHUNK if T % CHUNK == 0 else T
    nc = T // L
    causal = jnp.tril(jnp.ones((L, L), dtype=bool))

    def step(carry, inp):
        C, n, m = carry
        qc, kc, vc, ic, fc = inp
        b = jnp.cumsum(fc, axis=-1)
        dmat = ic[..., None, :] + b[..., :, None] - b[..., None, :]
        dmat = jnp.where(causal, dmat, -jnp.inf)
        inter = b + m[..., None]
        m_t = jnp.maximum(inter, jnp.max(dmat, axis=-1))
        w_inter = jnp.exp(inter - m_t)
        s = jnp.einsum('bhtd,bhsd->bhts', qc, kc) * jnp.exp(dmat - m_t[..., None])
        num = w_inter[..., None] * jnp.einsum('bhvk,bhtk->bhtv', C, qc) + jnp.einsum('bhts,bhsv->bhtv', s, vc)
        den = w_inter * jnp.einsum('bhk,bhtk->bht', n, qc) + jnp.sum(s, axis=-1)
        h = num / jnp.maximum(jnp.abs(den), jnp.exp(-m_t))[..., None]
        m_new = m_t[..., -1]
        g_state = jnp.exp(b[..., -1] + m - m_new)
        g_in = jnp.exp(ic + b[..., -1:] - b - m_new[..., None])
        C_new = g_state[..., None, None] * C + jnp.einsum('bhs,bhsv,bhsk->bhvk', g_in, vc, kc)
        n_new = g_state[..., None] * n + jnp.einsum('bhs,bhsk->bhk', g_in, kc)
        return (C_new, n_new, m_new), h

    xs = (to_chunks(q, nc, L), to_chunks(k, nc, L), to_chunks(v, nc, L), to_chunks(ig, nc, L), to_chunks(lf, nc, L))
    init = (C0.astype(jnp.float32), n0.astype(jnp.float32), m0.astype(jnp.float32))
    (C, n, m), hs = lax.scan(step, init, xs)
    hs = jnp.moveaxis(hs, 0, 2).reshape(B, H, T, Dh)
    return hs, C, n, m


def mixer(h, rg_h, rg_conv, C, n, m, ml_conv, w):
    B, T, _ = h.shape
    z = h @ w['w_in']
    xr = z[..., OFF_RX:OFF_RG]
    gr = z[..., OFF_RG:OFF_MU]
    u = z[..., OFF_MU:OFF_MV]
    v = z[..., OFF_MV:OFF_MO]
    og = z[..., OFF_MO:OFF_MI]
    ig_pre = z[..., OFF_MI:OFF_MF]
    fg_pre = z[..., OFF_MF:D_IN]
    xc, rg_conv_new = causal_conv(xr, rg_conv, w['conv_rnn_w'], w['conv_rnn_b'])
    hs, rg_h_new = rglru(xc, rg_h, w['lru_wa'], w['lru_ba'], w['lru_wx'], w['lru_bx'], w['lru_lambda'])
    y_rnn = rmsnorm(hs * jax.nn.gelu(gr.astype(jnp.float32)), w['g_rnn_out'])
    uc, ml_conv_new = causal_conv(u, ml_conv, w['conv_ml_w'], w['conv_ml_b'])
    uc = jax.nn.silu(uc.astype(jnp.float32)).reshape(B, T, MLSTM_HEADS, MLSTM_HD)
    q = jnp.einsum('bthi,hij->bhtj', uc, w['ml_wq'].astype(jnp.float32))
    k = jnp.einsum('bthi,hij->bhtj', uc, w['ml_wk'].astype(jnp.float32)) * (MLSTM_HD ** -0.5)
    vv = v.astype(jnp.float32).reshape(B, T, MLSTM_HEADS, MLSTM_HD).transpose(0, 2, 1, 3)
    ig = (ig_pre.astype(jnp.float32) + w['ml_bi']).transpose(0, 2, 1)
    lf = jax.nn.log_sigmoid(fg_pre.astype(jnp.float32) + w['ml_bf']).transpose(0, 2, 1)
    hm, C_new, n_new, m_new = mlstm_chunkwise(q, k, vv, ig, lf, C, n, m)
    hm = hm.transpose(0, 2, 1, 3)
    o = jax.nn.sigmoid(og.astype(jnp.float32)).reshape(B, T, MLSTM_HEADS, MLSTM_HD)
    y_ml = (rmsnorm(hm, w['g_ml_out']) * o).reshape(B, T, D_MLSTM)
    y = jnp.concatenate([y_rnn, y_ml], axis=-1).astype(h.dtype) @ w['w_out']
    new = (rg_h_new.astype(rg_h.dtype), rg_conv_new.astype(rg_conv.dtype), C_new.astype(C.dtype),
           n_new.astype(n.dtype), m_new.astype(m.dtype), ml_conv_new.astype(ml_conv.dtype))
    return y, new


def mem_kv(mem, w):
    B = mem.shape[0]
    mn = rmsnorm(mem, w['g_mem'])
    mk = (mn @ w['w_mk']).reshape(B, N_MEM, X_HEADS, X_HD)
    mv = (mn @ w['w_mv']).reshape(B, N_MEM, X_HEADS, X_HD)
    return mk, mv


def cross_attn(h, mk, mv, w):
    B, T, _ = h.shape
    q = (h @ w['w_cq']).reshape(B, T, X_HEADS, X_HD)
    s = jnp.einsum('bthd,bnhd->bhtn', q, mk.astype(q.dtype)).astype(jnp.float32) * (X_HD ** -0.5)
    p = jax.nn.softmax(s, axis=-1)
    o = jnp.einsum('bhtn,bnhd->bthd', p.astype(h.dtype), mv.astype(h.dtype)).reshape(B, T, D_MODEL)
    return o @ w['w_co']


def block(x, rg_h, rg_conv, C, n, m, ml_conv, mk, mv, w):
    y, new = mixer(rmsnorm(x, w['g_mix']), rg_h, rg_conv, C, n, m, ml_conv, w)
    x = x + y.astype(x.dtype)
    x = x + cross_attn(rmsnorm(x, w['g_xattn']), mk, mv, w).astype(x.dtype)
    hf = rmsnorm(x, w['g_ffn'])
    x = x + (jnp.square(jax.nn.relu(hf @ w['w_up'])) @ w['w_down']).astype(x.dtype)
    return x, new


def setup_inputs(seed: int = 0) -> dict:
    key = jax.random.key(seed)
    ks = iter(jax.random.split(key, 64))

    def nrm(shape, scale=1.0):
        return jax.random.normal(next(ks), shape, jnp.float32) * scale

    def gain(shape):
        return 1.0 + nrm(shape, 0.01)

    Dp = DEPTH
    lam_a = jax.random.uniform(next(ks), (Dp, D_RNN), jnp.float32, 0.9, 0.999)
    return {
        'x_prompt': nrm((BATCH, SEQ, D_MODEL)),
        'x_sample': nrm((DEC_BATCH, DEC_SEQ, D_MODEL)),
        'mem_prompt': nrm((BATCH, N_MEM, D_MODEL)),
        'state_rglru_h': nrm((Dp, DEC_BATCH, D_RNN), 0.5),
        'state_rglru_conv': nrm((Dp, DEC_BATCH, CONV_W - 1, D_RNN)),
        'state_mlstm_C': nrm((Dp, DEC_BATCH, MLSTM_HEADS, MLSTM_HD, MLSTM_HD), 0.1),
        'state_mlstm_n': nrm((Dp, DEC_BATCH, MLSTM_HEADS, MLSTM_HD), 0.1),
        'state_mlstm_m': nrm((Dp, DEC_BATCH, MLSTM_HEADS)),
        'state_mlstm_conv': nrm((Dp, DEC_BATCH, CONV_W - 1, D_MLSTM)),
        'cache_mem_k': nrm((Dp, DEC_BATCH, N_MEM, X_HEADS, X_HD)),
        'cache_mem_v': nrm((Dp, DEC_BATCH, N_MEM, X_HEADS, X_HD)),
        'g_mix': gain((Dp, D_MODEL)),
        'w_in': nrm((Dp, D_MODEL, D_IN), D_MODEL ** -0.5),
        'conv_rnn_w': nrm((Dp, CONV_W, D_RNN), CONV_W ** -0.5),
        'conv_rnn_b': nrm((Dp, D_RNN), 0.01),
        'lru_wa': nrm((Dp, RNN_HEADS, RNN_HD, RNN_HD), RNN_HD ** -0.5),
        'lru_ba': nrm((Dp, RNN_HEADS, RNN_HD), 0.01),
        'lru_wx': nrm((Dp, RNN_HEADS, RNN_HD, RNN_HD), RNN_HD ** -0.5),
        'lru_bx': nrm((Dp, RNN_HEADS, RNN_HD), 0.01),
        'lru_lambda': jnp.log(lam_a) - jnp.log1p(-lam_a),
        'g_rnn_out': gain((Dp, D_RNN)),
        'conv_ml_w': nrm((Dp, CONV_W, D_MLSTM), CONV_W ** -0.5),
        'conv_ml_b': nrm((Dp, D_MLSTM), 0.01),
        'ml_wq': nrm((Dp, MLSTM_HEADS, MLSTM_HD, MLSTM_HD), MLSTM_HD ** -0.5),
        'ml_wk': nrm((Dp, MLSTM_HEADS, MLSTM_HD, MLSTM_HD), MLSTM_HD ** -0.5),
        'ml_bi': nrm((Dp, MLSTM_HEADS), 0.1),
        'ml_bf': jnp.linspace(3.0, 6.0, MLSTM_HEADS, dtype=jnp.float32)[None, :] + nrm((Dp, MLSTM_HEADS), 0.1),
        'g_ml_out': gain((Dp, MLSTM_HD)),
        'w_out': nrm((Dp, D_MODEL, D_MODEL), D_MODEL ** -0.5),
        'g_xattn': gain((Dp, D_MODEL)),
        'g_mem': gain((Dp, D_MODEL)),
        'w_cq': nrm((Dp, D_MODEL, D_MODEL), D_MODEL ** -0.5),
        'w_mk': nrm((Dp, D_MODEL, D_MODEL), D_MODEL ** -0.5),
        'w_mv': nrm((Dp, D_MODEL, D_MODEL), D_MODEL ** -0.5),
        'w_co': nrm((Dp, D_MODEL, D_MODEL), D_MODEL ** -0.5),
        'g_ffn': gain((Dp, D_MODEL)),
        'w_up': nrm((Dp, D_MODEL, D_FF), D_MODEL ** -0.5),
        'w_down': nrm((Dp, D_FF, D_MODEL), D_FF ** -0.5),
        'g_final': gain((D_MODEL,)),
    }


def reference(x_prompt, x_sample, mem_prompt, state_rglru_h, state_rglru_conv, state_mlstm_C,
              state_mlstm_n, state_mlstm_m, state_mlstm_conv, cache_mem_k, cache_mem_v,
              g_mix, w_in, conv_rnn_w, conv_rnn_b, lru_wa, lru_ba, lru_wx, lru_bx, lru_lambda,
              g_rnn_out, conv_ml_w, conv_ml_b, ml_wq, ml_wk, ml_bi, ml_bf, g_ml_out, w_out,
              g_xattn, g_mem, w_cq, w_mk, w_mv, w_co, g_ffn, w_up, w_down, g_final):
    B = x_prompt.shape[0]
    dt = x_prompt.dtype
    xp, xs = x_prompt, x_sample
    p_out = [[] for _ in range(8)]
    s_out = [[] for _ in range(6)]
    for l in range(DEPTH):
        w = dict(g_mix=g_mix[l], w_in=w_in[l], conv_rnn_w=conv_rnn_w[l], conv_rnn_b=conv_rnn_b[l],
                 lru_wa=lru_wa[l], lru_ba=lru_ba[l], lru_wx=lru_wx[l], lru_bx=lru_bx[l],
                 lru_lambda=lru_lambda[l], g_rnn_out=g_rnn_out[l], conv_ml_w=conv_ml_w[l],
                 conv_ml_b=conv_ml_b[l], ml_wq=ml_wq[l], ml_wk=ml_wk[l], ml_bi=ml_bi[l], ml_bf=ml_bf[l],
                 g_ml_out=g_ml_out[l], w_out=w_out[l], g_xattn=g_xattn[l], g_mem=g_mem[l],
                 w_cq=w_cq[l], w_mk=w_mk[l], w_mv=w_mv[l], w_co=w_co[l], g_ffn=g_ffn[l],
                 w_up=w_up[l], w_down=w_down[l])
        z_h = jnp.zeros((B, D_RNN), dt)
        z_rc = jnp.zeros((B, CONV_W - 1, D_RNN), dt)
        z_C = jnp.zeros((B, MLSTM_HEADS, MLSTM_HD, MLSTM_HD), dt)
        z_n = jnp.zeros((B, MLSTM_HEADS, MLSTM_HD), dt)
        z_m = jnp.zeros((B, MLSTM_HEADS), dt)
        z_mc = jnp.zeros((B, CONV_W - 1, D_MLSTM), dt)
        mk_p, mv_p = mem_kv(mem_prompt, w)
        xp, new_p = block(xp, z_h, z_rc, z_C, z_n, z_m, z_mc, mk_p, mv_p, w)
        for j, a in enumerate(new_p):
            p_out[j].append(a)
        p_out[6].append(mk_p)
        p_out[7].append(mv_p)
        xs, new_s = block(xs, state_rglru_h[l], state_rglru_conv[l], state_mlstm_C[l], state_mlstm_n[l],
                          state_mlstm_m[l], state_mlstm_conv[l], cache_mem_k[l], cache_mem_v[l], w)
        for j, a in enumerate(new_s):
            s_out[j].append(a)
    y_prompt = rmsnorm(xp, g_final)
    y_sample = rmsnorm(xs, g_final)
    P = [jnp.stack(a, axis=0) for a in p_out]
    S = [jnp.stack(a, axis=0) for a in s_out]
    return (y_prompt, y_sample, P[0], P[1], P[2], P[3], P[4], P[5], P[6], P[7],
            S[0], S[1], S[2], S[3], S[4], S[5])
```

```python
import functools

import jax
import jax.numpy as jnp
from jax import lax
from jax.experimental import pallas as pl
from jax.experimental.pallas import tpu as pltpu

F32 = jnp.float32
BF16 = jnp.bfloat16

D_MODEL = 2048
D_RNN = 1024
RNN_HEADS = 8
RNN_HD = 128
CONV_W = 4
LRU_C = 8.0
D_MLSTM = 1024
MLSTM_HEADS = 4
MLSTM_HD = 256
CHUNK = 128
N_MEM = 256
X_HEADS = 4
X_HD = 512
D_FF = 8192
EPS = 1e-6
D_MAIN = 5 * 1024
N_GATE_PAD = 128

V7X_VMEM_LIMIT_BYTES = 56 * 1024 * 1024


def _params(*sem):
    return pltpu.CompilerParams(dimension_semantics=sem, vmem_limit_bytes=V7X_VMEM_LIMIT_BYTES)


def _rms(x, g):
    ms = jnp.mean(x * x, axis=-1, keepdims=True)
    return x * lax.rsqrt(ms + EPS) * g


def _dot(a, b):
    return jnp.dot(a, b, preferred_element_type=F32)


def _norm_matmul_kernel(*refs, has_extra):
    if has_extra:
        x_ref, g_ref, w_ref, we_ref, o_ref, oe_ref, xn_ref = refs
    else:
        x_ref, g_ref, w_ref, o_ref, xn_ref = refs

    @pl.when(pl.program_id(1) == 0)
    def _():
        xn = _rms(x_ref[...], g_ref[...]).astype(BF16)
        xn_ref[...] = xn
        if has_extra:
            oe_ref[...] = _dot(xn, we_ref[...])

    o_ref[...] = _dot(xn_ref[...], w_ref[...]).astype(o_ref.dtype)


def _norm_matmul(x, g, w, *, tm, tn, out_dtype, w_extra=None):
    m, k = x.shape
    n = w.shape[1]
    has_extra = w_extra is not None
    in_specs = [pl.BlockSpec((tm, k), lambda i, j: (i, 0)),
                pl.BlockSpec((1, k), lambda i, j: (0, 0)),
                pl.BlockSpec((k, tn), lambda i, j: (0, j))]
    out_specs = [pl.BlockSpec((tm, tn), lambda i, j: (i, j))]
    out_shape = [jax.ShapeDtypeStruct((m, n), out_dtype)]
    args = [x, g.reshape(1, k), w]
    if has_extra:
        ne = w_extra.shape[1]
        in_specs.append(pl.BlockSpec((k, ne), lambda i, j: (0, 0)))
        out_specs.append(pl.BlockSpec((tm, ne), lambda i, j: (i, 0)))
        out_shape.append(jax.ShapeDtypeStruct((m, ne), F32))
        args.append(w_extra)
    res = pl.pallas_call(
        functools.partial(_norm_matmul_kernel, has_extra=has_extra),
        grid=(m // tm, n // tn),
        in_specs=in_specs, out_specs=out_specs, out_shape=out_shape,
        scratch_shapes=[pltpu.VMEM((tm, k), BF16)],
        compiler_params=_params("parallel", "arbitrary"),
        name="norm_matmul",
    )(*args)
    return res if has_extra else res[0]


def _matmul_res_kernel(a_ref, w_ref, r_ref, o_ref):
    o_ref[...] = r_ref[...] + _dot(a_ref[...], w_ref[...])


def _matmul_res(a, w, res, *, tm, tn):
    m, k = a.shape
    n = w.shape[1]
    return pl.pallas_call(
        _matmul_res_kernel,
        grid=(m // tm, n // tn),
        in_specs=[pl.BlockSpec((tm, k), lambda i, j: (i, 0)),
                  pl.BlockSpec((k, tn), lambda i, j: (0, j)),
                  pl.BlockSpec((tm, tn), lambda i, j: (i, j))],
        out_specs=pl.BlockSpec((tm, tn), lambda i, j: (i, j)),
        out_shape=jax.ShapeDtypeStruct((m, n), F32),
        compiler_params=_params("parallel", "arbitrary"),
        name="matmul_res",
    )(a, w, res)


def _ffn_kernel(x_ref, g_ref, wu_ref, wd_ref, gf_ref, o_ref, xn_ref, *, final_norm):
    f = pl.program_id(1)

    @pl.when(f == 0)
    def _():
        xn_ref[...] = _rms(x_ref[...], g_ref[...]).astype(BF16)
        o_ref[...] = jnp.zeros_like(o_ref)

    h = _dot(xn_ref[...], wu_ref[...])
    h = jnp.square(jnp.maximum(h, 0.0)).astype(BF16)
    o_ref[...] += _dot(h, wd_ref[...])

    @pl.when(f == pl.num_programs(1) - 1)
    def _():
        y = x_ref[...] + o_ref[...]
        if final_norm:
            y = _rms(y, gf_ref[...])
        o_ref[...] = y


def _ffn(x, g, w_up, w_down, g_final, *, tm, tf, final_norm):
    m, d = x.shape
    dff = w_up.shape[1]
    return pl.pallas_call(
        functools.partial(_ffn_kernel, final_norm=final_norm),
        grid=(m // tm, dff // tf),
        in_specs=[pl.BlockSpec((tm, d), lambda i, f: (i, 0)),
                  pl.BlockSpec((1, d), lambda i, f: (0, 0)),
                  pl.BlockSpec((d, tf), lambda i, f: (0, f)),
                  pl.BlockSpec((tf, d), lambda i, f: (f, 0)),
                  pl.BlockSpec((1, d), lambda i, f: (0, 0))],
        out_specs=pl.BlockSpec((tm, d), lambda i, f: (i, 0)),
        out_shape=jax.ShapeDtypeStruct((m, d), F32),
        scratch_shapes=[pltpu.VMEM((tm, d), BF16)],
        compiler_params=_params("parallel", "arbitrary"),
        name="ffn",
    )(x, g.reshape(1, d), w_up, w_down, g_final.reshape(1, d))


def _lru_gates(xc, wg, bg, lam):
    g = _dot(xc.astype(BF16), wg) + bg
    r = jax.nn.sigmoid(g[:, :RNN_HD])
    i = jax.nn.sigmoid(g[:, RNN_HD:])
    log_a = -LRU_C * r * jax.nn.softplus(-lam)
    a = jnp.exp(log_a)
    u = jnp.sqrt(-jnp.tanh(log_a) * (a * a + 1.0)) * (i * xc)
    return a, u


def _rglru_kernel(xr_ref, gr_ref, h0_ref, c0_ref, cw_ref, cb_ref, wg_ref, bg_ref, lam_ref, gout_ref,
                  y_ref, hn_ref, cn_ref, xbuf, hc, ybuf, *, tc):
    c = pl.program_id(1)

    @pl.when(c == 0)
    def _():
        xbuf[0:8, :] = jnp.zeros((8, D_RNN), F32)
        xbuf[5:8, :] = c0_ref[0]
        hc[...] = h0_ref[0]

    @pl.when(c > 0)
    def _():
        xbuf[0:8, :] = xbuf[tc:tc + 8, :]

    xbuf[8:8 + tc, :] = xr_ref[0]
    row = lax.broadcasted_iota(jnp.int32, (tc, RNN_HD), 0)
    ssq = jnp.zeros((tc, 1), F32)
    for h in range(RNN_HEADS):
        sl = slice(h * RNN_HD, (h + 1) * RNN_HD)
        xc = cb_ref[:, sl]
        for j in range(CONV_W):
            xc = xc + cw_ref[j:j + 1, sl] * xbuf[5 + j:5 + j + tc, sl]
        a, u = _lru_gates(xc, wg_ref[h], bg_ref[h], lam_ref[:, sl])
        d = 1
        while d < tc:
            keep = row >= d
            a_prev = jnp.where(keep, pltpu.roll(a, d, 0), 1.0)
            u_prev = jnp.where(keep, pltpu.roll(u, d, 0), 0.0)
            u = u + a * u_prev
            a = a * a_prev
            d *= 2
        hs = u + a * hc[:, sl]
        hc[:, sl] = hs[tc - 1:tc, :]
        yv = hs * jax.nn.gelu(gr_ref[0, :, sl])
        ssq = ssq + jnp.sum(yv * yv, axis=-1, keepdims=True)
        ybuf[:, sl] = yv
    y = ybuf[...] * lax.rsqrt(ssq * (1.0 / D_RNN) + EPS) * gout_ref[...]
    y_ref[0] = y.astype(y_ref.dtype)

    @pl.when(c == pl.num_programs(1) - 1)
    def _():
        hn_ref[0] = hc[...]
        cn_ref[0] = xbuf[tc + 5:tc + 8, :]


def _rglru_prompt(z3, h0, conv0, w, *, tc):
    b, t, _ = z3.shape
    full2 = lambda bi, ci: (0, 0)
    return pl.pallas_call(
        functools.partial(_rglru_kernel, tc=tc),
        grid=(b, t // tc),
        in_specs=[pl.BlockSpec((1, tc, D_RNN), lambda bi, ci: (bi, ci, 0)),
                  pl.BlockSpec((1, tc, D_RNN), lambda bi, ci: (bi, ci, 1)),
                  pl.BlockSpec((1, 1, D_RNN), lambda bi, ci: (bi, 0, 0)),
                  pl.BlockSpec((1, CONV_W - 1, D_RNN), lambda bi, ci: (bi, 0, 0)),
                  pl.BlockSpec((CONV_W, D_RNN), full2),
                  pl.BlockSpec((1, D_RNN), full2),
                  pl.BlockSpec((RNN_HEADS, RNN_HD, 2 * RNN_HD), lambda bi, ci: (0, 0, 0)),
                  pl.BlockSpec((RNN_HEADS, 1, 2 * RNN_HD), lambda bi, ci: (0, 0, 0)),
                  pl.BlockSpec((1, D_RNN), full2),
                  pl.BlockSpec((1, D_RNN), full2)],
        out_specs=[pl.BlockSpec((1, tc, D_RNN), lambda bi, ci: (bi, ci, 0)),
                   pl.BlockSpec((1, 1, D_RNN), lambda bi, ci: (bi, 0, 0)),
                   pl.BlockSpec((1, CONV_W - 1, D_RNN), lambda bi, ci: (bi, 0, 0))],
        out_shape=[jax.ShapeDtypeStruct((b, t, D_MODEL), BF16),
                   jax.ShapeDtypeStruct((b, 1, D_RNN), F32),
                   jax.ShapeDtypeStruct((b, CONV_W - 1, D_RNN), F32)],
        scratch_shapes=[pltpu.VMEM((tc + 8, D_RNN), F32),
                        pltpu.VMEM((1, D_RNN), F32),
                        pltpu.VMEM((tc, D_RNN), F32)],
        compiler_params=_params("parallel", "arbitrary"),
        name="rglru_prompt",
    )(z3, z3, h0, conv0, w['conv_rnn_w'], w['conv_rnn_b'], w['lru_wg'], w['lru_bg'],
      w['lru_lambda'], w['g_rnn_out'])


def _mlstm_kernel(u_ref, v_ref, og_ref, zg_ref, gb_ref, c0_ref, n0_ref, m0_ref, cv0_ref,
                  cw_ref, cb_ref, wqk_ref, gout_ref, yin_ref,
                  y_ref, cn_ref, nn_ref, mn_ref, cvn_ref, ubuf, cst, nst, mst):
    del yin_ref
    hd = pl.program_id(1)
    c = pl.program_id(2)
    L = CHUNK

    @pl.when(c == 0)
    def _():
        ubuf[0:8, :] = jnp.zeros((8, MLSTM_HD), F32)
        ubuf[5:8, :] = cv0_ref[0]
        cst[...] = c0_ref[0, 0]
        nst[...] = n0_ref[0, 0]
        mst[...] = m0_ref[0, 0]

    @pl.when(c > 0)
    def _():
        ubuf[0:8, :] = ubuf[L:L + 8, :]

    ubuf[8:8 + L, :] = u_ref[0]
    uc = cb_ref[...]
    for j in range(CONV_W):
        uc = uc + cw_ref[j:j + 1, :] * ubuf[5 + j:5 + j + L, :]
    uc = jax.nn.silu(uc)
    qk = _dot(uc.astype(BF16), wqk_ref[0])
    q = qk[:, :MLSTM_HD]
    k = qk[:, MLSTM_HD:] * (MLSTM_HD ** -0.5)
    v = v_ref[0]
    qb, kb = q.astype(BF16), k.astype(BF16)

    zg = zg_ref[0] + gb_ref[...]
    zgt = zg.T
    lane = lax.broadcasted_iota(jnp.int32, (L, N_GATE_PAD), 1)
    sub = lax.broadcasted_iota(jnp.int32, (N_GATE_PAD, L), 0)
    icol = jnp.sum(jnp.where(lane == hd, zg, 0.0), axis=1, keepdims=True)
    fcol = jax.nn.log_sigmoid(jnp.sum(jnp.where(lane == hd + MLSTM_HEADS, zg, 0.0), axis=1, keepdims=True))
    irow = jnp.sum(jnp.where(sub == hd, zgt, 0.0), axis=0, keepdims=True)
    frow = jax.nn.log_sigmoid(jnp.sum(jnp.where(sub == hd + MLSTM_HEADS, zgt, 0.0), axis=0, keepdims=True))

    ti = lax.broadcasted_iota(jnp.int32, (L, L), 0)
    si = lax.broadcasted_iota(jnp.int32, (L, L), 1)
    causal = si <= ti
    bcol = jnp.sum(jnp.where(causal, frow, 0.0), axis=1, keepdims=True)
    brow = jnp.sum(jnp.where(ti <= si, fcol, 0.0), axis=0, keepdims=True)
    dmat = jnp.where(causal, irow + bcol - brow, -jnp.inf)
    m_prev = mst[:, 0:1]
    inter = bcol + m_prev
    m_t = jnp.maximum(inter, jnp.max(dmat, axis=1, keepdims=True))
    w_inter = jnp.exp(inter - m_t)
    s = lax.dot_general(qb, kb, (((1,), (1,)), ((), ())), preferred_element_type=F32) * jnp.exp(dmat - m_t)
    cmat = cst[...]
    nrow = nst[...]
    cq = lax.dot_general(qb, cmat.astype(BF16), (((1,), (1,)), ((), ())), preferred_element_type=F32)
    num = w_inter * cq + _dot(s.astype(BF16), v.astype(BF16))
    den = w_inter * jnp.sum(q * nrow, axis=1, keepdims=True) + jnp.sum(s, axis=1, keepdims=True)
    hm = num / jnp.maximum(jnp.abs(den), jnp.exp(-m_t))

    m_new = m_t[L - 1:L, :]
    b_last = bcol[L - 1:L, :]
    g_state = jnp.exp(b_last + m_prev - m_new)
    g_in = jnp.exp(icol + b_last - bcol - m_new)
    cst[...] = g_state * cmat + lax.dot_general((g_in * v).astype(BF16), kb, (((0,), (0,)), ((), ())),
                                                preferred_element_type=F32)
    nst[...] = g_state * nrow + jnp.sum(g_in * k, axis=0, keepdims=True)
    mst[...] = jnp.broadcast_to(m_new, mst.shape)

    y = _rms(hm, gout_ref[...]) * jax.nn.sigmoid(og_ref[0])
    y_ref[0] = y.astype(y_ref.dtype)

    @pl.when(c == pl.num_programs(2) - 1)
    def _():
        cn_ref[0, 0] = cst[...]
        nn_ref[0, 0] = nst[...]
        mn_ref[0, 0] = mst[...]
        cvn_ref[0] = ubuf[L + 5:L + 8, :]


def _mlstm_prompt(z3, zg3, y_in, c0, n0, m0, conv0, w):
    b, t, _ = z3.shape
    nc = t // CHUNK
    hdb = D_MAIN // MLSTM_HD
    u_blk, v_blk, o_blk = 2 * D_RNN // MLSTM_HD, 3 * D_RNN // MLSTM_HD, 4 * D_RNN // MLSTM_HD
    del hdb
    n_in = 14
    return pl.pallas_call(
        _mlstm_kernel,
        grid=(b, MLSTM_HEADS, nc),
        in_specs=[pl.BlockSpec((1, CHUNK, MLSTM_HD), lambda bi, hi, ci: (bi, ci, u_blk + hi)),
                  pl.BlockSpec((1, CHUNK, MLSTM_HD), lambda bi, hi, ci: (bi, ci, v_blk + hi)),
                  pl.BlockSpec((1, CHUNK, MLSTM_HD), lambda bi, hi, ci: (bi, ci, o_blk + hi)),
                  pl.BlockSpec((1, CHUNK, N_GATE_PAD), lambda bi, hi, ci: (bi, ci, 0)),
                  pl.BlockSpec((1, N_GATE_PAD), lambda bi, hi, ci: (0, 0)),
                  pl.BlockSpec((1, 1, MLSTM_HD, MLSTM_HD), lambda bi, hi, ci: (bi, hi, 0, 0)),
                  pl.BlockSpec((1, 1, 1, MLSTM_HD), lambda bi, hi, ci: (bi, hi, 0, 0)),
                  pl.BlockSpec((1, 1, 1, N_GATE_PAD), lambda bi, hi, ci: (bi, hi, 0, 0)),
                  pl.BlockSpec((1, CONV_W - 1, MLSTM_HD), lambda bi, hi, ci: (bi, 0, hi)),
                  pl.BlockSpec((CONV_W, MLSTM_HD), lambda bi, hi, ci: (0, hi)),
                  pl.BlockSpec((1, MLSTM_HD), lambda bi, hi, ci: (0, hi)),
                  pl.BlockSpec((1, MLSTM_HD, 2 * MLSTM_HD), lambda bi, hi, ci: (hi, 0, 0)),
                  pl.BlockSpec((1, MLSTM_HD), lambda bi, hi, ci: (0, 0)),
                  pl.BlockSpec(memory_space=pl.ANY)],
        out_specs=[pl.BlockSpec((1, CHUNK, MLSTM_HD), lambda bi, hi, ci: (bi, ci, D_RNN // MLSTM_HD + hi)),
                   pl.BlockSpec((1, 1, MLSTM_HD, MLSTM_HD), lambda bi, hi, ci: (bi, hi, 0, 0)),
                   pl.BlockSpec((1, 1, 1, MLSTM_HD), lambda bi, hi, ci: (bi, hi, 0, 0)),
                   pl.BlockSpec((1, 1, 1, N_GATE_PAD), lambda bi, hi, ci: (bi, hi, 0, 0)),
                   pl.BlockSpec((1, CONV_W - 1, MLSTM_HD), lambda bi, hi, ci: (bi, 0, hi))],
        out_shape=[jax.ShapeDtypeStruct((b, t, D_MODEL), BF16),
                   jax.ShapeDtypeStruct((b, MLSTM_HEADS, MLSTM_HD, MLSTM_HD), F32),
                   jax.ShapeDtypeStruct((b, MLSTM_HEADS, 1, MLSTM_HD), F32),
                   jax.ShapeDtypeStruct((b, MLSTM_HEADS, 1, N_GATE_PAD), F32),
                   jax.ShapeDtypeStruct((b, CONV_W - 1, D_MLSTM), F32)],
        scratch_shapes=[pltpu.VMEM((CHUNK + 8, MLSTM_HD), F32),
                        pltpu.VMEM((MLSTM_HD, MLSTM_HD), F32),
                        pltpu.VMEM((1, MLSTM_HD), F32),
                        pltpu.VMEM((1, N_GATE_PAD), F32)],
        input_output_aliases={n_in - 1: 0},
        compiler_params=_params("parallel", "parallel", "arbitrary"),
        name="mlstm_prompt",
    )(z3, z3, z3, zg3, w['gate_bias'], c0, n0, m0, conv0, w['conv_ml_w'], w['conv_ml_b'],
      w['ml_wqk'], w['g_ml_out'], y_in)


def _xattn_kernel(q_ref, k_ref, v_ref, o_ref):
    for h in range(X_HEADS):
        sl = slice(h * X_HD, (h + 1) * X_HD)
        s = lax.dot_general(q_ref[0, :, sl], k_ref[0, :, sl].astype(BF16), (((1,), (1,)), ((), ())),
                            preferred_element_type=F32) * (X_HD ** -0.5)
        e = jnp.exp(s - jnp.max(s, axis=-1, keepdims=True))
        p = e / jnp.sum(e, axis=-1, keepdims=True)
        o_ref[0, :, sl] = _dot(p.astype(BF16), v_ref[0, :, sl].astype(BF16)).astype(o_ref.dtype)


def _xattn_prompt(q3, mk3, mv3, *, tq):
    b, t, d = q3.shape
    return pl.pallas_call(
        _xattn_kernel,
        grid=(b, t // tq),
        in_specs=[pl.BlockSpec((1, tq, d), lambda bi, ti: (bi, ti, 0)),
                  pl.BlockSpec((1, N_MEM, d), lambda bi, ti: (bi, 0, 0)),
                  pl.BlockSpec((1, N_MEM, d), lambda bi, ti: (bi, 0, 0))],
        out_specs=pl.BlockSpec((1, tq, d), lambda bi, ti: (bi, ti, 0)),
        out_shape=jax.ShapeDtypeStruct((b, t, d), BF16),
        compiler_params=_params("parallel", "arbitrary"),
        name="xattn_prompt",
    )(q3, mk3, mv3)


def _xattn_sample_kernel(q_ref, k_ref, v_ref, o_ref):
    prod = k_ref[0] * q_ref[0]
    v = v_ref[0]
    for h in range(X_HEADS):
        sl = slice(h * X_HD, (h + 1) * X_HD)
        s = jnp.sum(prod[:, sl], axis=1, keepdims=True) * (X_HD ** -0.5)
        e = jnp.exp(s - jnp.max(s, axis=0, keepdims=True))
        p = e / jnp.sum(e, axis=0, keepdims=True)
        o_ref[0, :, sl] = jnp.sum(p * v[:, sl], axis=0, keepdims=True).astype(o_ref.dtype)


def _xattn_sample(q3, k3, v3):
    b, _, d = q3.shape
    return pl.pallas_call(
        _xattn_sample_kernel,
        grid=(b,),
        in_specs=[pl.BlockSpec((1, 1, d), lambda bi: (bi, 0, 0)),
                  pl.BlockSpec((1, N_MEM, d), lambda bi: (bi, 0, 0)),
                  pl.BlockSpec((1, N_MEM, d), lambda bi: (bi, 0, 0))],
        out_specs=pl.BlockSpec((1, 1, d), lambda bi: (bi, 0, 0)),
        out_shape=jax.ShapeDtypeStruct((b, 1, d), BF16),
        compiler_params=_params("parallel"),
        name="xattn_sample",
    )(q3, k3, v3)


def _smix_rows_kernel(z_ref, h0_ref, rc_ref, mc_ref, cwr_ref, cbr_ref, wg_ref, bg_ref, lam_ref, gout_ref,
                      cwm_ref, cbm_ref, wqk_ref,
                      yr_ref, hn_ref, rcn_ref, mcn_ref, q_ref, k_ref, vt_ref, ybuf):
    nb = z_ref.shape[0]
    ssq = jnp.zeros((nb, 1), F32)
    for h in range(RNN_HEADS):
        sl = slice(h * RNN_HD, (h + 1) * RNN_HD)
        xc = cbr_ref[:, sl] + cwr_ref[CONV_W - 1:CONV_W, sl] * z_ref[:, sl]
        for j in range(CONV_W - 1):
            xc = xc + cwr_ref[j:j + 1, sl] * rc_ref[:, j * D_RNN + h * RNN_HD:j * D_RNN + (h + 1) * RNN_HD]
        a, u = _lru_gates(xc, wg_ref[h], bg_ref[h], lam_ref[:, sl])
        hs = a * h0_ref[:, sl] + u
        hn_ref[:, sl] = hs
        yv = hs * jax.nn.gelu(z_ref[:, D_RNN + h * RNN_HD:D_RNN + (h + 1) * RNN_HD])
        ssq = ssq + jnp.sum(yv * yv, axis=-1, keepdims=True)
        ybuf[:, sl] = yv
    yr_ref[...] = (ybuf[...] * lax.rsqrt(ssq * (1.0 / D_RNN) + EPS) * gout_ref[...]).astype(yr_ref.dtype)
    rcn_ref[:, 0:2 * D_RNN] = rc_ref[:, D_RNN:3 * D_RNN]
    rcn_ref[:, 2 * D_RNN:3 * D_RNN] = z_ref[:, 0:D_RNN]

    for h in range(MLSTM_HEADS):
        sl = slice(h * MLSTM_HD, (h + 1) * MLSTM_HD)
        uc = cbm_ref[:, sl] + cwm_ref[CONV_W - 1:CONV_W, sl] * z_ref[:, 2 * D_RNN + h * MLSTM_HD:2 * D_RNN + (h + 1) * MLSTM_HD]
        for j in range(CONV_W - 1):
            uc = uc + cwm_ref[j:j + 1, sl] * mc_ref[:, j * D_MLSTM + h * MLSTM_HD:j * D_MLSTM + (h + 1) * MLSTM_HD]
        uc = jax.nn.silu(uc)
        qk = _dot(uc.astype(BF16), wqk_ref[h])
        q_ref[:, sl] = qk[:, :MLSTM_HD]
        k_ref[:, sl] = qk[:, MLSTM_HD:] * (MLSTM_HD ** -0.5)
    mcn_ref[:, 0:2 * D_MLSTM] = mc_ref[:, D_MLSTM:3 * D_MLSTM]
    mcn_ref[:, 2 * D_MLSTM:3 * D_MLSTM] = z_ref[:, 2 * D_RNN:2 * D_RNN + D_MLSTM]
    vt_ref[...] = z_ref[:, 3 * D_RNN:3 * D_RNN + D_MLSTM].T


def _smix_rows(z, h0, rconv, mconv, w):
    nb = z.shape[0]
    outs = [jax.ShapeDtypeStruct((nb, D_RNN), BF16),
            jax.ShapeDtypeStruct((nb, D_RNN), F32),
            jax.ShapeDtypeStruct((nb, 3 * D_RNN), F32),
            jax.ShapeDtypeStruct((nb, 3 * D_MLSTM), F32),
            jax.ShapeDtypeStruct((nb, D_MLSTM), F32),
            jax.ShapeDtypeStruct((nb, D_MLSTM), F32),
            jax.ShapeDtypeStruct((D_MLSTM, nb), F32)]
    return pl.pallas_call(
        _smix_rows_kernel,
        out_shape=outs,
        scratch_shapes=[pltpu.VMEM((nb, D_RNN), F32)],
        compiler_params=pltpu.CompilerParams(vmem_limit_bytes=V7X_VMEM_LIMIT_BYTES),
        name="smix_rows",
    )(z, h0, rconv, mconv, w['conv_rnn_w'], w['conv_rnn_b'], w['lru_wg'], w['lru_bg'], w['lru_lambda'],
      w['g_rnn_out'], w['conv_ml_w'], w['conv_ml_b'], w['ml_wqk'])


def _smix_state_kernel(c_ref, q_ref, k_ref, n_ref, zg_ref, gb_ref, m_ref, vt_ref,
                       cn_ref, nn_ref, mn_ref, ht_ref):
    bi = pl.program_id(0)
    nb = vt_ref.shape[1]
    lane_b = lax.broadcasted_iota(jnp.int32, (MLSTM_HD, nb), 1)
    lane_g = lax.broadcasted_iota(jnp.int32, (1, N_GATE_PAD), 1)
    zg = zg_ref[0] + gb_ref[...]
    m_out = jnp.zeros((1, N_GATE_PAD), F32)

    @pl.when(bi == 0)
    def _():
        ht_ref[...] = jnp.zeros_like(ht_ref)

    for h in range(MLSTM_HEADS):
        sl = slice(h * MLSTM_HD, (h + 1) * MLSTM_HD)
        q = q_ref[0, :, sl]
        k = k_ref[0, :, sl]
        n = n_ref[0, :, sl]
        ig = zg[:, h:h + 1]
        lf = jax.nn.log_sigmoid(zg[:, MLSTM_HEADS + h:MLSTM_HEADS + h + 1])
        m_prev = m_ref[0, :, h:h + 1]
        m_t = jnp.maximum(lf + m_prev, ig)
        w_inter = jnp.exp(lf + m_prev - m_t)
        g_in = jnp.exp(ig - m_t)
        s = jnp.sum(q * k, axis=1, keepdims=True) * g_in
        den = w_inter * jnp.sum(n * q, axis=1, keepdims=True) + s
        denom = jnp.maximum(jnp.abs(den), jnp.exp(-m_t))
        cmat = c_ref[0, h]
        cq = jnp.sum(cmat * q, axis=1, keepdims=True)
        vcol = jnp.sum(jnp.where(lane_b == bi, vt_ref[sl, :], 0.0), axis=1, keepdims=True)
        hcol = (w_inter * cq + s * vcol) / denom
        cn_ref[0, h] = w_inter * cmat + (g_in * vcol) * k
        nn_ref[0, :, sl] = w_inter * n + g_in * k
        m_out = jnp.where(lane_g == h, m_t, m_out)
        ht_ref[sl, :] = jnp.where(lane_b == bi, hcol, ht_ref[sl, :])
    mn_ref[0] = m_out


def _smix_state(c, q3, k3, n3, zg3, gate_bias, m3, vt):
    nb = c.shape[0]
    row = lambda width: pl.BlockSpec((1, 1, width), lambda bi: (bi, 0, 0))
    return pl.pallas_call(
        _smix_state_kernel,
        grid=(nb,),
        in_specs=[pl.BlockSpec((1, MLSTM_HEADS, MLSTM_HD, MLSTM_HD), lambda bi: (bi, 0, 0, 0)),
                  row(D_MLSTM), row(D_MLSTM), row(D_MLSTM), row(N_GATE_PAD),
                  pl.BlockSpec((1, N_GATE_PAD), lambda bi: (0, 0)),
                  row(MLSTM_HEADS),
                  pl.BlockSpec((D_MLSTM, nb), lambda bi: (0, 0))],
        out_specs=[pl.BlockSpec((1, MLSTM_HEADS, MLSTM_HD, MLSTM_HD), lambda bi: (bi, 0, 0, 0)),
                   row(D_MLSTM), row(N_GATE_PAD),
                   pl.BlockSpec((D_MLSTM, nb), lambda bi: (0, 0))],
        out_shape=[jax.ShapeDtypeStruct(c.shape, F32),
                   jax.ShapeDtypeStruct((nb, 1, D_MLSTM), F32),
                   jax.ShapeDtypeStruct((nb, 1, N_GATE_PAD), F32),
                   jax.ShapeDtypeStruct((D_MLSTM, nb), F32)],
        compiler_params=_params("arbitrary"),
        name="smix_state",
    )(c, q3, k3, n3, zg3, gate_bias, m3, vt)


def _smix_out_kernel(ht_ref, og_ref, gout_ref, y_ref):
    hm = ht_ref[...].T
    for h in range(MLSTM_HEADS):
        sl = slice(h * MLSTM_HD, (h + 1) * MLSTM_HD)
        y = _rms(hm[:, sl], gout_ref[...]) * jax.nn.sigmoid(og_ref[:, sl])
        y_ref[:, sl] = y.astype(y_ref.dtype)


def _smix_out(ht, og, gout):
    nb = ht.shape[1]
    return pl.pallas_call(
        _smix_out_kernel,
        out_shape=jax.ShapeDtypeStruct((nb, D_MLSTM), BF16),
        name="smix_out",
    )(ht, og, gout)


def _prompt_block(x, mem, w, g_final, final_norm):
    b, t, d = x.shape
    m = b * t
    tm = 1024
    x2 = x.reshape(m, d)
    z, zg = _norm_matmul(x2, w['g_mix'], w['w_in_main'], tm=tm, tn=512, out_dtype=F32, w_extra=w['w_in_gate'])
    z3 = z.reshape(b, t, D_MAIN)
    zg3 = zg.reshape(b, t, N_GATE_PAD)
    h0 = jnp.zeros((b, 1, D_RNN), F32)
    conv0 = jnp.zeros((b, CONV_W - 1, D_RNN), F32)
    y, rg_h, rg_conv = _rglru_prompt(z3, h0, conv0, w, tc=256)
    c0 = jnp.zeros((b, MLSTM_HEADS, MLSTM_HD, MLSTM_HD), F32)
    n0 = jnp.zeros((b, MLSTM_HEADS, 1, MLSTM_HD), F32)
    m0 = jnp.zeros((b, MLSTM_HEADS, 1, N_GATE_PAD), F32)
    mconv0 = jnp.zeros((b, CONV_W - 1, D_MLSTM), F32)
    y, c_new, n_new, m_new, ml_conv = _mlstm_prompt(z3, zg3, y, c0, n0, m0, mconv0, w)
    x1 = _matmul_res(y.reshape(m, d), w['w_out'], x2, tm=tm, tn=1024)

    mem2 = mem.reshape(b * N_MEM, d)
    mk = _norm_matmul(mem2, w['g_mem'], w['w_mk'], tm=b * N_MEM, tn=512, out_dtype=F32)
    mv = _norm_matmul(mem2, w['g_mem'], w['w_mv'], tm=b * N_MEM, tn=512, out_dtype=F32)
    q = _norm_matmul(x1, w['g_xattn'], w['w_cq'], tm=tm, tn=1024, out_dtype=BF16)
    o = _xattn_prompt(q.reshape(b, t, d), mk.reshape(b, N_MEM, d), mv.reshape(b, N_MEM, d), tq=512)
    x2_ = _matmul_res(o.reshape(m, d), w['w_co'], x1, tm=tm, tn=1024)
    xo = _ffn(x2_, w['g_ffn'], w['w_up'], w['w_down'], g_final, tm=tm, tf=512, final_norm=final_norm)
    new = (rg_h.reshape(b, D_RNN), rg_conv, c_new, n_new.reshape(b, MLSTM_HEADS, MLSTM_HD),
           m_new[:, :, 0, 0], ml_conv,
           mk.reshape(b, N_MEM, X_HEADS, X_HD), mv.reshape(b, N_MEM, X_HEADS, X_HD))
    return xo.reshape(b, t, d), new


def _sample_block(x, rg_h, rg_conv, c, n, mst, ml_conv, ck, cv, w, g_final, final_norm):
    nb, t, d = x.shape
    x2 = x.reshape(nb, d)
    z, zg = _norm_matmul(x2, w['g_mix'], w['w_in_main'], tm=nb, tn=512, out_dtype=F32, w_extra=w['w_in_gate'])
    y_rnn, h_new, rconv_new, mconv_new, q, k, vt = _smix_rows(
        z, rg_h, rg_conv.reshape(nb, 3 * D_RNN), ml_conv.reshape(nb, 3 * D_MLSTM), w)
    c_new, n_new, m_new, ht = _smix_state(
        c, q.reshape(nb, 1, D_MLSTM), k.reshape(nb, 1, D_MLSTM), n.reshape(nb, 1, D_MLSTM),
        zg.reshape(nb, 1, N_GATE_PAD), w['gate_bias'], mst.reshape(nb, 1, MLSTM_HEADS), vt)
    y_ml = _smix_out(ht, z[:, 4 * D_RNN:5 * D_RNN], w['g_ml_out'])
    y = jnp.concatenate([y_rnn, y_ml], axis=-1)
    x1 = _matmul_res(y, w['w_out'], x2, tm=nb, tn=1024)
    q = _norm_matmul(x1, w['g_xattn'], w['w_cq'], tm=nb, tn=1024, out_dtype=F32)
    o = _xattn_sample(q.reshape(nb, 1, d), ck.reshape(nb, N_MEM, d), cv.reshape(nb, N_MEM, d))
    x2_ = _matmul_res(o.reshape(nb, d), w['w_co'], x1, tm=nb, tn=1024)
    xo = _ffn(x2_, w['g_ffn'], w['w_up'], w['w_down'], g_final, tm=nb, tf=512, final_norm=final_norm)
    new = (h_new, rconv_new.reshape(nb, CONV_W - 1, D_RNN), c_new, n_new.reshape(nb, MLSTM_HEADS, MLSTM_HD),
           m_new[:, 0, :MLSTM_HEADS], mconv_new.reshape(nb, CONV_W - 1, D_MLSTM))
    return xo.reshape(nb, t, d), new


def _layer_weights(l, g_mix, w_in, conv_rnn_w, conv_rnn_b, lru_wa, lru_ba, lru_wx, lru_bx, lru_lambda,
                   g_rnn_out, conv_ml_w, conv_ml_b, ml_wq, ml_wk, ml_bi, ml_bf, g_ml_out, w_out,
                   g_xattn, g_mem, w_cq, w_mk, w_mv, w_co, g_ffn, w_up, w_down):
    n_gate = 2 * MLSTM_HEADS
    w_gate = jnp.pad(w_in[l][:, D_MAIN:], ((0, 0), (0, N_GATE_PAD - n_gate)))
    gate_bias = jnp.pad(jnp.concatenate([ml_bi[l], ml_bf[l]]), (0, N_GATE_PAD - n_gate))
    return dict(
        g_mix=g_mix[l], w_in_main=w_in[l][:, :D_MAIN].astype(BF16), w_in_gate=w_gate.astype(BF16),
        conv_rnn_w=conv_rnn_w[l], conv_rnn_b=conv_rnn_b[l].reshape(1, D_RNN),
        lru_wg=jnp.concatenate([lru_wa[l], lru_wx[l]], axis=-1).astype(BF16),
        lru_bg=jnp.concatenate([lru_ba[l], lru_bx[l]], axis=-1).reshape(RNN_HEADS, 1, 2 * RNN_HD),
        lru_lambda=lru_lambda[l].reshape(1, D_RNN), g_rnn_out=g_rnn_out[l].reshape(1, D_RNN),
        conv_ml_w=conv_ml_w[l], conv_ml_b=conv_ml_b[l].reshape(1, D_MLSTM),
        ml_wqk=jnp.concatenate([ml_wq[l], ml_wk[l]], axis=-1).astype(BF16),
        gate_bias=gate_bias.reshape(1, N_GATE_PAD), g_ml_out=g_ml_out[l].reshape(1, MLSTM_HD),
        w_out=w_out[l].astype(BF16), g_xattn=g_xattn[l], g_mem=g_mem[l],
        w_cq=w_cq[l].astype(BF16), w_mk=w_mk[l].astype(BF16), w_mv=w_mv[l].astype(BF16),
        w_co=w_co[l].astype(BF16), g_ffn=g_ffn[l], w_up=w_up[l].astype(BF16), w_down=w_down[l].astype(BF16))


def kernel(x_prompt, x_sample, mem_prompt, state_rglru_h, state_rglru_conv, state_mlstm_C, state_mlstm_n, state_mlstm_m, state_mlstm_conv, cache_mem_k, cache_mem_v, g_mix, w_in, conv_rnn_w, conv_rnn_b, lru_wa, lru_ba, lru_wx, lru_bx, lru_lambda, g_rnn_out, conv_ml_w, conv_ml_b, ml_wq, ml_wk, ml_bi, ml_bf, g_ml_out, w_out, g_xattn, g_mem, w_cq, w_mk, w_mv, w_co, g_ffn, w_up, w_down, g_final):
    depth = w_in.shape[0]
    xp, xs = x_prompt, x_sample
    p_out = [[] for _ in range(8)]
    s_out = [[] for _ in range(6)]
    for l in range(depth):
        w = _layer_weights(l, g_mix, w_in, conv_rnn_w, conv_rnn_b, lru_wa, lru_ba, lru_wx, lru_bx, lru_lambda,
                           g_rnn_out, conv_ml_w, conv_ml_b, ml_wq, ml_wk, ml_bi, ml_bf, g_ml_out, w_out,
                           g_xattn, g_mem, w_cq, w_mk, w_mv, w_co, g_ffn, w_up, w_down)
        last = l == depth - 1
        xp, new_p = _prompt_block(xp, mem_prompt, w, g_final, last)
        for j, a in enumerate(new_p):
            p_out[j].append(a)
        xs, new_s = _sample_block(xs, state_rglru_h[l], state_rglru_conv[l], state_mlstm_C[l], state_mlstm_n[l],
                                  state_mlstm_m[l], state_mlstm_conv[l], cache_mem_k[l], cache_mem_v[l],
                                  w, g_final, last)
        for j, a in enumerate(new_s):
            s_out[j].append(a)
    P = [jnp.stack(a, axis=0) for a in p_out]
    S = [jnp.stack(a, axis=0) for a in s_out]
    return (xp, xs, P[0], P[1], P[2], P[3], P[4], P[5], P[6], P[7],
            S[0], S[1], S[2], S[3], S[4], S[5])
```

```python
import functools

import jax
import jax.numpy as jnp
from jax import lax
from jax.experimental import pallas as pl
from jax.experimental.pallas import tpu as pltpu

F32 = jnp.float32
BF16 = jnp.bfloat16

D_MODEL = 2048
D_RNN = 1024
RNN_HEADS = 8
RNN_HD = 128
CONV_W = 4
LRU_C = 8.0
D_MLSTM = 1024
MLSTM_HEADS = 4
MLSTM_HD = 256
CHUNK = 128
N_MEM = 256
X_HEADS = 4
X_HD = 512
D_FF = 8192
EPS = 1e-6
D_MAIN = 5 * 1024
N_GATE_PAD = 128

V7X_VMEM_LIMIT_BYTES = 56 * 1024 * 1024


def _params(*sem):
    return pltpu.CompilerParams(dimension_semantics=sem, vmem_limit_bytes=V7X_VMEM_LIMIT_BYTES)


def _rms(x, g):
    ms = jnp.mean(x * x, axis=-1, keepdims=True)
    return x * lax.rsqrt(ms + EPS) * g


def _dot(a, b):
    return jnp.dot(a, b, preferred_element_type=F32)


def _norm_matmul_kernel(*refs, has_extra):
    if has_extra:
        x_ref, g_ref, w_ref, we_ref, o_ref, oe_ref, xn_ref = refs
    else:
        x_ref, g_ref, w_ref, o_ref, xn_ref = refs

    @pl.when(pl.program_id(1) == 0)
    def _():
        xn = _rms(x_ref[...], g_ref[...]).astype(BF16)
        xn_ref[...] = xn
        if has_extra:
            oe_ref[...] = _dot(xn, we_ref[...])

    o_ref[...] = _dot(xn_ref[...], w_ref[...]).astype(o_ref.dtype)


def _norm_matmul(x, g, w, *, tm, tn, out_dtype, w_extra=None, n_cols=None):
    m, k = x.shape
    n = n_cols or w.shape[1]
    has_extra = w_extra is not None
    in_specs = [pl.BlockSpec((tm, k), lambda i, j: (i, 0)),
                pl.BlockSpec((1, k), lambda i, j: (0, 0)),
                pl.BlockSpec((k, tn), lambda i, j: (0, j))]
    out_specs = [pl.BlockSpec((tm, tn), lambda i, j: (i, j))]
    out_shape = [jax.ShapeDtypeStruct((m, n), out_dtype)]
    args = [x, g.reshape(1, k), w]
    if has_extra:
        ne = w_extra.shape[1]
        in_specs.append(pl.BlockSpec((k, ne), lambda i, j: (0, 0)))
        out_specs.append(pl.BlockSpec((tm, ne), lambda i, j: (i, 0)))
        out_shape.append(jax.ShapeDtypeStruct((m, ne), F32))
        args.append(w_extra)
    res = pl.pallas_call(
        functools.partial(_norm_matmul_kernel, has_extra=has_extra),
        grid=(m // tm, n // tn),
        in_specs=in_specs, out_specs=out_specs, out_shape=out_shape,
        scratch_shapes=[pltpu.VMEM((tm, k), BF16)],
        compiler_params=_params("parallel", "arbitrary"),
        name="norm_matmul",
    )(*args)
    return res if has_extra else res[0]


def _matmul_res_kernel(a_ref, w_ref, r_ref, o_ref):
    o_ref[...] = r_ref[...] + _dot(a_ref[...], w_ref[...])


def _matmul_res(a, w, res, *, tm, tn):
    m, k = a.shape
    n = w.shape[1]
    return pl.pallas_call(
        _matmul_res_kernel,
        grid=(m // tm, n // tn),
        in_specs=[pl.BlockSpec((tm, k), lambda i, j: (i, 0)),
                  pl.BlockSpec((k, tn), lambda i, j: (0, j)),
                  pl.BlockSpec((tm, tn), lambda i, j: (i, j))],
        out_specs=pl.BlockSpec((tm, tn), lambda i, j: (i, j)),
        out_shape=jax.ShapeDtypeStruct((m, n), F32),
        compiler_params=_params("parallel", "arbitrary"),
        name="matmul_res",
    )(a, w, res)


def _ffn_kernel(x_ref, g_ref, wu_ref, wd_ref, gf_ref, o_ref, xn_ref, *, final_norm):
    f = pl.program_id(1)

    @pl.when(f == 0)
    def _():
        xn_ref[...] = _rms(x_ref[...], g_ref[...]).astype(BF16)
        o_ref[...] = jnp.zeros_like(o_ref)

    h = _dot(xn_ref[...], wu_ref[...])
    h = jnp.square(jnp.maximum(h, 0.0)).astype(BF16)
    o_ref[...] += _dot(h, wd_ref[...])

    @pl.when(f == pl.num_programs(1) - 1)
    def _():
        y = x_ref[...] + o_ref[...]
        if final_norm:
            y = _rms(y, gf_ref[...])
        o_ref[...] = y


def _ffn(x, g, w_up, w_down, g_final, *, tm, tf, final_norm):
    m, d = x.shape
    dff = w_up.shape[1]
    return pl.pallas_call(
        functools.partial(_ffn_kernel, final_norm=final_norm),
        grid=(m // tm, dff // tf),
        in_specs=[pl.BlockSpec((tm, d), lambda i, f: (i, 0)),
                  pl.BlockSpec((1, d), lambda i, f: (0, 0)),
                  pl.BlockSpec((d, tf), lambda i, f: (0, f)),
                  pl.BlockSpec((tf, d), lambda i, f: (f, 0)),
                  pl.BlockSpec((1, d), lambda i, f: (0, 0))],
        out_specs=pl.BlockSpec((tm, d), lambda i, f: (i, 0)),
        out_shape=jax.ShapeDtypeStruct((m, d), F32),
        scratch_shapes=[pltpu.VMEM((tm, d), BF16)],
        compiler_params=_params("parallel", "arbitrary"),
        name="ffn",
    )(x, g.reshape(1, d), w_up, w_down, g_final.reshape(1, d))


def _lru_gates(xc, wg, bg, lam):
    g = _dot(xc.astype(BF16), wg) + bg
    r = jax.nn.sigmoid(g[:, :RNN_HD])
    i = jax.nn.sigmoid(g[:, RNN_HD:])
    log_a = -LRU_C * r * jax.nn.softplus(-lam)
    a = jnp.exp(log_a)
    u = jnp.sqrt(-jnp.tanh(log_a) * (a * a + 1.0)) * (i * xc)
    return a, u


def _rglru_kernel(xr_ref, gr_ref, h0_ref, c0_ref, cw_ref, cb_ref, wg_ref, bg_ref, lam_ref, gout_ref,
                  y_ref, hn_ref, cn_ref, xbuf, hc, ybuf, *, tc):
    c = pl.program_id(1)

    @pl.when(c == 0)
    def _():
        xbuf[0:8, :] = jnp.zeros((8, D_RNN), F32)
        xbuf[5:8, :] = c0_ref[0]
        hc[...] = h0_ref[0]

    @pl.when(c > 0)
    def _():
        xbuf[0:8, :] = xbuf[tc:tc + 8, :]

    xbuf[8:8 + tc, :] = xr_ref[0]
    row = lax.broadcasted_iota(jnp.int32, (tc, RNN_HD), 0)
    ssq = jnp.zeros((tc, 1), F32)
    for h in range(RNN_HEADS):
        sl = slice(h * RNN_HD, (h + 1) * RNN_HD)
        xc = cb_ref[:, sl]
        for j in range(CONV_W):
            xc = xc + cw_ref[j:j + 1, sl] * xbuf[5 + j:5 + j + tc, sl]
        a, u = _lru_gates(xc, wg_ref[h], bg_ref[h], lam_ref[:, sl])
        d = 1
        while d < tc:
            keep = row >= d
            a_prev = jnp.where(keep, pltpu.roll(a, d, 0), 1.0)
            u_prev = jnp.where(keep, pltpu.roll(u, d, 0), 0.0)
            u = u + a * u_prev
            a = a * a_prev
            d *= 2
        hs = u + a * hc[:, sl]
        hc[:, sl] = hs[tc - 1:tc, :]
        yv = hs * jax.nn.gelu(gr_ref[0, :, sl])
        ssq = ssq + jnp.sum(yv * yv, axis=-1, keepdims=True)
        ybuf[:, sl] = yv
    y = ybuf[...] * lax.rsqrt(ssq * (1.0 / D_RNN) + EPS) * gout_ref[...]
    y_ref[0] = y.astype(y_ref.dtype)

    @pl.when(c == pl.num_programs(1) - 1)
    def _():
        hn_ref[0] = hc[...]
        cn_ref[0] = xbuf[tc + 5:tc + 8, :]


def _rglru_prompt(z3, h0, conv0, w, *, tc):
    b, t, _ = z3.shape
    full2 = lambda bi, ci: (0, 0)
    return pl.pallas_call(
        functools.partial(_rglru_kernel, tc=tc),
        grid=(b, t // tc),
        in_specs=[pl.BlockSpec((1, tc, D_RNN), lambda bi, ci: (bi, ci, 0)),
                  pl.BlockSpec((1, tc, D_RNN), lambda bi, ci: (bi, ci, 1)),
                  pl.BlockSpec((1, 1, D_RNN), lambda bi, ci: (bi, 0, 0)),
                  pl.BlockSpec((1, CONV_W - 1, D_RNN), lambda bi, ci: (bi, 0, 0)),
                  pl.BlockSpec((CONV_W, D_RNN), full2),
                  pl.BlockSpec((1, D_RNN), full2),
                  pl.BlockSpec((RNN_HEADS, RNN_HD, 2 * RNN_HD), lambda bi, ci: (0, 0, 0)),
                  pl.BlockSpec((RNN_HEADS, 1, 2 * RNN_HD), lambda bi, ci: (0, 0, 0)),
                  pl.BlockSpec((1, D_RNN), full2),
                  pl.BlockSpec((1, D_RNN), full2)],
        out_specs=[pl.BlockSpec((1, tc, D_RNN), lambda bi, ci: (bi, ci, 0)),
                   pl.BlockSpec((1, 1, D_RNN), lambda bi, ci: (bi, 0, 0)),
                   pl.BlockSpec((1, CONV_W - 1, D_RNN), lambda bi, ci: (bi, 0, 0))],
        out_shape=[jax.ShapeDtypeStruct((b, t, D_MODEL), BF16),
                   jax.ShapeDtypeStruct((b, 1, D_RNN), F32),
                   jax.ShapeDtypeStruct((b, CONV_W - 1, D_RNN), F32)],
        scratch_shapes=[pltpu.VMEM((tc + 8, D_RNN), F32),
                        pltpu.VMEM((1, D_RNN), F32),
                        pltpu.VMEM((tc, D_RNN), F32)],
        compiler_params=_params("parallel", "arbitrary"),
        name="rglru_prompt",
    )(z3, z3, h0, conv0, w['conv_rnn_w'], w['conv_rnn_b'], w['lru_wg'], w['lru_bg'],
      w['lru_lambda'], w['g_rnn_out'])


def _mlstm_kernel(u_ref, v_ref, og_ref, zg_ref, gb_ref, c0_ref, n0_ref, m0_ref, cv0_ref,
                  cw_ref, cb_ref, wqk_ref, gout_ref, yin_ref,
                  y_ref, cn_ref, nn_ref, mn_ref, cvn_ref, ubuf, cst, nst, mst):
    del yin_ref
    hd = pl.program_id(1)
    c = pl.program_id(2)
    L = CHUNK

    @pl.when(c == 0)
    def _():
        ubuf[0:8, :] = jnp.zeros((8, MLSTM_HD), F32)
        ubuf[5:8, :] = cv0_ref[0]
        cst[...] = c0_ref[0, 0]
        nst[...] = n0_ref[0, 0]
        mst[...] = m0_ref[0, 0]

    @pl.when(c > 0)
    def _():
        ubuf[0:8, :] = ubuf[L:L + 8, :]

    ubuf[8:8 + L, :] = u_ref[0]
    uc = cb_ref[...]
    for j in range(CONV_W):
        uc = uc + cw_ref[j:j + 1, :] * ubuf[5 + j:5 + j + L, :]
    uc = jax.nn.silu(uc)
    qk = _dot(uc.astype(BF16), wqk_ref[0])
    q = qk[:, :MLSTM_HD]
    k = qk[:, MLSTM_HD:] * (MLSTM_HD ** -0.5)
    v = v_ref[0]
    qb, kb = q.astype(BF16), k.astype(BF16)

    zg = zg_ref[0] + gb_ref[...]
    zgt = zg.T
    lane = lax.broadcasted_iota(jnp.int32, (L, N_GATE_PAD), 1)
    sub = lax.broadcasted_iota(jnp.int32, (N_GATE_PAD, L), 0)
    icol = jnp.sum(jnp.where(lane == hd, zg, 0.0), axis=1, keepdims=True)
    fcol = jax.nn.log_sigmoid(jnp.sum(jnp.where(lane == hd + MLSTM_HEADS, zg, 0.0), axis=1, keepdims=True))
    irow = jnp.sum(jnp.where(sub == hd, zgt, 0.0), axis=0, keepdims=True)
    frow = jax.nn.log_sigmoid(jnp.sum(jnp.where(sub == hd + MLSTM_HEADS, zgt, 0.0), axis=0, keepdims=True))

    ti = lax.broadcasted_iota(jnp.int32, (L, L), 0)
    si = lax.broadcasted_iota(jnp.int32, (L, L), 1)
    causal = si <= ti
    bcol = jnp.sum(jnp.where(causal, frow, 0.0), axis=1, keepdims=True)
    brow = jnp.sum(jnp.where(ti <= si, fcol, 0.0), axis=0, keepdims=True)
    dmat = jnp.where(causal, irow + bcol - brow, -jnp.inf)
    m_prev = mst[:, 0:1]
    inter = bcol + m_prev
    m_t = jnp.maximum(inter, jnp.max(dmat, axis=1, keepdims=True))
    w_inter = jnp.exp(inter - m_t)
    s = lax.dot_general(qb, kb, (((1,), (1,)), ((), ())), preferred_element_type=F32) * jnp.exp(dmat - m_t)
    cmat = cst[...]
    nrow = nst[...]
    cq = lax.dot_general(qb, cmat.astype(BF16), (((1,), (1,)), ((), ())), preferred_element_type=F32)
    num = w_inter * cq + _dot(s.astype(BF16), v.astype(BF16))
    den = w_inter * jnp.sum(q * nrow, axis=1, keepdims=True) + jnp.sum(s, axis=1, keepdims=True)
    hm = num / jnp.maximum(jnp.abs(den), jnp.exp(-m_t))

    m_new = m_t[L - 1:L, :]
    b_last = bcol[L - 1:L, :]
    g_state = jnp.exp(b_last + m_prev - m_new)
    g_in = jnp.exp(icol + b_last - bcol - m_new)
    cst[...] = g_state * cmat + lax.dot_general((g_in * v).astype(BF16), kb, (((0,), (0,)), ((), ())),
                                                preferred_element_type=F32)
    nst[...] = g_state * nrow + jnp.sum(g_in * k, axis=0, keepdims=True)
    mst[...] = jnp.broadcast_to(m_new, mst.shape)

    y = _rms(hm, gout_ref[...]) * jax.nn.sigmoid(og_ref[0])
    y_ref[0] = y.astype(y_ref.dtype)

    @pl.when(c == pl.num_programs(2) - 1)
    def _():
        cn_ref[0, 0] = cst[...]
        nn_ref[0, 0] = nst[...]
        mn_ref[0, 0] = mst[...]
        cvn_ref[0] = ubuf[L + 5:L + 8, :]


def _mlstm_prompt(z3, zg3, y_in, c0, n0, m0, conv0, w):
    b, t, _ = z3.shape
    nc = t // CHUNK
    hdb = D_MAIN // MLSTM_HD
    u_blk, v_blk, o_blk = 2 * D_RNN // MLSTM_HD, 3 * D_RNN // MLSTM_HD, 4 * D_RNN // MLSTM_HD
    del hdb
    n_in = 14
    return pl.pallas_call(
        _mlstm_kernel,
        grid=(b, MLSTM_HEADS, nc),
        in_specs=[pl.BlockSpec((1, CHUNK, MLSTM_HD), lambda bi, hi, ci: (bi, ci, u_blk + hi)),
                  pl.BlockSpec((1, CHUNK, MLSTM_HD), lambda bi, hi, ci: (bi, ci, v_blk + hi)),
                  pl.BlockSpec((1, CHUNK, MLSTM_HD), lambda bi, hi, ci: (bi, ci, o_blk + hi)),
                  pl.BlockSpec((1, CHUNK, N_GATE_PAD), lambda bi, hi, ci: (bi, ci, 0)),
                  pl.BlockSpec((1, N_GATE_PAD), lambda bi, hi, ci: (0, 0)),
                  pl.BlockSpec((1, 1, MLSTM_HD, MLSTM_HD), lambda bi, hi, ci: (bi, hi, 0, 0)),
                  pl.BlockSpec((1, 1, 1, MLSTM_HD), lambda bi, hi, ci: (bi, hi, 0, 0)),
                  pl.BlockSpec((1, 1, 1, N_GATE_PAD), lambda bi, hi, ci: (bi, hi, 0, 0)),
                  pl.BlockSpec((1, CONV_W - 1, MLSTM_HD), lambda bi, hi, ci: (bi, 0, hi)),
                  pl.BlockSpec((CONV_W, MLSTM_HD), lambda bi, hi, ci: (0, hi)),
                  pl.BlockSpec((1, MLSTM_HD), lambda bi, hi, ci: (0, hi)),
                  pl.BlockSpec((1, MLSTM_HD, 2 * MLSTM_HD), lambda bi, hi, ci: (hi, 0, 0)),
                  pl.BlockSpec((1, MLSTM_HD), lambda bi, hi, ci: (0, 0)),
                  pl.BlockSpec(memory_space=pl.ANY)],
        out_specs=[pl.BlockSpec((1, CHUNK, MLSTM_HD), lambda bi, hi, ci: (bi, ci, D_RNN // MLSTM_HD + hi)),
                   pl.BlockSpec((1, 1, MLSTM_HD, MLSTM_HD), lambda bi, hi, ci: (bi, hi, 0, 0)),
                   pl.BlockSpec((1, 1, 1, MLSTM_HD), lambda bi, hi, ci: (bi, hi, 0, 0)),
                   pl.BlockSpec((1, 1, 1, N_GATE_PAD), lambda bi, hi, ci: (bi, hi, 0, 0)),
                   pl.BlockSpec((1, CONV_W - 1, MLSTM_HD), lambda bi, hi, ci: (bi, 0, hi))],
        out_shape=[jax.ShapeDtypeStruct((b, t, D_MODEL), BF16),
                   jax.ShapeDtypeStruct((b, MLSTM_HEADS, MLSTM_HD, MLSTM_HD), F32),
                   jax.ShapeDtypeStruct((b, MLSTM_HEADS, 1, MLSTM_HD), F32),
                   jax.ShapeDtypeStruct((b, MLSTM_HEADS, 1, N_GATE_PAD), F32),
                   jax.ShapeDtypeStruct((b, CONV_W - 1, D_MLSTM), F32)],
        scratch_shapes=[pltpu.VMEM((CHUNK + 8, MLSTM_HD), F32),
                        pltpu.VMEM((MLSTM_HD, MLSTM_HD), F32),
                        pltpu.VMEM((1, MLSTM_HD), F32),
                        pltpu.VMEM((1, N_GATE_PAD), F32)],
        input_output_aliases={n_in - 1: 0},
        compiler_params=_params("parallel", "parallel", "arbitrary"),
        name="mlstm_prompt",
    )(z3, z3, z3, zg3, w['gate_bias'], c0, n0, m0, conv0, w['conv_ml_w'], w['conv_ml_b'],
      w['ml_wqk'], w['g_ml_out'], y_in)


def _xattn_kernel(q_ref, k_ref, v_ref, o_ref):
    for h in range(X_HEADS):
        sl = slice(h * X_HD, (h + 1) * X_HD)
        s = lax.dot_general(q_ref[0, :, sl], k_ref[0, :, sl].astype(BF16), (((1,), (1,)), ((), ())),
                            preferred_element_type=F32) * (X_HD ** -0.5)
        e = jnp.exp(s - jnp.max(s, axis=-1, keepdims=True))
        p = e / jnp.sum(e, axis=-1, keepdims=True)
        o_ref[0, :, sl] = _dot(p.astype(BF16), v_ref[0, :, sl].astype(BF16)).astype(o_ref.dtype)


def _xattn_prompt(q3, mk3, mv3, *, tq):
    b, t, d = q3.shape
    return pl.pallas_call(
        _xattn_kernel,
        grid=(b, t // tq),
        in_specs=[pl.BlockSpec((1, tq, d), lambda bi, ti: (bi, ti, 0)),
                  pl.BlockSpec((1, N_MEM, d), lambda bi, ti: (bi, 0, 0)),
                  pl.BlockSpec((1, N_MEM, d), lambda bi, ti: (bi, 0, 0))],
        out_specs=pl.BlockSpec((1, tq, d), lambda bi, ti: (bi, ti, 0)),
        out_shape=jax.ShapeDtypeStruct((b, t, d), BF16),
        compiler_params=_params("parallel", "arbitrary"),
        name="xattn_prompt",
    )(q3, mk3, mv3)


def _xattn_sample_kernel(q_ref, k_ref, v_ref, o_ref):
    s = jnp.sum(k_ref[0] * q_ref[...], axis=-1, keepdims=True) * (X_HD ** -0.5)
    e = jnp.exp(s - jnp.max(s, axis=0, keepdims=True))
    p = e / jnp.sum(e, axis=0, keepdims=True)
    o_ref[0] = jnp.sum(p * v_ref[0], axis=0)


def _xattn_sample(q3, k4, v4):
    b = q3.shape[0]
    return pl.pallas_call(
        _xattn_sample_kernel,
        grid=(b,),
        in_specs=[pl.BlockSpec((1, X_HEADS, X_HD), lambda bi: (bi, 0, 0)),
                  pl.BlockSpec((1, N_MEM, X_HEADS, X_HD), lambda bi: (bi, 0, 0, 0)),
                  pl.BlockSpec((1, N_MEM, X_HEADS, X_HD), lambda bi: (bi, 0, 0, 0))],
        out_specs=pl.BlockSpec((1, X_HEADS, X_HD), lambda bi: (bi, 0, 0)),
        out_shape=jax.ShapeDtypeStruct((b, X_HEADS, X_HD), F32),
        compiler_params=_params("parallel"),
        name="xattn_sample",
    )(q3, k4, v4)


def _smix_rows_kernel(z_ref, h0_ref, rc_ref, mc_ref, cwr_ref, cbr_ref, wg_ref, bg_ref, lam_ref, gout_ref,
                      cwm_ref, cbm_ref, wqk_ref,
                      yr_ref, hn_ref, rcn_ref, mcn_ref, q_ref, k_ref, ybuf):
    nb = z_ref.shape[0]
    ssq = jnp.zeros((nb, 1), F32)
    for h in range(RNN_HEADS):
        sl = slice(h * RNN_HD, (h + 1) * RNN_HD)
        xc = cbr_ref[:, sl] + cwr_ref[CONV_W - 1:CONV_W, sl] * z_ref[:, sl]
        for j in range(CONV_W - 1):
            xc = xc + cwr_ref[j:j + 1, sl] * rc_ref[:, j * D_RNN + h * RNN_HD:j * D_RNN + (h + 1) * RNN_HD]
        a, u = _lru_gates(xc, wg_ref[h], bg_ref[h], lam_ref[:, sl])
        hs = a * h0_ref[:, sl] + u
        hn_ref[:, sl] = hs
        yv = hs * jax.nn.gelu(z_ref[:, D_RNN + h * RNN_HD:D_RNN + (h + 1) * RNN_HD])
        ssq = ssq + jnp.sum(yv * yv, axis=-1, keepdims=True)
        ybuf[:, sl] = yv
    yr_ref[...] = (ybuf[...] * lax.rsqrt(ssq * (1.0 / D_RNN) + EPS) * gout_ref[...]).astype(yr_ref.dtype)
    rcn_ref[:, 0:2 * D_RNN] = rc_ref[:, D_RNN:3 * D_RNN]
    rcn_ref[:, 2 * D_RNN:3 * D_RNN] = z_ref[:, 0:D_RNN]

    for h in range(MLSTM_HEADS):
        sl = slice(h * MLSTM_HD, (h + 1) * MLSTM_HD)
        uc = cbm_ref[:, sl] + cwm_ref[CONV_W - 1:CONV_W, sl] * z_ref[:, 2 * D_RNN + h * MLSTM_HD:2 * D_RNN + (h + 1) * MLSTM_HD]
        for j in range(CONV_W - 1):
            uc = uc + cwm_ref[j:j + 1, sl] * mc_ref[:, j * D_MLSTM + h * MLSTM_HD:j * D_MLSTM + (h + 1) * MLSTM_HD]
        uc = jax.nn.silu(uc)
        qk = _dot(uc.astype(BF16), wqk_ref[h])
        q_ref[:, sl] = qk[:, :MLSTM_HD]
        k_ref[:, sl] = qk[:, MLSTM_HD:] * (MLSTM_HD ** -0.5)
    mcn_ref[:, 0:2 * D_MLSTM] = mc_ref[:, D_MLSTM:3 * D_MLSTM]
    mcn_ref[:, 2 * D_MLSTM:3 * D_MLSTM] = z_ref[:, 2 * D_RNN:2 * D_RNN + D_MLSTM]


def _smix_rows(z, h0, rconv, mconv, w):
    nb = z.shape[0]
    outs = [jax.ShapeDtypeStruct((nb, D_RNN), BF16),
            jax.ShapeDtypeStruct((nb, D_RNN), F32),
            jax.ShapeDtypeStruct((nb, 3 * D_RNN), F32),
            jax.ShapeDtypeStruct((nb, 3 * D_MLSTM), F32),
            jax.ShapeDtypeStruct((nb, D_MLSTM), F32),
            jax.ShapeDtypeStruct((nb, D_MLSTM), F32)]
    return pl.pallas_call(
        _smix_rows_kernel,
        out_shape=outs,
        scratch_shapes=[pltpu.VMEM((nb, D_RNN), F32)],
        compiler_params=pltpu.CompilerParams(vmem_limit_bytes=V7X_VMEM_LIMIT_BYTES),
        name="smix_rows",
    )(z, h0, rconv, mconv, w['conv_rnn_w'], w['conv_rnn_b'], w['lru_wg'], w['lru_bg'], w['lru_lambda'],
      w['g_rnn_out'], w['conv_ml_w'], w['conv_ml_b'], w['ml_wqk'])


SMIX_ROWS = 4
MXU_MIN_ROWS = 8


def _smix_state_kernel(c_ref, q_ref, k_ref, n_ref, v_ref, og_ref, zg_ref, gb_ref, m_ref, gout_ref,
                       cn_ref, nn_ref, mn_ref, y_ref):
    lane_g = lax.broadcasted_iota(jnp.int32, (1, N_GATE_PAD), 1)
    first_row = lax.broadcasted_iota(jnp.int32, (MXU_MIN_ROWS, MLSTM_HD), 0) == 0
    for r in range(SMIX_ROWS):
        zg = zg_ref[r] + gb_ref[...]
        m_out = jnp.zeros((1, N_GATE_PAD), F32)
        for h in range(MLSTM_HEADS):
            sl = slice(h * MLSTM_HD, (h + 1) * MLSTM_HD)
            q = q_ref[r, :, sl]
            k = k_ref[r, :, sl]
            n = n_ref[r, :, sl]
            v = v_ref[r, :, sl]
            ig = zg[:, h:h + 1]
            lf = jax.nn.log_sigmoid(zg[:, MLSTM_HEADS + h:MLSTM_HEADS + h + 1])
            m_prev = m_ref[r, :, h:h + 1]
            m_t = jnp.maximum(lf + m_prev, ig)
            w_inter = jnp.exp(lf + m_prev - m_t)
            g_in = jnp.exp(ig - m_t)
            s = jnp.sum(q * k, axis=1, keepdims=True) * g_in
            den = w_inter * jnp.sum(n * q, axis=1, keepdims=True) + s
            denom = jnp.maximum(jnp.abs(den), jnp.exp(-m_t))
            cmat = c_ref[r, h]
            q8 = jnp.broadcast_to(q, (MXU_MIN_ROWS, MLSTM_HD)).astype(BF16)
            cq = lax.dot_general(q8, cmat.astype(BF16), (((1,), (1,)), ((), ())),
                                 preferred_element_type=F32)[0:1, :]
            hm = (w_inter * cq + s * v) / denom
            gv8 = jnp.where(first_row, jnp.broadcast_to(g_in * v, (MXU_MIN_ROWS, MLSTM_HD)), 0.0).astype(BF16)
            k8 = jnp.broadcast_to(k, (MXU_MIN_ROWS, MLSTM_HD)).astype(BF16)
            outer = lax.dot_general(gv8, k8, (((0,), (0,)), ((), ())), preferred_element_type=F32)
            cn_ref[r, h] = w_inter * cmat + outer
            nn_ref[r, :, sl] = w_inter * n + g_in * k
            m_out = jnp.where(lane_g == h, m_t, m_out)
            y = _rms(hm, gout_ref[...]) * jax.nn.sigmoid(og_ref[r, :, sl])
            y_ref[r, :, sl] = y.astype(y_ref.dtype)
        mn_ref[r] = m_out


def _smix_state(c, q3, k3, n3, z3, zg3, gate_bias, m3, gout):
    nb = c.shape[0]
    row = lambda width, col=0: pl.BlockSpec((SMIX_ROWS, 1, width), lambda bi: (bi, 0, col))
    cblk = pl.BlockSpec((SMIX_ROWS, MLSTM_HEADS, MLSTM_HD, MLSTM_HD), lambda bi: (bi, 0, 0, 0))
    return pl.pallas_call(
        _smix_state_kernel,
        grid=(nb // SMIX_ROWS,),
        in_specs=[cblk, row(D_MLSTM), row(D_MLSTM), row(D_MLSTM),
                  row(D_MLSTM, 3 * D_RNN // D_MLSTM), row(D_MLSTM, 4 * D_RNN // D_MLSTM),
                  row(N_GATE_PAD),
                  pl.BlockSpec((1, N_GATE_PAD), lambda bi: (0, 0)),
                  row(MLSTM_HEADS),
                  pl.BlockSpec((1, MLSTM_HD), lambda bi: (0, 0))],
        out_specs=[cblk, row(D_MLSTM), row(N_GATE_PAD), row(D_MLSTM)],
        out_shape=[jax.ShapeDtypeStruct(c.shape, F32),
                   jax.ShapeDtypeStruct((nb, 1, D_MLSTM), F32),
                   jax.ShapeDtypeStruct((nb, 1, N_GATE_PAD), F32),
                   jax.ShapeDtypeStruct((nb, 1, D_MLSTM), F32)],
        compiler_params=_params("parallel"),
        name="smix_state",
    )(c, q3, k3, n3, z3, z3, zg3, gate_bias, m3, gout)


def _prompt_block(x, mem, w, g_final, final_norm):
    b, t, d = x.shape
    m = b * t
    tm = 1024
    x2 = x.reshape(m, d)
    z, zg = _norm_matmul(x2, w['g_mix'], w['w_in_main'], tm=tm, tn=512, out_dtype=F32, w_extra=w['w_in_gate'],
                          n_cols=D_MAIN)
    z3 = z.reshape(b, t, D_MAIN)
    zg3 = zg.reshape(b, t, N_GATE_PAD)
    h0 = jnp.zeros((b, 1, D_RNN), F32)
    conv0 = jnp.zeros((b, CONV_W - 1, D_RNN), F32)
    y, rg_h, rg_conv = _rglru_prompt(z3, h0, conv0, w, tc=256)
    c0 = jnp.zeros((b, MLSTM_HEADS, MLSTM_HD, MLSTM_HD), F32)
    n0 = jnp.zeros((b, MLSTM_HEADS, 1, MLSTM_HD), F32)
    m0 = jnp.zeros((b, MLSTM_HEADS, 1, N_GATE_PAD), F32)
    mconv0 = jnp.zeros((b, CONV_W - 1, D_MLSTM), F32)
    y, c_new, n_new, m_new, ml_conv = _mlstm_prompt(z3, zg3, y, c0, n0, m0, mconv0, w)
    x1 = _matmul_res(y.reshape(m, d), w['w_out'], x2, tm=tm, tn=1024)

    mem2 = mem.reshape(b * N_MEM, d)
    mk = _norm_matmul(mem2, w['g_mem'], w['w_mk'], tm=b * N_MEM, tn=512, out_dtype=F32)
    mv = _norm_matmul(mem2, w['g_mem'], w['w_mv'], tm=b * N_MEM, tn=512, out_dtype=F32)
    q = _norm_matmul(x1, w['g_xattn'], w['w_cq'], tm=tm, tn=1024, out_dtype=BF16)
    o = _xattn_prompt(q.reshape(b, t, d), mk.reshape(b, N_MEM, d), mv.reshape(b, N_MEM, d), tq=512)
    x2_ = _matmul_res(o.reshape(m, d), w['w_co'], x1, tm=tm, tn=1024)
    xo = _ffn(x2_, w['g_ffn'], w['w_up'], w['w_down'], g_final, tm=tm, tf=512, final_norm=final_norm)
    new = (rg_h.reshape(b, D_RNN), rg_conv, c_new, n_new.reshape(b, MLSTM_HEADS, MLSTM_HD),
           m_new[:, :, 0, 0], ml_conv,
           mk.reshape(b, N_MEM, X_HEADS, X_HD), mv.reshape(b, N_MEM, X_HEADS, X_HD))
    return xo.reshape(b, t, d), new


def _sample_block(x, rg_h, rg_conv, c, n, mst, ml_conv, ck, cv, w, g_final, final_norm):
    nb, t, d = x.shape
    x2 = x.reshape(nb, d)
    z, zg = _norm_matmul(x2, w['g_mix'], w['w_in_main'], tm=nb, tn=512, out_dtype=F32, w_extra=w['w_in_gate'],
                          n_cols=D_MAIN)
    y_rnn, h_new, rconv_new, mconv_new, q, k = _smix_rows(
        z, rg_h, rg_conv.reshape(nb, 3 * D_RNN), ml_conv.reshape(nb, 3 * D_MLSTM), w)
    c_new, n_new, m_new, y_ml = _smix_state(
        c, q.reshape(nb, 1, D_MLSTM), k.reshape(nb, 1, D_MLSTM), n.reshape(nb, 1, D_MLSTM),
        z.reshape(nb, 1, D_MAIN), zg.reshape(nb, 1, N_GATE_PAD), w['gate_bias'],
        mst.reshape(nb, 1, MLSTM_HEADS), w['g_ml_out'])
    y = jnp.concatenate([y_rnn, y_ml.reshape(nb, D_MLSTM).astype(BF16)], axis=-1)
    x1 = _matmul_res(y, w['w_out'], x2, tm=nb, tn=1024)
    q = _norm_matmul(x1, w['g_xattn'], w['w_cq'], tm=nb, tn=1024, out_dtype=F32)
    o = _xattn_sample(q.reshape(nb, X_HEADS, X_HD), ck, cv).astype(BF16)
    x2_ = _matmul_res(o.reshape(nb, d), w['w_co'], x1, tm=nb, tn=1024)
    xo = _ffn(x2_, w['g_ffn'], w['w_up'], w['w_down'], g_final, tm=nb, tf=512, final_norm=final_norm)
    new = (h_new, rconv_new.reshape(nb, CONV_W - 1, D_RNN), c_new, n_new.reshape(nb, MLSTM_HEADS, MLSTM_HD),
           m_new[:, 0, :MLSTM_HEADS], mconv_new.reshape(nb, CONV_W - 1, D_MLSTM))
    return xo.reshape(nb, t, d), new


def _layer_weights(l, g_mix, w_in, conv_rnn_w, conv_rnn_b, lru_wa, lru_ba, lru_wx, lru_bx, lru_lambda,
                   g_rnn_out, conv_ml_w, conv_ml_b, ml_wq, ml_wk, ml_bi, ml_bf, g_ml_out, w_out,
                   g_xattn, g_mem, w_cq, w_mk, w_mv, w_co, g_ffn, w_up, w_down):
    n_gate = 2 * MLSTM_HEADS
    w_gate = jnp.pad(w_in[l][:, D_MAIN:], ((0, 0), (0, N_GATE_PAD - n_gate)))
    gate_bias = jnp.pad(jnp.concatenate([ml_bi[l], ml_bf[l]]), (0, N_GATE_PAD - n_gate))
    return dict(
        g_mix=g_mix[l], w_in_main=w_in[l].astype(BF16), w_in_gate=w_gate.astype(BF16),
        conv_rnn_w=conv_rnn_w[l], conv_rnn_b=conv_rnn_b[l].reshape(1, D_RNN),
        lru_wg=jnp.concatenate([lru_wa[l], lru_wx[l]], axis=-1).astype(BF16),
        lru_bg=jnp.concatenate([lru_ba[l], lru_bx[l]], axis=-1).reshape(RNN_HEADS, 1, 2 * RNN_HD),
        lru_lambda=lru_lambda[l].reshape(1, D_RNN), g_rnn_out=g_rnn_out[l].reshape(1, D_RNN),
        conv_ml_w=conv_ml_w[l], conv_ml_b=conv_ml_b[l].reshape(1, D_MLSTM),
        ml_wqk=jnp.concatenate([ml_wq[l], ml_wk[l]], axis=-1).astype(BF16),
        gate_bias=gate_bias.reshape(1, N_GATE_PAD), g_ml_out=g_ml_out[l].reshape(1, MLSTM_HD),
        w_out=w_out[l].astype(BF16), g_xattn=g_xattn[l], g_mem=g_mem[l],
        w_cq=w_cq[l].astype(BF16), w_mk=w_mk[l].astype(BF16), w_mv=w_mv[l].astype(BF16),
        w_co=w_co[l].astype(BF16), g_ffn=g_ffn[l], w_up=w_up[l].astype(BF16), w_down=w_down[l].astype(BF16))


def kernel(x_prompt, x_sample, mem_prompt, state_rglru_h, state_rglru_conv, state_mlstm_C, state_mlstm_n, state_mlstm_m, state_mlstm_conv, cache_mem_k, cache_mem_v, g_mix, w_in, conv_rnn_w, conv_rnn_b, lru_wa, lru_ba, lru_wx, lru_bx, lru_lambda, g_rnn_out, conv_ml_w, conv_ml_b, ml_wq, ml_wk, ml_bi, ml_bf, g_ml_out, w_out, g_xattn, g_mem, w_cq, w_mk, w_mv, w_co, g_ffn, w_up, w_down, g_final):
    depth = w_in.shape[0]
    xp, xs = x_prompt, x_sample
    p_out = [[] for _ in range(8)]
    s_out = [[] for _ in range(6)]
    for l in range(depth):
        w = _layer_weights(l, g_mix, w_in, conv_rnn_w, conv_rnn_b, lru_wa, lru_ba, lru_wx, lru_bx, lru_lambda,
                           g_rnn_out, conv_ml_w, conv_ml_b, ml_wq, ml_wk, ml_bi, ml_bf, g_ml_out, w_out,
                           g_xattn, g_mem, w_cq, w_mk, w_mv, w_co, g_ffn, w_up, w_down)
        last = l == depth - 1
        xp, new_p = _prompt_block(xp, mem_prompt, w, g_final, last)
        for j, a in enumerate(new_p):
            p_out[j].append(a)
        xs, new_s = _sample_block(xs, state_rglru_h[l], state_rglru_conv[l], state_mlstm_C[l], state_mlstm_n[l],
                                  state_mlstm_m[l], state_mlstm_conv[l], cache_mem_k[l], cache_mem_v[l],
                                  w, g_final, last)
        for j, a in enumerate(new_s):
            s_out[j].append(a)
    P = [jnp.stack(a, axis=0) for a in p_out]
    S = [jnp.stack(a, axis=0) for a in s_out]
    return (xp, xs, P[0], P[1], P[2], P[3], P[4], P[5], P[6], P[7],
            S[0], S[1], S[2], S[3], S[4], S[5])
```

```python
import functools

import jax
import jax.numpy as jnp
from jax import lax
from jax.experimental import pallas as pl
from jax.experimental.pallas import tpu as pltpu

F32 = jnp.float32
BF16 = jnp.bfloat16

D_MODEL = 2048
D_RNN = 1024
RNN_HEADS = 8
RNN_HD = 128
CONV_W = 4
LRU_C = 8.0
D_MLSTM = 1024
MLSTM_HEADS = 4
MLSTM_HD = 256
CHUNK = 128
N_MEM = 256
X_HEADS = 4
X_HD = 512
D_FF = 8192
EPS = 1e-6
D_MAIN = 5 * 1024
N_GATE_PAD = 128

V7X_VMEM_LIMIT_BYTES = 56 * 1024 * 1024


def _params(*sem):
    return pltpu.CompilerParams(dimension_semantics=sem, vmem_limit_bytes=V7X_VMEM_LIMIT_BYTES)


def _rms(x, g):
    ms = jnp.mean(x * x, axis=-1, keepdims=True)
    return x * lax.rsqrt(ms + EPS) * g


def _dot(a, b):
    return jnp.dot(a, b, preferred_element_type=F32)


def _norm_matmul_kernel(*refs, has_extra):
    if has_extra:
        x_ref, g_ref, w_ref, we_ref, o_ref, oe_ref, xn_ref = refs
    else:
        x_ref, g_ref, w_ref, o_ref, xn_ref = refs

    @pl.when(pl.program_id(1) == 0)
    def _():
        xn = _rms(x_ref[...], g_ref[...]).astype(BF16)
        xn_ref[...] = xn
        if has_extra:
            oe_ref[...] = _dot(xn, we_ref[...])

    o_ref[...] = _dot(xn_ref[...], w_ref[...]).astype(o_ref.dtype)


def _norm_matmul(x, g, w, *, tm, tn, out_dtype, w_extra=None, n_cols=None):
    m, k = x.shape
    n = n_cols or w.shape[1]
    has_extra = w_extra is not None
    in_specs = [pl.BlockSpec((tm, k), lambda i, j: (i, 0)),
                pl.BlockSpec((1, k), lambda i, j: (0, 0)),
                pl.BlockSpec((k, tn), lambda i, j: (0, j))]
    out_specs = [pl.BlockSpec((tm, tn), lambda i, j: (i, j))]
    out_shape = [jax.ShapeDtypeStruct((m, n), out_dtype)]
    args = [x, g.reshape(1, k), w]
    if has_extra:
        ne = w_extra.shape[1]
        in_specs.append(pl.BlockSpec((k, ne), lambda i, j: (0, 0)))
        out_specs.append(pl.BlockSpec((tm, ne), lambda i, j: (i, 0)))
        out_shape.append(jax.ShapeDtypeStruct((m, ne), F32))
        args.append(w_extra)
    res = pl.pallas_call(
        functools.partial(_norm_matmul_kernel, has_extra=has_extra),
        grid=(m // tm, n // tn),
        in_specs=in_specs, out_specs=out_specs, out_shape=out_shape,
        scratch_shapes=[pltpu.VMEM((tm, k), BF16)],
        compiler_params=_params("parallel", "arbitrary"),
        name="norm_matmul",
    )(*args)
    return res if has_extra else res[0]


def _matmul_res_kernel(*refs, n_parts):
    a_refs, w_refs = refs[:n_parts], refs[n_parts:2 * n_parts]
    r_ref, o_ref = refs[2 * n_parts:]
    acc = r_ref[...]
    for a_ref, w_ref in zip(a_refs, w_refs):
        acc = acc + _dot(a_ref[...], w_ref[...])
    o_ref[...] = acc


def _matmul_res(a_parts, w, res, *, tm, tn):
    n_parts = len(a_parts)
    m, kp = a_parts[0].shape
    n = w.shape[1]
    a_specs = [pl.BlockSpec((tm, kp), lambda i, j: (i, 0)) for _ in a_parts]
    w_specs = [pl.BlockSpec((kp, tn), lambda i, j, p=p: (p, j)) for p in range(n_parts)]
    return pl.pallas_call(
        functools.partial(_matmul_res_kernel, n_parts=n_parts),
        grid=(m // tm, n // tn),
        in_specs=a_specs + w_specs + [pl.BlockSpec((tm, tn), lambda i, j: (i, j))],
        out_specs=pl.BlockSpec((tm, tn), lambda i, j: (i, j)),
        out_shape=jax.ShapeDtypeStruct((m, n), F32),
        compiler_params=_params("parallel", "arbitrary"),
        name="matmul_res",
    )(*a_parts, *([w] * n_parts), res)


def _ffn_kernel(x_ref, g_ref, wu_ref, wd_ref, gf_ref, o_ref, xn_ref, *, final_norm):
    f = pl.program_id(1)

    @pl.when(f == 0)
    def _():
        xn_ref[...] = _rms(x_ref[...], g_ref[...]).astype(BF16)
        o_ref[...] = jnp.zeros_like(o_ref)

    h = _dot(xn_ref[...], wu_ref[...])
    h = jnp.square(jnp.maximum(h, 0.0)).astype(BF16)
    o_ref[...] += _dot(h, wd_ref[...])

    @pl.when(f == pl.num_programs(1) - 1)
    def _():
        y = x_ref[...] + o_ref[...]
        if final_norm:
            y = _rms(y, gf_ref[...])
        o_ref[...] = y


def _ffn(x, g, w_up, w_down, g_final, *, tm, tf, final_norm):
    m, d = x.shape
    dff = w_up.shape[1]
    return pl.pallas_call(
        functools.partial(_ffn_kernel, final_norm=final_norm),
        grid=(m // tm, dff // tf),
        in_specs=[pl.BlockSpec((tm, d), lambda i, f: (i, 0)),
                  pl.BlockSpec((1, d), lambda i, f: (0, 0)),
                  pl.BlockSpec((d, tf), lambda i, f: (0, f)),
                  pl.BlockSpec((tf, d), lambda i, f: (f, 0)),
                  pl.BlockSpec((1, d), lambda i, f: (0, 0))],
        out_specs=pl.BlockSpec((tm, d), lambda i, f: (i, 0)),
        out_shape=jax.ShapeDtypeStruct((m, d), F32),
        scratch_shapes=[pltpu.VMEM((tm, d), BF16)],
        compiler_params=_params("parallel", "arbitrary"),
        name="ffn",
    )(x, g.reshape(1, d), w_up, w_down, g_final.reshape(1, d))


def _lru_gates(xc, wg, bg, lam):
    g = _dot(xc.astype(BF16), wg) + bg
    r = jax.nn.sigmoid(g[:, :RNN_HD])
    i = jax.nn.sigmoid(g[:, RNN_HD:])
    log_a = -LRU_C * r * jax.nn.softplus(-lam)
    a = jnp.exp(log_a)
    u = jnp.sqrt(-jnp.tanh(log_a) * (a * a + 1.0)) * (i * xc)
    return a, u


def _rglru_kernel(xr_ref, gr_ref, h0_ref, c0_ref, cw_ref, cb_ref, wg_ref, bg_ref, lam_ref, gout_ref,
                  y_ref, hn_ref, cn_ref, xbuf, hc, ybuf, *, tc):
    c = pl.program_id(1)

    @pl.when(c == 0)
    def _():
        xbuf[0:8, :] = jnp.zeros((8, D_RNN), F32)
        xbuf[5:8, :] = c0_ref[0]
        hc[...] = h0_ref[0]

    @pl.when(c > 0)
    def _():
        xbuf[0:8, :] = xbuf[tc:tc + 8, :]

    xbuf[8:8 + tc, :] = xr_ref[0]
    row = lax.broadcasted_iota(jnp.int32, (tc, RNN_HD), 0)
    ssq = jnp.zeros((tc, 1), F32)
    for h in range(RNN_HEADS):
        sl = slice(h * RNN_HD, (h + 1) * RNN_HD)
        xc = cb_ref[:, sl]
        for j in range(CONV_W):
            xc = xc + cw_ref[j:j + 1, sl] * xbuf[5 + j:5 + j + tc, sl]
        a, u = _lru_gates(xc, wg_ref[h], bg_ref[h], lam_ref[:, sl])
        d = 1
        while d < tc:
            keep = row >= d
            a_prev = jnp.where(keep, pltpu.roll(a, d, 0), 1.0)
            u_prev = jnp.where(keep, pltpu.roll(u, d, 0), 0.0)
            u = u + a * u_prev
            a = a * a_prev
            d *= 2
        hs = u + a * hc[:, sl]
        hc[:, sl] = hs[tc - 1:tc, :]
        yv = hs * jax.nn.gelu(gr_ref[0, :, sl])
        ssq = ssq + jnp.sum(yv * yv, axis=-1, keepdims=True)
        ybuf[:, sl] = yv
    y = ybuf[...] * lax.rsqrt(ssq * (1.0 / D_RNN) + EPS) * gout_ref[...]
    y_ref[0] = y.astype(y_ref.dtype)

    @pl.when(c == pl.num_programs(1) - 1)
    def _():
        hn_ref[0] = hc[...]
        cn_ref[0] = xbuf[tc + 5:tc + 8, :]


def _rglru_prompt(z3, h0, conv0, w, *, tc):
    b, t, _ = z3.shape
    full2 = lambda bi, ci: (0, 0)
    return pl.pallas_call(
        functools.partial(_rglru_kernel, tc=tc),
        grid=(b, t // tc),
        in_specs=[pl.BlockSpec((1, tc, D_RNN), lambda bi, ci: (bi, ci, 0)),
                  pl.BlockSpec((1, tc, D_RNN), lambda bi, ci: (bi, ci, 1)),
                  pl.BlockSpec((1, 1, D_RNN), lambda bi, ci: (bi, 0, 0)),
                  pl.BlockSpec((1, CONV_W - 1, D_RNN), lambda bi, ci: (bi, 0, 0)),
                  pl.BlockSpec((CONV_W, D_RNN), full2),
                  pl.BlockSpec((1, D_RNN), full2),
                  pl.BlockSpec((RNN_HEADS, RNN_HD, 2 * RNN_HD), lambda bi, ci: (0, 0, 0)),
                  pl.BlockSpec((RNN_HEADS, 1, 2 * RNN_HD), lambda bi, ci: (0, 0, 0)),
                  pl.BlockSpec((1, D_RNN), full2),
                  pl.BlockSpec((1, D_RNN), full2)],
        out_specs=[pl.BlockSpec((1, tc, D_RNN), lambda bi, ci: (bi, ci, 0)),
                   pl.BlockSpec((1, 1, D_RNN), lambda bi, ci: (bi, 0, 0)),
                   pl.BlockSpec((1, CONV_W - 1, D_RNN), lambda bi, ci: (bi, 0, 0))],
        out_shape=[jax.ShapeDtypeStruct((b, t, D_RNN), BF16),
                   jax.ShapeDtypeStruct((b, 1, D_RNN), F32),
                   jax.ShapeDtypeStruct((b, CONV_W - 1, D_RNN), F32)],
        scratch_shapes=[pltpu.VMEM((tc + 8, D_RNN), F32),
                        pltpu.VMEM((1, D_RNN), F32),
                        pltpu.VMEM((tc, D_RNN), F32)],
        compiler_params=_params("parallel", "arbitrary"),
        name="rglru_prompt",
    )(z3, z3, h0, conv0, w['conv_rnn_w'], w['conv_rnn_b'], w['lru_wg'], w['lru_bg'],
      w['lru_lambda'], w['g_rnn_out'])


def _mlstm_kernel(u_ref, v_ref, og_ref, zg_ref, gb_ref, c0_ref, n0_ref, m0_ref, cv0_ref,
                  cw_ref, cb_ref, wqk_ref, gout_ref,
                  y_ref, cn_ref, nn_ref, mn_ref, cvn_ref, ubuf, cst, nst, mst):
    c = pl.program_id(1)
    L = CHUNK

    @pl.when(c == 0)
    def _():
        ubuf[0:8, :] = jnp.zeros((8, D_MLSTM), F32)
        ubuf[5:8, :] = cv0_ref[0]
        cst[...] = c0_ref[0]
        nst[...] = n0_ref[0]
        mst[...] = m0_ref[0]

    @pl.when(c > 0)
    def _():
        ubuf[0:8, :] = ubuf[L:L + 8, :]

    ubuf[8:8 + L, :] = u_ref[0]

    zg = zg_ref[0] + gb_ref[...]
    zgt = zg.T
    ti = lax.broadcasted_iota(jnp.int32, (L, L), 0)
    si = lax.broadcasted_iota(jnp.int32, (L, L), 1)
    causal = si <= ti
    anti = ti <= si

    for h in range(MLSTM_HEADS):
        sl = slice(h * MLSTM_HD, (h + 1) * MLSTM_HD)
        uc = cb_ref[:, sl]
        for j in range(CONV_W):
            uc = uc + cw_ref[j:j + 1, sl] * ubuf[5 + j:5 + j + L, sl]
        uc = jax.nn.silu(uc)
        qk = _dot(uc.astype(BF16), wqk_ref[h])
        q = qk[:, :MLSTM_HD]
        k = qk[:, MLSTM_HD:] * (MLSTM_HD ** -0.5)
        v = v_ref[0, :, sl]
        qb, kb = q.astype(BF16), k.astype(BF16)

        icol = zg[:, h:h + 1]
        fcol = jax.nn.log_sigmoid(zg[:, MLSTM_HEADS + h:MLSTM_HEADS + h + 1])
        irow = zgt[h:h + 1, :]
        frow = jax.nn.log_sigmoid(zgt[MLSTM_HEADS + h:MLSTM_HEADS + h + 1, :])
        bcol = jnp.sum(jnp.where(causal, frow, 0.0), axis=1, keepdims=True)
        brow = jnp.sum(jnp.where(anti, fcol, 0.0), axis=0, keepdims=True)
        dmat = jnp.where(causal, irow + bcol - brow, -jnp.inf)
        m_prev = mst[h, :, 0:1]
        inter = bcol + m_prev
        m_t = jnp.maximum(inter, jnp.max(dmat, axis=1, keepdims=True))
        w_inter = jnp.exp(inter - m_t)
        s = lax.dot_general(qb, kb, (((1,), (1,)), ((), ())), preferred_element_type=F32) * jnp.exp(dmat - m_t)
        cmat = cst[h]
        nrow = nst[h]
        cq = lax.dot_general(qb, cmat.astype(BF16), (((1,), (1,)), ((), ())), preferred_element_type=F32)
        num = w_inter * cq + _dot(s.astype(BF16), v.astype(BF16))
        den = w_inter * jnp.sum(q * nrow, axis=1, keepdims=True) + jnp.sum(s, axis=1, keepdims=True)
        hm = num / jnp.maximum(jnp.abs(den), jnp.exp(-m_t))

        m_new = m_t[L - 1:L, :]
        b_last = bcol[L - 1:L, :]
        g_state = jnp.exp(b_last + m_prev - m_new)
        g_in = jnp.exp(icol + b_last - bcol - m_new)
        cst[h] = g_state * cmat + lax.dot_general((g_in * v).astype(BF16), kb, (((0,), (0,)), ((), ())),
                                                  preferred_element_type=F32)
        nst[h] = g_state * nrow + jnp.sum(g_in * k, axis=0, keepdims=True)
        mst[h] = jnp.broadcast_to(m_new, (1, N_GATE_PAD))

        y = _rms(hm, gout_ref[...]) * jax.nn.sigmoid(og_ref[0, :, sl])
        y_ref[0, :, sl] = y.astype(y_ref.dtype)

    @pl.when(c == pl.num_programs(1) - 1)
    def _():
        cn_ref[0] = cst[...]
        nn_ref[0] = nst[...]
        mn_ref[0] = mst[...]
        cvn_ref[0] = ubuf[L + 5:L + 8, :]


def _mlstm_prompt(z3, zg3, c0, n0, m0, conv0, w):
    b, t, _ = z3.shape
    nc = t // CHUNK
    u_blk, v_blk, o_blk = 2 * D_RNN // D_MLSTM, 3 * D_RNN // D_MLSTM, 4 * D_RNN // D_MLSTM
    zcol = lambda blk: pl.BlockSpec((1, CHUNK, D_MLSTM), lambda bi, ci: (bi, ci, blk))
    state4 = lambda *minor: pl.BlockSpec((1, MLSTM_HEADS) + minor, lambda bi, ci: (bi, 0, 0, 0))
    conv_spec = pl.BlockSpec((1, CONV_W - 1, D_MLSTM), lambda bi, ci: (bi, 0, 0))
    const2 = lambda shape: pl.BlockSpec(shape, lambda bi, ci: (0, 0))
    return pl.pallas_call(
        _mlstm_kernel,
        grid=(b, nc),
        in_specs=[zcol(u_blk), zcol(v_blk), zcol(o_blk),
                  pl.BlockSpec((1, CHUNK, N_GATE_PAD), lambda bi, ci: (bi, ci, 0)),
                  const2((1, N_GATE_PAD)),
                  state4(MLSTM_HD, MLSTM_HD), state4(1, MLSTM_HD), state4(1, N_GATE_PAD), conv_spec,
                  const2((CONV_W, D_MLSTM)), const2((1, D_MLSTM)),
                  pl.BlockSpec((MLSTM_HEADS, MLSTM_HD, 2 * MLSTM_HD), lambda bi, ci: (0, 0, 0)),
                  const2((1, MLSTM_HD))],
        out_specs=[pl.BlockSpec((1, CHUNK, D_MLSTM), lambda bi, ci: (bi, ci, 0)),
                   state4(MLSTM_HD, MLSTM_HD), state4(1, MLSTM_HD), state4(1, N_GATE_PAD), conv_spec],
        out_shape=[jax.ShapeDtypeStruct((b, t, D_MLSTM), BF16),
                   jax.ShapeDtypeStruct((b, MLSTM_HEADS, MLSTM_HD, MLSTM_HD), F32),
                   jax.ShapeDtypeStruct((b, MLSTM_HEADS, 1, MLSTM_HD), F32),
                   jax.ShapeDtypeStruct((b, MLSTM_HEADS, 1, N_GATE_PAD), F32),
                   jax.ShapeDtypeStruct((b, CONV_W - 1, D_MLSTM), F32)],
        scratch_shapes=[pltpu.VMEM((CHUNK + 8, D_MLSTM), F32),
                        pltpu.VMEM((MLSTM_HEADS, MLSTM_HD, MLSTM_HD), F32),
                        pltpu.VMEM((MLSTM_HEADS, 1, MLSTM_HD), F32),
                        pltpu.VMEM((MLSTM_HEADS, 1, N_GATE_PAD), F32)],
        compiler_params=_params("parallel", "arbitrary"),
        name="mlstm_prompt",
    )(z3, z3, z3, zg3, w['gate_bias'], c0, n0, m0, conv0, w['conv_ml_w'], w['conv_ml_b'],
      w['ml_wqk'], w['g_ml_out'])


def _xattn_kernel(q_ref, k_ref, v_ref, o_ref):
    for h in range(X_HEADS):
        sl = slice(h * X_HD, (h + 1) * X_HD)
        s = lax.dot_general(q_ref[0, :, sl], k_ref[0, :, sl].astype(BF16), (((1,), (1,)), ((), ())),
                            preferred_element_type=F32) * (X_HD ** -0.5)
        e = jnp.exp(s - jnp.max(s, axis=-1, keepdims=True))
        p = e / jnp.sum(e, axis=-1, keepdims=True)
        o_ref[0, :, sl] = _dot(p.astype(BF16), v_ref[0, :, sl].astype(BF16)).astype(o_ref.dtype)


def _xattn_prompt(q3, mk3, mv3, *, tq):
    b, t, d = q3.shape
    return pl.pallas_call(
        _xattn_kernel,
        grid=(b, t // tq),
        in_specs=[pl.BlockSpec((1, tq, d), lambda bi, ti: (bi, ti, 0)),
                  pl.BlockSpec((1, N_MEM, d), lambda bi, ti: (bi, 0, 0)),
                  pl.BlockSpec((1, N_MEM, d), lambda bi, ti: (bi, 0, 0))],
        out_specs=pl.BlockSpec((1, tq, d), lambda bi, ti: (bi, ti, 0)),
        out_shape=jax.ShapeDtypeStruct((b, t, d), BF16),
        compiler_params=_params("parallel", "arbitrary"),
        name="xattn_prompt",
    )(q3, mk3, mv3)


XS_ROWS = 2
XS_SUB = 2 * X_HEADS
XS_PAIRS = X_HD // (2 * 128)


def _pack_heads(x):
    lead = x.shape[:-1]
    x = x.reshape(*lead, X_HEADS, 2 * XS_PAIRS, 128)
    x = jnp.swapaxes(x, -3, -2)
    return x.reshape(*lead, XS_PAIRS, XS_SUB, 128)


def _unpack_heads(x):
    lead = x.shape[:-3]
    x = x.reshape(*lead, 2 * XS_PAIRS, X_HEADS, 128)
    x = jnp.swapaxes(x, -3, -2)
    return x.reshape(*lead, X_HEADS * X_HD)


def _xattn_sample_kernel(q_ref, k_ref, v_ref, o_ref):
    for r in range(XS_ROWS):
        t = jnp.sum(k_ref[r] * q_ref[r], axis=1)
        t = t + pltpu.roll(t, X_HEADS, 1)
        s = jnp.sum(t, axis=-1, keepdims=True) * (X_HD ** -0.5)
        e = jnp.exp(s - jnp.max(s, axis=0, keepdims=True))
        p = e / jnp.sum(e, axis=0, keepdims=True)
        o_ref[r] = jnp.sum(p[:, None] * v_ref[r], axis=0)


def _xattn_sample(q, ck, cv):
    b = q.shape[0]
    kv_spec = pl.BlockSpec((XS_ROWS, N_MEM, XS_PAIRS, XS_SUB, 128), lambda bi: (bi, 0, 0, 0, 0))
    q_spec = pl.BlockSpec((XS_ROWS, XS_PAIRS, XS_SUB, 128), lambda bi: (bi, 0, 0, 0))
    o = pl.pallas_call(
        _xattn_sample_kernel,
        grid=(b // XS_ROWS,),
        in_specs=[q_spec, kv_spec, kv_spec],
        out_specs=q_spec,
        out_shape=jax.ShapeDtypeStruct((b, XS_PAIRS, XS_SUB, 128), F32),
        compiler_params=_params("parallel"),
        name="xattn_sample",
    )(_pack_heads(q), _pack_heads(ck.reshape(b, N_MEM, X_HEADS * X_HD)),
      _pack_heads(cv.reshape(b, N_MEM, X_HEADS * X_HD)))
    return _unpack_heads(o)


def _smix_rows_kernel(z_ref, h0_ref, rc_ref, mc_ref, cwr_ref, cbr_ref, wg_ref, bg_ref, lam_ref, gout_ref,
                      cwm_ref, cbm_ref, wqk_ref,
                      yr_ref, hn_ref, rcn_ref, mcn_ref, q_ref, k_ref, ybuf):
    nb = z_ref.shape[0]
    ssq = jnp.zeros((nb, 1), F32)
    for h in range(RNN_HEADS):
        sl = slice(h * RNN_HD, (h + 1) * RNN_HD)
        xc = cbr_ref[:, sl] + cwr_ref[CONV_W - 1:CONV_W, sl] * z_ref[:, sl]
        for j in range(CONV_W - 1):
            xc = xc + cwr_ref[j:j + 1, sl] * rc_ref[:, j * D_RNN + h * RNN_HD:j * D_RNN + (h + 1) * RNN_HD]
        a, u = _lru_gates(xc, wg_ref[h], bg_ref[h], lam_ref[:, sl])
        hs = a * h0_ref[:, sl] + u
        hn_ref[:, sl] = hs
        yv = hs * jax.nn.gelu(z_ref[:, D_RNN + h * RNN_HD:D_RNN + (h + 1) * RNN_HD])
        ssq = ssq + jnp.sum(yv * yv, axis=-1, keepdims=True)
        ybuf[:, sl] = yv
    yr_ref[...] = (ybuf[...] * lax.rsqrt(ssq * (1.0 / D_RNN) + EPS) * gout_ref[...]).astype(yr_ref.dtype)
    rcn_ref[:, 0:2 * D_RNN] = rc_ref[:, D_RNN:3 * D_RNN]
    rcn_ref[:, 2 * D_RNN:3 * D_RNN] = z_ref[:, 0:D_RNN]

    for h in range(MLSTM_HEADS):
        sl = slice(h * MLSTM_HD, (h + 1) * MLSTM_HD)
        uc = cbm_ref[:, sl] + cwm_ref[CONV_W - 1:CONV_W, sl] * z_ref[:, 2 * D_RNN + h * MLSTM_HD:2 * D_RNN + (h + 1) * MLSTM_HD]
        for j in range(CONV_W - 1):
            uc = uc + cwm_ref[j:j + 1, sl] * mc_ref[:, j * D_MLSTM + h * MLSTM_HD:j * D_MLSTM + (h + 1) * MLSTM_HD]
        uc = jax.nn.silu(uc)
        qk = _dot(uc.astype(BF16), wqk_ref[h])
        q_ref[:, sl] = qk[:, :MLSTM_HD]
        k_ref[:, sl] = qk[:, MLSTM_HD:] * (MLSTM_HD ** -0.5)
    mcn_ref[:, 0:2 * D_MLSTM] = mc_ref[:, D_MLSTM:3 * D_MLSTM]
    mcn_ref[:, 2 * D_MLSTM:3 * D_MLSTM] = z_ref[:, 2 * D_RNN:2 * D_RNN + D_MLSTM]


def _smix_rows(z, h0, rconv, mconv, w):
    nb = z.shape[0]
    outs = [jax.ShapeDtypeStruct((nb, D_RNN), BF16),
            jax.ShapeDtypeStruct((nb, D_RNN), F32),
            jax.ShapeDtypeStruct((nb, 3 * D_RNN), F32),
            jax.ShapeDtypeStruct((nb, 3 * D_MLSTM), F32),
            jax.ShapeDtypeStruct((nb, D_MLSTM), F32),
            jax.ShapeDtypeStruct((nb, D_MLSTM), F32)]
    return pl.pallas_call(
        _smix_rows_kernel,
        out_shape=outs,
        scratch_shapes=[pltpu.VMEM((nb, D_RNN), F32)],
        compiler_params=pltpu.CompilerParams(vmem_limit_bytes=V7X_VMEM_LIMIT_BYTES),
        name="smix_rows",
    )(z, h0, rconv, mconv, w['conv_rnn_w'], w['conv_rnn_b'], w['lru_wg'], w['lru_bg'], w['lru_lambda'],
      w['g_rnn_out'], w['conv_ml_w'], w['conv_ml_b'], w['ml_wqk'])


SMIX_ROWS = 4
MXU_MIN_ROWS = 8


def _smix_state_kernel(c_ref, q_ref, k_ref, n_ref, v_ref, og_ref, zg_ref, gb_ref, m_ref, gout_ref,
                       cn_ref, nn_ref, mn_ref, y_ref):
    lane_g = lax.broadcasted_iota(jnp.int32, (1, N_GATE_PAD), 1)
    first_row = lax.broadcasted_iota(jnp.int32, (MXU_MIN_ROWS, MLSTM_HD), 0) == 0
    for r in range(SMIX_ROWS):
        zg = zg_ref[r] + gb_ref[...]
        m_out = jnp.zeros((1, N_GATE_PAD), F32)
        for h in range(MLSTM_HEADS):
            sl = slice(h * MLSTM_HD, (h + 1) * MLSTM_HD)
            q = q_ref[r, :, sl]
            k = k_ref[r, :, sl]
            n = n_ref[r, :, sl]
            v = v_ref[r, :, sl]
            ig = zg[:, h:h + 1]
            lf = jax.nn.log_sigmoid(zg[:, MLSTM_HEADS + h:MLSTM_HEADS + h + 1])
            m_prev = m_ref[r, :, h:h + 1]
            m_t = jnp.maximum(lf + m_prev, ig)
            w_inter = jnp.exp(lf + m_prev - m_t)
            g_in = jnp.exp(ig - m_t)
            s = jnp.sum(q * k, axis=1, keepdims=True) * g_in
            den = w_inter * jnp.sum(n * q, axis=1, keepdims=True) + s
            denom = jnp.maximum(jnp.abs(den), jnp.exp(-m_t))
            cmat = c_ref[r, h]
            q8 = jnp.broadcast_to(q, (MXU_MIN_ROWS, MLSTM_HD)).astype(BF16)
            cq = lax.dot_general(q8, cmat.astype(BF16), (((1,), (1,)), ((), ())),
                                 preferred_element_type=F32)[0:1, :]
            hm = (w_inter * cq + s * v) / denom
            gv8 = jnp.where(first_row, jnp.broadcast_to(g_in * v, (MXU_MIN_ROWS, MLSTM_HD)), 0.0).astype(BF16)
            k8 = jnp.broadcast_to(k, (MXU_MIN_ROWS, MLSTM_HD)).astype(BF16)
            outer = lax.dot_general(gv8, k8, (((0,), (0,)), ((), ())), preferred_element_type=F32)
            cn_ref[r, h] = w_inter * cmat + outer
            nn_ref[r, :, sl] = w_inter * n + g_in * k
            m_out = jnp.where(lane_g == h, m_t, m_out)
            y = _rms(hm, gout_ref[...]) * jax.nn.sigmoid(og_ref[r, :, sl])
            y_ref[r, :, sl] = y.astype(y_ref.dtype)
        mn_ref[r] = m_out


def _smix_state(c, q3, k3, n3, z3, zg3, gate_bias, m3, gout):
    nb = c.shape[0]
    row = lambda width, col=0: pl.BlockSpec((SMIX_ROWS, 1, width), lambda bi: (bi, 0, col))
    cblk = pl.BlockSpec((SMIX_ROWS, MLSTM_HEADS, MLSTM_HD, MLSTM_HD), lambda bi: (bi, 0, 0, 0))
    return pl.pallas_call(
        _smix_state_kernel,
        grid=(nb // SMIX_ROWS,),
        in_specs=[cblk, row(D_MLSTM), row(D_MLSTM), row(D_MLSTM),
                  row(D_MLSTM, 3 * D_RNN // D_MLSTM), row(D_MLSTM, 4 * D_RNN // D_MLSTM),
                  row(N_GATE_PAD),
                  pl.BlockSpec((1, N_GATE_PAD), lambda bi: (0, 0)),
                  row(MLSTM_HEADS),
                  pl.BlockSpec((1, MLSTM_HD), lambda bi: (0, 0))],
        out_specs=[cblk, row(D_MLSTM), row(N_GATE_PAD), row(D_MLSTM)],
        out_shape=[jax.ShapeDtypeStruct(c.shape, F32),
                   jax.ShapeDtypeStruct((nb, 1, D_MLSTM), F32),
                   jax.ShapeDtypeStruct((nb, 1, N_GATE_PAD), F32),
                   jax.ShapeDtypeStruct((nb, 1, D_MLSTM), F32)],
        compiler_params=_params("parallel"),
        name="smix_state",
    )(c, q3, k3, n3, z3, z3, zg3, gate_bias, m3, gout)


def _prompt_block(x, mem, w, g_final, final_norm):
    b, t, d = x.shape
    m = b * t
    tm = 1024
    x2 = x.reshape(m, d)
    z, zg = _norm_matmul(x2, w['g_mix'], w['w_in_main'], tm=tm, tn=512, out_dtype=F32, w_extra=w['w_in_gate'],
                          n_cols=D_MAIN)
    z3 = z.reshape(b, t, D_MAIN)
    zg3 = zg.reshape(b, t, N_GATE_PAD)
    h0 = jnp.zeros((b, 1, D_RNN), F32)
    conv0 = jnp.zeros((b, CONV_W - 1, D_RNN), F32)
    y_rnn, rg_h, rg_conv = _rglru_prompt(z3, h0, conv0, w, tc=256)
    c0 = jnp.zeros((b, MLSTM_HEADS, MLSTM_HD, MLSTM_HD), F32)
    n0 = jnp.zeros((b, MLSTM_HEADS, 1, MLSTM_HD), F32)
    m0 = jnp.zeros((b, MLSTM_HEADS, 1, N_GATE_PAD), F32)
    mconv0 = jnp.zeros((b, CONV_W - 1, D_MLSTM), F32)
    y_ml, c_new, n_new, m_new, ml_conv = _mlstm_prompt(z3, zg3, c0, n0, m0, mconv0, w)
    x1 = _matmul_res([y_rnn.reshape(m, D_RNN), y_ml.reshape(m, D_MLSTM)], w['w_out'], x2, tm=tm, tn=1024)

    mem2 = mem.reshape(b * N_MEM, d)
    mk = _norm_matmul(mem2, w['g_mem'], w['w_mk'], tm=b * N_MEM, tn=512, out_dtype=F32)
    mv = _norm_matmul(mem2, w['g_mem'], w['w_mv'], tm=b * N_MEM, tn=512, out_dtype=F32)
    q = _norm_matmul(x1, w['g_xattn'], w['w_cq'], tm=tm, tn=1024, out_dtype=BF16)
    o = _xattn_prompt(q.reshape(b, t, d), mk.reshape(b, N_MEM, d), mv.reshape(b, N_MEM, d), tq=512)
    x2_ = _matmul_res([o.reshape(m, d)], w['w_co'], x1, tm=tm, tn=1024)
    xo = _ffn(x2_, w['g_ffn'], w['w_up'], w['w_down'], g_final, tm=tm, tf=512, final_norm=final_norm)
    new = (rg_h.reshape(b, D_RNN), rg_conv, c_new, n_new.reshape(b, MLSTM_HEADS, MLSTM_HD),
           m_new[:, :, 0, 0], ml_conv,
           mk.reshape(b, N_MEM, X_HEADS, X_HD), mv.reshape(b, N_MEM, X_HEADS, X_HD))
    return xo.reshape(b, t, d), new


def _sample_block(x, rg_h, rg_conv, c, n, mst, ml_conv, ck, cv, w, g_final, final_norm):
    nb, t, d = x.shape
    x2 = x.reshape(nb, d)
    z, zg = _norm_matmul(x2, w['g_mix'], w['w_in_main'], tm=nb, tn=512, out_dtype=F32, w_extra=w['w_in_gate'],
                          n_cols=D_MAIN)
    y_rnn, h_new, rconv_new, mconv_new, q, k = _smix_rows(
        z, rg_h, rg_conv.reshape(nb, 3 * D_RNN), ml_conv.reshape(nb, 3 * D_MLSTM), w)
    c_new, n_new, m_new, y_ml = _smix_state(
        c, q.reshape(nb, 1, D_MLSTM), k.reshape(nb, 1, D_MLSTM), n.reshape(nb, 1, D_MLSTM),
        z.reshape(nb, 1, D_MAIN), zg.reshape(nb, 1, N_GATE_PAD), w['gate_bias'],
        mst.reshape(nb, 1, MLSTM_HEADS), w['g_ml_out'])
    x1 = _matmul_res([y_rnn, y_ml.reshape(nb, D_MLSTM).astype(BF16)], w['w_out'], x2, tm=nb, tn=1024)
    q = _norm_matmul(x1, w['g_xattn'], w['w_cq'], tm=nb, tn=1024, out_dtype=F32)
    o = _xattn_sample(q, ck, cv).astype(BF16)
    x2_ = _matmul_res([o], w['w_co'], x1, tm=nb, tn=1024)
    xo = _ffn(x2_, w['g_ffn'], w['w_up'], w['w_down'], g_final, tm=nb, tf=512, final_norm=final_norm)
    new = (h_new, rconv_new.reshape(nb, CONV_W - 1, D_RNN), c_new, n_new.reshape(nb, MLSTM_HEADS, MLSTM_HD),
           m_new[:, 0, :MLSTM_HEADS], mconv_new.reshape(nb, CONV_W - 1, D_MLSTM))
    return xo.reshape(nb, t, d), new


def _layer_weights(l, g_mix, w_in, conv_rnn_w, conv_rnn_b, lru_wa, lru_ba, lru_wx, lru_bx, lru_lambda,
                   g_rnn_out, conv_ml_w, conv_ml_b, ml_wq, ml_wk, ml_bi, ml_bf, g_ml_out, w_out,
                   g_xattn, g_mem, w_cq, w_mk, w_mv, w_co, g_ffn, w_up, w_down):
    n_gate = 2 * MLSTM_HEADS
    w_gate = jnp.pad(w_in[l][:, D_MAIN:], ((0, 0), (0, N_GATE_PAD - n_gate)))
    gate_bias = jnp.pad(jnp.concatenate([ml_bi[l], ml_bf[l]]), (0, N_GATE_PAD - n_gate))
    return dict(
        g_mix=g_mix[l], w_in_main=w_in[l].astype(BF16), w_in_gate=w_gate.astype(BF16),
        conv_rnn_w=conv_rnn_w[l], conv_rnn_b=conv_rnn_b[l].reshape(1, D_RNN),
        lru_wg=jnp.concatenate([lru_wa[l], lru_wx[l]], axis=-1).astype(BF16),
        lru_bg=jnp.concatenate([lru_ba[l], lru_bx[l]], axis=-1).reshape(RNN_HEADS, 1, 2 * RNN_HD),
        lru_lambda=lru_lambda[l].reshape(1, D_RNN), g_rnn_out=g_rnn_out[l].reshape(1, D_RNN),
        conv_ml_w=conv_ml_w[l], conv_ml_b=conv_ml_b[l].reshape(1, D_MLSTM),
        ml_wqk=jnp.concatenate([ml_wq[l], ml_wk[l]], axis=-1).astype(BF16),
        gate_bias=gate_bias.reshape(1, N_GATE_PAD), g_ml_out=g_ml_out[l].reshape(1, MLSTM_HD),
        w_out=w_out[l].astype(BF16), g_xattn=g_xattn[l], g_mem=g_mem[l],
        w_cq=w_cq[l].astype(BF16), w_mk=w_mk[l].astype(BF16), w_mv=w_mv[l].astype(BF16),
        w_co=w_co[l].astype(BF16), g_ffn=g_ffn[l], w_up=w_up[l].astype(BF16), w_down=w_down[l].astype(BF16))


def kernel(x_prompt, x_sample, mem_prompt, state_rglru_h, state_rglru_conv, state_mlstm_C, state_mlstm_n, state_mlstm_m, state_mlstm_conv, cache_mem_k, cache_mem_v, g_mix, w_in, conv_rnn_w, conv_rnn_b, lru_wa, lru_ba, lru_wx, lru_bx, lru_lambda, g_rnn_out, conv_ml_w, conv_ml_b, ml_wq, ml_wk, ml_bi, ml_bf, g_ml_out, w_out, g_xattn, g_mem, w_cq, w_mk, w_mv, w_co, g_ffn, w_up, w_down, g_final):
    depth = w_in.shape[0]
    xp, xs = x_prompt, x_sample
    p_out = [[] for _ in range(8)]
    s_out = [[] for _ in range(6)]
    for l in range(depth):
        w = _layer_weights(l, g_mix, w_in, conv_rnn_w, conv_rnn_b, lru_wa, lru_ba, lru_wx, lru_bx, lru_lambda,
                           g_rnn_out, conv_ml_w, conv_ml_b, ml_wq, ml_wk, ml_bi, ml_bf, g_ml_out, w_out,
                           g_xattn, g_mem, w_cq, w_mk, w_mv, w_co, g_ffn, w_up, w_down)
        last = l == depth - 1
        xp, new_p = _prompt_block(xp, mem_prompt, w, g_final, last)
        for j, a in enumerate(new_p):
            p_out[j].append(a)
        xs, new_s = _sample_block(xs, state_rglru_h[l], state_rglru_conv[l], state_mlstm_C[l], state_mlstm_n[l],
                                  state_mlstm_m[l], state_mlstm_conv[l], cache_mem_k[l], cache_mem_v[l],
                                  w, g_final, last)
        for j, a in enumerate(new_s):
            s_out[j].append(a)
    P = [jnp.stack(a, axis=0) for a in p_out]
    S = [jnp.stack(a, axis=0) for a in s_out]
    return (xp, xs, P[0], P[1], P[2], P[3], P[4], P[5], P[6], P[7],
            S[0], S[1], S[2], S[3], S[4], S[5])
```

```python
import functools

import jax
import jax.numpy as jnp
from jax import lax
from jax.experimental import pallas as pl
from jax.experimental.pallas import tpu as pltpu

F32 = jnp.float32
BF16 = jnp.bfloat16

D_MODEL = 2048
D_RNN = 1024
RNN_HEADS = 8
RNN_HD = 128
CONV_W = 4
LRU_C = 8.0
D_MLSTM = 1024
MLSTM_HEADS = 4
MLSTM_HD = 256
CHUNK = 128
N_MEM = 256
X_HEADS = 4
X_HD = 512
D_FF = 8192
EPS = 1e-6
D_MAIN = 5 * 1024
N_GATE_PAD = 128

V7X_VMEM_LIMIT_BYTES = 56 * 1024 * 1024
SUBLANES = 8


def _params(*sem):
    return pltpu.CompilerParams(dimension_semantics=sem, vmem_limit_bytes=V7X_VMEM_LIMIT_BYTES)


def _rms(x, g):
    ms = jnp.mean(x * x, axis=-1, keepdims=True)
    return x * lax.rsqrt(ms + EPS) * g


def _dot(a, b):
    return jnp.dot(a.astype(BF16), b.astype(BF16), preferred_element_type=F32)


def _dot_t(a, bt):
    return lax.dot_general(a.astype(BF16), bt.astype(BF16), (((1,), (1,)), ((), ())), preferred_element_type=F32)


def _norm_matmul_kernel(*refs, has_extra, w_is_transposed):
    if has_extra:
        x_ref, g_ref, w_ref, we_ref, o_ref, oe_ref, xn_ref = refs
    else:
        x_ref, g_ref, w_ref, o_ref, xn_ref = refs
    dot = _dot_t if w_is_transposed else _dot

    @pl.when(pl.program_id(1) == 0)
    def _():
        xn = _rms(x_ref[...], g_ref[...]).astype(BF16)
        xn_ref[...] = xn
        if has_extra:
            oe_ref[...] = dot(xn, we_ref[...])

    o_ref[...] = dot(xn_ref[...], w_ref[...]).astype(o_ref.dtype)


def _norm_matmul(x, g, w, *, tm, tn, out_dtype, w_extra=None, n_cols=None, w_is_transposed=False):
    m, k = x.shape
    n = n_cols or w.shape[0 if w_is_transposed else 1]
    has_extra = w_extra is not None
    w_spec = (pl.BlockSpec((tn, k), lambda i, j: (j, 0)) if w_is_transposed
              else pl.BlockSpec((k, tn), lambda i, j: (0, j)))
    in_specs = [pl.BlockSpec((tm, k), lambda i, j: (i, 0)),
                pl.BlockSpec((1, k), lambda i, j: (0, 0)),
                w_spec]
    out_specs = [pl.BlockSpec((tm, tn), lambda i, j: (i, j))]
    out_shape = [jax.ShapeDtypeStruct((m, n), out_dtype)]
    args = [x, g.reshape(1, k), w]
    if has_extra:
        ne = w_extra.shape[0 if w_is_transposed else 1]
        in_specs.append(pl.BlockSpec(w_extra.shape, lambda i, j: (0, 0)))
        out_specs.append(pl.BlockSpec((tm, ne), lambda i, j: (i, 0)))
        out_shape.append(jax.ShapeDtypeStruct((m, ne), F32))
        args.append(w_extra)
    res = pl.pallas_call(
        functools.partial(_norm_matmul_kernel, has_extra=has_extra, w_is_transposed=w_is_transposed),
        grid=(m // tm, n // tn),
        in_specs=in_specs, out_specs=out_specs, out_shape=out_shape,
        scratch_shapes=[pltpu.VMEM((tm, k), BF16)],
        compiler_params=_params("parallel", "arbitrary"),
        name="norm_matmul",
    )(*args)
    return res if has_extra else res[0]


def _matmul_res_kernel(*refs, n_parts):
    a_refs, w_refs = refs[:n_parts], refs[n_parts:2 * n_parts]
    r_ref, o_ref = refs[2 * n_parts:]
    acc = r_ref[...]
    for a_ref, w_ref in zip(a_refs, w_refs):
        acc = acc + _dot(a_ref[...], w_ref[...])
    o_ref[...] = acc


def _matmul_res(a_parts, w, res, *, tm, tn):
    n_parts = len(a_parts)
    m, kp = a_parts[0].shape
    n = w.shape[1]
    a_specs = [pl.BlockSpec((tm, kp), lambda i, j: (i, 0)) for _ in a_parts]
    w_specs = [pl.BlockSpec((kp, tn), lambda i, j, p=p: (p, j)) for p in range(n_parts)]
    return pl.pallas_call(
        functools.partial(_matmul_res_kernel, n_parts=n_parts),
        grid=(m // tm, n // tn),
        in_specs=a_specs + w_specs + [pl.BlockSpec((tm, tn), lambda i, j: (i, j))],
        out_specs=pl.BlockSpec((tm, tn), lambda i, j: (i, j)),
        out_shape=jax.ShapeDtypeStruct((m, n), F32),
        compiler_params=_params("parallel", "arbitrary"),
        name="matmul_res",
    )(*a_parts, *([w] * n_parts), res)


def _ffn_kernel(x_ref, g_ref, wu_ref, wd_ref, gf_ref, o_ref, xn_ref, *, final_norm):
    f = pl.program_id(1)

    @pl.when(f == 0)
    def _():
        xn_ref[...] = _rms(x_ref[...], g_ref[...]).astype(BF16)
        o_ref[...] = jnp.zeros_like(o_ref)

    h = _dot(xn_ref[...], wu_ref[...])
    h = jnp.square(jnp.maximum(h, 0.0)).astype(BF16)
    o_ref[...] += _dot(h, wd_ref[...])

    @pl.when(f == pl.num_programs(1) - 1)
    def _():
        y = x_ref[...] + o_ref[...]
        if final_norm:
            y = _rms(y, gf_ref[...])
        o_ref[...] = y


def _ffn(x, g, w_up, w_down, g_final, *, tm, tf, final_norm):
    m, d = x.shape
    dff = w_up.shape[1]
    return pl.pallas_call(
        functools.partial(_ffn_kernel, final_norm=final_norm),
        grid=(m // tm, dff // tf),
        in_specs=[pl.BlockSpec((tm, d), lambda i, f: (i, 0)),
                  pl.BlockSpec((1, d), lambda i, f: (0, 0)),
                  pl.BlockSpec((d, tf), lambda i, f: (0, f)),
                  pl.BlockSpec((tf, d), lambda i, f: (f, 0)),
                  pl.BlockSpec((1, d), lambda i, f: (0, 0))],
        out_specs=pl.BlockSpec((tm, d), lambda i, f: (i, 0)),
        out_shape=jax.ShapeDtypeStruct((m, d), F32),
        scratch_shapes=[pltpu.VMEM((tm, d), BF16)],
        compiler_params=_params("parallel", "arbitrary"),
        name="ffn",
    )(x, g.reshape(1, d), w_up, w_down, g_final.reshape(1, d))


def _lru_gates(xc, wg, bg, lam):
    g = _dot(xc.astype(BF16), wg) + bg
    r = jax.nn.sigmoid(g[:, :RNN_HD])
    i = jax.nn.sigmoid(g[:, RNN_HD:])
    log_a = -LRU_C * r * jax.nn.softplus(-lam)
    a = jnp.exp(log_a)
    u = jnp.sqrt(-jnp.tanh(log_a) * (a * a + 1.0)) * (i * xc)
    return a, u


def _rglru_kernel(xr_ref, gr_ref, h0_ref, c0_ref, cw_ref, cb_ref, wg_ref, bg_ref, lam_ref, gout_ref,
                  y_ref, hn_ref, cn_ref, xbuf, hc, ybuf, *, tc):
    c = pl.program_id(1)

    @pl.when(c == 0)
    def _():
        xbuf[0:8, :] = jnp.zeros((8, D_RNN), F32)
        xbuf[5:8, :] = c0_ref[0]
        hc[...] = h0_ref[0]

    @pl.when(c > 0)
    def _():
        xbuf[0:8, :] = xbuf[tc:tc + 8, :]

    xbuf[8:8 + tc, :] = xr_ref[0]
    ng = tc // SUBLANES
    sub = lax.broadcasted_iota(jnp.int32, (ng, SUBLANES, RNN_HD), 1)
    ssq = jnp.zeros((tc, 1), F32)
    for h in range(RNN_HEADS):
        sl = slice(h * RNN_HD, (h + 1) * RNN_HD)
        xc = cb_ref[:, sl]
        for j in range(CONV_W):
            xc = xc + cw_ref[j:j + 1, sl] * xbuf[5 + j:5 + j + tc, sl]
        a, u = _lru_gates(xc, wg_ref[h], bg_ref[h], lam_ref[:, sl])
        a = a.reshape(ng, SUBLANES, RNN_HD)
        u = u.reshape(ng, SUBLANES, RNN_HD)
        d = 1
        while d < SUBLANES:
            keep = sub >= d
            a_prev = jnp.where(keep, pltpu.roll(a, d, 1), 1.0)
            u_prev = jnp.where(keep, pltpu.roll(u, d, 1), 0.0)
            u = u + a * u_prev
            a = a * a_prev
            d *= 2
        carry = hc[:, sl]
        for g in range(ng):
            hg = u[g] + a[g] * carry
            carry = hg[SUBLANES - 1:SUBLANES, :]
            ybuf[g * SUBLANES:(g + 1) * SUBLANES, sl] = hg
        hc[:, sl] = carry
        yv = ybuf[:, sl] * jax.nn.gelu(gr_ref[0, :, sl])
        ssq = ssq + jnp.sum(yv * yv, axis=-1, keepdims=True)
        ybuf[:, sl] = yv
    y = ybuf[...] * lax.rsqrt(ssq * (1.0 / D_RNN) + EPS) * gout_ref[...]
    y_ref[0] = y.astype(y_ref.dtype)

    @pl.when(c == pl.num_programs(1) - 1)
    def _():
        hn_ref[0] = hc[...]
        cn_ref[0] = xbuf[tc + 5:tc + 8, :]


def _rglru_prompt(z3, h0, conv0, w, *, tc):
    b, t, _ = z3.shape
    full2 = lambda bi, ci: (0, 0)
    return pl.pallas_call(
        functools.partial(_rglru_kernel, tc=tc),
        grid=(b, t // tc),
        in_specs=[pl.BlockSpec((1, tc, D_RNN), lambda bi, ci: (bi, ci, 0)),
                  pl.BlockSpec((1, tc, D_RNN), lambda bi, ci: (bi, ci, 1)),
                  pl.BlockSpec((1, 1, D_RNN), lambda bi, ci: (bi, 0, 0)),
                  pl.BlockSpec((1, CONV_W - 1, D_RNN), lambda bi, ci: (bi, 0, 0)),
                  pl.BlockSpec((CONV_W, D_RNN), full2),
                  pl.BlockSpec((1, D_RNN), full2),
                  pl.BlockSpec((RNN_HEADS, RNN_HD, 2 * RNN_HD), lambda bi, ci: (0, 0, 0)),
                  pl.BlockSpec((RNN_HEADS, 1, 2 * RNN_HD), lambda bi, ci: (0, 0, 0)),
                  pl.BlockSpec((1, D_RNN), full2),
                  pl.BlockSpec((1, D_RNN), full2)],
        out_specs=[pl.BlockSpec((1, tc, D_RNN), lambda bi, ci: (bi, ci, 0)),
                   pl.BlockSpec((1, 1, D_RNN), lambda bi, ci: (bi, 0, 0)),
                   pl.BlockSpec((1, CONV_W - 1, D_RNN), lambda bi, ci: (bi, 0, 0))],
        out_shape=[jax.ShapeDtypeStruct((b, t, D_RNN), BF16),
                   jax.ShapeDtypeStruct((b, 1, D_RNN), F32),
                   jax.ShapeDtypeStruct((b, CONV_W - 1, D_RNN), F32)],
        scratch_shapes=[pltpu.VMEM((tc + 8, D_RNN), F32),
                        pltpu.VMEM((1, D_RNN), F32),
                        pltpu.VMEM((tc, D_RNN), F32)],
        compiler_params=_params("parallel", "arbitrary"),
        name="rglru_prompt",
    )(z3, z3, h0, conv0, w['conv_rnn_w'], w['conv_rnn_b'], w['lru_wg'], w['lru_bg'],
      w['lru_lambda'], w['g_rnn_out'])


def _mlstm_kernel(u_ref, v_ref, og_ref, zg_ref, gb_ref, c0_ref, n0_ref, m0_ref, cv0_ref,
                  cw_ref, cb_ref, wqk_ref, gout_ref,
                  y_ref, cn_ref, nn_ref, mn_ref, cvn_ref, ubuf, cst, nst, mst):
    c = pl.program_id(1)
    L = CHUNK

    @pl.when(c == 0)
    def _():
        ubuf[0:8, :] = jnp.zeros((8, D_MLSTM), F32)
        ubuf[5:8, :] = cv0_ref[0]
        cst[...] = c0_ref[0]
        nst[...] = n0_ref[0]
        mst[...] = m0_ref[0]

    @pl.when(c > 0)
    def _():
        ubuf[0:8, :] = ubuf[L:L + 8, :]

    ubuf[8:8 + L, :] = u_ref[0]

    zg = zg_ref[0] + gb_ref[...]
    zgt = zg.T
    ti = lax.broadcasted_iota(jnp.int32, (L, L), 0)
    si = lax.broadcasted_iota(jnp.int32, (L, L), 1)
    causal = si <= ti
    anti = ti <= si

    for h in range(MLSTM_HEADS):
        sl = slice(h * MLSTM_HD, (h + 1) * MLSTM_HD)
        uc = cb_ref[:, sl]
        for j in range(CONV_W):
            uc = uc + cw_ref[j:j + 1, sl] * ubuf[5 + j:5 + j + L, sl]
        uc = jax.nn.silu(uc)
        qk = _dot(uc.astype(BF16), wqk_ref[h])
        q = qk[:, :MLSTM_HD]
        k = qk[:, MLSTM_HD:] * (MLSTM_HD ** -0.5)
        v = v_ref[0, :, sl]
        qb, kb = q.astype(BF16), k.astype(BF16)

        icol = zg[:, h:h + 1]
        fcol = jax.nn.log_sigmoid(zg[:, MLSTM_HEADS + h:MLSTM_HEADS + h + 1])
        irow = zgt[h:h + 1, :]
        frow = jax.nn.log_sigmoid(zgt[MLSTM_HEADS + h:MLSTM_HEADS + h + 1, :])
        bcol = jnp.sum(jnp.where(causal, frow, 0.0), axis=1, keepdims=True)
        brow = jnp.sum(jnp.where(anti, fcol, 0.0), axis=0, keepdims=True)
        dmat = jnp.where(causal, irow + bcol - brow, -jnp.inf)
        m_prev = mst[h, :, 0:1]
        inter = bcol + m_prev
        m_t = jnp.maximum(inter, jnp.max(dmat, axis=1, keepdims=True))
        w_inter = jnp.exp(inter - m_t)
        s = lax.dot_general(qb, kb, (((1,), (1,)), ((), ())), preferred_element_type=F32) * jnp.exp(dmat - m_t)
        cmat = cst[h]
        nrow = nst[h]
        cq = lax.dot_general(qb, cmat.astype(BF16), (((1,), (1,)), ((), ())), preferred_element_type=F32)
        num = w_inter * cq + _dot(s.astype(BF16), v.astype(BF16))
        den = w_inter * jnp.sum(q * nrow, axis=1, keepdims=True) + jnp.sum(s, axis=1, keepdims=True)
        hm = num / jnp.maximum(jnp.abs(den), jnp.exp(-m_t))

        m_new = m_t[L - 1:L, :]
        b_last = bcol[L - 1:L, :]
        g_state = jnp.exp(b_last + m_prev - m_new)
        g_in = jnp.exp(icol + b_last - bcol - m_new)
        cst[h] = g_state * cmat + lax.dot_general((g_in * v).astype(BF16), kb, (((0,), (0,)), ((), ())),
                                                  preferred_element_type=F32)
        nst[h] = g_state * nrow + jnp.sum(g_in * k, axis=0, keepdims=True)
        mst[h] = jnp.broadcast_to(m_new, (1, N_GATE_PAD))

        y = _rms(hm, gout_ref[...]) * jax.nn.sigmoid(og_ref[0, :, sl])
        y_ref[0, :, sl] = y.astype(y_ref.dtype)

    @pl.when(c == pl.num_programs(1) - 1)
    def _():
        cn_ref[0] = cst[...]
        nn_ref[0] = nst[...]
        mn_ref[0] = mst[...]
        cvn_ref[0] = ubuf[L + 5:L + 8, :]


def _mlstm_prompt(z3, zg3, c0, n0, m0, conv0, w):
    b, t, _ = z3.shape
    nc = t // CHUNK
    u_blk, v_blk, o_blk = 2 * D_RNN // D_MLSTM, 3 * D_RNN // D_MLSTM, 4 * D_RNN // D_MLSTM
    zcol = lambda blk: pl.BlockSpec((1, CHUNK, D_MLSTM), lambda bi, ci: (bi, ci, blk))
    state4 = lambda *minor: pl.BlockSpec((1, MLSTM_HEADS) + minor, lambda bi, ci: (bi, 0, 0, 0))
    conv_spec = pl.BlockSpec((1, CONV_W - 1, D_MLSTM), lambda bi, ci: (bi, 0, 0))
    const2 = lambda shape: pl.BlockSpec(shape, lambda bi, ci: (0, 0))
    return pl.pallas_call(
        _mlstm_kernel,
        grid=(b, nc),
        in_specs=[zcol(u_blk), zcol(v_blk), zcol(o_blk),
                  pl.BlockSpec((1, CHUNK, N_GATE_PAD), lambda bi, ci: (bi, ci, 0)),
                  const2((1, N_GATE_PAD)),
                  state4(MLSTM_HD, MLSTM_HD), state4(1, MLSTM_HD), state4(1, N_GATE_PAD), conv_spec,
                  const2((CONV_W, D_MLSTM)), const2((1, D_MLSTM)),
                  pl.BlockSpec((MLSTM_HEADS, MLSTM_HD, 2 * MLSTM_HD), lambda bi, ci: (0, 0, 0)),
                  const2((1, MLSTM_HD))],
        out_specs=[pl.BlockSpec((1, CHUNK, D_MLSTM), lambda bi, ci: (bi, ci, 0)),
                   state4(MLSTM_HD, MLSTM_HD), state4(1, MLSTM_HD), state4(1, N_GATE_PAD), conv_spec],
        out_shape=[jax.ShapeDtypeStruct((b, t, D_MLSTM), BF16),
                   jax.ShapeDtypeStruct((b, MLSTM_HEADS, MLSTM_HD, MLSTM_HD), F32),
                   jax.ShapeDtypeStruct((b, MLSTM_HEADS, 1, MLSTM_HD), F32),
                   jax.ShapeDtypeStruct((b, MLSTM_HEADS, 1, N_GATE_PAD), F32),
                   jax.ShapeDtypeStruct((b, CONV_W - 1, D_MLSTM), F32)],
        scratch_shapes=[pltpu.VMEM((CHUNK + 8, D_MLSTM), F32),
                        pltpu.VMEM((MLSTM_HEADS, MLSTM_HD, MLSTM_HD), F32),
                        pltpu.VMEM((MLSTM_HEADS, 1, MLSTM_HD), F32),
                        pltpu.VMEM((MLSTM_HEADS, 1, N_GATE_PAD), F32)],
        compiler_params=_params("parallel", "arbitrary"),
        name="mlstm_prompt",
    )(z3, z3, z3, zg3, w['gate_bias'], c0, n0, m0, conv0, w['conv_ml_w'], w['conv_ml_b'],
      w['ml_wqk'], w['g_ml_out'])


def _xattn_kernel(q_ref, k_ref, v_ref, o_ref):
    for h in range(X_HEADS):
        sl = slice(h * X_HD, (h + 1) * X_HD)
        s = lax.dot_general(q_ref[0, :, sl], k_ref[0, :, sl].astype(BF16), (((1,), (1,)), ((), ())),
                            preferred_element_type=F32) * (X_HD ** -0.5)
        e = jnp.exp(s - jnp.max(s, axis=-1, keepdims=True))
        p = e / jnp.sum(e, axis=-1, keepdims=True)
        o_ref[0, :, sl] = _dot(p.astype(BF16), v_ref[0, :, sl].astype(BF16)).astype(o_ref.dtype)


def _xattn_prompt(q3, mk3, mv3, *, tq):
    b, t, d = q3.shape
    return pl.pallas_call(
        _xattn_kernel,
        grid=(b, t // tq),
        in_specs=[pl.BlockSpec((1, tq, d), lambda bi, ti: (bi, ti, 0)),
                  pl.BlockSpec((1, N_MEM, d), lambda bi, ti: (bi, 0, 0)),
                  pl.BlockSpec((1, N_MEM, d), lambda bi, ti: (bi, 0, 0))],
        out_specs=pl.BlockSpec((1, tq, d), lambda bi, ti: (bi, ti, 0)),
        out_shape=jax.ShapeDtypeStruct((b, t, d), BF16),
        compiler_params=_params("parallel", "arbitrary"),
        name="xattn_prompt",
    )(q3, mk3, mv3)


XS_ROWS = 2
XS_SUB = 2 * X_HEADS
XS_PAIRS = X_HD // (2 * 128)


def _pack_heads(x):
    lead = x.shape[:-1]
    x = x.reshape(*lead, X_HEADS, 2 * XS_PAIRS, 128)
    x = jnp.swapaxes(x, -3, -2)
    return x.reshape(*lead, XS_PAIRS, XS_SUB, 128)


def _unpack_heads(x):
    lead = x.shape[:-3]
    x = x.reshape(*lead, 2 * XS_PAIRS, X_HEADS, 128)
    x = jnp.swapaxes(x, -3, -2)
    return x.reshape(*lead, X_HEADS * X_HD)


def _xattn_sample_kernel(q_ref, k_ref, v_ref, o_ref):
    for r in range(XS_ROWS):
        t = jnp.sum(k_ref[r] * q_ref[r], axis=1)
        t = t + pltpu.roll(t, X_HEADS, 1)
        s = jnp.sum(t, axis=-1, keepdims=True) * (X_HD ** -0.5)
        e = jnp.exp(s - jnp.max(s, axis=0, keepdims=True))
        p = e / jnp.sum(e, axis=0, keepdims=True)
        o_ref[r] = jnp.sum(p[:, None] * v_ref[r], axis=0)


def _xattn_sample(q, ck, cv):
    b = q.shape[0]
    kv_spec = pl.BlockSpec((XS_ROWS, N_MEM, XS_PAIRS, XS_SUB, 128), lambda bi: (bi, 0, 0, 0, 0))
    q_spec = pl.BlockSpec((XS_ROWS, XS_PAIRS, XS_SUB, 128), lambda bi: (bi, 0, 0, 0))
    o = pl.pallas_call(
        _xattn_sample_kernel,
        grid=(b // XS_ROWS,),
        in_specs=[q_spec, kv_spec, kv_spec],
        out_specs=q_spec,
        out_shape=jax.ShapeDtypeStruct((b, XS_PAIRS, XS_SUB, 128), F32),
        compiler_params=_params("parallel"),
        name="xattn_sample",
    )(_pack_heads(q), _pack_heads(ck.reshape(b, N_MEM, X_HEADS * X_HD)),
      _pack_heads(cv.reshape(b, N_MEM, X_HEADS * X_HD)))
    return _unpack_heads(o)


def _smix_rows_kernel(z_ref, h0_ref, rc_ref, mc_ref, cwr_ref, cbr_ref, wg_ref, bg_ref, lam_ref, gout_ref,
                      cwm_ref, cbm_ref, wqk_ref,
                      yr_ref, hn_ref, rcn_ref, mcn_ref, q_ref, k_ref, ybuf):
    nb = z_ref.shape[0]
    ssq = jnp.zeros((nb, 1), F32)
    for h in range(RNN_HEADS):
        sl = slice(h * RNN_HD, (h + 1) * RNN_HD)
        xc = cbr_ref[:, sl] + cwr_ref[CONV_W - 1:CONV_W, sl] * z_ref[:, sl]
        for j in range(CONV_W - 1):
            xc = xc + cwr_ref[j:j + 1, sl] * rc_ref[:, j * D_RNN + h * RNN_HD:j * D_RNN + (h + 1) * RNN_HD]
        a, u = _lru_gates(xc, wg_ref[h], bg_ref[h], lam_ref[:, sl])
        hs = a * h0_ref[:, sl] + u
        hn_ref[:, sl] = hs
        yv = hs * jax.nn.gelu(z_ref[:, D_RNN + h * RNN_HD:D_RNN + (h + 1) * RNN_HD])
        ssq = ssq + jnp.sum(yv * yv, axis=-1, keepdims=True)
        ybuf[:, sl] = yv
    yr_ref[...] = (ybuf[...] * lax.rsqrt(ssq * (1.0 / D_RNN) + EPS) * gout_ref[...]).astype(yr_ref.dtype)
    rcn_ref[:, 0:2 * D_RNN] = rc_ref[:, D_RNN:3 * D_RNN]
    rcn_ref[:, 2 * D_RNN:3 * D_RNN] = z_ref[:, 0:D_RNN]

    for h in range(MLSTM_HEADS):
        sl = slice(h * MLSTM_HD, (h + 1) * MLSTM_HD)
        uc = cbm_ref[:, sl] + cwm_ref[CONV_W - 1:CONV_W, sl] * z_ref[:, 2 * D_RNN + h * MLSTM_HD:2 * D_RNN + (h + 1) * MLSTM_HD]
        for j in range(CONV_W - 1):
            uc = uc + cwm_ref[j:j + 1, sl] * mc_ref[:, j * D_MLSTM + h * MLSTM_HD:j * D_MLSTM + (h + 1) * MLSTM_HD]
        uc = jax.nn.silu(uc)
        qk = _dot(uc.astype(BF16), wqk_ref[h])
        q_ref[:, sl] = qk[:, :MLSTM_HD]
        k_ref[:, sl] = qk[:, MLSTM_HD:] * (MLSTM_HD ** -0.5)
    mcn_ref[:, 0:2 * D_MLSTM] = mc_ref[:, D_MLSTM:3 * D_MLSTM]
    mcn_ref[:, 2 * D_MLSTM:3 * D_MLSTM] = z_ref[:, 2 * D_RNN:2 * D_RNN + D_MLSTM]


def _smix_rows(z, h0, rconv, mconv, w):
    nb = z.shape[0]
    outs = [jax.ShapeDtypeStruct((nb, D_RNN), BF16),
            jax.ShapeDtypeStruct((nb, D_RNN), F32),
            jax.ShapeDtypeStruct((nb, 3 * D_RNN), F32),
            jax.ShapeDtypeStruct((nb, 3 * D_MLSTM), F32),
            jax.ShapeDtypeStruct((nb, D_MLSTM), F32),
            jax.ShapeDtypeStruct((nb, D_MLSTM), F32)]
    return pl.pallas_call(
        _smix_rows_kernel,
        out_shape=outs,
        scratch_shapes=[pltpu.VMEM((nb, D_RNN), F32)],
        compiler_params=pltpu.CompilerParams(vmem_limit_bytes=V7X_VMEM_LIMIT_BYTES),
        name="smix_rows",
    )(z, h0, rconv, mconv, w['conv_rnn_w'], w['conv_rnn_b'], w['lru_wg'], w['lru_bg'], w['lru_lambda'],
      w['g_rnn_out'], w['conv_ml_w'], w['conv_ml_b'], w['ml_wqk'])


SMIX_ROWS = 4
MXU_MIN_ROWS = 8


def _smix_state_kernel(c_ref, q_ref, k_ref, n_ref, v_ref, og_ref, zg_ref, gb_ref, m_ref, gout_ref,
                       cn_ref, nn_ref, mn_ref, y_ref):
    lane_g = lax.broadcasted_iota(jnp.int32, (1, N_GATE_PAD), 1)
    first_row = lax.broadcasted_iota(jnp.int32, (MXU_MIN_ROWS, MLSTM_HD), 0) == 0
    for r in range(SMIX_ROWS):
        zg = zg_ref[r] + gb_ref[...]
        m_out = jnp.zeros((1, N_GATE_PAD), F32)
        for h in range(MLSTM_HEADS):
            sl = slice(h * MLSTM_HD, (h + 1) * MLSTM_HD)
            q = q_ref[r, :, sl]
            k = k_ref[r, :, sl]
            n = n_ref[r, :, sl]
            v = v_ref[r, :, sl]
            ig = zg[:, h:h + 1]
            lf = jax.nn.log_sigmoid(zg[:, MLSTM_HEADS + h:MLSTM_HEADS + h + 1])
            m_prev = m_ref[r, :, h:h + 1]
            m_t = jnp.maximum(lf + m_prev, ig)
            w_inter = jnp.exp(lf + m_prev - m_t)
            g_in = jnp.exp(ig - m_t)
            s = jnp.sum(q * k, axis=1, keepdims=True) * g_in
            den = w_inter * jnp.sum(n * q, axis=1, keepdims=True) + s
            denom = jnp.maximum(jnp.abs(den), jnp.exp(-m_t))
            cmat = c_ref[r, h]
            q8 = jnp.broadcast_to(q, (MXU_MIN_ROWS, MLSTM_HD)).astype(BF16)
            cq = lax.dot_general(q8, cmat.astype(BF16), (((1,), (1,)), ((), ())),
                                 preferred_element_type=F32)[0:1, :]
            hm = (w_inter * cq + s * v) / denom
            gv8 = jnp.where(first_row, jnp.broadcast_to(g_in * v, (MXU_MIN_ROWS, MLSTM_HD)), 0.0).astype(BF16)
            k8 = jnp.broadcast_to(k, (MXU_MIN_ROWS, MLSTM_HD)).astype(BF16)
            outer = lax.dot_general(gv8, k8, (((0,), (0,)), ((), ())), preferred_element_type=F32)
            cn_ref[r, h] = w_inter * cmat + outer
            nn_ref[r, :, sl] = w_inter * n + g_in * k
            m_out = jnp.where(lane_g == h, m_t, m_out)
            y = _rms(hm, gout_ref[...]) * jax.nn.sigmoid(og_ref[r, :, sl])
            y_ref[r, :, sl] = y.astype(y_ref.dtype)
        mn_ref[r] = m_out


def _smix_state(c, q3, k3, n3, z3, zg3, gate_bias, m3, gout):
    nb = c.shape[0]
    row = lambda width, col=0: pl.BlockSpec((SMIX_ROWS, 1, width), lambda bi: (bi, 0, col))
    cblk = pl.BlockSpec((SMIX_ROWS, MLSTM_HEADS, MLSTM_HD, MLSTM_HD), lambda bi: (bi, 0, 0, 0))
    return pl.pallas_call(
        _smix_state_kernel,
        grid=(nb // SMIX_ROWS,),
        in_specs=[cblk, row(D_MLSTM), row(D_MLSTM), row(D_MLSTM),
                  row(D_MLSTM, 3 * D_RNN // D_MLSTM), row(D_MLSTM, 4 * D_RNN // D_MLSTM),
                  row(N_GATE_PAD),
                  pl.BlockSpec((1, N_GATE_PAD), lambda bi: (0, 0)),
                  row(MLSTM_HEADS),
                  pl.BlockSpec((1, MLSTM_HD), lambda bi: (0, 0))],
        out_specs=[cblk, row(D_MLSTM), row(N_GATE_PAD), row(D_MLSTM)],
        out_shape=[jax.ShapeDtypeStruct(c.shape, F32),
                   jax.ShapeDtypeStruct((nb, 1, D_MLSTM), F32),
                   jax.ShapeDtypeStruct((nb, 1, N_GATE_PAD), F32),
                   jax.ShapeDtypeStruct((nb, 1, D_MLSTM), F32)],
        compiler_params=_params("parallel"),
        name="smix_state",
    )(c, q3, k3, n3, z3, z3, zg3, gate_bias, m3, gout)


def _prompt_block(x, mem, w, g_final, final_norm):
    b, t, d = x.shape
    m = b * t
    tm = 1024
    x2 = x.reshape(m, d)
    z, zg = _norm_matmul(x2, w['g_mix'], w['w_in_main'], tm=tm, tn=1024, out_dtype=F32, w_extra=w['w_in_gate'],
                          n_cols=D_MAIN, w_is_transposed=True)
    z3 = z.reshape(b, t, D_MAIN)
    zg3 = zg.reshape(b, t, N_GATE_PAD)
    h0 = jnp.zeros((b, 1, D_RNN), F32)
    conv0 = jnp.zeros((b, CONV_W - 1, D_RNN), F32)
    y_rnn, rg_h, rg_conv = _rglru_prompt(z3, h0, conv0, w, tc=256)
    c0 = jnp.zeros((b, MLSTM_HEADS, MLSTM_HD, MLSTM_HD), F32)
    n0 = jnp.zeros((b, MLSTM_HEADS, 1, MLSTM_HD), F32)
    m0 = jnp.zeros((b, MLSTM_HEADS, 1, N_GATE_PAD), F32)
    mconv0 = jnp.zeros((b, CONV_W - 1, D_MLSTM), F32)
    y_ml, c_new, n_new, m_new, ml_conv = _mlstm_prompt(z3, zg3, c0, n0, m0, mconv0, w)
    x1 = _matmul_res([y_rnn.reshape(m, D_RNN), y_ml.reshape(m, D_MLSTM)], w['w_out'], x2, tm=tm, tn=1024)

    mem2 = mem.reshape(b * N_MEM, d)
    mk = _norm_matmul(mem2, w['g_mem'], w['w_mk'], tm=b * N_MEM, tn=1024, out_dtype=F32)
    mv = _norm_matmul(mem2, w['g_mem'], w['w_mv'], tm=b * N_MEM, tn=1024, out_dtype=F32)
    q = _norm_matmul(x1, w['g_xattn'], w['w_cq'], tm=tm, tn=1024, out_dtype=BF16)
    o = _xattn_prompt(q.reshape(b, t, d), mk.reshape(b, N_MEM, d), mv.reshape(b, N_MEM, d), tq=512)
    x2_ = _matmul_res([o.reshape(m, d)], w['w_co'], x1, tm=tm, tn=1024)
    xo = _ffn(x2_, w['g_ffn'], w['w_up'], w['w_down'], g_final, tm=tm, tf=512, final_norm=final_norm)
    new = (rg_h.reshape(b, D_RNN), rg_conv, c_new, n_new.reshape(b, MLSTM_HEADS, MLSTM_HD),
           m_new[:, :, 0, 0], ml_conv,
           mk.reshape(b, N_MEM, X_HEADS, X_HD), mv.reshape(b, N_MEM, X_HEADS, X_HD))
    return xo.reshape(b, t, d), new


def _sample_block(x, rg_h, rg_conv, c, n, mst, ml_conv, ck, cv, w, g_final, final_norm):
    nb, t, d = x.shape
    x2 = x.reshape(nb, d)
    z, zg = _norm_matmul(x2, w['g_mix'], w['w_in_main'], tm=nb, tn=1024, out_dtype=F32, w_extra=w['w_in_gate'],
                          n_cols=D_MAIN, w_is_transposed=True)
    y_rnn, h_new, rconv_new, mconv_new, q, k = _smix_rows(
        z, rg_h, rg_conv.reshape(nb, 3 * D_RNN), ml_conv.reshape(nb, 3 * D_MLSTM), w)
    c_new, n_new, m_new, y_ml = _smix_state(
        c, q.reshape(nb, 1, D_MLSTM), k.reshape(nb, 1, D_MLSTM), n.reshape(nb, 1, D_MLSTM),
        z.reshape(nb, 1, D_MAIN), zg.reshape(nb, 1, N_GATE_PAD), w['gate_bias'],
        mst.reshape(nb, 1, MLSTM_HEADS), w['g_ml_out'])
    x1 = _matmul_res([y_rnn, y_ml.reshape(nb, D_MLSTM).astype(BF16)], w['w_out'], x2, tm=nb, tn=1024)
    q = _norm_matmul(x1, w['g_xattn'], w['w_cq'], tm=nb, tn=1024, out_dtype=F32)
    o = _xattn_sample(q, ck, cv).astype(BF16)
    x2_ = _matmul_res([o], w['w_co'], x1, tm=nb, tn=1024)
    xo = _ffn(x2_, w['g_ffn'], w['w_up'], w['w_down'], g_final, tm=nb, tf=512, final_norm=final_norm)
    new = (h_new, rconv_new.reshape(nb, CONV_W - 1, D_RNN), c_new, n_new.reshape(nb, MLSTM_HEADS, MLSTM_HD),
           m_new[:, 0, :MLSTM_HEADS], mconv_new.reshape(nb, CONV_W - 1, D_MLSTM))
    return xo.reshape(nb, t, d), new


def _layer_weights(l, g_mix, w_in, conv_rnn_w, conv_rnn_b, lru_wa, lru_ba, lru_wx, lru_bx, lru_lambda,
                   g_rnn_out, conv_ml_w, conv_ml_b, ml_wq, ml_wk, ml_bi, ml_bf, g_ml_out, w_out,
                   g_xattn, g_mem, w_cq, w_mk, w_mv, w_co, g_ffn, w_up, w_down):
    n_gate = 2 * MLSTM_HEADS
    w_in_t = jnp.swapaxes(w_in[l], 0, 1)
    w_gate_t = jnp.pad(w_in_t[D_MAIN:], ((0, N_GATE_PAD - n_gate), (0, 0)))
    gate_bias = jnp.pad(jnp.concatenate([ml_bi[l], ml_bf[l]]), (0, N_GATE_PAD - n_gate))
    return dict(
        g_mix=g_mix[l], w_in_main=w_in_t, w_in_gate=w_gate_t.astype(BF16),
        conv_rnn_w=conv_rnn_w[l], conv_rnn_b=conv_rnn_b[l].reshape(1, D_RNN),
        lru_wg=jnp.concatenate([lru_wa[l], lru_wx[l]], axis=-1).astype(BF16),
        lru_bg=jnp.concatenate([lru_ba[l], lru_bx[l]], axis=-1).reshape(RNN_HEADS, 1, 2 * RNN_HD),
        lru_lambda=lru_lambda[l].reshape(1, D_RNN), g_rnn_out=g_rnn_out[l].reshape(1, D_RNN),
        conv_ml_w=conv_ml_w[l], conv_ml_b=conv_ml_b[l].reshape(1, D_MLSTM),
        ml_wqk=jnp.concatenate([ml_wq[l], ml_wk[l]], axis=-1).astype(BF16),
        gate_bias=gate_bias.reshape(1, N_GATE_PAD), g_ml_out=g_ml_out[l].reshape(1, MLSTM_HD),
        w_out=w_out[l], g_xattn=g_xattn[l], g_mem=g_mem[l], w_cq=w_cq[l], w_mk=w_mk[l], w_mv=w_mv[l],
        w_co=w_co[l], g_ffn=g_ffn[l], w_up=w_up[l].astype(BF16), w_down=w_down[l].astype(BF16))


def kernel(x_prompt, x_sample, mem_prompt, state_rglru_h, state_rglru_conv, state_mlstm_C, state_mlstm_n, state_mlstm_m, state_mlstm_conv, cache_mem_k, cache_mem_v, g_mix, w_in, conv_rnn_w, conv_rnn_b, lru_wa, lru_ba, lru_wx, lru_bx, lru_lambda, g_rnn_out, conv_ml_w, conv_ml_b, ml_wq, ml_wk, ml_bi, ml_bf, g_ml_out, w_out, g_xattn, g_mem, w_cq, w_mk, w_mv, w_co, g_ffn, w_up, w_down, g_final):
    depth = w_in.shape[0]
    xp, xs = x_prompt, x_sample
    p_out = [[] for _ in range(8)]
    s_out = [[] for _ in range(6)]
    for l in range(depth):
        w = _layer_weights(l, g_mix, w_in, conv_rnn_w, conv_rnn_b, lru_wa, lru_ba, lru_wx, lru_bx, lru_lambda,
                           g_rnn_out, conv_ml_w, conv_ml_b, ml_wq, ml_wk, ml_bi, ml_bf, g_ml_out, w_out,
                           g_xattn, g_mem, w_cq, w_mk, w_mv, w_co, g_ffn, w_up, w_down)
        last = l == depth - 1
        xp, new_p = _prompt_block(xp, mem_prompt, w, g_final, last)
        for j, a in enumerate(new_p):
            p_out[j].append(a)
        xs, new_s = _sample_block(xs, state_rglru_h[l], state_rglru_conv[l], state_mlstm_C[l], state_mlstm_n[l],
                                  state_mlstm_m[l], state_mlstm_conv[l], cache_mem_k[l], cache_mem_v[l],
                                  w, g_final, last)
        for j, a in enumerate(new_s):
            s_out[j].append(a)
    P = [jnp.stack(a, axis=0) for a in p_out]
    S = [jnp.stack(a, axis=0) for a in s_out]
    return (xp, xs, P[0], P[1], P[2], P[3], P[4], P[5], P[6], P[7],
            S[0], S[1], S[2], S[3], S[4], S[5])
```

```python
import functools

import jax
import jax.numpy as jnp
from jax import lax
from jax.experimental import pallas as pl
from jax.experimental.pallas import tpu as pltpu

F32 = jnp.float32
BF16 = jnp.bfloat16

D_MODEL = 2048
D_RNN = 1024
RNN_HEADS = 8
RNN_HD = 128
CONV_W = 4
LRU_C = 8.0
D_MLSTM = 1024
MLSTM_HEADS = 4
MLSTM_HD = 256
CHUNK = 128
N_MEM = 256
X_HEADS = 4
X_HD = 512
D_FF = 8192
EPS = 1e-6
D_MAIN = 5 * 1024
N_GATE_PAD = 128

V7X_VMEM_LIMIT_BYTES = 56 * 1024 * 1024
SUBLANES = 8


def _params(*sem):
    return pltpu.CompilerParams(dimension_semantics=sem, vmem_limit_bytes=V7X_VMEM_LIMIT_BYTES)


def _rms(x, g):
    ms = jnp.mean(x * x, axis=-1, keepdims=True)
    return x * lax.rsqrt(ms + EPS) * g


def _dot(a, b):
    return jnp.dot(a.astype(BF16), b.astype(BF16), preferred_element_type=F32)


def _dot_t(a, bt):
    return lax.dot_general(a.astype(BF16), bt.astype(BF16), (((1,), (1,)), ((), ())), preferred_element_type=F32)


def _with_rider(host_body, rider, n_in, n_out):
    if rider is None:
        return host_body
    r_in, r_out = len(rider['in_specs']), len(rider['out_specs'])

    def body(*refs):
        host_in, refs = refs[:n_in], refs[n_in:]
        rider_in, refs = refs[:r_in], refs[r_in:]
        host_out, refs = refs[:n_out], refs[n_out:]
        rider_out, scratch = refs[:r_out], refs[r_out:]
        units = [u for r in range(rider['rows']) for u in rider['units'](*rider_in, *rider_out, r)]
        host_body(*host_in, *host_out, *scratch, rider=units)
    return body


def _run_rider_share(rider, i, n):
    if rider is None:
        return
    for unit in rider[i * len(rider) // n:(i + 1) * len(rider) // n]:
        unit()


def _rider_parts(rider):
    if rider is None:
        return [], [], [], []
    return rider['in_specs'], rider['out_specs'], rider['out_shape'], rider['args']


def _norm_matmul_kernel(*refs, has_extra, w_is_transposed):
    if has_extra:
        x_ref, g_ref, w_ref, we_ref, o_ref, oe_ref, xn_ref = refs
    else:
        x_ref, g_ref, w_ref, o_ref, xn_ref = refs
    dot = _dot_t if w_is_transposed else _dot

    @pl.when(pl.program_id(1) == 0)
    def _():
        xn = _rms(x_ref[...], g_ref[...]).astype(BF16)
        xn_ref[...] = xn
        if has_extra:
            oe_ref[...] = dot(xn, we_ref[...])

    o_ref[...] = dot(xn_ref[...], w_ref[...]).astype(o_ref.dtype)


def _norm_matmul(x, g, w, *, tm, tn, out_dtype, w_extra=None, n_cols=None, w_is_transposed=False):
    m, k = x.shape
    n = n_cols or w.shape[0 if w_is_transposed else 1]
    has_extra = w_extra is not None
    w_spec = (pl.BlockSpec((tn, k), lambda i, j: (j, 0)) if w_is_transposed
              else pl.BlockSpec((k, tn), lambda i, j: (0, j)))
    in_specs = [pl.BlockSpec((tm, k), lambda i, j: (i, 0)),
                pl.BlockSpec((1, k), lambda i, j: (0, 0)),
                w_spec]
    out_specs = [pl.BlockSpec((tm, tn), lambda i, j: (i, j))]
    out_shape = [jax.ShapeDtypeStruct((m, n), out_dtype)]
    args = [x, g.reshape(1, k), w]
    if has_extra:
        ne = w_extra.shape[0 if w_is_transposed else 1]
        in_specs.append(pl.BlockSpec(w_extra.shape, lambda i, j: (0, 0)))
        out_specs.append(pl.BlockSpec((tm, ne), lambda i, j: (i, 0)))
        out_shape.append(jax.ShapeDtypeStruct((m, ne), F32))
        args.append(w_extra)
    res = pl.pallas_call(
        functools.partial(_norm_matmul_kernel, has_extra=has_extra, w_is_transposed=w_is_transposed),
        grid=(m // tm, n // tn),
        in_specs=in_specs, out_specs=out_specs, out_shape=out_shape,
        scratch_shapes=[pltpu.VMEM((tm, k), BF16)],
        compiler_params=_params("parallel", "arbitrary"),
        name="norm_matmul",
    )(*args)
    return res if has_extra else res[0]


def _matmul_res_kernel(*refs, n_parts):
    a_refs, w_refs = refs[:n_parts], refs[n_parts:2 * n_parts]
    r_ref, o_ref = refs[2 * n_parts:]
    acc = r_ref[...]
    for a_ref, w_ref in zip(a_refs, w_refs):
        acc = acc + _dot(a_ref[...], w_ref[...])
    o_ref[...] = acc


def _matmul_res(a_parts, w, res, *, tm, tn):
    n_parts = len(a_parts)
    m, kp = a_parts[0].shape
    n = w.shape[1]
    a_specs = [pl.BlockSpec((tm, kp), lambda i, j: (i, 0)) for _ in a_parts]
    w_specs = [pl.BlockSpec((kp, tn), lambda i, j, p=p: (p, j)) for p in range(n_parts)]
    return pl.pallas_call(
        functools.partial(_matmul_res_kernel, n_parts=n_parts),
        grid=(m // tm, n // tn),
        in_specs=a_specs + w_specs + [pl.BlockSpec((tm, tn), lambda i, j: (i, j))],
        out_specs=pl.BlockSpec((tm, tn), lambda i, j: (i, j)),
        out_shape=jax.ShapeDtypeStruct((m, n), F32),
        compiler_params=_params("parallel", "arbitrary"),
        name="matmul_res",
    )(*a_parts, *([w] * n_parts), res)


def _ffn_kernel(x_ref, g_ref, wu_ref, wd_ref, gf_ref, o_ref, xn_ref, *, final_norm):
    f = pl.program_id(1)

    @pl.when(f == 0)
    def _():
        xn_ref[...] = _rms(x_ref[...], g_ref[...]).astype(BF16)
        o_ref[...] = jnp.zeros_like(o_ref)

    h = _dot(xn_ref[...], wu_ref[...])
    h = jnp.square(jnp.maximum(h, 0.0)).astype(BF16)
    o_ref[...] += _dot(h, wd_ref[...])

    @pl.when(f == pl.num_programs(1) - 1)
    def _():
        y = x_ref[...] + o_ref[...]
        if final_norm:
            y = _rms(y, gf_ref[...])
        o_ref[...] = y


def _ffn(x, g, w_up, w_down, g_final, *, tm, tf, final_norm):
    m, d = x.shape
    dff = w_up.shape[1]
    return pl.pallas_call(
        functools.partial(_ffn_kernel, final_norm=final_norm),
        grid=(m // tm, dff // tf),
        in_specs=[pl.BlockSpec((tm, d), lambda i, f: (i, 0)),
                  pl.BlockSpec((1, d), lambda i, f: (0, 0)),
                  pl.BlockSpec((d, tf), lambda i, f: (0, f)),
                  pl.BlockSpec((tf, d), lambda i, f: (f, 0)),
                  pl.BlockSpec((1, d), lambda i, f: (0, 0))],
        out_specs=pl.BlockSpec((tm, d), lambda i, f: (i, 0)),
        out_shape=jax.ShapeDtypeStruct((m, d), F32),
        scratch_shapes=[pltpu.VMEM((tm, d), BF16)],
        compiler_params=_params("parallel", "arbitrary"),
        name="ffn",
    )(x, g.reshape(1, d), w_up, w_down, g_final.reshape(1, d))


def _lru_gates(xc, wg, bg, lam):
    g = _dot(xc.astype(BF16), wg) + bg
    r = jax.nn.sigmoid(g[:, :RNN_HD])
    i = jax.nn.sigmoid(g[:, RNN_HD:])
    log_a = -LRU_C * r * jax.nn.softplus(-lam)
    a = jnp.exp(log_a)
    u = jnp.sqrt(-jnp.tanh(log_a) * (a * a + 1.0)) * (i * xc)
    return a, u


def _rglru_kernel(xr_ref, gr_ref, h0_ref, c0_ref, cw_ref, cb_ref, wg_ref, bg_ref, lam_ref, gout_ref,
                  y_ref, hn_ref, cn_ref, xbuf, hc, ybuf, *, tc, rider=None):
    c = pl.program_id(1)

    @pl.when(c == 0)
    def _():
        xbuf[0:8, :] = jnp.zeros((8, D_RNN), F32)
        xbuf[5:8, :] = c0_ref[0]
        hc[...] = h0_ref[0]

    @pl.when(c > 0)
    def _():
        xbuf[0:8, :] = xbuf[tc:tc + 8, :]

    xbuf[8:8 + tc, :] = xr_ref[0]
    ng = tc // SUBLANES
    sub = lax.broadcasted_iota(jnp.int32, (ng, SUBLANES, RNN_HD), 1)
    ssq = jnp.zeros((tc, 1), F32)
    for h in range(RNN_HEADS):
        _run_rider_share(rider, h, RNN_HEADS)
        sl = slice(h * RNN_HD, (h + 1) * RNN_HD)
        xc = cb_ref[:, sl]
        for j in range(CONV_W):
            xc = xc + cw_ref[j:j + 1, sl] * xbuf[5 + j:5 + j + tc, sl]
        a, u = _lru_gates(xc, wg_ref[h], bg_ref[h], lam_ref[:, sl])
        a = a.reshape(ng, SUBLANES, RNN_HD)
        u = u.reshape(ng, SUBLANES, RNN_HD)
        d = 1
        while d < SUBLANES:
            keep = sub >= d
            a_prev = jnp.where(keep, pltpu.roll(a, d, 1), 1.0)
            u_prev = jnp.where(keep, pltpu.roll(u, d, 1), 0.0)
            u = u + a * u_prev
            a = a * a_prev
            d *= 2
        carry = hc[:, sl]
        for g in range(ng):
            hg = u[g] + a[g] * carry
            carry = hg[SUBLANES - 1:SUBLANES, :]
            ybuf[g * SUBLANES:(g + 1) * SUBLANES, sl] = hg
        hc[:, sl] = carry
        yv = ybuf[:, sl] * jax.nn.gelu(gr_ref[0, :, sl])
        ssq = ssq + jnp.sum(yv * yv, axis=-1, keepdims=True)
        ybuf[:, sl] = yv
    y = ybuf[...] * lax.rsqrt(ssq * (1.0 / D_RNN) + EPS) * gout_ref[...]
    y_ref[0] = y.astype(y_ref.dtype)

    @pl.when(c == pl.num_programs(1) - 1)
    def _():
        hn_ref[0] = hc[...]
        cn_ref[0] = xbuf[tc + 5:tc + 8, :]


def _rglru_prompt(z3, h0, conv0, w, *, tc, rider=None):
    b, t, _ = z3.shape
    full2 = lambda bi, ci: (0, 0)
    r_in, r_out, r_shape, r_args = _rider_parts(rider)
    in_specs = [pl.BlockSpec((1, tc, D_RNN), lambda bi, ci: (bi, ci, 0)),
                pl.BlockSpec((1, tc, D_RNN), lambda bi, ci: (bi, ci, 1)),
                pl.BlockSpec((1, 1, D_RNN), lambda bi, ci: (bi, 0, 0)),
                pl.BlockSpec((1, CONV_W - 1, D_RNN), lambda bi, ci: (bi, 0, 0)),
                pl.BlockSpec((CONV_W, D_RNN), full2),
                pl.BlockSpec((1, D_RNN), full2),
                pl.BlockSpec((RNN_HEADS, RNN_HD, 2 * RNN_HD), lambda bi, ci: (0, 0, 0)),
                pl.BlockSpec((RNN_HEADS, 1, 2 * RNN_HD), lambda bi, ci: (0, 0, 0)),
                pl.BlockSpec((1, D_RNN), full2),
                pl.BlockSpec((1, D_RNN), full2)]
    out_specs = [pl.BlockSpec((1, tc, D_RNN), lambda bi, ci: (bi, ci, 0)),
                 pl.BlockSpec((1, 1, D_RNN), lambda bi, ci: (bi, 0, 0)),
                 pl.BlockSpec((1, CONV_W - 1, D_RNN), lambda bi, ci: (bi, 0, 0))]
    out_shape = [jax.ShapeDtypeStruct((b, t, D_RNN), BF16),
                 jax.ShapeDtypeStruct((b, 1, D_RNN), F32),
                 jax.ShapeDtypeStruct((b, CONV_W - 1, D_RNN), F32)]
    return pl.pallas_call(
        _with_rider(functools.partial(_rglru_kernel, tc=tc), rider, len(in_specs), len(out_specs)),
        grid=(b, t // tc),
        in_specs=in_specs + r_in, out_specs=out_specs + r_out, out_shape=out_shape + r_shape,
        scratch_shapes=[pltpu.VMEM((tc + 8, D_RNN), F32),
                        pltpu.VMEM((1, D_RNN), F32),
                        pltpu.VMEM((tc, D_RNN), F32)],
        compiler_params=_params("parallel", "arbitrary"),
        name="rglru_prompt",
    )(z3, z3, h0, conv0, w['conv_rnn_w'], w['conv_rnn_b'], w['lru_wg'], w['lru_bg'],
      w['lru_lambda'], w['g_rnn_out'], *r_args)


def _mlstm_kernel(u_ref, v_ref, og_ref, zg_ref, gb_ref, c0_ref, n0_ref, m0_ref, cv0_ref,
                  cw_ref, cb_ref, wqk_ref, gout_ref,
                  y_ref, cn_ref, nn_ref, mn_ref, cvn_ref, ubuf, cst, nst, mst, rider=None):
    c = pl.program_id(1)
    L = CHUNK

    @pl.when(c == 0)
    def _():
        ubuf[0:8, :] = jnp.zeros((8, D_MLSTM), F32)
        ubuf[5:8, :] = cv0_ref[0]
        cst[...] = c0_ref[0]
        nst[...] = n0_ref[0]
        mst[...] = m0_ref[0]

    @pl.when(c > 0)
    def _():
        ubuf[0:8, :] = ubuf[L:L + 8, :]

    ubuf[8:8 + L, :] = u_ref[0]

    zg = zg_ref[0] + gb_ref[...]
    zgt = zg.T
    ti = lax.broadcasted_iota(jnp.int32, (L, L), 0)
    si = lax.broadcasted_iota(jnp.int32, (L, L), 1)
    causal = si <= ti
    anti = ti <= si

    for h in range(MLSTM_HEADS):
        _run_rider_share(rider, h, MLSTM_HEADS)
        sl = slice(h * MLSTM_HD, (h + 1) * MLSTM_HD)
        uc = cb_ref[:, sl]
        for j in range(CONV_W):
            uc = uc + cw_ref[j:j + 1, sl] * ubuf[5 + j:5 + j + L, sl]
        uc = jax.nn.silu(uc)
        qk = _dot(uc.astype(BF16), wqk_ref[h])
        q = qk[:, :MLSTM_HD]
        k = qk[:, MLSTM_HD:] * (MLSTM_HD ** -0.5)
        v = v_ref[0, :, sl]
        qb, kb = q.astype(BF16), k.astype(BF16)

        icol = zg[:, h:h + 1]
        fcol = jax.nn.log_sigmoid(zg[:, MLSTM_HEADS + h:MLSTM_HEADS + h + 1])
        irow = zgt[h:h + 1, :]
        frow = jax.nn.log_sigmoid(zgt[MLSTM_HEADS + h:MLSTM_HEADS + h + 1, :])
        bcol = jnp.sum(jnp.where(causal, frow, 0.0), axis=1, keepdims=True)
        brow = jnp.sum(jnp.where(anti, fcol, 0.0), axis=0, keepdims=True)
        dmat = jnp.where(causal, irow + bcol - brow, -jnp.inf)
        m_prev = mst[h, :, 0:1]
        inter = bcol + m_prev
        m_t = jnp.maximum(inter, jnp.max(dmat, axis=1, keepdims=True))
        w_inter = jnp.exp(inter - m_t)
        s = lax.dot_general(qb, kb, (((1,), (1,)), ((), ())), preferred_element_type=F32) * jnp.exp(dmat - m_t)
        cmat = cst[h]
        nrow = nst[h]
        cq = lax.dot_general(qb, cmat.astype(BF16), (((1,), (1,)), ((), ())), preferred_element_type=F32)
        num = w_inter * cq + _dot(s.astype(BF16), v.astype(BF16))
        den = w_inter * jnp.sum(q * nrow, axis=1, keepdims=True) + jnp.sum(s, axis=1, keepdims=True)
        hm = num / jnp.maximum(jnp.abs(den), jnp.exp(-m_t))

        m_new = m_t[L - 1:L, :]
        b_last = bcol[L - 1:L, :]
        g_state = jnp.exp(b_last + m_prev - m_new)
        g_in = jnp.exp(icol + b_last - bcol - m_new)
        cst[h] = g_state * cmat + lax.dot_general((g_in * v).astype(BF16), kb, (((0,), (0,)), ((), ())),
                                                  preferred_element_type=F32)
        nst[h] = g_state * nrow + jnp.sum(g_in * k, axis=0, keepdims=True)
        mst[h] = jnp.broadcast_to(m_new, (1, N_GATE_PAD))

        y = _rms(hm, gout_ref[...]) * jax.nn.sigmoid(og_ref[0, :, sl])
        y_ref[0, :, sl] = y.astype(y_ref.dtype)

    @pl.when(c == pl.num_programs(1) - 1)
    def _():
        cn_ref[0] = cst[...]
        nn_ref[0] = nst[...]
        mn_ref[0] = mst[...]
        cvn_ref[0] = ubuf[L + 5:L + 8, :]


def _mlstm_prompt(z3, zg3, c0, n0, m0, conv0, w, *, rider=None):
    b, t, _ = z3.shape
    nc = t // CHUNK
    u_blk, v_blk, o_blk = 2 * D_RNN // D_MLSTM, 3 * D_RNN // D_MLSTM, 4 * D_RNN // D_MLSTM
    zcol = lambda blk: pl.BlockSpec((1, CHUNK, D_MLSTM), lambda bi, ci: (bi, ci, blk))
    state4 = lambda *minor: pl.BlockSpec((1, MLSTM_HEADS) + minor, lambda bi, ci: (bi, 0, 0, 0))
    conv_spec = pl.BlockSpec((1, CONV_W - 1, D_MLSTM), lambda bi, ci: (bi, 0, 0))
    const2 = lambda shape: pl.BlockSpec(shape, lambda bi, ci: (0, 0))
    r_in, r_out, r_shape, r_args = _rider_parts(rider)
    in_specs = [zcol(u_blk), zcol(v_blk), zcol(o_blk),
                pl.BlockSpec((1, CHUNK, N_GATE_PAD), lambda bi, ci: (bi, ci, 0)),
                const2((1, N_GATE_PAD)),
                state4(MLSTM_HD, MLSTM_HD), state4(1, MLSTM_HD), state4(1, N_GATE_PAD), conv_spec,
                const2((CONV_W, D_MLSTM)), const2((1, D_MLSTM)),
                pl.BlockSpec((MLSTM_HEADS, MLSTM_HD, 2 * MLSTM_HD), lambda bi, ci: (0, 0, 0)),
                const2((1, MLSTM_HD))]
    out_specs = [pl.BlockSpec((1, CHUNK, D_MLSTM), lambda bi, ci: (bi, ci, 0)),
                 state4(MLSTM_HD, MLSTM_HD), state4(1, MLSTM_HD), state4(1, N_GATE_PAD), conv_spec]
    out_shape = [jax.ShapeDtypeStruct((b, t, D_MLSTM), BF16),
                 jax.ShapeDtypeStruct((b, MLSTM_HEADS, MLSTM_HD, MLSTM_HD), F32),
                 jax.ShapeDtypeStruct((b, MLSTM_HEADS, 1, MLSTM_HD), F32),
                 jax.ShapeDtypeStruct((b, MLSTM_HEADS, 1, N_GATE_PAD), F32),
                 jax.ShapeDtypeStruct((b, CONV_W - 1, D_MLSTM), F32)]
    return pl.pallas_call(
        _with_rider(_mlstm_kernel, rider, len(in_specs), len(out_specs)),
        grid=(b, nc),
        in_specs=in_specs + r_in, out_specs=out_specs + r_out, out_shape=out_shape + r_shape,
        scratch_shapes=[pltpu.VMEM((CHUNK + 8, D_MLSTM), F32),
                        pltpu.VMEM((MLSTM_HEADS, MLSTM_HD, MLSTM_HD), F32),
                        pltpu.VMEM((MLSTM_HEADS, 1, MLSTM_HD), F32),
                        pltpu.VMEM((MLSTM_HEADS, 1, N_GATE_PAD), F32)],
        compiler_params=_params("parallel", "arbitrary"),
        name="mlstm_prompt",
    )(z3, z3, z3, zg3, w['gate_bias'], c0, n0, m0, conv0, w['conv_ml_w'], w['conv_ml_b'],
      w['ml_wqk'], w['g_ml_out'], *r_args)


def _xattn_kernel(q_ref, k_ref, v_ref, o_ref):
    for h in range(X_HEADS):
        sl = slice(h * X_HD, (h + 1) * X_HD)
        s = lax.dot_general(q_ref[0, :, sl], k_ref[0, :, sl].astype(BF16), (((1,), (1,)), ((), ())),
                            preferred_element_type=F32) * (X_HD ** -0.5)
        e = jnp.exp(s - jnp.max(s, axis=-1, keepdims=True))
        p = e / jnp.sum(e, axis=-1, keepdims=True)
        o_ref[0, :, sl] = _dot(p.astype(BF16), v_ref[0, :, sl].astype(BF16)).astype(o_ref.dtype)


def _xattn_prompt(q3, mk3, mv3, *, tq):
    b, t, d = q3.shape
    return pl.pallas_call(
        _xattn_kernel,
        grid=(b, t // tq),
        in_specs=[pl.BlockSpec((1, tq, d), lambda bi, ti: (bi, ti, 0)),
                  pl.BlockSpec((1, N_MEM, d), lambda bi, ti: (bi, 0, 0)),
                  pl.BlockSpec((1, N_MEM, d), lambda bi, ti: (bi, 0, 0))],
        out_specs=pl.BlockSpec((1, tq, d), lambda bi, ti: (bi, ti, 0)),
        out_shape=jax.ShapeDtypeStruct((b, t, d), BF16),
        compiler_params=_params("parallel", "arbitrary"),
        name="xattn_prompt",
    )(q3, mk3, mv3)


XS_SUB = 2 * X_HEADS
XS_PAIRS = X_HD // (2 * 128)


def _pack_heads(x):
    lead = x.shape[:-1]
    x = x.reshape(*lead, X_HEADS, 2 * XS_PAIRS, 128)
    x = jnp.swapaxes(x, -3, -2)
    return x.reshape(*lead, XS_PAIRS, XS_SUB, 128)


def _unpack_heads(x):
    lead = x.shape[:-3]
    x = x.reshape(*lead, 2 * XS_PAIRS, X_HEADS, 128)
    x = jnp.swapaxes(x, -3, -2)
    return x.reshape(*lead, X_HEADS * X_HD)


def _xattn_sample_units(q_ref, k_ref, v_ref, o_ref, r):
    def row():
        t = jnp.sum(k_ref[r] * q_ref[r], axis=1)
        t = t + pltpu.roll(t, X_HEADS, 1)
        s = jnp.sum(t, axis=-1, keepdims=True) * (X_HD ** -0.5)
        e = jnp.exp(s - jnp.max(s, axis=0, keepdims=True))
        p = e / jnp.sum(e, axis=0, keepdims=True)
        o_ref[r] = jnp.sum(p[:, None] * v_ref[r], axis=0)

    return [row]


def _xattn_sample_rider(grid, q, ck, cv):
    b = q.shape[0]
    rows = b // (grid[0] * grid[1])
    step = lambda i, j: i * grid[1] + j
    kv_spec = pl.BlockSpec((rows, N_MEM, XS_PAIRS, XS_SUB, 128), lambda i, j: (step(i, j), 0, 0, 0, 0))
    q_spec = pl.BlockSpec((rows, XS_PAIRS, XS_SUB, 128), lambda i, j: (step(i, j), 0, 0, 0))
    return dict(units=_xattn_sample_units, rows=rows,
                in_specs=[q_spec, kv_spec, kv_spec], out_specs=[q_spec],
                out_shape=[jax.ShapeDtypeStruct((b, XS_PAIRS, XS_SUB, 128), F32)],
                args=[_pack_heads(q), _pack_heads(ck.reshape(b, N_MEM, X_HEADS * X_HD)),
                      _pack_heads(cv.reshape(b, N_MEM, X_HEADS * X_HD))])


def _smix_rows_kernel(z_ref, h0_ref, rc_ref, mc_ref, cwr_ref, cbr_ref, wg_ref, bg_ref, lam_ref, gout_ref,
                      cwm_ref, cbm_ref, wqk_ref,
                      yr_ref, hn_ref, rcn_ref, mcn_ref, q_ref, k_ref, ybuf):
    nb = z_ref.shape[0]
    ssq = jnp.zeros((nb, 1), F32)
    for h in range(RNN_HEADS):
        sl = slice(h * RNN_HD, (h + 1) * RNN_HD)
        xc = cbr_ref[:, sl] + cwr_ref[CONV_W - 1:CONV_W, sl] * z_ref[:, sl]
        for j in range(CONV_W - 1):
            xc = xc + cwr_ref[j:j + 1, sl] * rc_ref[:, j * D_RNN + h * RNN_HD:j * D_RNN + (h + 1) * RNN_HD]
        a, u = _lru_gates(xc, wg_ref[h], bg_ref[h], lam_ref[:, sl])
        hs = a * h0_ref[:, sl] + u
        hn_ref[:, sl] = hs
        yv = hs * jax.nn.gelu(z_ref[:, D_RNN + h * RNN_HD:D_RNN + (h + 1) * RNN_HD])
        ssq = ssq + jnp.sum(yv * yv, axis=-1, keepdims=True)
        ybuf[:, sl] = yv
    yr_ref[...] = (ybuf[...] * lax.rsqrt(ssq * (1.0 / D_RNN) + EPS) * gout_ref[...]).astype(yr_ref.dtype)
    rcn_ref[:, 0:2 * D_RNN] = rc_ref[:, D_RNN:3 * D_RNN]
    rcn_ref[:, 2 * D_RNN:3 * D_RNN] = z_ref[:, 0:D_RNN]

    for h in range(MLSTM_HEADS):
        sl = slice(h * MLSTM_HD, (h + 1) * MLSTM_HD)
        uc = cbm_ref[:, sl] + cwm_ref[CONV_W - 1:CONV_W, sl] * z_ref[:, 2 * D_RNN + h * MLSTM_HD:2 * D_RNN + (h + 1) * MLSTM_HD]
        for j in range(CONV_W - 1):
            uc = uc + cwm_ref[j:j + 1, sl] * mc_ref[:, j * D_MLSTM + h * MLSTM_HD:j * D_MLSTM + (h + 1) * MLSTM_HD]
        uc = jax.nn.silu(uc)
        qk = _dot(uc.astype(BF16), wqk_ref[h])
        q_ref[:, sl] = qk[:, :MLSTM_HD]
        k_ref[:, sl] = qk[:, MLSTM_HD:] * (MLSTM_HD ** -0.5)
    mcn_ref[:, 0:2 * D_MLSTM] = mc_ref[:, D_MLSTM:3 * D_MLSTM]
    mcn_ref[:, 2 * D_MLSTM:3 * D_MLSTM] = z_ref[:, 2 * D_RNN:2 * D_RNN + D_MLSTM]


def _smix_rows(z, h0, rconv, mconv, w):
    nb = z.shape[0]
    outs = [jax.ShapeDtypeStruct((nb, D_RNN), BF16),
            jax.ShapeDtypeStruct((nb, D_RNN), F32),
            jax.ShapeDtypeStruct((nb, 3 * D_RNN), F32),
            jax.ShapeDtypeStruct((nb, 3 * D_MLSTM), F32),
            jax.ShapeDtypeStruct((nb, D_MLSTM), F32),
            jax.ShapeDtypeStruct((nb, D_MLSTM), F32)]
    return pl.pallas_call(
        _smix_rows_kernel,
        out_shape=outs,
        scratch_shapes=[pltpu.VMEM((nb, D_RNN), F32)],
        compiler_params=pltpu.CompilerParams(vmem_limit_bytes=V7X_VMEM_LIMIT_BYTES),
        name="smix_rows",
    )(z, h0, rconv, mconv, w['conv_rnn_w'], w['conv_rnn_b'], w['lru_wg'], w['lru_bg'], w['lru_lambda'],
      w['g_rnn_out'], w['conv_ml_w'], w['conv_ml_b'], w['ml_wqk'])


MXU_MIN_ROWS = 8


def _smix_state_units(c_ref, q_ref, k_ref, n_ref, v_ref, og_ref, zg_ref, gb_ref, m_ref, gout_ref,
                      cn_ref, nn_ref, mn_ref, y_ref, r):
    lane_g = lax.broadcasted_iota(jnp.int32, (1, N_GATE_PAD), 1)
    first_row = lax.broadcasted_iota(jnp.int32, (MXU_MIN_ROWS, MLSTM_HD), 0) == 0

    def head(h, zg, m_out):
        sl = slice(h * MLSTM_HD, (h + 1) * MLSTM_HD)
        q = q_ref[r, :, sl]
        k = k_ref[r, :, sl]
        n = n_ref[r, :, sl]
        v = v_ref[r, :, sl]
        ig = zg[:, h:h + 1]
        lf = jax.nn.log_sigmoid(zg[:, MLSTM_HEADS + h:MLSTM_HEADS + h + 1])
        m_prev = m_ref[r, :, h:h + 1]
        m_t = jnp.maximum(lf + m_prev, ig)
        w_inter = jnp.exp(lf + m_prev - m_t)
        g_in = jnp.exp(ig - m_t)
        s = jnp.sum(q * k, axis=1, keepdims=True) * g_in
        den = w_inter * jnp.sum(n * q, axis=1, keepdims=True) + s
        denom = jnp.maximum(jnp.abs(den), jnp.exp(-m_t))
        cmat = c_ref[r, h]
        q8 = jnp.broadcast_to(q, (MXU_MIN_ROWS, MLSTM_HD)).astype(BF16)
        cq = lax.dot_general(q8, cmat.astype(BF16), (((1,), (1,)), ((), ())),
                             preferred_element_type=F32)[0:1, :]
        hm = (w_inter * cq + s * v) / denom
        gv8 = jnp.where(first_row, jnp.broadcast_to(g_in * v, (MXU_MIN_ROWS, MLSTM_HD)), 0.0).astype(BF16)
        k8 = jnp.broadcast_to(k, (MXU_MIN_ROWS, MLSTM_HD)).astype(BF16)
        outer = lax.dot_general(gv8, k8, (((0,), (0,)), ((), ())), preferred_element_type=F32)
        cn_ref[r, h] = w_inter * cmat + outer
        nn_ref[r, :, sl] = w_inter * n + g_in * k
        y = _rms(hm, gout_ref[...]) * jax.nn.sigmoid(og_ref[r, :, sl])
        y_ref[r, :, sl] = y.astype(y_ref.dtype)
        return jnp.where(lane_g == h, m_t, m_out)

    def row():
        zg = zg_ref[r] + gb_ref[...]
        m_out = jnp.zeros((1, N_GATE_PAD), F32)
        for h in range(MLSTM_HEADS):
            m_out = head(h, zg, m_out)
        mn_ref[r] = m_out

    return [row]


def _smix_state_rider(grid, c, q3, k3, n3, z3, zg3, gate_bias, m3, gout):
    nb = c.shape[0]
    rows = nb // (grid[0] * grid[1])
    step = lambda i, j: i * grid[1] + j
    row = lambda width, col=0: pl.BlockSpec((rows, 1, width), lambda i, j: (step(i, j), 0, col))
    cblk = pl.BlockSpec((rows, MLSTM_HEADS, MLSTM_HD, MLSTM_HD), lambda i, j: (step(i, j), 0, 0, 0))
    return dict(
        units=_smix_state_units, rows=rows,
        in_specs=[cblk, row(D_MLSTM), row(D_MLSTM), row(D_MLSTM),
                  row(D_MLSTM, 3 * D_RNN // D_MLSTM), row(D_MLSTM, 4 * D_RNN // D_MLSTM),
                  row(N_GATE_PAD),
                  pl.BlockSpec((1, N_GATE_PAD), lambda i, j: (0, 0)),
                  row(MLSTM_HEADS),
                  pl.BlockSpec((1, MLSTM_HD), lambda i, j: (0, 0))],
        out_specs=[cblk, row(D_MLSTM), row(N_GATE_PAD), row(D_MLSTM)],
        out_shape=[jax.ShapeDtypeStruct(c.shape, F32),
                   jax.ShapeDtypeStruct((nb, 1, D_MLSTM), F32),
                   jax.ShapeDtypeStruct((nb, 1, N_GATE_PAD), F32),
                   jax.ShapeDtypeStruct((nb, 1, D_MLSTM), F32)],
        args=[c, q3, k3, n3, z3, z3, zg3, gate_bias, m3, gout])


RGLRU_CHUNK = 256
PROMPT_TM = 1024


def _layer(xp, xs, mem, rg_h, rg_conv, c, n, mst, ml_conv, ck, cv, w, g_final, final_norm):
    b, t, d = xp.shape
    nb = xs.shape[0]
    m = b * t
    tm = PROMPT_TM
    xp2 = xp.reshape(m, d)
    xs2 = xs.reshape(nb, d)
    w_in_args = dict(tn=1024, out_dtype=F32, w_extra=w['w_in_gate'], n_cols=D_MAIN, w_is_transposed=True)

    zs, zgs = _norm_matmul(xs2, w['g_mix'], w['w_in_main'], tm=nb, **w_in_args)
    ys_rnn, s_h, s_rconv, s_mconv, qs, ks = _smix_rows(
        zs, rg_h, rg_conv.reshape(nb, 3 * D_RNN), ml_conv.reshape(nb, 3 * D_MLSTM), w)

    z, zg = _norm_matmul(xp2, w['g_mix'], w['w_in_main'], tm=tm, **w_in_args)
    z3 = z.reshape(b, t, D_MAIN)
    zg3 = zg.reshape(b, t, N_GATE_PAD)
    h0 = jnp.zeros((b, 1, D_RNN), F32)
    conv0 = jnp.zeros((b, CONV_W - 1, D_RNN), F32)
    state_rider = _smix_state_rider(
        (b, t // RGLRU_CHUNK), c, qs.reshape(nb, 1, D_MLSTM), ks.reshape(nb, 1, D_MLSTM),
        n.reshape(nb, 1, D_MLSTM), zs.reshape(nb, 1, D_MAIN), zgs.reshape(nb, 1, N_GATE_PAD), w['gate_bias'],
        mst.reshape(nb, 1, MLSTM_HEADS), w['g_ml_out'])
    y_rnn, p_h, p_rconv, s_c, s_n, s_m, ys_ml = _rglru_prompt(z3, h0, conv0, w, tc=RGLRU_CHUNK, rider=state_rider)

    xs1 = _matmul_res([ys_rnn, ys_ml.reshape(nb, D_MLSTM).astype(BF16)], w['w_out'], xs2, tm=nb, tn=1024)
    qx = _norm_matmul(xs1, w['g_xattn'], w['w_cq'], tm=nb, tn=1024, out_dtype=F32)

    c0 = jnp.zeros((b, MLSTM_HEADS, MLSTM_HD, MLSTM_HD), F32)
    n0 = jnp.zeros((b, MLSTM_HEADS, 1, MLSTM_HD), F32)
    m0 = jnp.zeros((b, MLSTM_HEADS, 1, N_GATE_PAD), F32)
    mconv0 = jnp.zeros((b, CONV_W - 1, D_MLSTM), F32)
    y_ml, p_c, p_n, p_m, p_mconv, os_packed = _mlstm_prompt(
        z3, zg3, c0, n0, m0, mconv0, w, rider=_xattn_sample_rider((b, t // CHUNK), qx, ck, cv))

    os_ = _unpack_heads(os_packed).astype(BF16)
    xs2_ = _matmul_res([os_], w['w_co'], xs1, tm=nb, tn=1024)
    xs_out = _ffn(xs2_, w['g_ffn'], w['w_up'], w['w_down'], g_final, tm=nb, tf=512, final_norm=final_norm)

    x1 = _matmul_res([y_rnn.reshape(m, D_RNN), y_ml.reshape(m, D_MLSTM)], w['w_out'], xp2, tm=tm, tn=1024)
    mem2 = mem.reshape(b * N_MEM, d)
    mk = _norm_matmul(mem2, w['g_mem'], w['w_mk'], tm=b * N_MEM, tn=1024, out_dtype=F32)
    mv = _norm_matmul(mem2, w['g_mem'], w['w_mv'], tm=b * N_MEM, tn=1024, out_dtype=F32)
    q = _norm_matmul(x1, w['g_xattn'], w['w_cq'], tm=tm, tn=1024, out_dtype=BF16)
    o = _xattn_prompt(q.reshape(b, t, d), mk.reshape(b, N_MEM, d), mv.reshape(b, N_MEM, d), tq=512)
    x2 = _matmul_res([o.reshape(m, d)], w['w_co'], x1, tm=tm, tn=1024)
    xp_out = _ffn(x2, w['g_ffn'], w['w_up'], w['w_down'], g_final, tm=tm, tf=512, final_norm=final_norm)

    new_p = (p_h.reshape(b, D_RNN), p_rconv, p_c, p_n.reshape(b, MLSTM_HEADS, MLSTM_HD), p_m[:, :, 0, 0], p_mconv,
             mk.reshape(b, N_MEM, X_HEADS, X_HD), mv.reshape(b, N_MEM, X_HEADS, X_HD))
    new_s = (s_h, s_rconv.reshape(nb, CONV_W - 1, D_RNN), s_c, s_n.reshape(nb, MLSTM_HEADS, MLSTM_HD),
             s_m[:, 0, :MLSTM_HEADS], s_mconv.reshape(nb, CONV_W - 1, D_MLSTM))
    return xp_out.reshape(b, t, d), xs_out.reshape(nb, 1, d), new_p, new_s


def _layer_weights(l, g_mix, w_in, conv_rnn_w, conv_rnn_b, lru_wa, lru_ba, lru_wx, lru_bx, lru_lambda,
                   g_rnn_out, conv_ml_w, conv_ml_b, ml_wq, ml_wk, ml_bi, ml_bf, g_ml_out, w_out,
                   g_xattn, g_mem, w_cq, w_mk, w_mv, w_co, g_ffn, w_up, w_down):
    n_gate = 2 * MLSTM_HEADS
    w_in_t = jnp.swapaxes(w_in[l], 0, 1)
    w_gate_t = jnp.pad(w_in_t[D_MAIN:], ((0, N_GATE_PAD - n_gate), (0, 0)))
    gate_bias = jnp.pad(jnp.concatenate([ml_bi[l], ml_bf[l]]), (0, N_GATE_PAD - n_gate))
    return dict(
        g_mix=g_mix[l], w_in_main=w_in_t, w_in_gate=w_gate_t.astype(BF16),
        conv_rnn_w=conv_rnn_w[l], conv_rnn_b=conv_rnn_b[l].reshape(1, D_RNN),
        lru_wg=jnp.concatenate([lru_wa[l], lru_wx[l]], axis=-1).astype(BF16),
        lru_bg=jnp.concatenate([lru_ba[l], lru_bx[l]], axis=-1).reshape(RNN_HEADS, 1, 2 * RNN_HD),
        lru_lambda=lru_lambda[l].reshape(1, D_RNN), g_rnn_out=g_rnn_out[l].reshape(1, D_RNN),
        conv_ml_w=conv_ml_w[l], conv_ml_b=conv_ml_b[l].reshape(1, D_MLSTM),
        ml_wqk=jnp.concatenate([ml_wq[l], ml_wk[l]], axis=-1).astype(BF16),
        gate_bias=gate_bias.reshape(1, N_GATE_PAD), g_ml_out=g_ml_out[l].reshape(1, MLSTM_HD),
        w_out=w_out[l], g_xattn=g_xattn[l], g_mem=g_mem[l], w_cq=w_cq[l], w_mk=w_mk[l], w_mv=w_mv[l],
        w_co=w_co[l], g_ffn=g_ffn[l], w_up=w_up[l].astype(BF16), w_down=w_down[l].astype(BF16))


def kernel(x_prompt, x_sample, mem_prompt, state_rglru_h, state_rglru_conv, state_mlstm_C, state_mlstm_n, state_mlstm_m, state_mlstm_conv, cache_mem_k, cache_mem_v, g_mix, w_in, conv_rnn_w, conv_rnn_b, lru_wa, lru_ba, lru_wx, lru_bx, lru_lambda, g_rnn_out, conv_ml_w, conv_ml_b, ml_wq, ml_wk, ml_bi, ml_bf, g_ml_out, w_out, g_xattn, g_mem, w_cq, w_mk, w_mv, w_co, g_ffn, w_up, w_down, g_final):
    depth = w_in.shape[0]
    xp, xs = x_prompt, x_sample
    p_out = [[] for _ in range(8)]
    s_out = [[] for _ in range(6)]
    for l in range(depth):
        w = _layer_weights(l, g_mix, w_in, conv_rnn_w, conv_rnn_b, lru_wa, lru_ba, lru_wx, lru_bx, lru_lambda,
                           g_rnn_out, conv_ml_w, conv_ml_b, ml_wq, ml_wk, ml_bi, ml_bf, g_ml_out, w_out,
                           g_xattn, g_mem, w_cq, w_mk, w_mv, w_co, g_ffn, w_up, w_down)
        last = l == depth - 1
        xp, xs, new_p, new_s = _layer(xp, xs, mem_prompt, state_rglru_h[l], state_rglru_conv[l], state_mlstm_C[l],
                                      state_mlstm_n[l], state_mlstm_m[l], state_mlstm_conv[l], cache_mem_k[l],
                                      cache_mem_v[l], w, g_final, last)
        for j, a in enumerate(new_p):
            p_out[j].append(a)
        for j, a in enumerate(new_s):
            s_out[j].append(a)
    P = [jnp.stack(a, axis=0) for a in p_out]
    S = [jnp.stack(a, axis=0) for a in s_out]
    return (xp, xs, P[0], P[1], P[2], P[3], P[4], P[5], P[6], P[7],
            S[0], S[1], S[2], S[3], S[4], S[5])
```

```python
import functools

import jax
import jax.numpy as jnp
from jax import lax
from jax.experimental import pallas as pl
from jax.experimental.pallas import tpu as pltpu

F32 = jnp.float32
BF16 = jnp.bfloat16

D_MODEL = 2048
D_RNN = 1024
RNN_HEADS = 8
RNN_HD = 128
CONV_W = 4
LRU_C = 8.0
D_MLSTM = 1024
MLSTM_HEADS = 4
MLSTM_HD = 256
CHUNK = 128
N_MEM = 256
X_HEADS = 4
X_HD = 512
D_FF = 8192
EPS = 1e-6
D_MAIN = 5 * 1024
N_GATE_PAD = 128

V7X_VMEM_LIMIT_BYTES = 56 * 1024 * 1024
SUBLANES = 8


def _params(*sem):
    return pltpu.CompilerParams(dimension_semantics=sem, vmem_limit_bytes=V7X_VMEM_LIMIT_BYTES)


def _rms(x, g):
    ms = jnp.mean(x * x, axis=-1, keepdims=True)
    return x * lax.rsqrt(ms + EPS) * g


def _dot(a, b):
    return jnp.dot(a.astype(BF16), b.astype(BF16), preferred_element_type=F32)


def _dot_t(a, bt):
    return lax.dot_general(a.astype(BF16), bt.astype(BF16), (((1,), (1,)), ((), ())), preferred_element_type=F32)


def _with_rider(host_body, rider, n_in, n_out):
    if rider is None:
        return host_body
    r_in, r_out = len(rider['in_specs']), len(rider['out_specs'])

    def body(*refs):
        host_in, refs = refs[:n_in], refs[n_in:]
        rider_in, refs = refs[:r_in], refs[r_in:]
        host_out, refs = refs[:n_out], refs[n_out:]
        rider_out, scratch = refs[:r_out], refs[r_out:]
        units = [u for r in range(rider['rows']) for u in rider['units'](*rider_in, *rider_out, r)]
        host_body(*host_in, *host_out, *scratch, rider=units)
    return body


def _run_rider_share(rider, i, n):
    if rider is None:
        return
    for unit in rider[i * len(rider) // n:(i + 1) * len(rider) // n]:
        unit()


def _rider_parts(rider):
    if rider is None:
        return [], [], [], []
    return rider['in_specs'], rider['out_specs'], rider['out_shape'], rider['args']


def _norm_matmul_kernel(*refs, has_extra, w_is_transposed):
    if has_extra:
        x_ref, g_ref, w_ref, we_ref, o_ref, oe_ref, xn_ref = refs
    else:
        x_ref, g_ref, w_ref, o_ref, xn_ref = refs
    dot = _dot_t if w_is_transposed else _dot

    @pl.when(pl.program_id(1) == 0)
    def _():
        xn = _rms(x_ref[...], g_ref[...]).astype(BF16)
        xn_ref[...] = xn
        if has_extra:
            oe_ref[...] = dot(xn, we_ref[...])

    o_ref[...] = dot(xn_ref[...], w_ref[...]).astype(o_ref.dtype)


def _norm_matmul(x, g, w, *, tm, tn, out_dtype, w_extra=None, n_cols=None, w_is_transposed=False):
    m, k = x.shape
    n = n_cols or w.shape[0 if w_is_transposed else 1]
    has_extra = w_extra is not None
    w_spec = (pl.BlockSpec((tn, k), lambda i, j: (j, 0)) if w_is_transposed
              else pl.BlockSpec((k, tn), lambda i, j: (0, j)))
    in_specs = [pl.BlockSpec((tm, k), lambda i, j: (i, 0)),
                pl.BlockSpec((1, k), lambda i, j: (0, 0)),
                w_spec]
    out_specs = [pl.BlockSpec((tm, tn), lambda i, j: (i, j))]
    out_shape = [jax.ShapeDtypeStruct((m, n), out_dtype)]
    args = [x, g.reshape(1, k), w]
    if has_extra:
        ne = w_extra.shape[0 if w_is_transposed else 1]
        in_specs.append(pl.BlockSpec(w_extra.shape, lambda i, j: (0, 0)))
        out_specs.append(pl.BlockSpec((tm, ne), lambda i, j: (i, 0)))
        out_shape.append(jax.ShapeDtypeStruct((m, ne), F32))
        args.append(w_extra)
    res = pl.pallas_call(
        functools.partial(_norm_matmul_kernel, has_extra=has_extra, w_is_transposed=w_is_transposed),
        grid=(m // tm, n // tn),
        in_specs=in_specs, out_specs=out_specs, out_shape=out_shape,
        scratch_shapes=[pltpu.VMEM((tm, k), BF16)],
        compiler_params=_params("parallel", "arbitrary"),
        name="norm_matmul",
    )(*args)
    return res if has_extra else res[0]


def _matmul_res_kernel(*refs, n_parts):
    a_refs, w_refs = refs[:n_parts], refs[n_parts:2 * n_parts]
    r_ref, o_ref = refs[2 * n_parts:]
    acc = r_ref[...]
    for a_ref, w_ref in zip(a_refs, w_refs):
        acc = acc + _dot(a_ref[...], w_ref[...])
    o_ref[...] = acc


def _matmul_res(a_parts, w, res, *, tm, tn):
    n_parts = len(a_parts)
    m, kp = a_parts[0].shape
    n = w.shape[1]
    a_specs = [pl.BlockSpec((tm, kp), lambda i, j: (i, 0)) for _ in a_parts]
    w_specs = [pl.BlockSpec((kp, tn), lambda i, j, p=p: (p, j)) for p in range(n_parts)]
    return pl.pallas_call(
        functools.partial(_matmul_res_kernel, n_parts=n_parts),
        grid=(m // tm, n // tn),
        in_specs=a_specs + w_specs + [pl.BlockSpec((tm, tn), lambda i, j: (i, j))],
        out_specs=pl.BlockSpec((tm, tn), lambda i, j: (i, j)),
        out_shape=jax.ShapeDtypeStruct((m, n), F32),
        compiler_params=_params("parallel", "arbitrary"),
        name="matmul_res",
    )(*a_parts, *([w] * n_parts), res)


def _matmul_res_norm_matmul_kernel(*refs, n_parts, n1):
    a_refs = refs[:n_parts]
    w_ref, r_ref, g_ref, x1_ref, q_ref, x1_s, xn_s = refs[n_parts:]
    j = pl.program_id(1)
    tn = w_ref.shape[1]
    kp = a_refs[0].shape[1]

    @pl.when(j < n1)
    def _():
        x1 = r_ref[...]
        for p, a_ref in enumerate(a_refs):
            x1 = x1 + _dot(a_ref[...], w_ref[p * kp:(p + 1) * kp, :])
        x1_ref[...] = x1
        x1_s[j] = x1

    @pl.when(j == n1)
    def _():
        ssq = sum(jnp.sum(jnp.square(x1_s[c]), axis=-1, keepdims=True) for c in range(n1))
        scale = lax.rsqrt(ssq * (1.0 / (n1 * tn)) + EPS)
        for c in range(n1):
            xn_s[:, c * tn:(c + 1) * tn] = (x1_s[c] * scale * g_ref[:, c * tn:(c + 1) * tn]).astype(xn_s.dtype)

    @pl.when(j >= n1)
    def _():
        q_ref[...] = _dot(xn_s[...], w_ref[...]).astype(q_ref.dtype)


def _matmul_res_norm_matmul(a_parts, w12, res, g, *, tm, tn, out2_dtype):
    n_parts = len(a_parts)
    m, kp = a_parts[0].shape
    d = res.shape[1]
    n1 = d // tn
    n2 = (w12.shape[1] - d) // tn
    return pl.pallas_call(
        functools.partial(_matmul_res_norm_matmul_kernel, n_parts=n_parts, n1=n1),
        grid=(m // tm, n1 + n2),
        in_specs=[pl.BlockSpec((tm, kp), lambda i, j: (i, 0))] * n_parts + [
            pl.BlockSpec((n_parts * kp, tn), lambda i, j: (0, j)),
            pl.BlockSpec((tm, tn), lambda i, j: (i, jnp.minimum(j, n1 - 1))),
            pl.BlockSpec((1, d), lambda i, j: (0, 0))],
        out_specs=[pl.BlockSpec((tm, tn), lambda i, j: (i, jnp.minimum(j, n1 - 1))),
                   pl.BlockSpec((tm, tn), lambda i, j: (i, jnp.maximum(j - n1, 0)))],
        out_shape=[jax.ShapeDtypeStruct((m, d), F32), jax.ShapeDtypeStruct((m, n2 * tn), out2_dtype)],
        scratch_shapes=[pltpu.VMEM((n1, tm, tn), F32), pltpu.VMEM((tm, d), BF16)],
        compiler_params=_params("parallel", "arbitrary"),
        name="matmul_res_norm_matmul",
    )(*a_parts, w12, res, g.reshape(1, d))


def _ffn_kernel(x_ref, g_ref, wu_ref, wd_ref, gf_ref, o_ref, xn_ref, *, final_norm):
    f = pl.program_id(1)

    @pl.when(f == 0)
    def _():
        xn_ref[...] = _rms(x_ref[...], g_ref[...]).astype(BF16)
        o_ref[...] = jnp.zeros_like(o_ref)

    h = _dot(xn_ref[...], wu_ref[...])
    h = jnp.square(jnp.maximum(h, 0.0)).astype(BF16)
    o_ref[...] += _dot(h, wd_ref[...])

    @pl.when(f == pl.num_programs(1) - 1)
    def _():
        y = x_ref[...] + o_ref[...]
        if final_norm:
            y = _rms(y, gf_ref[...])
        o_ref[...] = y


def _ffn(x, g, w_up, w_down, g_final, *, tm, tf, final_norm):
    m, d = x.shape
    dff = w_up.shape[1]
    return pl.pallas_call(
        functools.partial(_ffn_kernel, final_norm=final_norm),
        grid=(m // tm, dff // tf),
        in_specs=[pl.BlockSpec((tm, d), lambda i, f: (i, 0)),
                  pl.BlockSpec((1, d), lambda i, f: (0, 0)),
                  pl.BlockSpec((d, tf), lambda i, f: (0, f)),
                  pl.BlockSpec((tf, d), lambda i, f: (f, 0)),
                  pl.BlockSpec((1, d), lambda i, f: (0, 0))],
        out_specs=pl.BlockSpec((tm, d), lambda i, f: (i, 0)),
        out_shape=jax.ShapeDtypeStruct((m, d), F32),
        scratch_shapes=[pltpu.VMEM((tm, d), BF16)],
        compiler_params=_params("parallel", "arbitrary"),
        name="ffn",
    )(x, g.reshape(1, d), w_up, w_down, g_final.reshape(1, d))


def _lru_gates(xc, wg, bg, lam):
    g = _dot(xc.astype(BF16), wg) + bg
    r = jax.nn.sigmoid(g[:, :RNN_HD])
    i = jax.nn.sigmoid(g[:, RNN_HD:])
    log_a = -LRU_C * r * jax.nn.softplus(-lam)
    a = jnp.exp(log_a)
    u = jnp.sqrt(-jnp.tanh(log_a) * (a * a + 1.0)) * (i * xc)
    return a, u


def _rglru_kernel(xr_ref, gr_ref, h0_ref, c0_ref, cw_ref, cb_ref, wg_ref, bg_ref, lam_ref, gout_ref,
                  y_ref, hn_ref, cn_ref, xbuf, hc, ybuf, *, tc, rider=None):
    c = pl.program_id(1)

    @pl.when(c == 0)
    def _():
        xbuf[0:8, :] = jnp.zeros((8, D_RNN), F32)
        xbuf[5:8, :] = c0_ref[0]
        hc[...] = h0_ref[0]

    @pl.when(c > 0)
    def _():
        xbuf[0:8, :] = xbuf[tc:tc + 8, :]

    xbuf[8:8 + tc, :] = xr_ref[0]
    ng = tc // SUBLANES
    sub = lax.broadcasted_iota(jnp.int32, (ng, SUBLANES, RNN_HD), 1)
    ssq = jnp.zeros((tc, 1), F32)
    for h in range(RNN_HEADS):
        _run_rider_share(rider, h, RNN_HEADS)
        sl = slice(h * RNN_HD, (h + 1) * RNN_HD)
        xc = cb_ref[:, sl]
        for j in range(CONV_W):
            xc = xc + cw_ref[j:j + 1, sl] * xbuf[5 + j:5 + j + tc, sl]
        a, u = _lru_gates(xc, wg_ref[h], bg_ref[h], lam_ref[:, sl])
        a = a.reshape(ng, SUBLANES, RNN_HD)
        u = u.reshape(ng, SUBLANES, RNN_HD)
        d = 1
        while d < SUBLANES:
            keep = sub >= d
            a_prev = jnp.where(keep, pltpu.roll(a, d, 1), 1.0)
            u_prev = jnp.where(keep, pltpu.roll(u, d, 1), 0.0)
            u = u + a * u_prev
            a = a * a_prev
            d *= 2
        carry = hc[:, sl]
        for g in range(ng):
            hg = u[g] + a[g] * carry
            carry = hg[SUBLANES - 1:SUBLANES, :]
            ybuf[g * SUBLANES:(g + 1) * SUBLANES, sl] = hg
        hc[:, sl] = carry
        yv = ybuf[:, sl] * jax.nn.gelu(gr_ref[0, :, sl])
        ssq = ssq + jnp.sum(yv * yv, axis=-1, keepdims=True)
        ybuf[:, sl] = yv
    y = ybuf[...] * lax.rsqrt(ssq * (1.0 / D_RNN) + EPS) * gout_ref[...]
    y_ref[0] = y.astype(y_ref.dtype)

    @pl.when(c == pl.num_programs(1) - 1)
    def _():
        hn_ref[0] = hc[...]
        cn_ref[0] = xbuf[tc + 5:tc + 8, :]


def _rglru_prompt(z3, h0, conv0, w, *, tc, rider=None):
    b, t, _ = z3.shape
    full2 = lambda bi, ci: (0, 0)
    r_in, r_out, r_shape, r_args = _rider_parts(rider)
    in_specs = [pl.BlockSpec((1, tc, D_RNN), lambda bi, ci: (bi, ci, 0)),
                pl.BlockSpec((1, tc, D_RNN), lambda bi, ci: (bi, ci, 1)),
                pl.BlockSpec((1, 1, D_RNN), lambda bi, ci: (bi, 0, 0)),
                pl.BlockSpec((1, CONV_W - 1, D_RNN), lambda bi, ci: (bi, 0, 0)),
                pl.BlockSpec((CONV_W, D_RNN), full2),
                pl.BlockSpec((1, D_RNN), full2),
                pl.BlockSpec((RNN_HEADS, RNN_HD, 2 * RNN_HD), lambda bi, ci: (0, 0, 0)),
                pl.BlockSpec((RNN_HEADS, 1, 2 * RNN_HD), lambda bi, ci: (0, 0, 0)),
                pl.BlockSpec((1, D_RNN), full2),
                pl.BlockSpec((1, D_RNN), full2)]
    out_specs = [pl.BlockSpec((1, tc, D_RNN), lambda bi, ci: (bi, ci, 0)),
                 pl.BlockSpec((1, 1, D_RNN), lambda bi, ci: (bi, 0, 0)),
                 pl.BlockSpec((1, CONV_W - 1, D_RNN), lambda bi, ci: (bi, 0, 0))]
    out_shape = [jax.ShapeDtypeStruct((b, t, D_RNN), BF16),
                 jax.ShapeDtypeStruct((b, 1, D_RNN), F32),
                 jax.ShapeDtypeStruct((b, CONV_W - 1, D_RNN), F32)]
    return pl.pallas_call(
        _with_rider(functools.partial(_rglru_kernel, tc=tc), rider, len(in_specs), len(out_specs)),
        grid=(b, t // tc),
        in_specs=in_specs + r_in, out_specs=out_specs + r_out, out_shape=out_shape + r_shape,
        scratch_shapes=[pltpu.VMEM((tc + 8, D_RNN), F32),
                        pltpu.VMEM((1, D_RNN), F32),
                        pltpu.VMEM((tc, D_RNN), F32)],
        compiler_params=_params("parallel", "arbitrary"),
        name="rglru_prompt",
    )(z3, z3, h0, conv0, w['conv_rnn_w'], w['conv_rnn_b'], w['lru_wg'], w['lru_bg'],
      w['lru_lambda'], w['g_rnn_out'], *r_args)


def _mlstm_kernel(u_ref, v_ref, og_ref, zg_ref, gb_ref, c0_ref, n0_ref, m0_ref, cv0_ref,
                  cw_ref, cb_ref, wqk_ref, gout_ref,
                  y_ref, cn_ref, nn_ref, mn_ref, cvn_ref, ubuf, cst, nst, mst, rider=None):
    c = pl.program_id(1)
    L = CHUNK

    @pl.when(c == 0)
    def _():
        ubuf[0:8, :] = jnp.zeros((8, D_MLSTM), F32)
        ubuf[5:8, :] = cv0_ref[0]
        cst[...] = c0_ref[0]
        nst[...] = n0_ref[0]
        mst[...] = m0_ref[0]

    @pl.when(c > 0)
    def _():
        ubuf[0:8, :] = ubuf[L:L + 8, :]

    ubuf[8:8 + L, :] = u_ref[0]

    zg = zg_ref[0] + gb_ref[...]
    zgt = zg.T
    ti = lax.broadcasted_iota(jnp.int32, (L, L), 0)
    si = lax.broadcasted_iota(jnp.int32, (L, L), 1)
    causal = si <= ti
    anti = ti <= si

    for h in range(MLSTM_HEADS):
        _run_rider_share(rider, h, MLSTM_HEADS)
        sl = slice(h * MLSTM_HD, (h + 1) * MLSTM_HD)
        uc = cb_ref[:, sl]
        for j in range(CONV_W):
            uc = uc + cw_ref[j:j + 1, sl] * ubuf[5 + j:5 + j + L, sl]
        uc = jax.nn.silu(uc)
        qk = _dot(uc.astype(BF16), wqk_ref[h])
        q = qk[:, :MLSTM_HD]
        k = qk[:, MLSTM_HD:] * (MLSTM_HD ** -0.5)
        v = v_ref[0, :, sl]
        qb, kb = q.astype(BF16), k.astype(BF16)

        icol = zg[:, h:h + 1]
        fcol = jax.nn.log_sigmoid(zg[:, MLSTM_HEADS + h:MLSTM_HEADS + h + 1])
        irow = zgt[h:h + 1, :]
        frow = jax.nn.log_sigmoid(zgt[MLSTM_HEADS + h:MLSTM_HEADS + h + 1, :])
        bcol = jnp.sum(jnp.where(causal, frow, 0.0), axis=1, keepdims=True)
        brow = jnp.sum(jnp.where(anti, fcol, 0.0), axis=0, keepdims=True)
        dmat = jnp.where(causal, irow + bcol - brow, -jnp.inf)
        m_prev = mst[h, :, 0:1]
        inter = bcol + m_prev
        m_t = jnp.maximum(inter, jnp.max(dmat, axis=1, keepdims=True))
        w_inter = jnp.exp(inter - m_t)
        s = lax.dot_general(qb, kb, (((1,), (1,)), ((), ())), preferred_element_type=F32) * jnp.exp(dmat - m_t)
        cmat = cst[h]
        nrow = nst[h]
        cq = lax.dot_general(qb, cmat.astype(BF16), (((1,), (1,)), ((), ())), preferred_element_type=F32)
        num = w_inter * cq + _dot(s.astype(BF16), v.astype(BF16))
        den = w_inter * jnp.sum(q * nrow, axis=1, keepdims=True) + jnp.sum(s, axis=1, keepdims=True)
        hm = num / jnp.maximum(jnp.abs(den), jnp.exp(-m_t))

        m_new = m_t[L - 1:L, :]
        b_last = bcol[L - 1:L, :]
        g_state = jnp.exp(b_last + m_prev - m_new)
        g_in = jnp.exp(icol + b_last - bcol - m_new)
        cst[h] = g_state * cmat + lax.dot_general((g_in * v).astype(BF16), kb, (((0,), (0,)), ((), ())),
                                                  preferred_element_type=F32)
        nst[h] = g_state * nrow + jnp.sum(g_in * k, axis=0, keepdims=True)
        mst[h] = jnp.broadcast_to(m_new, (1, N_GATE_PAD))

        y = _rms(hm, gout_ref[...]) * jax.nn.sigmoid(og_ref[0, :, sl])
        y_ref[0, :, sl] = y.astype(y_ref.dtype)

    @pl.when(c == pl.num_programs(1) - 1)
    def _():
        cn_ref[0] = cst[...]
        nn_ref[0] = nst[...]
        mn_ref[0] = mst[...]
        cvn_ref[0] = ubuf[L + 5:L + 8, :]


def _mlstm_prompt(z3, zg3, c0, n0, m0, conv0, w, *, rider=None):
    b, t, _ = z3.shape
    nc = t // CHUNK
    u_blk, v_blk, o_blk = 2 * D_RNN // D_MLSTM, 3 * D_RNN // D_MLSTM, 4 * D_RNN // D_MLSTM
    zcol = lambda blk: pl.BlockSpec((1, CHUNK, D_MLSTM), lambda bi, ci: (bi, ci, blk))
    state4 = lambda *minor: pl.BlockSpec((1, MLSTM_HEADS) + minor, lambda bi, ci: (bi, 0, 0, 0))
    conv_spec = pl.BlockSpec((1, CONV_W - 1, D_MLSTM), lambda bi, ci: (bi, 0, 0))
    const2 = lambda shape: pl.BlockSpec(shape, lambda bi, ci: (0, 0))
    r_in, r_out, r_shape, r_args = _rider_parts(rider)
    in_specs = [zcol(u_blk), zcol(v_blk), zcol(o_blk),
                pl.BlockSpec((1, CHUNK, N_GATE_PAD), lambda bi, ci: (bi, ci, 0)),
                const2((1, N_GATE_PAD)),
                state4(MLSTM_HD, MLSTM_HD), state4(1, MLSTM_HD), state4(1, N_GATE_PAD), conv_spec,
                const2((CONV_W, D_MLSTM)), const2((1, D_MLSTM)),
                pl.BlockSpec((MLSTM_HEADS, MLSTM_HD, 2 * MLSTM_HD), lambda bi, ci: (0, 0, 0)),
                const2((1, MLSTM_HD))]
    out_specs = [pl.BlockSpec((1, CHUNK, D_MLSTM), lambda bi, ci: (bi, ci, 0)),
                 state4(MLSTM_HD, MLSTM_HD), state4(1, MLSTM_HD), state4(1, N_GATE_PAD), conv_spec]
    out_shape = [jax.ShapeDtypeStruct((b, t, D_MLSTM), BF16),
                 jax.ShapeDtypeStruct((b, MLSTM_HEADS, MLSTM_HD, MLSTM_HD), F32),
                 jax.ShapeDtypeStruct((b, MLSTM_HEADS, 1, MLSTM_HD), F32),
                 jax.ShapeDtypeStruct((b, MLSTM_HEADS, 1, N_GATE_PAD), F32),
                 jax.ShapeDtypeStruct((b, CONV_W - 1, D_MLSTM), F32)]
    return pl.pallas_call(
        _with_rider(_mlstm_kernel, rider, len(in_specs), len(out_specs)),
        grid=(b, nc),
        in_specs=in_specs + r_in, out_specs=out_specs + r_out, out_shape=out_shape + r_shape,
        scratch_shapes=[pltpu.VMEM((CHUNK + 8, D_MLSTM), F32),
                        pltpu.VMEM((MLSTM_HEADS, MLSTM_HD, MLSTM_HD), F32),
                        pltpu.VMEM((MLSTM_HEADS, 1, MLSTM_HD), F32),
                        pltpu.VMEM((MLSTM_HEADS, 1, N_GATE_PAD), F32)],
        compiler_params=_params("parallel", "arbitrary"),
        name="mlstm_prompt",
    )(z3, z3, z3, zg3, w['gate_bias'], c0, n0, m0, conv0, w['conv_ml_w'], w['conv_ml_b'],
      w['ml_wqk'], w['g_ml_out'], *r_args)


def _xattn_kernel(q_ref, k_ref, v_ref, x_ref, wo_ref, o_ref, att):
    @pl.when(pl.program_id(2) == 0)
    def _():
        for h in range(X_HEADS):
            sl = slice(h * X_HD, (h + 1) * X_HD)
            s = _dot_t(q_ref[0, :, sl], k_ref[0, :, sl]) * (X_HD ** -0.5)
            e = jnp.exp(s - jnp.max(s, axis=-1, keepdims=True))
            p = e / jnp.sum(e, axis=-1, keepdims=True)
            att[:, sl] = _dot(p, v_ref[0, :, sl]).astype(att.dtype)

    o_ref[0] = x_ref[0] + _dot(att[...], wo_ref[...])


def _xattn_prompt(q3, mk3, mv3, x3, w_co, *, tq, tn):
    b, t, d = q3.shape
    mem = pl.BlockSpec((1, N_MEM, d), lambda bi, ti, j: (bi, 0, 0))
    cols = pl.BlockSpec((1, tq, tn), lambda bi, ti, j: (bi, ti, j))
    return pl.pallas_call(
        _xattn_kernel,
        grid=(b, t // tq, d // tn),
        in_specs=[pl.BlockSpec((1, tq, d), lambda bi, ti, j: (bi, ti, 0)), mem, mem, cols,
                  pl.BlockSpec((d, tn), lambda bi, ti, j: (0, j))],
        out_specs=cols,
        out_shape=jax.ShapeDtypeStruct((b, t, d), F32),
        scratch_shapes=[pltpu.VMEM((tq, d), BF16)],
        compiler_params=_params("parallel", "parallel", "arbitrary"),
        name="xattn_prompt",
    )(q3, mk3, mv3, x3, w_co)


XS_SUB = 2 * X_HEADS
XS_PAIRS = X_HD // (2 * 128)


def _pack_heads(x):
    lead = x.shape[:-1]
    x = x.reshape(*lead, X_HEADS, 2 * XS_PAIRS, 128)
    x = jnp.swapaxes(x, -3, -2)
    return x.reshape(*lead, XS_PAIRS, XS_SUB, 128)


def _unpack_heads(x):
    lead = x.shape[:-3]
    x = x.reshape(*lead, 2 * XS_PAIRS, X_HEADS, 128)
    x = jnp.swapaxes(x, -3, -2)
    return x.reshape(*lead, X_HEADS * X_HD)


def _xattn_sample_units(q_ref, k_ref, v_ref, o_ref, r):
    def row():
        t = jnp.sum(k_ref[r] * q_ref[r], axis=1)
        t = t + pltpu.roll(t, X_HEADS, 1)
        s = jnp.sum(t, axis=-1, keepdims=True) * (X_HD ** -0.5)
        e = jnp.exp(s - jnp.max(s, axis=0, keepdims=True))
        p = e / jnp.sum(e, axis=0, keepdims=True)
        o_ref[r] = jnp.sum(p[:, None] * v_ref[r], axis=0)

    return [row]


def _xattn_sample_rider(grid, q, ck, cv):
    b = q.shape[0]
    rows = b // (grid[0] * grid[1])
    step = lambda i, j: i * grid[1] + j
    kv_spec = pl.BlockSpec((rows, N_MEM, XS_PAIRS, XS_SUB, 128), lambda i, j: (step(i, j), 0, 0, 0, 0))
    q_spec = pl.BlockSpec((rows, XS_PAIRS, XS_SUB, 128), lambda i, j: (step(i, j), 0, 0, 0))
    return dict(units=_xattn_sample_units, rows=rows,
                in_specs=[q_spec, kv_spec, kv_spec], out_specs=[q_spec],
                out_shape=[jax.ShapeDtypeStruct((b, XS_PAIRS, XS_SUB, 128), F32)],
                args=[_pack_heads(q), _pack_heads(ck.reshape(b, N_MEM, X_HEADS * X_HD)),
                      _pack_heads(cv.reshape(b, N_MEM, X_HEADS * X_HD))])


def _smix_rows_kernel(z_ref, h0_ref, rc_ref, mc_ref, cwr_ref, cbr_ref, wg_ref, bg_ref, lam_ref, gout_ref,
                      cwm_ref, cbm_ref, wqk_ref,
                      yr_ref, hn_ref, rcn_ref, mcn_ref, q_ref, k_ref, ybuf):
    nb = z_ref.shape[0]
    ssq = jnp.zeros((nb, 1), F32)
    for h in range(RNN_HEADS):
        sl = slice(h * RNN_HD, (h + 1) * RNN_HD)
        xc = cbr_ref[:, sl] + cwr_ref[CONV_W - 1:CONV_W, sl] * z_ref[:, sl]
        for j in range(CONV_W - 1):
            xc = xc + cwr_ref[j:j + 1, sl] * rc_ref[:, j * D_RNN + h * RNN_HD:j * D_RNN + (h + 1) * RNN_HD]
        a, u = _lru_gates(xc, wg_ref[h], bg_ref[h], lam_ref[:, sl])
        hs = a * h0_ref[:, sl] + u
        hn_ref[:, sl] = hs
        yv = hs * jax.nn.gelu(z_ref[:, D_RNN + h * RNN_HD:D_RNN + (h + 1) * RNN_HD])
        ssq = ssq + jnp.sum(yv * yv, axis=-1, keepdims=True)
        ybuf[:, sl] = yv
    yr_ref[...] = (ybuf[...] * lax.rsqrt(ssq * (1.0 / D_RNN) + EPS) * gout_ref[...]).astype(yr_ref.dtype)
    rcn_ref[:, 0:2 * D_RNN] = rc_ref[:, D_RNN:3 * D_RNN]
    rcn_ref[:, 2 * D_RNN:3 * D_RNN] = z_ref[:, 0:D_RNN]

    for h in range(MLSTM_HEADS):
        sl = slice(h * MLSTM_HD, (h + 1) * MLSTM_HD)
        uc = cbm_ref[:, sl] + cwm_ref[CONV_W - 1:CONV_W, sl] * z_ref[:, 2 * D_RNN + h * MLSTM_HD:2 * D_RNN + (h + 1) * MLSTM_HD]
        for j in range(CONV_W - 1):
            uc = uc + cwm_ref[j:j + 1, sl] * mc_ref[:, j * D_MLSTM + h * MLSTM_HD:j * D_MLSTM + (h + 1) * MLSTM_HD]
        uc = jax.nn.silu(uc)
        qk = _dot(uc.astype(BF16), wqk_ref[h])
        q_ref[:, sl] = qk[:, :MLSTM_HD]
        k_ref[:, sl] = qk[:, MLSTM_HD:] * (MLSTM_HD ** -0.5)
    mcn_ref[:, 0:2 * D_MLSTM] = mc_ref[:, D_MLSTM:3 * D_MLSTM]
    mcn_ref[:, 2 * D_MLSTM:3 * D_MLSTM] = z_ref[:, 2 * D_RNN:2 * D_RNN + D_MLSTM]


def _smix_rows(z, h0, rconv, mconv, w):
    nb = z.shape[0]
    outs = [jax.ShapeDtypeStruct((nb, D_RNN), BF16),
            jax.ShapeDtypeStruct((nb, D_RNN), F32),
            jax.ShapeDtypeStruct((nb, 3 * D_RNN), F32),
            jax.ShapeDtypeStruct((nb, 3 * D_MLSTM), F32),
            jax.ShapeDtypeStruct((nb, D_MLSTM), F32),
            jax.ShapeDtypeStruct((nb, D_MLSTM), F32)]
    return pl.pallas_call(
        _smix_rows_kernel,
        out_shape=outs,
        scratch_shapes=[pltpu.VMEM((nb, D_RNN), F32)],
        compiler_params=pltpu.CompilerParams(vmem_limit_bytes=V7X_VMEM_LIMIT_BYTES),
        name="smix_rows",
    )(z, h0, rconv, mconv, w['conv_rnn_w'], w['conv_rnn_b'], w['lru_wg'], w['lru_bg'], w['lru_lambda'],
      w['g_rnn_out'], w['conv_ml_w'], w['conv_ml_b'], w['ml_wqk'])


MXU_MIN_ROWS = 8


def _smix_state_units(c_ref, q_ref, k_ref, n_ref, v_ref, og_ref, zg_ref, gb_ref, m_ref, gout_ref,
                      cn_ref, nn_ref, mn_ref, y_ref, r):
    lane_g = lax.broadcasted_iota(jnp.int32, (1, N_GATE_PAD), 1)
    first_row = lax.broadcasted_iota(jnp.int32, (MXU_MIN_ROWS, MLSTM_HD), 0) == 0

    def head(h, zg, m_out):
        sl = slice(h * MLSTM_HD, (h + 1) * MLSTM_HD)
        q = q_ref[r, :, sl]
        k = k_ref[r, :, sl]
        n = n_ref[r, :, sl]
        v = v_ref[r, :, sl]
        ig = zg[:, h:h + 1]
        lf = jax.nn.log_sigmoid(zg[:, MLSTM_HEADS + h:MLSTM_HEADS + h + 1])
        m_prev = m_ref[r, :, h:h + 1]
        m_t = jnp.maximum(lf + m_prev, ig)
        w_inter = jnp.exp(lf + m_prev - m_t)
        g_in = jnp.exp(ig - m_t)
        s = jnp.sum(q * k, axis=1, keepdims=True) * g_in
        den = w_inter * jnp.sum(n * q, axis=1, keepdims=True) + s
        denom = jnp.maximum(jnp.abs(den), jnp.exp(-m_t))
        cmat = c_ref[r, h]
        q8 = jnp.broadcast_to(q, (MXU_MIN_ROWS, MLSTM_HD)).astype(BF16)
        cq = lax.dot_general(q8, cmat.astype(BF16), (((1,), (1,)), ((), ())),
                             preferred_element_type=F32)[0:1, :]
        hm = (w_inter * cq + s * v) / denom
        gv8 = jnp.where(first_row, jnp.broadcast_to(g_in * v, (MXU_MIN_ROWS, MLSTM_HD)), 0.0).astype(BF16)
        k8 = jnp.broadcast_to(k, (MXU_MIN_ROWS, MLSTM_HD)).astype(BF16)
        outer = lax.dot_general(gv8, k8, (((0,), (0,)), ((), ())), preferred_element_type=F32)
        cn_ref[r, h] = w_inter * cmat + outer
        nn_ref[r, :, sl] = w_inter * n + g_in * k
        y = _rms(hm, gout_ref[...]) * jax.nn.sigmoid(og_ref[r, :, sl])
        y_ref[r, :, sl] = y.astype(y_ref.dtype)
        return jnp.where(lane_g == h, m_t, m_out)

    def row():
        zg = zg_ref[r] + gb_ref[...]
        m_out = jnp.zeros((1, N_GATE_PAD), F32)
        for h in range(MLSTM_HEADS):
            m_out = head(h, zg, m_out)
        mn_ref[r] = m_out

    return [row]


def _smix_state_rider(grid, c, q3, k3, n3, z3, zg3, gate_bias, m3, gout):
    nb = c.shape[0]
    rows = nb // (grid[0] * grid[1])
    step = lambda i, j: i * grid[1] + j
    row = lambda width, col=0: pl.BlockSpec((rows, 1, width), lambda i, j: (step(i, j), 0, col))
    cblk = pl.BlockSpec((rows, MLSTM_HEADS, MLSTM_HD, MLSTM_HD), lambda i, j: (step(i, j), 0, 0, 0))
    return dict(
        units=_smix_state_units, rows=rows,
        in_specs=[cblk, row(D_MLSTM), row(D_MLSTM), row(D_MLSTM),
                  row(D_MLSTM, 3 * D_RNN // D_MLSTM), row(D_MLSTM, 4 * D_RNN // D_MLSTM),
                  row(N_GATE_PAD),
                  pl.BlockSpec((1, N_GATE_PAD), lambda i, j: (0, 0)),
                  row(MLSTM_HEADS),
                  pl.BlockSpec((1, MLSTM_HD), lambda i, j: (0, 0))],
        out_specs=[cblk, row(D_MLSTM), row(N_GATE_PAD), row(D_MLSTM)],
        out_shape=[jax.ShapeDtypeStruct(c.shape, F32),
                   jax.ShapeDtypeStruct((nb, 1, D_MLSTM), F32),
                   jax.ShapeDtypeStruct((nb, 1, N_GATE_PAD), F32),
                   jax.ShapeDtypeStruct((nb, 1, D_MLSTM), F32)],
        args=[c, q3, k3, n3, z3, z3, zg3, gate_bias, m3, gout])


RGLRU_CHUNK = 256
PROMPT_TM = 1024


def _layer(xp, xs, mem, rg_h, rg_conv, c, n, mst, ml_conv, ck, cv, w, g_final, final_norm):
    b, t, d = xp.shape
    nb = xs.shape[0]
    m = b * t
    tm = PROMPT_TM
    xp2 = xp.reshape(m, d)
    xs2 = xs.reshape(nb, d)
    w_in_args = dict(tn=1024, out_dtype=F32, w_extra=w['w_in_gate'], n_cols=D_MAIN, w_is_transposed=True)

    zs, zgs = _norm_matmul(xs2, w['g_mix'], w['w_in_main'], tm=nb, **w_in_args)
    ys_rnn, s_h, s_rconv, s_mconv, qs, ks = _smix_rows(
        zs, rg_h, rg_conv.reshape(nb, 3 * D_RNN), ml_conv.reshape(nb, 3 * D_MLSTM), w)

    z, zg = _norm_matmul(xp2, w['g_mix'], w['w_in_main'], tm=tm, **w_in_args)
    z3 = z.reshape(b, t, D_MAIN)
    zg3 = zg.reshape(b, t, N_GATE_PAD)
    h0 = jnp.zeros((b, 1, D_RNN), F32)
    conv0 = jnp.zeros((b, CONV_W - 1, D_RNN), F32)
    state_rider = _smix_state_rider(
        (b, t // RGLRU_CHUNK), c, qs.reshape(nb, 1, D_MLSTM), ks.reshape(nb, 1, D_MLSTM),
        n.reshape(nb, 1, D_MLSTM), zs.reshape(nb, 1, D_MAIN), zgs.reshape(nb, 1, N_GATE_PAD), w['gate_bias'],
        mst.reshape(nb, 1, MLSTM_HEADS), w['g_ml_out'])
    y_rnn, p_h, p_rconv, s_c, s_n, s_m, ys_ml = _rglru_prompt(z3, h0, conv0, w, tc=RGLRU_CHUNK, rider=state_rider)

    xs1, qx = _matmul_res_norm_matmul([ys_rnn, ys_ml.reshape(nb, D_MLSTM).astype(BF16)], w['w_out_cq'], xs2,
                                      w['g_xattn'], tm=nb, tn=1024, out2_dtype=F32)

    c0 = jnp.zeros((b, MLSTM_HEADS, MLSTM_HD, MLSTM_HD), F32)
    n0 = jnp.zeros((b, MLSTM_HEADS, 1, MLSTM_HD), F32)
    m0 = jnp.zeros((b, MLSTM_HEADS, 1, N_GATE_PAD), F32)
    mconv0 = jnp.zeros((b, CONV_W - 1, D_MLSTM), F32)
    y_ml, p_c, p_n, p_m, p_mconv, os_packed = _mlstm_prompt(
        z3, zg3, c0, n0, m0, mconv0, w, rider=_xattn_sample_rider((b, t // CHUNK), qx, ck, cv))

    os_ = _unpack_heads(os_packed).astype(BF16)
    xs2_ = _matmul_res([os_], w['w_co'], xs1, tm=nb, tn=1024)
    xs_out = _ffn(xs2_, w['g_ffn'], w['w_up'], w['w_down'], g_final, tm=nb, tf=512, final_norm=final_norm)

    x1, q = _matmul_res_norm_matmul([y_rnn.reshape(m, D_RNN), y_ml.reshape(m, D_MLSTM)], w['w_out_cq'], xp2,
                                    w['g_xattn'], tm=tm, tn=1024, out2_dtype=BF16)
    mem2 = mem.reshape(b * N_MEM, d)
    mk = _norm_matmul(mem2, w['g_mem'], w['w_mk'], tm=b * N_MEM, tn=1024, out_dtype=F32)
    mv = _norm_matmul(mem2, w['g_mem'], w['w_mv'], tm=b * N_MEM, tn=1024, out_dtype=F32)
    x2 = _xattn_prompt(q.reshape(b, t, d), mk.reshape(b, N_MEM, d), mv.reshape(b, N_MEM, d), x1.reshape(b, t, d),
                       w['w_co'], tq=tm, tn=1024)
    xp_out = _ffn(x2.reshape(m, d), w['g_ffn'], w['w_up'], w['w_down'], g_final, tm=tm, tf=512,
                  final_norm=final_norm)

    new_p = (p_h.reshape(b, D_RNN), p_rconv, p_c, p_n.reshape(b, MLSTM_HEADS, MLSTM_HD), p_m[:, :, 0, 0], p_mconv,
             mk.reshape(b, N_MEM, X_HEADS, X_HD), mv.reshape(b, N_MEM, X_HEADS, X_HD))
    new_s = (s_h, s_rconv.reshape(nb, CONV_W - 1, D_RNN), s_c, s_n.reshape(nb, MLSTM_HEADS, MLSTM_HD),
             s_m[:, 0, :MLSTM_HEADS], s_mconv.reshape(nb, CONV_W - 1, D_MLSTM))
    return xp_out.reshape(b, t, d), xs_out.reshape(nb, 1, d), new_p, new_s


def _layer_weights(l, g_mix, w_in, conv_rnn_w, conv_rnn_b, lru_wa, lru_ba, lru_wx, lru_bx, lru_lambda,
                   g_rnn_out, conv_ml_w, conv_ml_b, ml_wq, ml_wk, ml_bi, ml_bf, g_ml_out, w_out,
                   g_xattn, g_mem, w_cq, w_mk, w_mv, w_co, g_ffn, w_up, w_down):
    n_gate = 2 * MLSTM_HEADS
    w_in_t = jnp.swapaxes(w_in[l], 0, 1)
    w_gate_t = jnp.pad(w_in_t[D_MAIN:], ((0, N_GATE_PAD - n_gate), (0, 0)))
    gate_bias = jnp.pad(jnp.concatenate([ml_bi[l], ml_bf[l]]), (0, N_GATE_PAD - n_gate))
    return dict(
        g_mix=g_mix[l], w_in_main=w_in_t, w_in_gate=w_gate_t.astype(BF16),
        conv_rnn_w=conv_rnn_w[l], conv_rnn_b=conv_rnn_b[l].reshape(1, D_RNN),
        lru_wg=jnp.concatenate([lru_wa[l], lru_wx[l]], axis=-1).astype(BF16),
        lru_bg=jnp.concatenate([lru_ba[l], lru_bx[l]], axis=-1).reshape(RNN_HEADS, 1, 2 * RNN_HD),
        lru_lambda=lru_lambda[l].reshape(1, D_RNN), g_rnn_out=g_rnn_out[l].reshape(1, D_RNN),
        conv_ml_w=conv_ml_w[l], conv_ml_b=conv_ml_b[l].reshape(1, D_MLSTM),
        ml_wqk=jnp.concatenate([ml_wq[l], ml_wk[l]], axis=-1).astype(BF16),
        gate_bias=gate_bias.reshape(1, N_GATE_PAD), g_ml_out=g_ml_out[l].reshape(1, MLSTM_HD),
        w_out_cq=jnp.concatenate([w_out[l].astype(BF16), w_cq[l].astype(BF16)], axis=1),
        g_xattn=g_xattn[l], g_mem=g_mem[l], w_mk=w_mk[l], w_mv=w_mv[l], w_co=w_co[l].astype(BF16), g_ffn=g_ffn[l],
        w_up=w_up[l].astype(BF16), w_down=w_down[l].astype(BF16))


def kernel(x_prompt, x_sample, mem_prompt, state_rglru_h, state_rglru_conv, state_mlstm_C, state_mlstm_n, state_mlstm_m, state_mlstm_conv, cache_mem_k, cache_mem_v, g_mix, w_in, conv_rnn_w, conv_rnn_b, lru_wa, lru_ba, lru_wx, lru_bx, lru_lambda, g_rnn_out, conv_ml_w, conv_ml_b, ml_wq, ml_wk, ml_bi, ml_bf, g_ml_out, w_out, g_xattn, g_mem, w_cq, w_mk, w_mv, w_co, g_ffn, w_up, w_down, g_final):
    depth = w_in.shape[0]
    xp, xs = x_prompt, x_sample
    p_out = [[] for _ in range(8)]
    s_out = [[] for _ in range(6)]
    for l in range(depth):
        w = _layer_weights(l, g_mix, w_in, conv_rnn_w, conv_rnn_b, lru_wa, lru_ba, lru_wx, lru_bx, lru_lambda,
                           g_rnn_out, conv_ml_w, conv_ml_b, ml_wq, ml_wk, ml_bi, ml_bf, g_ml_out, w_out,
                           g_xattn, g_mem, w_cq, w_mk, w_mv, w_co, g_ffn, w_up, w_down)
        last = l == depth - 1
        xp, xs, new_p, new_s = _layer(xp, xs, mem_prompt, state_rglru_h[l], state_rglru_conv[l], state_mlstm_C[l],
                                      state_mlstm_n[l], state_mlstm_m[l], state_mlstm_conv[l], cache_mem_k[l],
                                      cache_mem_v[l], w, g_final, last)
        for j, a in enumerate(new_p):
            p_out[j].append(a)
        for j, a in enumerate(new_s):
            s_out[j].append(a)
    P = [jnp.stack(a, axis=0) for a in p_out]
    S = [jnp.stack(a, axis=0) for a in s_out]
    return (xp, xs, P[0], P[1], P[2], P[3], P[4], P[5], P[6], P[7],
            S[0], S[1], S[2], S[3], S[4], S[5])
```

```python
import functools

import jax
import jax.numpy as jnp
from jax import lax
from jax.experimental import pallas as pl
from jax.experimental.pallas import tpu as pltpu

F32 = jnp.float32
BF16 = jnp.bfloat16

D_MODEL = 2048
D_RNN = 1024
RNN_HEADS = 8
RNN_HD = 128
CONV_W = 4
LRU_C = 8.0
D_MLSTM = 1024
MLSTM_HEADS = 4
MLSTM_HD = 256
CHUNK = 128
N_MEM = 256
X_HEADS = 4
X_HD = 512
D_FF = 8192
EPS = 1e-6
D_MAIN = 5 * 1024
N_GATE_PAD = 128

V7X_VMEM_LIMIT_BYTES = 56 * 1024 * 1024
SUBLANES = 8


def _params(*sem):
    return pltpu.CompilerParams(dimension_semantics=sem, vmem_limit_bytes=V7X_VMEM_LIMIT_BYTES)


def _rms(x, g):
    ms = jnp.mean(x * x, axis=-1, keepdims=True)
    return x * lax.rsqrt(ms + EPS) * g


def _dot(a, b):
    return jnp.dot(a.astype(BF16), b.astype(BF16), preferred_element_type=F32)


def _dot_t(a, bt):
    return lax.dot_general(a.astype(BF16), bt.astype(BF16), (((1,), (1,)), ((), ())), preferred_element_type=F32)


def _with_rider(host_body, rider, n_in, n_out):
    if rider is None:
        return host_body
    r_in, r_out = len(rider['in_specs']), len(rider['out_specs'])

    def body(*refs):
        host_in, refs = refs[:n_in], refs[n_in:]
        rider_in, refs = refs[:r_in], refs[r_in:]
        host_out, refs = refs[:n_out], refs[n_out:]
        rider_out, scratch = refs[:r_out], refs[r_out:]
        units = [u for r in range(rider['rows']) for u in rider['units'](*rider_in, *rider_out, r)]
        host_body(*host_in, *host_out, *scratch, rider=units)
    return body


def _run_rider_share(rider, i, n):
    if rider is None:
        return
    for unit in rider[i * len(rider) // n:(i + 1) * len(rider) // n]:
        unit()


def _rider_parts(rider):
    if rider is None:
        return [], [], [], []
    return rider['in_specs'], rider['out_specs'], rider['out_shape'], rider['args']


def _norm_matmul_kernel(*refs, has_extra, w_is_transposed):
    if has_extra:
        x_ref, g_ref, w_ref, we_ref, o_ref, oe_ref, xn_ref = refs
    else:
        x_ref, g_ref, w_ref, o_ref, xn_ref = refs
    dot = _dot_t if w_is_transposed else _dot

    @pl.when(pl.program_id(1) == 0)
    def _():
        xn = _rms(x_ref[...], g_ref[...]).astype(BF16)
        xn_ref[...] = xn
        if has_extra:
            oe_ref[...] = dot(xn, we_ref[...])

    o_ref[...] = dot(xn_ref[...], w_ref[...]).astype(o_ref.dtype)


def _norm_matmul(x, g, w, *, tm, tn, out_dtype, w_extra=None, n_cols=None, w_is_transposed=False):
    m, k = x.shape
    n = n_cols or w.shape[0 if w_is_transposed else 1]
    has_extra = w_extra is not None
    w_spec = (pl.BlockSpec((tn, k), lambda i, j: (j, 0)) if w_is_transposed
              else pl.BlockSpec((k, tn), lambda i, j: (0, j)))
    in_specs = [pl.BlockSpec((tm, k), lambda i, j: (i, 0)),
                pl.BlockSpec((1, k), lambda i, j: (0, 0)),
                w_spec]
    out_specs = [pl.BlockSpec((tm, tn), lambda i, j: (i, j))]
    out_shape = [jax.ShapeDtypeStruct((m, n), out_dtype)]
    args = [x, g.reshape(1, k), w]
    if has_extra:
        ne = w_extra.shape[0 if w_is_transposed else 1]
        in_specs.append(pl.BlockSpec(w_extra.shape, lambda i, j: (0, 0)))
        out_specs.append(pl.BlockSpec((tm, ne), lambda i, j: (i, 0)))
        out_shape.append(jax.ShapeDtypeStruct((m, ne), F32))
        args.append(w_extra)
    res = pl.pallas_call(
        functools.partial(_norm_matmul_kernel, has_extra=has_extra, w_is_transposed=w_is_transposed),
        grid=(m // tm, n // tn),
        in_specs=in_specs, out_specs=out_specs, out_shape=out_shape,
        scratch_shapes=[pltpu.VMEM((tm, k), BF16)],
        compiler_params=_params("parallel", "arbitrary"),
        name="norm_matmul",
    )(*args)
    return res if has_extra else res[0]


def _matmul_res_kernel(*refs, n_parts):
    a_refs, w_refs = refs[:n_parts], refs[n_parts:2 * n_parts]
    r_ref, o_ref = refs[2 * n_parts:]
    acc = r_ref[...]
    for a_ref, w_ref in zip(a_refs, w_refs):
        acc = acc + _dot(a_ref[...], w_ref[...])
    o_ref[...] = acc


def _matmul_res(a_parts, w, res, *, tm, tn):
    n_parts = len(a_parts)
    m, kp = a_parts[0].shape
    n = w.shape[1]
    a_specs = [pl.BlockSpec((tm, kp), lambda i, j: (i, 0)) for _ in a_parts]
    w_specs = [pl.BlockSpec((kp, tn), lambda i, j, p=p: (p, j)) for p in range(n_parts)]
    return pl.pallas_call(
        functools.partial(_matmul_res_kernel, n_parts=n_parts),
        grid=(m // tm, n // tn),
        in_specs=a_specs + w_specs + [pl.BlockSpec((tm, tn), lambda i, j: (i, j))],
        out_specs=pl.BlockSpec((tm, tn), lambda i, j: (i, j)),
        out_shape=jax.ShapeDtypeStruct((m, n), F32),
        compiler_params=_params("parallel", "arbitrary"),
        name="matmul_res",
    )(*a_parts, *([w] * n_parts), res)


def _matmul_res_norm_matmul_kernel(*refs, n_parts, n1):
    a_refs = refs[:n_parts]
    w_ref, r_ref, g_ref, x1_ref, q_ref, x1_s, xn_s = refs[n_parts:]
    j = pl.program_id(1)
    tn = w_ref.shape[1]
    kp = a_refs[0].shape[1]

    @pl.when(j < n1)
    def _():
        x1 = r_ref[...]
        for p, a_ref in enumerate(a_refs):
            x1 = x1 + _dot(a_ref[...], w_ref[p * kp:(p + 1) * kp, :])
        x1_ref[...] = x1
        x1_s[j] = x1

    @pl.when(j == n1)
    def _():
        ssq = sum(jnp.sum(jnp.square(x1_s[c]), axis=-1, keepdims=True) for c in range(n1))
        scale = lax.rsqrt(ssq * (1.0 / (n1 * tn)) + EPS)
        for c in range(n1):
            xn_s[:, c * tn:(c + 1) * tn] = (x1_s[c] * scale * g_ref[:, c * tn:(c + 1) * tn]).astype(xn_s.dtype)

    @pl.when(j >= n1)
    def _():
        q_ref[...] = _dot(xn_s[...], w_ref[...]).astype(q_ref.dtype)


def _matmul_res_norm_matmul(a_parts, w12, res, g, *, tm, tn, out2_dtype):
    n_parts = len(a_parts)
    m, kp = a_parts[0].shape
    d = res.shape[1]
    n1 = d // tn
    n2 = (w12.shape[1] - d) // tn
    return pl.pallas_call(
        functools.partial(_matmul_res_norm_matmul_kernel, n_parts=n_parts, n1=n1),
        grid=(m // tm, n1 + n2),
        in_specs=[pl.BlockSpec((tm, kp), lambda i, j: (i, 0))] * n_parts + [
            pl.BlockSpec((n_parts * kp, tn), lambda i, j: (0, j)),
            pl.BlockSpec((tm, tn), lambda i, j: (i, jnp.minimum(j, n1 - 1))),
            pl.BlockSpec((1, d), lambda i, j: (0, 0))],
        out_specs=[pl.BlockSpec((tm, tn), lambda i, j: (i, jnp.minimum(j, n1 - 1))),
                   pl.BlockSpec((tm, tn), lambda i, j: (i, jnp.maximum(j - n1, 0)))],
        out_shape=[jax.ShapeDtypeStruct((m, d), F32), jax.ShapeDtypeStruct((m, n2 * tn), out2_dtype)],
        scratch_shapes=[pltpu.VMEM((n1, tm, tn), F32), pltpu.VMEM((tm, d), BF16)],
        compiler_params=_params("parallel", "arbitrary"),
        name="matmul_res_norm_matmul",
    )(*a_parts, w12, res, g.reshape(1, d))


def _ffn_kernel(x_ref, g_ref, wu_ref, wd_ref, gf_ref, o_ref, xn_ref, *, final_norm, rider=None):
    f = pl.program_id(1)

    @pl.when(f == 0)
    def _():
        xn_ref[...] = _rms(x_ref[...], g_ref[...]).astype(BF16)
        o_ref[...] = jnp.zeros_like(o_ref)

    _run_rider_share(rider, 0, 1)
    h = _dot(xn_ref[...], wu_ref[...])
    h = jnp.square(jnp.maximum(h, 0.0)).astype(BF16)
    o_ref[...] += _dot(h, wd_ref[...])

    @pl.when(f == pl.num_programs(1) - 1)
    def _():
        y = x_ref[...] + o_ref[...]
        if final_norm:
            y = _rms(y, gf_ref[...])
        o_ref[...] = y


def _ffn(x, g, w_up, w_down, g_final, *, tm, tf, final_norm, rider=None):
    m, d = x.shape
    dff = w_up.shape[1]
    r_in, r_out, r_shape, r_args = _rider_parts(rider)
    in_specs = [pl.BlockSpec((tm, d), lambda i, f: (i, 0), pipeline_mode=pl.Buffered(1)),
                pl.BlockSpec((1, d), lambda i, f: (0, 0)),
                pl.BlockSpec((d, tf), lambda i, f: (0, f)),
                pl.BlockSpec((tf, d), lambda i, f: (f, 0)),
                pl.BlockSpec((1, d), lambda i, f: (0, 0))]
    out_specs = [pl.BlockSpec((tm, d), lambda i, f: (i, 0))]
    return pl.pallas_call(
        _with_rider(functools.partial(_ffn_kernel, final_norm=final_norm), rider, len(in_specs), len(out_specs)),
        grid=(m // tm, dff // tf),
        in_specs=in_specs + r_in, out_specs=out_specs + r_out,
        out_shape=[jax.ShapeDtypeStruct((m, d), F32)] + r_shape,
        scratch_shapes=[pltpu.VMEM((tm, d), BF16)],
        compiler_params=_params("parallel", "arbitrary"),
        name="ffn",
    )(x, g.reshape(1, d), w_up, w_down, g_final.reshape(1, d), *r_args)


def _lru_gates(xc, wg, bg, lam):
    g = _dot(xc.astype(BF16), wg) + bg
    r = jax.nn.sigmoid(g[:, :RNN_HD])
    i = jax.nn.sigmoid(g[:, RNN_HD:])
    log_a = -LRU_C * r * jax.nn.softplus(-lam)
    a = jnp.exp(log_a)
    u = jnp.sqrt(-jnp.tanh(log_a) * (a * a + 1.0)) * (i * xc)
    return a, u


def _rglru_kernel(xr_ref, gr_ref, h0_ref, c0_ref, cw_ref, cb_ref, wg_ref, bg_ref, lam_ref, gout_ref,
                  y_ref, hn_ref, cn_ref, xbuf, hc, ybuf, *, tc, rider=None):
    c = pl.program_id(1)

    @pl.when(c == 0)
    def _():
        xbuf[0:8, :] = jnp.zeros((8, D_RNN), F32)
        xbuf[5:8, :] = c0_ref[0]
        hc[...] = h0_ref[0]

    @pl.when(c > 0)
    def _():
        xbuf[0:8, :] = xbuf[tc:tc + 8, :]

    xbuf[8:8 + tc, :] = xr_ref[0]
    ng = tc // SUBLANES
    sub = lax.broadcasted_iota(jnp.int32, (ng, SUBLANES, RNN_HD), 1)
    ssq = jnp.zeros((tc, 1), F32)
    for h in range(RNN_HEADS):
        _run_rider_share(rider, h, RNN_HEADS)
        sl = slice(h * RNN_HD, (h + 1) * RNN_HD)
        xc = cb_ref[:, sl]
        for j in range(CONV_W):
            xc = xc + cw_ref[j:j + 1, sl] * xbuf[5 + j:5 + j + tc, sl]
        a, u = _lru_gates(xc, wg_ref[h], bg_ref[h], lam_ref[:, sl])
        a = a.reshape(ng, SUBLANES, RNN_HD)
        u = u.reshape(ng, SUBLANES, RNN_HD)
        d = 1
        while d < SUBLANES:
            keep = sub >= d
            a_prev = jnp.where(keep, pltpu.roll(a, d, 1), 1.0)
            u_prev = jnp.where(keep, pltpu.roll(u, d, 1), 0.0)
            u = u + a * u_prev
            a = a * a_prev
            d *= 2
        carry = hc[:, sl]
        for g in range(ng):
            hg = u[g] + a[g] * carry
            carry = hg[SUBLANES - 1:SUBLANES, :]
            ybuf[g * SUBLANES:(g + 1) * SUBLANES, sl] = hg
        hc[:, sl] = carry
        yv = ybuf[:, sl] * jax.nn.gelu(gr_ref[0, :, sl])
        ssq = ssq + jnp.sum(yv * yv, axis=-1, keepdims=True)
        ybuf[:, sl] = yv
    y = ybuf[...] * lax.rsqrt(ssq * (1.0 / D_RNN) + EPS) * gout_ref[...]
    y_ref[0] = y.astype(y_ref.dtype)

    @pl.when(c == pl.num_programs(1) - 1)
    def _():
        hn_ref[0] = hc[...]
        cn_ref[0] = xbuf[tc + 5:tc + 8, :]


def _rglru_prompt(z3, h0, conv0, w, *, tc, rider=None):
    b, t, _ = z3.shape
    full2 = lambda bi, ci: (0, 0)
    r_in, r_out, r_shape, r_args = _rider_parts(rider)
    in_specs = [pl.BlockSpec((1, tc, D_RNN), lambda bi, ci: (bi, ci, 0)),
                pl.BlockSpec((1, tc, D_RNN), lambda bi, ci: (bi, ci, 1)),
                pl.BlockSpec((1, 1, D_RNN), lambda bi, ci: (bi, 0, 0)),
                pl.BlockSpec((1, CONV_W - 1, D_RNN), lambda bi, ci: (bi, 0, 0)),
                pl.BlockSpec((CONV_W, D_RNN), full2),
                pl.BlockSpec((1, D_RNN), full2),
                pl.BlockSpec((RNN_HEADS, RNN_HD, 2 * RNN_HD), lambda bi, ci: (0, 0, 0)),
                pl.BlockSpec((RNN_HEADS, 1, 2 * RNN_HD), lambda bi, ci: (0, 0, 0)),
                pl.BlockSpec((1, D_RNN), full2),
                pl.BlockSpec((1, D_RNN), full2)]
    out_specs = [pl.BlockSpec((1, tc, D_RNN), lambda bi, ci: (bi, ci, 0)),
                 pl.BlockSpec((1, 1, D_RNN), lambda bi, ci: (bi, 0, 0)),
                 pl.BlockSpec((1, CONV_W - 1, D_RNN), lambda bi, ci: (bi, 0, 0))]
    out_shape = [jax.ShapeDtypeStruct((b, t, D_RNN), BF16),
                 jax.ShapeDtypeStruct((b, 1, D_RNN), F32),
                 jax.ShapeDtypeStruct((b, CONV_W - 1, D_RNN), F32)]
    return pl.pallas_call(
        _with_rider(functools.partial(_rglru_kernel, tc=tc), rider, len(in_specs), len(out_specs)),
        grid=(b, t // tc),
        in_specs=in_specs + r_in, out_specs=out_specs + r_out, out_shape=out_shape + r_shape,
        scratch_shapes=[pltpu.VMEM((tc + 8, D_RNN), F32),
                        pltpu.VMEM((1, D_RNN), F32),
                        pltpu.VMEM((tc, D_RNN), F32)],
        compiler_params=_params("parallel", "arbitrary"),
        name="rglru_prompt",
    )(z3, z3, h0, conv0, w['conv_rnn_w'], w['conv_rnn_b'], w['lru_wg'], w['lru_bg'],
      w['lru_lambda'], w['g_rnn_out'], *r_args)


def _mlstm_kernel(u_ref, v_ref, og_ref, zg_ref, gb_ref, c0_ref, n0_ref, m0_ref, cv0_ref,
                  cw_ref, cb_ref, wqk_ref, gout_ref,
                  y_ref, cn_ref, nn_ref, mn_ref, cvn_ref, ubuf, cst, nst, mst, rider=None):
    c = pl.program_id(1)
    L = CHUNK

    @pl.when(c == 0)
    def _():
        ubuf[0:8, :] = jnp.zeros((8, D_MLSTM), F32)
        ubuf[5:8, :] = cv0_ref[0]
        cst[...] = c0_ref[0]
        nst[...] = n0_ref[0]
        mst[...] = m0_ref[0]

    @pl.when(c > 0)
    def _():
        ubuf[0:8, :] = ubuf[L:L + 8, :]

    ubuf[8:8 + L, :] = u_ref[0]

    zg = zg_ref[0] + gb_ref[...]
    zgt = zg.T
    ti = lax.broadcasted_iota(jnp.int32, (L, L), 0)
    si = lax.broadcasted_iota(jnp.int32, (L, L), 1)
    causal = si <= ti
    anti = ti <= si

    for h in range(MLSTM_HEADS):
        _run_rider_share(rider, h, MLSTM_HEADS)
        sl = slice(h * MLSTM_HD, (h + 1) * MLSTM_HD)
        uc = cb_ref[:, sl]
        for j in range(CONV_W):
            uc = uc + cw_ref[j:j + 1, sl] * ubuf[5 + j:5 + j + L, sl]
        uc = jax.nn.silu(uc)
        qk = _dot(uc.astype(BF16), wqk_ref[h])
        q = qk[:, :MLSTM_HD]
        k = qk[:, MLSTM_HD:] * (MLSTM_HD ** -0.5)
        v = v_ref[0, :, sl]
        qb, kb = q.astype(BF16), k.astype(BF16)

        icol = zg[:, h:h + 1]
        fcol = jax.nn.log_sigmoid(zg[:, MLSTM_HEADS + h:MLSTM_HEADS + h + 1])
        irow = zgt[h:h + 1, :]
        frow = jax.nn.log_sigmoid(zgt[MLSTM_HEADS + h:MLSTM_HEADS + h + 1, :])
        bcol = jnp.sum(jnp.where(causal, frow, 0.0), axis=1, keepdims=True)
        brow = jnp.sum(jnp.where(anti, fcol, 0.0), axis=0, keepdims=True)
        dmat = jnp.where(causal, irow + bcol - brow, -jnp.inf)
        m_prev = mst[h, :, 0:1]
        inter = bcol + m_prev
        m_t = jnp.maximum(inter, jnp.max(dmat, axis=1, keepdims=True))
        w_inter = jnp.exp(inter - m_t)
        s = lax.dot_general(qb, kb, (((1,), (1,)), ((), ())), preferred_element_type=F32) * jnp.exp(dmat - m_t)
        cmat = cst[h]
        nrow = nst[h]
        cq = lax.dot_general(qb, cmat.astype(BF16), (((1,), (1,)), ((), ())), preferred_element_type=F32)
        num = w_inter * cq + _dot(s.astype(BF16), v.astype(BF16))
        den = w_inter * jnp.sum(q * nrow, axis=1, keepdims=True) + jnp.sum(s, axis=1, keepdims=True)
        hm = num / jnp.maximum(jnp.abs(den), jnp.exp(-m_t))

        m_new = m_t[L - 1:L, :]
        b_last = bcol[L - 1:L, :]
        g_state = jnp.exp(b_last + m_prev - m_new)
        g_in = jnp.exp(icol + b_last - bcol - m_new)
        cst[h] = g_state * cmat + lax.dot_general((g_in * v).astype(BF16), kb, (((0,), (0,)), ((), ())),
                                                  preferred_element_type=F32)
        nst[h] = g_state * nrow + jnp.sum(g_in * k, axis=0, keepdims=True)
        mst[h] = jnp.broadcast_to(m_new, (1, N_GATE_PAD))

        y = _rms(hm, gout_ref[...]) * jax.nn.sigmoid(og_ref[0, :, sl])
        y_ref[0, :, sl] = y.astype(y_ref.dtype)

    @pl.when(c == pl.num_programs(1) - 1)
    def _():
        cn_ref[0] = cst[...]
        nn_ref[0] = nst[...]
        mn_ref[0] = mst[...]
        cvn_ref[0] = ubuf[L + 5:L + 8, :]


def _mlstm_prompt(z3, zg3, c0, n0, m0, conv0, w, *, rider=None):
    b, t, _ = z3.shape
    nc = t // CHUNK
    u_blk, v_blk, o_blk = 2 * D_RNN // D_MLSTM, 3 * D_RNN // D_MLSTM, 4 * D_RNN // D_MLSTM
    zcol = lambda blk: pl.BlockSpec((1, CHUNK, D_MLSTM), lambda bi, ci: (bi, ci, blk))
    state4 = lambda *minor: pl.BlockSpec((1, MLSTM_HEADS) + minor, lambda bi, ci: (bi, 0, 0, 0))
    conv_spec = pl.BlockSpec((1, CONV_W - 1, D_MLSTM), lambda bi, ci: (bi, 0, 0))
    const2 = lambda shape: pl.BlockSpec(shape, lambda bi, ci: (0, 0))
    r_in, r_out, r_shape, r_args = _rider_parts(rider)
    in_specs = [zcol(u_blk), zcol(v_blk), zcol(o_blk),
                pl.BlockSpec((1, CHUNK, N_GATE_PAD), lambda bi, ci: (bi, ci, 0)),
                const2((1, N_GATE_PAD)),
                state4(MLSTM_HD, MLSTM_HD), state4(1, MLSTM_HD), state4(1, N_GATE_PAD), conv_spec,
                const2((CONV_W, D_MLSTM)), const2((1, D_MLSTM)),
                pl.BlockSpec((MLSTM_HEADS, MLSTM_HD, 2 * MLSTM_HD), lambda bi, ci: (0, 0, 0)),
                const2((1, MLSTM_HD))]
    out_specs = [pl.BlockSpec((1, CHUNK, D_MLSTM), lambda bi, ci: (bi, ci, 0)),
                 state4(MLSTM_HD, MLSTM_HD), state4(1, MLSTM_HD), state4(1, N_GATE_PAD), conv_spec]
    out_shape = [jax.ShapeDtypeStruct((b, t, D_MLSTM), BF16),
                 jax.ShapeDtypeStruct((b, MLSTM_HEADS, MLSTM_HD, MLSTM_HD), F32),
                 jax.ShapeDtypeStruct((b, MLSTM_HEADS, 1, MLSTM_HD), F32),
                 jax.ShapeDtypeStruct((b, MLSTM_HEADS, 1, N_GATE_PAD), F32),
                 jax.ShapeDtypeStruct((b, CONV_W - 1, D_MLSTM), F32)]
    return pl.pallas_call(
        _with_rider(_mlstm_kernel, rider, len(in_specs), len(out_specs)),
        grid=(b, nc),
        in_specs=in_specs + r_in, out_specs=out_specs + r_out, out_shape=out_shape + r_shape,
        scratch_shapes=[pltpu.VMEM((CHUNK + 8, D_MLSTM), F32),
                        pltpu.VMEM((MLSTM_HEADS, MLSTM_HD, MLSTM_HD), F32),
                        pltpu.VMEM((MLSTM_HEADS, 1, MLSTM_HD), F32),
                        pltpu.VMEM((MLSTM_HEADS, 1, N_GATE_PAD), F32)],
        compiler_params=_params("parallel", "arbitrary"),
        name="mlstm_prompt",
    )(z3, z3, z3, zg3, w['gate_bias'], c0, n0, m0, conv0, w['conv_ml_w'], w['conv_ml_b'],
      w['ml_wqk'], w['g_ml_out'], *r_args)


def _xattn_kernel(q_ref, k_ref, v_ref, x_ref, wo_ref, o_ref, att):
    @pl.when(pl.program_id(2) == 0)
    def _():
        for h in range(X_HEADS):
            sl = slice(h * X_HD, (h + 1) * X_HD)
            s = _dot_t(q_ref[0, :, sl], k_ref[0, :, sl]) * (X_HD ** -0.5)
            e = jnp.exp(s - jnp.max(s, axis=-1, keepdims=True))
            p = e / jnp.sum(e, axis=-1, keepdims=True)
            att[:, sl] = _dot(p, v_ref[0, :, sl]).astype(att.dtype)

    o_ref[0] = x_ref[0] + _dot(att[...], wo_ref[...])


def _xattn_prompt(q3, mk3, mv3, x3, w_co, *, tq, tn):
    b, t, d = q3.shape
    mem = pl.BlockSpec((1, N_MEM, d), lambda bi, ti, j: (bi, 0, 0))
    cols = pl.BlockSpec((1, tq, tn), lambda bi, ti, j: (bi, ti, j))
    return pl.pallas_call(
        _xattn_kernel,
        grid=(b, t // tq, d // tn),
        in_specs=[pl.BlockSpec((1, tq, d), lambda bi, ti, j: (bi, ti, 0)), mem, mem, cols,
                  pl.BlockSpec((d, tn), lambda bi, ti, j: (0, j))],
        out_specs=cols,
        out_shape=jax.ShapeDtypeStruct((b, t, d), F32),
        scratch_shapes=[pltpu.VMEM((tq, d), BF16)],
        compiler_params=_params("parallel", "parallel", "arbitrary"),
        name="xattn_prompt",
    )(q3, mk3, mv3, x3, w_co)


XS_SUB = 2 * X_HEADS
XS_PAIRS = X_HD // (2 * 128)


def _pack_heads(x):
    lead = x.shape[:-1]
    x = x.reshape(*lead, X_HEADS, 2 * XS_PAIRS, 128)
    x = jnp.swapaxes(x, -3, -2)
    return x.reshape(*lead, XS_PAIRS, XS_SUB, 128)


def _unpack_heads(x):
    lead = x.shape[:-3]
    x = x.reshape(*lead, 2 * XS_PAIRS, X_HEADS, 128)
    x = jnp.swapaxes(x, -3, -2)
    return x.reshape(*lead, X_HEADS * X_HD)


def _xattn_sample_units(q_ref, k_ref, v_ref, o_ref, r):
    def row():
        t = jnp.sum(k_ref[r] * q_ref[r], axis=1)
        t = t + pltpu.roll(t, X_HEADS, 1)
        s = jnp.sum(t, axis=-1, keepdims=True) * (X_HD ** -0.5)
        e = jnp.exp(s - jnp.max(s, axis=0, keepdims=True))
        p = e / jnp.sum(e, axis=0, keepdims=True)
        o_ref[r] = jnp.sum(p[:, None] * v_ref[r], axis=0)

    return [row]


def _xattn_sample_rider(grid, q, ck, cv):
    b = q.shape[0]
    rows = b // (grid[0] * grid[1])
    step = lambda i, j: i * grid[1] + j
    kv_spec = pl.BlockSpec((rows, N_MEM, XS_PAIRS, XS_SUB, 128), lambda i, j: (step(i, j), 0, 0, 0, 0))
    q_spec = pl.BlockSpec((rows, XS_PAIRS, XS_SUB, 128), lambda i, j: (step(i, j), 0, 0, 0))
    return dict(units=_xattn_sample_units, rows=rows,
                in_specs=[q_spec, kv_spec, kv_spec], out_specs=[q_spec],
                out_shape=[jax.ShapeDtypeStruct((b, XS_PAIRS, XS_SUB, 128), F32)],
                args=[_pack_heads(q), _pack_heads(ck.reshape(b, N_MEM, X_HEADS * X_HD)),
                      _pack_heads(cv.reshape(b, N_MEM, X_HEADS * X_HD))])


def _smix_rows_kernel(z_ref, h0_ref, rc_ref, mc_ref, cwr_ref, cbr_ref, wg_ref, bg_ref, lam_ref, gout_ref,
                      cwm_ref, cbm_ref, wqk_ref,
                      yr_ref, hn_ref, rcn_ref, mcn_ref, q_ref, k_ref, ybuf):
    nb = z_ref.shape[0]
    ssq = jnp.zeros((nb, 1), F32)
    for h in range(RNN_HEADS):
        sl = slice(h * RNN_HD, (h + 1) * RNN_HD)
        xc = cbr_ref[:, sl] + cwr_ref[CONV_W - 1:CONV_W, sl] * z_ref[:, sl]
        for j in range(CONV_W - 1):
            xc = xc + cwr_ref[j:j + 1, sl] * rc_ref[:, j * D_RNN + h * RNN_HD:j * D_RNN + (h + 1) * RNN_HD]
        a, u = _lru_gates(xc, wg_ref[h], bg_ref[h], lam_ref[:, sl])
        hs = a * h0_ref[:, sl] + u
        hn_ref[:, sl] = hs
        yv = hs * jax.nn.gelu(z_ref[:, D_RNN + h * RNN_HD:D_RNN + (h + 1) * RNN_HD])
        ssq = ssq + jnp.sum(yv * yv, axis=-1, keepdims=True)
        ybuf[:, sl] = yv
    yr_ref[...] = (ybuf[...] * lax.rsqrt(ssq * (1.0 / D_RNN) + EPS) * gout_ref[...]).astype(yr_ref.dtype)
    rcn_ref[:, 0:2 * D_RNN] = rc_ref[:, D_RNN:3 * D_RNN]
    rcn_ref[:, 2 * D_RNN:3 * D_RNN] = z_ref[:, 0:D_RNN]

    for h in range(MLSTM_HEADS):
        sl = slice(h * MLSTM_HD, (h + 1) * MLSTM_HD)
        uc = cbm_ref[:, sl] + cwm_ref[CONV_W - 1:CONV_W, sl] * z_ref[:, 2 * D_RNN + h * MLSTM_HD:2 * D_RNN + (h + 1) * MLSTM_HD]
        for j in range(CONV_W - 1):
            uc = uc + cwm_ref[j:j + 1, sl] * mc_ref[:, j * D_MLSTM + h * MLSTM_HD:j * D_MLSTM + (h + 1) * MLSTM_HD]
        uc = jax.nn.silu(uc)
        qk = _dot(uc.astype(BF16), wqk_ref[h])
        q_ref[:, sl] = qk[:, :MLSTM_HD]
        k_ref[:, sl] = qk[:, MLSTM_HD:] * (MLSTM_HD ** -0.5)
    mcn_ref[:, 0:2 * D_MLSTM] = mc_ref[:, D_MLSTM:3 * D_MLSTM]
    mcn_ref[:, 2 * D_MLSTM:3 * D_MLSTM] = z_ref[:, 2 * D_RNN:2 * D_RNN + D_MLSTM]


def _smix_rows(z, h0, rconv, mconv, w):
    nb = z.shape[0]
    outs = [jax.ShapeDtypeStruct((nb, D_RNN), BF16),
            jax.ShapeDtypeStruct((nb, D_RNN), F32),
            jax.ShapeDtypeStruct((nb, 3 * D_RNN), F32),
            jax.ShapeDtypeStruct((nb, 3 * D_MLSTM), F32),
            jax.ShapeDtypeStruct((nb, D_MLSTM), F32),
            jax.ShapeDtypeStruct((nb, D_MLSTM), F32)]
    return pl.pallas_call(
        _smix_rows_kernel,
        out_shape=outs,
        scratch_shapes=[pltpu.VMEM((nb, D_RNN), F32)],
        compiler_params=pltpu.CompilerParams(vmem_limit_bytes=V7X_VMEM_LIMIT_BYTES),
        name="smix_rows",
    )(z, h0, rconv, mconv, w['conv_rnn_w'], w['conv_rnn_b'], w['lru_wg'], w['lru_bg'], w['lru_lambda'],
      w['g_rnn_out'], w['conv_ml_w'], w['conv_ml_b'], w['ml_wqk'])


MXU_MIN_ROWS = 8


def _smix_state_units(c_ref, q_ref, k_ref, n_ref, v_ref, og_ref, zg_ref, gb_ref, m_ref, gout_ref,
                      cn_ref, nn_ref, mn_ref, y_ref, r):
    lane_g = lax.broadcasted_iota(jnp.int32, (1, N_GATE_PAD), 1)
    first_row = lax.broadcasted_iota(jnp.int32, (MXU_MIN_ROWS, MLSTM_HD), 0) == 0

    def head(h, zg, m_out):
        sl = slice(h * MLSTM_HD, (h + 1) * MLSTM_HD)
        q = q_ref[r, :, sl]
        k = k_ref[r, :, sl]
        n = n_ref[r, :, sl]
        v = v_ref[r, :, sl]
        ig = zg[:, h:h + 1]
        lf = jax.nn.log_sigmoid(zg[:, MLSTM_HEADS + h:MLSTM_HEADS + h + 1])
        m_prev = m_ref[r, :, h:h + 1]
        m_t = jnp.maximum(lf + m_prev, ig)
        w_inter = jnp.exp(lf + m_prev - m_t)
        g_in = jnp.exp(ig - m_t)
        s = jnp.sum(q * k, axis=1, keepdims=True) * g_in
        den = w_inter * jnp.sum(n * q, axis=1, keepdims=True) + s
        denom = jnp.maximum(jnp.abs(den), jnp.exp(-m_t))
        cmat = c_ref[r, h]
        q8 = jnp.broadcast_to(q, (MXU_MIN_ROWS, MLSTM_HD)).astype(BF16)
        cq = lax.dot_general(q8, cmat.astype(BF16), (((1,), (1,)), ((), ())),
                             preferred_element_type=F32)[0:1, :]
        hm = (w_inter * cq + s * v) / denom
        gv8 = jnp.where(first_row, jnp.broadcast_to(g_in * v, (MXU_MIN_ROWS, MLSTM_HD)), 0.0).astype(BF16)
        k8 = jnp.broadcast_to(k, (MXU_MIN_ROWS, MLSTM_HD)).astype(BF16)
        outer = lax.dot_general(gv8, k8, (((0,), (0,)), ((), ())), preferred_element_type=F32)
        cn_ref[r, h] = w_inter * cmat + outer
        nn_ref[r, :, sl] = w_inter * n + g_in * k
        y = _rms(hm, gout_ref[...]) * jax.nn.sigmoid(og_ref[r, :, sl])
        y_ref[r, :, sl] = y.astype(y_ref.dtype)
        return jnp.where(lane_g == h, m_t, m_out)

    def row():
        zg = zg_ref[r] + gb_ref[...]
        m_out = jnp.zeros((1, N_GATE_PAD), F32)
        for h in range(MLSTM_HEADS):
            m_out = head(h, zg, m_out)
        mn_ref[r] = m_out

    return [row]


def _smix_state_rider(grid, c, q3, k3, n3, z3, zg3, gate_bias, m3, gout):
    nb = c.shape[0]
    rows = nb // (grid[0] * grid[1])
    step = lambda i, j: i * grid[1] + j
    row = lambda width, col=0: pl.BlockSpec((rows, 1, width), lambda i, j: (step(i, j), 0, col))
    cblk = pl.BlockSpec((rows, MLSTM_HEADS, MLSTM_HD, MLSTM_HD), lambda i, j: (step(i, j), 0, 0, 0))
    return dict(
        units=_smix_state_units, rows=rows,
        in_specs=[cblk, row(D_MLSTM), row(D_MLSTM), row(D_MLSTM),
                  row(D_MLSTM, 3 * D_RNN // D_MLSTM), row(D_MLSTM, 4 * D_RNN // D_MLSTM),
                  row(N_GATE_PAD),
                  pl.BlockSpec((1, N_GATE_PAD), lambda i, j: (0, 0)),
                  row(MLSTM_HEADS),
                  pl.BlockSpec((1, MLSTM_HD), lambda i, j: (0, 0))],
        out_specs=[cblk, row(D_MLSTM), row(N_GATE_PAD), row(D_MLSTM)],
        out_shape=[jax.ShapeDtypeStruct(c.shape, F32),
                   jax.ShapeDtypeStruct((nb, 1, D_MLSTM), F32),
                   jax.ShapeDtypeStruct((nb, 1, N_GATE_PAD), F32),
                   jax.ShapeDtypeStruct((nb, 1, D_MLSTM), F32)],
        args=[c, q3, k3, n3, z3, z3, zg3, gate_bias, m3, gout])


RGLRU_CHUNK = 256
PROMPT_TM = 1024


def _layer(xp, xs, mem, rg_h, rg_conv, c, n, mst, ml_conv, ck, cv, w, g_final, final_norm):
    b, t, d = xp.shape
    nb = xs.shape[0]
    m = b * t
    tm = PROMPT_TM
    xp2 = xp.reshape(m, d)
    xs2 = xs.reshape(nb, d)
    w_in_args = dict(tn=1024, out_dtype=F32, w_extra=w['w_in_gate'], n_cols=D_MAIN, w_is_transposed=True)

    zs, zgs = _norm_matmul(xs2, w['g_mix'], w['w_in_main'], tm=nb, **w_in_args)
    ys_rnn, s_h, s_rconv, s_mconv, qs, ks = _smix_rows(
        zs, rg_h, rg_conv.reshape(nb, 3 * D_RNN), ml_conv.reshape(nb, 3 * D_MLSTM), w)

    z, zg = _norm_matmul(xp2, w['g_mix'], w['w_in_main'], tm=tm, **w_in_args)
    z3 = z.reshape(b, t, D_MAIN)
    zg3 = zg.reshape(b, t, N_GATE_PAD)
    h0 = jnp.zeros((b, 1, D_RNN), F32)
    conv0 = jnp.zeros((b, CONV_W - 1, D_RNN), F32)
    state_rider = _smix_state_rider(
        (b, t // RGLRU_CHUNK), c, qs.reshape(nb, 1, D_MLSTM), ks.reshape(nb, 1, D_MLSTM),
        n.reshape(nb, 1, D_MLSTM), zs.reshape(nb, 1, D_MAIN), zgs.reshape(nb, 1, N_GATE_PAD), w['gate_bias'],
        mst.reshape(nb, 1, MLSTM_HEADS), w['g_ml_out'])
    y_rnn, p_h, p_rconv, s_c, s_n, s_m, ys_ml = _rglru_prompt(z3, h0, conv0, w, tc=RGLRU_CHUNK, rider=state_rider)

    xs1, qx = _matmul_res_norm_matmul([ys_rnn, ys_ml.reshape(nb, D_MLSTM).astype(BF16)], w['w_out_cq'], xs2,
                                      w['g_xattn'], tm=nb, tn=1024, out2_dtype=F32)

    c0 = jnp.zeros((b, MLSTM_HEADS, MLSTM_HD, MLSTM_HD), F32)
    n0 = jnp.zeros((b, MLSTM_HEADS, 1, MLSTM_HD), F32)
    m0 = jnp.zeros((b, MLSTM_HEADS, 1, N_GATE_PAD), F32)
    mconv0 = jnp.zeros((b, CONV_W - 1, D_MLSTM), F32)
    y_ml, p_c, p_n, p_m, p_mconv = _mlstm_prompt(z3, zg3, c0, n0, m0, mconv0, w)

    x1, q = _matmul_res_norm_matmul([y_rnn.reshape(m, D_RNN), y_ml.reshape(m, D_MLSTM)], w['w_out_cq'], xp2,
                                    w['g_xattn'], tm=tm, tn=1024, out2_dtype=BF16)
    mem2 = mem.reshape(b * N_MEM, d)
    mk = _norm_matmul(mem2, w['g_mem'], w['w_mk'], tm=b * N_MEM, tn=1024, out_dtype=F32)
    mv = _norm_matmul(mem2, w['g_mem'], w['w_mv'], tm=b * N_MEM, tn=1024, out_dtype=F32)
    x2 = _xattn_prompt(q.reshape(b, t, d), mk.reshape(b, N_MEM, d), mv.reshape(b, N_MEM, d), x1.reshape(b, t, d),
                       w['w_co'], tq=tm, tn=1024)
    ffn_tf = 512
    xp_out, os_packed = _ffn(x2.reshape(m, d), w['g_ffn'], w['w_up'], w['w_down'], g_final, tm=tm, tf=ffn_tf,
                             final_norm=final_norm,
                             rider=_xattn_sample_rider((m // tm, D_FF // ffn_tf), qx, ck, cv))

    os_ = _unpack_heads(os_packed).astype(BF16)
    xs2_ = _matmul_res([os_], w['w_co'], xs1, tm=nb, tn=1024)
    xs_out, = _ffn(xs2_, w['g_ffn'], w['w_up'], w['w_down'], g_final, tm=nb, tf=ffn_tf, final_norm=final_norm)

    new_p = (p_h.reshape(b, D_RNN), p_rconv, p_c, p_n.reshape(b, MLSTM_HEADS, MLSTM_HD), p_m[:, :, 0, 0], p_mconv,
             mk.reshape(b, N_MEM, X_HEADS, X_HD), mv.reshape(b, N_MEM, X_HEADS, X_HD))
    new_s = (s_h, s_rconv.reshape(nb, CONV_W - 1, D_RNN), s_c, s_n.reshape(nb, MLSTM_HEADS, MLSTM_HD),
             s_m[:, 0, :MLSTM_HEADS], s_mconv.reshape(nb, CONV_W - 1, D_MLSTM))
    return xp_out.reshape(b, t, d), xs_out.reshape(nb, 1, d), new_p, new_s


def _layer_weights(l, g_mix, w_in, conv_rnn_w, conv_rnn_b, lru_wa, lru_ba, lru_wx, lru_bx, lru_lambda,
                   g_rnn_out, conv_ml_w, conv_ml_b, ml_wq, ml_wk, ml_bi, ml_bf, g_ml_out, w_out,
                   g_xattn, g_mem, w_cq, w_mk, w_mv, w_co, g_ffn, w_up, w_down):
    n_gate = 2 * MLSTM_HEADS
    w_in_t = jnp.swapaxes(w_in[l], 0, 1)
    w_gate_t = jnp.pad(w_in_t[D_MAIN:], ((0, N_GATE_PAD - n_gate), (0, 0)))
    gate_bias = jnp.pad(jnp.concatenate([ml_bi[l], ml_bf[l]]), (0, N_GATE_PAD - n_gate))
    return dict(
        g_mix=g_mix[l], w_in_main=w_in_t, w_in_gate=w_gate_t.astype(BF16),
        conv_rnn_w=conv_rnn_w[l], conv_rnn_b=conv_rnn_b[l].reshape(1, D_RNN),
        lru_wg=jnp.concatenate([lru_wa[l], lru_wx[l]], axis=-1).astype(BF16),
        lru_bg=jnp.concatenate([lru_ba[l], lru_bx[l]], axis=-1).reshape(RNN_HEADS, 1, 2 * RNN_HD),
        lru_lambda=lru_lambda[l].reshape(1, D_RNN), g_rnn_out=g_rnn_out[l].reshape(1, D_RNN),
        conv_ml_w=conv_ml_w[l], conv_ml_b=conv_ml_b[l].reshape(1, D_MLSTM),
        ml_wqk=jnp.concatenate([ml_wq[l], ml_wk[l]], axis=-1).astype(BF16),
        gate_bias=gate_bias.reshape(1, N_GATE_PAD), g_ml_out=g_ml_out[l].reshape(1, MLSTM_HD),
        w_out_cq=jnp.concatenate([w_out[l].astype(BF16), w_cq[l].astype(BF16)], axis=1),
        g_xattn=g_xattn[l], g_mem=g_mem[l], w_mk=w_mk[l], w_mv=w_mv[l], w_co=w_co[l].astype(BF16), g_ffn=g_ffn[l],
        w_up=w_up[l].astype(BF16), w_down=w_down[l].astype(BF16))


def kernel(x_prompt, x_sample, mem_prompt, state_rglru_h, state_rglru_conv, state_mlstm_C, state_mlstm_n, state_mlstm_m, state_mlstm_conv, cache_mem_k, cache_mem_v, g_mix, w_in, conv_rnn_w, conv_rnn_b, lru_wa, lru_ba, lru_wx, lru_bx, lru_lambda, g_rnn_out, conv_ml_w, conv_ml_b, ml_wq, ml_wk, ml_bi, ml_bf, g_ml_out, w_out, g_xattn, g_mem, w_cq, w_mk, w_mv, w_co, g_ffn, w_up, w_down, g_final):
    depth = w_in.shape[0]
    xp, xs = x_prompt, x_sample
    p_out = [[] for _ in range(8)]
    s_out = [[] for _ in range(6)]
    for l in range(depth):
        w = _layer_weights(l, g_mix, w_in, conv_rnn_w, conv_rnn_b, lru_wa, lru_ba, lru_wx, lru_bx, lru_lambda,
                           g_rnn_out, conv_ml_w, conv_ml_b, ml_wq, ml_wk, ml_bi, ml_bf, g_ml_out, w_out,
                           g_xattn, g_mem, w_cq, w_mk, w_mv, w_co, g_ffn, w_up, w_down)
        last = l == depth - 1
        xp, xs, new_p, new_s = _layer(xp, xs, mem_prompt, state_rglru_h[l], state_rglru_conv[l], state_mlstm_C[l],
                                      state_mlstm_n[l], state_mlstm_m[l], state_mlstm_conv[l], cache_mem_k[l],
                                      cache_mem_v[l], w, g_final, last)
        for j, a in enumerate(new_p):
            p_out[j].append(a)
        for j, a in enumerate(new_s):
            s_out[j].append(a)
    P = [jnp.stack(a, axis=0) for a in p_out]
    S = [jnp.stack(a, axis=0) for a in s_out]
    return (xp, xs, P[0], P[1], P[2], P[3], P[4], P[5], P[6], P[7],
            S[0], S[1], S[2], S[3], S[4], S[5])
```

```python
import functools

import jax
import jax.numpy as jnp
from jax import lax
from jax.experimental import pallas as pl
from jax.experimental.pallas import tpu as pltpu

F32 = jnp.float32
BF16 = jnp.bfloat16

D_MODEL = 2048
D_RNN = 1024
RNN_HEADS = 8
RNN_HD = 128
CONV_W = 4
LRU_C = 8.0
D_MLSTM = 1024
MLSTM_HEADS = 4
MLSTM_HD = 256
CHUNK = 128
N_MEM = 256
X_HEADS = 4
X_HD = 512
D_FF = 8192
EPS = 1e-6
D_MAIN = 5 * 1024
N_GATE_PAD = 128

V7X_VMEM_LIMIT_BYTES = 56 * 1024 * 1024
SUBLANES = 8


def _params(*sem):
    return pltpu.CompilerParams(dimension_semantics=sem, vmem_limit_bytes=V7X_VMEM_LIMIT_BYTES)


def _rms(x, g):
    ms = jnp.mean(x * x, axis=-1, keepdims=True)
    return x * lax.rsqrt(ms + EPS) * g


def _dot(a, b):
    return jnp.dot(a.astype(BF16), b.astype(BF16), preferred_element_type=F32)


def _dot_t(a, bt):
    return lax.dot_general(a.astype(BF16), bt.astype(BF16), (((1,), (1,)), ((), ())), preferred_element_type=F32)


def _with_rider(host_body, rider, n_in, n_out):
    if rider is None:
        return host_body
    r_in, r_out = len(rider['in_specs']), len(rider['out_specs'])

    def body(*refs):
        host_in, refs = refs[:n_in], refs[n_in:]
        rider_in, refs = refs[:r_in], refs[r_in:]
        host_out, refs = refs[:n_out], refs[n_out:]
        rider_out, scratch = refs[:r_out], refs[r_out:]
        host_body(*host_in, *host_out, *scratch, rider=rider['make_units'](*rider_in, *rider_out))
    return body


def _per_row_units(row_units, rows):
    return lambda *refs: [u for r in range(rows) for u in row_units(*refs, r)]


def _merge_riders(a, b):
    na_in, na_out = len(a['in_specs']), len(a['out_specs'])
    nb_in = len(b['in_specs'])

    def make_units(*refs):
        a_in, refs = refs[:na_in], refs[na_in:]
        b_in, refs = refs[:nb_in], refs[nb_in:]
        a_out, b_out = refs[:na_out], refs[na_out:]
        return a['make_units'](*a_in, *a_out) + b['make_units'](*b_in, *b_out)

    return dict(make_units=make_units, **{k: a[k] + b[k] for k in ('in_specs', 'out_specs', 'out_shape', 'args')})


def _cast_rider(grid, groups):
    n_steps = grid[0] * grid[1]
    step = lambda i, j: i * grid[1] + j
    in_specs, out_specs, out_shape, args, widths = [], [], [], [], []
    for group in groups:
        rps = group[0].shape[0] // n_steps
        for a in group:
            in_specs.append(pl.BlockSpec((rps, a.shape[1]), lambda i, j: (step(i, j), 0)))
            args.append(a)
        cols = sum(a.shape[1] for a in group)
        out_specs.append(pl.BlockSpec((rps, cols), lambda i, j: (step(i, j), 0)))
        out_shape.append(jax.ShapeDtypeStruct((group[0].shape[0], cols), BF16))
        widths.append([a.shape[1] for a in group])

    def make_units(*refs):
        in_refs, out_refs = list(refs[:len(args)]), refs[len(args):]
        units = []
        for o_ref, ws in zip(out_refs, widths):
            srcs, in_refs = in_refs[:len(ws)], in_refs[len(ws):]

            def cast(o_ref=o_ref, srcs=srcs, ws=ws):
                c0 = 0
                for s_ref, wd in zip(srcs, ws):
                    o_ref[:, c0:c0 + wd] = s_ref[...].astype(BF16)
                    c0 += wd
            units.append(cast)
        return units

    return dict(make_units=make_units, in_specs=in_specs, out_specs=out_specs, out_shape=out_shape, args=args)


def _run_rider_share(rider, i, n):
    if rider is None:
        return
    for unit in rider[i * len(rider) // n:(i + 1) * len(rider) // n]:
        unit()


def _rider_parts(rider):
    if rider is None:
        return [], [], [], []
    return rider['in_specs'], rider['out_specs'], rider['out_shape'], rider['args']


def _norm_matmul_kernel(*refs, has_extra, w_is_transposed, emit_w_bf16):
    refs = list(refs)
    xn_ref = refs.pop()
    wb_ref = refs.pop() if emit_w_bf16 else None
    if has_extra:
        x_ref, g_ref, w_ref, we_ref, o_ref, oe_ref = refs
    else:
        x_ref, g_ref, w_ref, o_ref = refs
    dot = _dot_t if w_is_transposed else _dot

    @pl.when(pl.program_id(1) == 0)
    def _():
        xn = _rms(x_ref[...], g_ref[...]).astype(BF16)
        xn_ref[...] = xn
        if has_extra:
            oe_ref[...] = dot(xn, we_ref[...])

    wb = w_ref[...].astype(BF16)
    if emit_w_bf16:
        wb_ref[...] = wb
    o_ref[...] = dot(xn_ref[...], wb).astype(o_ref.dtype)


def _norm_matmul(x, g, w, *, tm, tn, out_dtype, w_extra=None, n_cols=None, w_is_transposed=False,
                 emit_w_bf16=False):
    m, k = x.shape
    n = n_cols or w.shape[0 if w_is_transposed else 1]
    has_extra = w_extra is not None
    w_spec = (pl.BlockSpec((tn, k), lambda i, j: (j, 0)) if w_is_transposed
              else pl.BlockSpec((k, tn), lambda i, j: (0, j)))
    in_specs = [pl.BlockSpec((tm, k), lambda i, j: (i, 0)),
                pl.BlockSpec((1, k), lambda i, j: (0, 0)),
                w_spec]
    out_specs = [pl.BlockSpec((tm, tn), lambda i, j: (i, j))]
    out_shape = [jax.ShapeDtypeStruct((m, n), out_dtype)]
    args = [x, g.reshape(1, k), w]
    if has_extra:
        ne = w_extra.shape[0 if w_is_transposed else 1]
        in_specs.append(pl.BlockSpec(w_extra.shape, lambda i, j: (0, 0)))
        out_specs.append(pl.BlockSpec((tm, ne), lambda i, j: (i, 0)))
        out_shape.append(jax.ShapeDtypeStruct((m, ne), F32))
        args.append(w_extra)
    if emit_w_bf16:
        assert m == tm
        out_specs.append(w_spec)
        out_shape.append(jax.ShapeDtypeStruct((n, k) if w_is_transposed else (k, n), BF16))
    res = pl.pallas_call(
        functools.partial(_norm_matmul_kernel, has_extra=has_extra, w_is_transposed=w_is_transposed,
                          emit_w_bf16=emit_w_bf16),
        grid=(m // tm, n // tn),
        in_specs=in_specs, out_specs=out_specs, out_shape=out_shape,
        scratch_shapes=[pltpu.VMEM((tm, k), BF16)],
        compiler_params=_params("parallel", "arbitrary"),
        name="norm_matmul",
    )(*args)
    return res if len(res) > 1 else res[0]


def _matmul_res_kernel(*refs, n_parts):
    a_refs, w_refs = refs[:n_parts], refs[n_parts:2 * n_parts]
    r_ref, o_ref = refs[2 * n_parts:]
    acc = r_ref[...]
    for a_ref, w_ref in zip(a_refs, w_refs):
        acc = acc + _dot(a_ref[...], w_ref[...])
    o_ref[...] = acc


def _matmul_res(a_parts, w, res, *, tm, tn):
    n_parts = len(a_parts)
    m, kp = a_parts[0].shape
    n = w.shape[1]
    a_specs = [pl.BlockSpec((tm, kp), lambda i, j: (i, 0)) for _ in a_parts]
    w_specs = [pl.BlockSpec((kp, tn), lambda i, j, p=p: (p, j)) for p in range(n_parts)]
    return pl.pallas_call(
        functools.partial(_matmul_res_kernel, n_parts=n_parts),
        grid=(m // tm, n // tn),
        in_specs=a_specs + w_specs + [pl.BlockSpec((tm, tn), lambda i, j: (i, j))],
        out_specs=pl.BlockSpec((tm, tn), lambda i, j: (i, j)),
        out_shape=jax.ShapeDtypeStruct((m, n), F32),
        compiler_params=_params("parallel", "arbitrary"),
        name="matmul_res",
    )(*a_parts, *([w] * n_parts), res)


def _matmul_res_norm_matmul_kernel(*refs, n_parts, n1):
    a_refs = refs[:n_parts]
    w_ref, r_ref, g_ref, x1_ref, q_ref, x1_s, xn_s = refs[n_parts:]
    j = pl.program_id(1)
    tn = w_ref.shape[1]
    kp = a_refs[0].shape[1]

    @pl.when(j < n1)
    def _():
        x1 = r_ref[...]
        for p, a_ref in enumerate(a_refs):
            x1 = x1 + _dot(a_ref[...], w_ref[p * kp:(p + 1) * kp, :])
        x1_ref[...] = x1
        x1_s[j] = x1

    @pl.when(j == n1)
    def _():
        ssq = sum(jnp.sum(jnp.square(x1_s[c]), axis=-1, keepdims=True) for c in range(n1))
        scale = lax.rsqrt(ssq * (1.0 / (n1 * tn)) + EPS)
        for c in range(n1):
            xn_s[:, c * tn:(c + 1) * tn] = (x1_s[c] * scale * g_ref[:, c * tn:(c + 1) * tn]).astype(xn_s.dtype)

    @pl.when(j >= n1)
    def _():
        q_ref[...] = _dot(xn_s[...], w_ref[...]).astype(q_ref.dtype)


def _matmul_res_norm_matmul(a_parts, w12, res, g, *, tm, tn, out2_dtype):
    n_parts = len(a_parts)
    m, kp = a_parts[0].shape
    d = res.shape[1]
    n1 = d // tn
    n2 = (w12.shape[1] - d) // tn
    return pl.pallas_call(
        functools.partial(_matmul_res_norm_matmul_kernel, n_parts=n_parts, n1=n1),
        grid=(m // tm, n1 + n2),
        in_specs=[pl.BlockSpec((tm, kp), lambda i, j: (i, 0))] * n_parts + [
            pl.BlockSpec((n_parts * kp, tn), lambda i, j: (0, j)),
            pl.BlockSpec((tm, tn), lambda i, j: (i, jnp.minimum(j, n1 - 1))),
            pl.BlockSpec((1, d), lambda i, j: (0, 0))],
        out_specs=[pl.BlockSpec((tm, tn), lambda i, j: (i, jnp.minimum(j, n1 - 1))),
                   pl.BlockSpec((tm, tn), lambda i, j: (i, jnp.maximum(j - n1, 0)))],
        out_shape=[jax.ShapeDtypeStruct((m, d), F32), jax.ShapeDtypeStruct((m, n2 * tn), out2_dtype)],
        scratch_shapes=[pltpu.VMEM((n1, tm, tn), F32), pltpu.VMEM((tm, d), BF16)],
        compiler_params=_params("parallel", "arbitrary"),
        name="matmul_res_norm_matmul",
    )(*a_parts, w12, res, g.reshape(1, d))


def _ffn_kernel(x_ref, g_ref, wu_ref, wd_ref, gf_ref, o_ref, xn_ref, *, final_norm, rider=None):
    f = pl.program_id(1)

    @pl.when(f == 0)
    def _():
        xn_ref[...] = _rms(x_ref[...], g_ref[...]).astype(BF16)
        o_ref[...] = jnp.zeros_like(o_ref)

    _run_rider_share(rider, 0, 1)
    h = _dot(xn_ref[...], wu_ref[...])
    h = jnp.square(jnp.maximum(h, 0.0)).astype(BF16)
    o_ref[...] += _dot(h, wd_ref[...])

    @pl.when(f == pl.num_programs(1) - 1)
    def _():
        y = x_ref[...] + o_ref[...]
        if final_norm:
            y = _rms(y, gf_ref[...])
        o_ref[...] = y


def _ffn(x, g, w_up, w_down, g_final, *, tm, tf, final_norm, rider=None):
    m, d = x.shape
    dff = w_up.shape[1]
    r_in, r_out, r_shape, r_args = _rider_parts(rider)
    in_specs = [pl.BlockSpec((tm, d), lambda i, f: (i, 0), pipeline_mode=pl.Buffered(1)),
                pl.BlockSpec((1, d), lambda i, f: (0, 0)),
                pl.BlockSpec((d, tf), lambda i, f: (0, f)),
                pl.BlockSpec((tf, d), lambda i, f: (f, 0)),
                pl.BlockSpec((1, d), lambda i, f: (0, 0))]
    out_specs = [pl.BlockSpec((tm, d), lambda i, f: (i, 0))]
    return pl.pallas_call(
        _with_rider(functools.partial(_ffn_kernel, final_norm=final_norm), rider, len(in_specs), len(out_specs)),
        grid=(m // tm, dff // tf),
        in_specs=in_specs + r_in, out_specs=out_specs + r_out,
        out_shape=[jax.ShapeDtypeStruct((m, d), F32)] + r_shape,
        scratch_shapes=[pltpu.VMEM((tm, d), BF16)],
        compiler_params=_params("parallel", "arbitrary"),
        name="ffn",
    )(x, g.reshape(1, d), w_up, w_down, g_final.reshape(1, d), *r_args)


def _lru_gates(xc, wg, bg, lam):
    g = _dot(xc.astype(BF16), wg) + bg
    r = jax.nn.sigmoid(g[:, :RNN_HD])
    i = jax.nn.sigmoid(g[:, RNN_HD:])
    log_a = -LRU_C * r * jax.nn.softplus(-lam)
    a = jnp.exp(log_a)
    u = jnp.sqrt(-jnp.tanh(log_a) * (a * a + 1.0)) * (i * xc)
    return a, u


def _rglru_kernel(xr_ref, gr_ref, h0_ref, c0_ref, cw_ref, cb_ref, wg_ref, bg_ref, lam_ref, gout_ref,
                  y_ref, hn_ref, cn_ref, xbuf, hc, ybuf, *, tc, rider=None):
    c = pl.program_id(1)

    @pl.when(c == 0)
    def _():
        xbuf[0:8, :] = jnp.zeros((8, D_RNN), F32)
        xbuf[5:8, :] = c0_ref[0]
        hc[...] = h0_ref[0]

    @pl.when(c > 0)
    def _():
        xbuf[0:8, :] = xbuf[tc:tc + 8, :]

    xbuf[8:8 + tc, :] = xr_ref[0]
    ng = tc // SUBLANES
    sub = lax.broadcasted_iota(jnp.int32, (ng, SUBLANES, RNN_HD), 1)
    ssq = jnp.zeros((tc, 1), F32)
    for h in range(RNN_HEADS):
        _run_rider_share(rider, h, RNN_HEADS)
        sl = slice(h * RNN_HD, (h + 1) * RNN_HD)
        xc = cb_ref[:, sl]
        for j in range(CONV_W):
            xc = xc + cw_ref[j:j + 1, sl] * xbuf[5 + j:5 + j + tc, sl]
        a, u = _lru_gates(xc, wg_ref[h], bg_ref[h], lam_ref[:, sl])
        a = a.reshape(ng, SUBLANES, RNN_HD)
        u = u.reshape(ng, SUBLANES, RNN_HD)
        d = 1
        while d < SUBLANES:
            keep = sub >= d
            a_prev = jnp.where(keep, pltpu.roll(a, d, 1), 1.0)
            u_prev = jnp.where(keep, pltpu.roll(u, d, 1), 0.0)
            u = u + a * u_prev
            a = a * a_prev
            d *= 2
        carry = hc[:, sl]
        for g in range(ng):
            hg = u[g] + a[g] * carry
            carry = hg[SUBLANES - 1:SUBLANES, :]
            ybuf[g * SUBLANES:(g + 1) * SUBLANES, sl] = hg
        hc[:, sl] = carry
        yv = ybuf[:, sl] * jax.nn.gelu(gr_ref[0, :, sl])
        ssq = ssq + jnp.sum(yv * yv, axis=-1, keepdims=True)
        ybuf[:, sl] = yv
    y = ybuf[...] * lax.rsqrt(ssq * (1.0 / D_RNN) + EPS) * gout_ref[...]
    y_ref[0] = y.astype(y_ref.dtype)

    @pl.when(c == pl.num_programs(1) - 1)
    def _():
        hn_ref[0] = hc[...]
        cn_ref[0] = xbuf[tc + 5:tc + 8, :]


def _rglru_prompt(z3, h0, conv0, w, *, tc, rider=None):
    b, t, _ = z3.shape
    full2 = lambda bi, ci: (0, 0)
    r_in, r_out, r_shape, r_args = _rider_parts(rider)
    in_specs = [pl.BlockSpec((1, tc, D_RNN), lambda bi, ci: (bi, ci, 0)),
                pl.BlockSpec((1, tc, D_RNN), lambda bi, ci: (bi, ci, 1)),
                pl.BlockSpec((1, 1, D_RNN), lambda bi, ci: (bi, 0, 0)),
                pl.BlockSpec((1, CONV_W - 1, D_RNN), lambda bi, ci: (bi, 0, 0)),
                pl.BlockSpec((CONV_W, D_RNN), full2),
                pl.BlockSpec((1, D_RNN), full2),
                pl.BlockSpec((RNN_HEADS, RNN_HD, 2 * RNN_HD), lambda bi, ci: (0, 0, 0)),
                pl.BlockSpec((RNN_HEADS, 1, 2 * RNN_HD), lambda bi, ci: (0, 0, 0)),
                pl.BlockSpec((1, D_RNN), full2),
                pl.BlockSpec((1, D_RNN), full2)]
    out_specs = [pl.BlockSpec((1, tc, D_RNN), lambda bi, ci: (bi, ci, 0)),
                 pl.BlockSpec((1, 1, D_RNN), lambda bi, ci: (bi, 0, 0)),
                 pl.BlockSpec((1, CONV_W - 1, D_RNN), lambda bi, ci: (bi, 0, 0))]
    out_shape = [jax.ShapeDtypeStruct((b, t, D_RNN), BF16),
                 jax.ShapeDtypeStruct((b, 1, D_RNN), F32),
                 jax.ShapeDtypeStruct((b, CONV_W - 1, D_RNN), F32)]
    return pl.pallas_call(
        _with_rider(functools.partial(_rglru_kernel, tc=tc), rider, len(in_specs), len(out_specs)),
        grid=(b, t // tc),
        in_specs=in_specs + r_in, out_specs=out_specs + r_out, out_shape=out_shape + r_shape,
        scratch_shapes=[pltpu.VMEM((tc + 8, D_RNN), F32),
                        pltpu.VMEM((1, D_RNN), F32),
                        pltpu.VMEM((tc, D_RNN), F32)],
        compiler_params=_params("parallel", "arbitrary"),
        name="rglru_prompt",
    )(z3, z3, h0, conv0, w['conv_rnn_w'], w['conv_rnn_b'], w['lru_wg'], w['lru_bg'],
      w['lru_lambda'], w['g_rnn_out'], *r_args)


def _mlstm_kernel(u_ref, v_ref, og_ref, zg_ref, gb_ref, c0_ref, n0_ref, m0_ref, cv0_ref,
                  cw_ref, cb_ref, wqk_ref, gout_ref,
                  y_ref, cn_ref, nn_ref, mn_ref, cvn_ref, ubuf, cst, nst, mst, rider=None):
    c = pl.program_id(1)
    L = CHUNK

    @pl.when(c == 0)
    def _():
        ubuf[0:8, :] = jnp.zeros((8, D_MLSTM), F32)
        ubuf[5:8, :] = cv0_ref[0]
        cst[...] = c0_ref[0]
        nst[...] = n0_ref[0]
        mst[...] = m0_ref[0]

    @pl.when(c > 0)
    def _():
        ubuf[0:8, :] = ubuf[L:L + 8, :]

    ubuf[8:8 + L, :] = u_ref[0]

    zg = zg_ref[0] + gb_ref[...]
    zgt = zg.T
    ti = lax.broadcasted_iota(jnp.int32, (L, L), 0)
    si = lax.broadcasted_iota(jnp.int32, (L, L), 1)
    causal = si <= ti
    anti = ti <= si

    for h in range(MLSTM_HEADS):
        _run_rider_share(rider, h, MLSTM_HEADS)
        sl = slice(h * MLSTM_HD, (h + 1) * MLSTM_HD)
        uc = cb_ref[:, sl]
        for j in range(CONV_W):
            uc = uc + cw_ref[j:j + 1, sl] * ubuf[5 + j:5 + j + L, sl]
        uc = jax.nn.silu(uc)
        qk = _dot(uc.astype(BF16), wqk_ref[h])
        q = qk[:, :MLSTM_HD]
        k = qk[:, MLSTM_HD:] * (MLSTM_HD ** -0.5)
        v = v_ref[0, :, sl]
        qb, kb = q.astype(BF16), k.astype(BF16)

        icol = zg[:, h:h + 1]
        fcol = jax.nn.log_sigmoid(zg[:, MLSTM_HEADS + h:MLSTM_HEADS + h + 1])
        irow = zgt[h:h + 1, :]
        frow = jax.nn.log_sigmoid(zgt[MLSTM_HEADS + h:MLSTM_HEADS + h + 1, :])
        bcol = jnp.sum(jnp.where(causal, frow, 0.0), axis=1, keepdims=True)
        brow = jnp.sum(jnp.where(anti, fcol, 0.0), axis=0, keepdims=True)
        dmat = jnp.where(causal, irow + bcol - brow, -jnp.inf)
        m_prev = mst[h, :, 0:1]
        inter = bcol + m_prev
        m_t = jnp.maximum(inter, jnp.max(dmat, axis=1, keepdims=True))
        w_inter = jnp.exp(inter - m_t)
        s = lax.dot_general(qb, kb, (((1,), (1,)), ((), ())), preferred_element_type=F32) * jnp.exp(dmat - m_t)
        cmat = cst[h]
        nrow = nst[h]
        cq = lax.dot_general(qb, cmat.astype(BF16), (((1,), (1,)), ((), ())), preferred_element_type=F32)
        num = w_inter * cq + _dot(s.astype(BF16), v.astype(BF16))
        den = w_inter * jnp.sum(q * nrow, axis=1, keepdims=True) + jnp.sum(s, axis=1, keepdims=True)
        hm = num / jnp.maximum(jnp.abs(den), jnp.exp(-m_t))

        m_new = m_t[L - 1:L, :]
        b_last = bcol[L - 1:L, :]
        g_state = jnp.exp(b_last + m_prev - m_new)
        g_in = jnp.exp(icol + b_last - bcol - m_new)
        cst[h] = g_state * cmat + lax.dot_general((g_in * v).astype(BF16), kb, (((0,), (0,)), ((), ())),
                                                  preferred_element_type=F32)
        nst[h] = g_state * nrow + jnp.sum(g_in * k, axis=0, keepdims=True)
        mst[h] = jnp.broadcast_to(m_new, (1, N_GATE_PAD))

        y = _rms(hm, gout_ref[...]) * jax.nn.sigmoid(og_ref[0, :, sl])
        y_ref[0, :, sl] = y.astype(y_ref.dtype)

    @pl.when(c == pl.num_programs(1) - 1)
    def _():
        cn_ref[0] = cst[...]
        nn_ref[0] = nst[...]
        mn_ref[0] = mst[...]
        cvn_ref[0] = ubuf[L + 5:L + 8, :]


def _mlstm_prompt(z3, zg3, c0, n0, m0, conv0, w, *, rider=None):
    b, t, _ = z3.shape
    nc = t // CHUNK
    u_blk, v_blk, o_blk = 2 * D_RNN // D_MLSTM, 3 * D_RNN // D_MLSTM, 4 * D_RNN // D_MLSTM
    zcol = lambda blk: pl.BlockSpec((1, CHUNK, D_MLSTM), lambda bi, ci: (bi, ci, blk))
    state4 = lambda *minor: pl.BlockSpec((1, MLSTM_HEADS) + minor, lambda bi, ci: (bi, 0, 0, 0))
    conv_spec = pl.BlockSpec((1, CONV_W - 1, D_MLSTM), lambda bi, ci: (bi, 0, 0))
    const2 = lambda shape: pl.BlockSpec(shape, lambda bi, ci: (0, 0))
    r_in, r_out, r_shape, r_args = _rider_parts(rider)
    in_specs = [zcol(u_blk), zcol(v_blk), zcol(o_blk),
                pl.BlockSpec((1, CHUNK, N_GATE_PAD), lambda bi, ci: (bi, ci, 0)),
                const2((1, N_GATE_PAD)),
                state4(MLSTM_HD, MLSTM_HD), state4(1, MLSTM_HD), state4(1, N_GATE_PAD), conv_spec,
                const2((CONV_W, D_MLSTM)), const2((1, D_MLSTM)),
                pl.BlockSpec((MLSTM_HEADS, MLSTM_HD, 2 * MLSTM_HD), lambda bi, ci: (0, 0, 0)),
                const2((1, MLSTM_HD))]
    out_specs = [pl.BlockSpec((1, CHUNK, D_MLSTM), lambda bi, ci: (bi, ci, 0)),
                 state4(MLSTM_HD, MLSTM_HD), state4(1, MLSTM_HD), state4(1, N_GATE_PAD), conv_spec]
    out_shape = [jax.ShapeDtypeStruct((b, t, D_MLSTM), BF16),
                 jax.ShapeDtypeStruct((b, MLSTM_HEADS, MLSTM_HD, MLSTM_HD), F32),
                 jax.ShapeDtypeStruct((b, MLSTM_HEADS, 1, MLSTM_HD), F32),
                 jax.ShapeDtypeStruct((b, MLSTM_HEADS, 1, N_GATE_PAD), F32),
                 jax.ShapeDtypeStruct((b, CONV_W - 1, D_MLSTM), F32)]
    return pl.pallas_call(
        _with_rider(_mlstm_kernel, rider, len(in_specs), len(out_specs)),
        grid=(b, nc),
        in_specs=in_specs + r_in, out_specs=out_specs + r_out, out_shape=out_shape + r_shape,
        scratch_shapes=[pltpu.VMEM((CHUNK + 8, D_MLSTM), F32),
                        pltpu.VMEM((MLSTM_HEADS, MLSTM_HD, MLSTM_HD), F32),
                        pltpu.VMEM((MLSTM_HEADS, 1, MLSTM_HD), F32),
                        pltpu.VMEM((MLSTM_HEADS, 1, N_GATE_PAD), F32)],
        compiler_params=_params("parallel", "arbitrary"),
        name="mlstm_prompt",
    )(z3, z3, z3, zg3, w['gate_bias'], c0, n0, m0, conv0, w['conv_ml_w'], w['conv_ml_b'],
      w['ml_wqk'], w['g_ml_out'], *r_args)


def _xattn_kernel(q_ref, k_ref, v_ref, x_ref, wo_ref, o_ref, att):
    @pl.when(pl.program_id(2) == 0)
    def _():
        for h in range(X_HEADS):
            sl = slice(h * X_HD, (h + 1) * X_HD)
            s = _dot_t(q_ref[0, :, sl], k_ref[0, :, sl]) * (X_HD ** -0.5)
            e = jnp.exp(s - jnp.max(s, axis=-1, keepdims=True))
            p = e / jnp.sum(e, axis=-1, keepdims=True)
            att[:, sl] = _dot(p, v_ref[0, :, sl]).astype(att.dtype)

    o_ref[0] = x_ref[0] + _dot(att[...], wo_ref[...])


def _xattn_prompt(q3, mk3, mv3, x3, w_co, *, tq, tn):
    b, t, d = q3.shape
    mem = pl.BlockSpec((1, N_MEM, d), lambda bi, ti, j: (bi, 0, 0))
    cols = pl.BlockSpec((1, tq, tn), lambda bi, ti, j: (bi, ti, j))
    return pl.pallas_call(
        _xattn_kernel,
        grid=(b, t // tq, d // tn),
        in_specs=[pl.BlockSpec((1, tq, d), lambda bi, ti, j: (bi, ti, 0)), mem, mem, cols,
                  pl.BlockSpec((d, tn), lambda bi, ti, j: (0, j))],
        out_specs=cols,
        out_shape=jax.ShapeDtypeStruct((b, t, d), F32),
        scratch_shapes=[pltpu.VMEM((tq, d), BF16)],
        compiler_params=_params("parallel", "parallel", "arbitrary"),
        name="xattn_prompt",
    )(q3, mk3, mv3, x3, w_co)


XS_SUB = 2 * X_HEADS
XS_PAIRS = X_HD // (2 * 128)


def _pack_heads(x):
    lead = x.shape[:-1]
    x = x.reshape(*lead, X_HEADS, 2 * XS_PAIRS, 128)
    x = jnp.swapaxes(x, -3, -2)
    return x.reshape(*lead, XS_PAIRS, XS_SUB, 128)


def _unpack_heads(x):
    lead = x.shape[:-3]
    x = x.reshape(*lead, 2 * XS_PAIRS, X_HEADS, 128)
    x = jnp.swapaxes(x, -3, -2)
    return x.reshape(*lead, X_HEADS * X_HD)


def _xattn_sample_units(q_ref, k_ref, v_ref, o_ref, r):
    def row():
        t = jnp.sum(k_ref[r] * q_ref[r], axis=1)
        t = t + pltpu.roll(t, X_HEADS, 1)
        s = jnp.sum(t, axis=-1, keepdims=True) * (X_HD ** -0.5)
        e = jnp.exp(s - jnp.max(s, axis=0, keepdims=True))
        p = e / jnp.sum(e, axis=0, keepdims=True)
        o_ref[r] = jnp.sum(p[:, None] * v_ref[r], axis=0)

    return [row]


def _xattn_sample_rider(grid, q, ck, cv):
    b = q.shape[0]
    rows = b // (grid[0] * grid[1])
    step = lambda i, j: i * grid[1] + j
    kv_spec = pl.BlockSpec((rows, N_MEM, XS_PAIRS, XS_SUB, 128), lambda i, j: (step(i, j), 0, 0, 0, 0))
    q_spec = pl.BlockSpec((rows, XS_PAIRS, XS_SUB, 128), lambda i, j: (step(i, j), 0, 0, 0))
    return dict(make_units=_per_row_units(_xattn_sample_units, rows),
                in_specs=[q_spec, kv_spec, kv_spec], out_specs=[q_spec],
                out_shape=[jax.ShapeDtypeStruct((b, XS_PAIRS, XS_SUB, 128), F32)],
                args=[_pack_heads(q), _pack_heads(ck.reshape(b, N_MEM, X_HEADS * X_HD)),
                      _pack_heads(cv.reshape(b, N_MEM, X_HEADS * X_HD))])


def _smix_rows_kernel(z_ref, h0_ref, rc_ref, mc_ref, cwr_ref, cbr_ref, wg_ref, bg_ref, lam_ref, gout_ref,
                      cwm_ref, cbm_ref, wqk_ref,
                      yr_ref, hn_ref, rcn_ref, mcn_ref, q_ref, k_ref, ybuf):
    nb = z_ref.shape[0]
    ssq = jnp.zeros((nb, 1), F32)
    for h in range(RNN_HEADS):
        sl = slice(h * RNN_HD, (h + 1) * RNN_HD)
        xc = cbr_ref[:, sl] + cwr_ref[CONV_W - 1:CONV_W, sl] * z_ref[:, sl]
        for j in range(CONV_W - 1):
            xc = xc + cwr_ref[j:j + 1, sl] * rc_ref[:, j * D_RNN + h * RNN_HD:j * D_RNN + (h + 1) * RNN_HD]
        a, u = _lru_gates(xc, wg_ref[h], bg_ref[h], lam_ref[:, sl])
        hs = a * h0_ref[:, sl] + u
        hn_ref[:, sl] = hs
        yv = hs * jax.nn.gelu(z_ref[:, D_RNN + h * RNN_HD:D_RNN + (h + 1) * RNN_HD])
        ssq = ssq + jnp.sum(yv * yv, axis=-1, keepdims=True)
        ybuf[:, sl] = yv
    yr_ref[...] = (ybuf[...] * lax.rsqrt(ssq * (1.0 / D_RNN) + EPS) * gout_ref[...]).astype(yr_ref.dtype)
    rcn_ref[:, 0:2 * D_RNN] = rc_ref[:, D_RNN:3 * D_RNN]
    rcn_ref[:, 2 * D_RNN:3 * D_RNN] = z_ref[:, 0:D_RNN]

    for h in range(MLSTM_HEADS):
        sl = slice(h * MLSTM_HD, (h + 1) * MLSTM_HD)
        uc = cbm_ref[:, sl] + cwm_ref[CONV_W - 1:CONV_W, sl] * z_ref[:, 2 * D_RNN + h * MLSTM_HD:2 * D_RNN + (h + 1) * MLSTM_HD]
        for j in range(CONV_W - 1):
            uc = uc + cwm_ref[j:j + 1, sl] * mc_ref[:, j * D_MLSTM + h * MLSTM_HD:j * D_MLSTM + (h + 1) * MLSTM_HD]
        uc = jax.nn.silu(uc)
        qk = _dot(uc.astype(BF16), wqk_ref[h])
        q_ref[:, sl] = qk[:, :MLSTM_HD]
        k_ref[:, sl] = qk[:, MLSTM_HD:] * (MLSTM_HD ** -0.5)
    mcn_ref[:, 0:2 * D_MLSTM] = mc_ref[:, D_MLSTM:3 * D_MLSTM]
    mcn_ref[:, 2 * D_MLSTM:3 * D_MLSTM] = z_ref[:, 2 * D_RNN:2 * D_RNN + D_MLSTM]


def _smix_rows(z, h0, rconv, mconv, w):
    nb = z.shape[0]
    outs = [jax.ShapeDtypeStruct((nb, D_RNN), BF16),
            jax.ShapeDtypeStruct((nb, D_RNN), F32),
            jax.ShapeDtypeStruct((nb, 3 * D_RNN), F32),
            jax.ShapeDtypeStruct((nb, 3 * D_MLSTM), F32),
            jax.ShapeDtypeStruct((nb, D_MLSTM), F32),
            jax.ShapeDtypeStruct((nb, D_MLSTM), F32)]
    return pl.pallas_call(
        _smix_rows_kernel,
        out_shape=outs,
        scratch_shapes=[pltpu.VMEM((nb, D_RNN), F32)],
        compiler_params=pltpu.CompilerParams(vmem_limit_bytes=V7X_VMEM_LIMIT_BYTES),
        name="smix_rows",
    )(z, h0, rconv, mconv, w['conv_rnn_w'], w['conv_rnn_b'], w['lru_wg'], w['lru_bg'], w['lru_lambda'],
      w['g_rnn_out'], w['conv_ml_w'], w['conv_ml_b'], w['ml_wqk'])


MXU_MIN_ROWS = 8


def _smix_state_units(c_ref, q_ref, k_ref, n_ref, v_ref, og_ref, zg_ref, gb_ref, m_ref, gout_ref,
                      cn_ref, nn_ref, mn_ref, y_ref, r):
    lane_g = lax.broadcasted_iota(jnp.int32, (1, N_GATE_PAD), 1)
    first_row = lax.broadcasted_iota(jnp.int32, (MXU_MIN_ROWS, MLSTM_HD), 0) == 0

    def head(h, zg, m_out):
        sl = slice(h * MLSTM_HD, (h + 1) * MLSTM_HD)
        q = q_ref[r, :, sl]
        k = k_ref[r, :, sl]
        n = n_ref[r, :, sl]
        v = v_ref[r, :, sl]
        ig = zg[:, h:h + 1]
        lf = jax.nn.log_sigmoid(zg[:, MLSTM_HEADS + h:MLSTM_HEADS + h + 1])
        m_prev = m_ref[r, :, h:h + 1]
        m_t = jnp.maximum(lf + m_prev, ig)
        w_inter = jnp.exp(lf + m_prev - m_t)
        g_in = jnp.exp(ig - m_t)
        s = jnp.sum(q * k, axis=1, keepdims=True) * g_in
        den = w_inter * jnp.sum(n * q, axis=1, keepdims=True) + s
        denom = jnp.maximum(jnp.abs(den), jnp.exp(-m_t))
        cmat = c_ref[r, h]
        q8 = jnp.broadcast_to(q, (MXU_MIN_ROWS, MLSTM_HD)).astype(BF16)
        cq = lax.dot_general(q8, cmat.astype(BF16), (((1,), (1,)), ((), ())),
                             preferred_element_type=F32)[0:1, :]
        hm = (w_inter * cq + s * v) / denom
        gv8 = jnp.where(first_row, jnp.broadcast_to(g_in * v, (MXU_MIN_ROWS, MLSTM_HD)), 0.0).astype(BF16)
        k8 = jnp.broadcast_to(k, (MXU_MIN_ROWS, MLSTM_HD)).astype(BF16)
        outer = lax.dot_general(gv8, k8, (((0,), (0,)), ((), ())), preferred_element_type=F32)
        cn_ref[r, h] = w_inter * cmat + outer
        nn_ref[r, :, sl] = w_inter * n + g_in * k
        y = _rms(hm, gout_ref[...]) * jax.nn.sigmoid(og_ref[r, :, sl])
        y_ref[r, :, sl] = y.astype(y_ref.dtype)
        return jnp.where(lane_g == h, m_t, m_out)

    def row():
        zg = zg_ref[r] + gb_ref[...]
        m_out = jnp.zeros((1, N_GATE_PAD), F32)
        for h in range(MLSTM_HEADS):
            m_out = head(h, zg, m_out)
        mn_ref[r] = m_out

    return [row]


def _smix_state_rider(grid, c, q3, k3, n3, z3, zg3, gate_bias, m3, gout):
    nb = c.shape[0]
    rows = nb // (grid[0] * grid[1])
    step = lambda i, j: i * grid[1] + j
    row = lambda width, col=0: pl.BlockSpec((rows, 1, width), lambda i, j: (step(i, j), 0, col))
    cblk = pl.BlockSpec((rows, MLSTM_HEADS, MLSTM_HD, MLSTM_HD), lambda i, j: (step(i, j), 0, 0, 0))
    return dict(
        make_units=_per_row_units(_smix_state_units, rows),
        in_specs=[cblk, row(D_MLSTM), row(D_MLSTM), row(D_MLSTM),
                  row(D_MLSTM, 3 * D_RNN // D_MLSTM), row(D_MLSTM, 4 * D_RNN // D_MLSTM),
                  row(N_GATE_PAD),
                  pl.BlockSpec((1, N_GATE_PAD), lambda i, j: (0, 0)),
                  row(MLSTM_HEADS),
                  pl.BlockSpec((1, MLSTM_HD), lambda i, j: (0, 0))],
        out_specs=[cblk, row(D_MLSTM), row(N_GATE_PAD), row(D_MLSTM)],
        out_shape=[jax.ShapeDtypeStruct(c.shape, F32),
                   jax.ShapeDtypeStruct((nb, 1, D_MLSTM), F32),
                   jax.ShapeDtypeStruct((nb, 1, N_GATE_PAD), F32),
                   jax.ShapeDtypeStruct((nb, 1, D_MLSTM), F32)],
        args=[c, q3, k3, n3, z3, z3, zg3, gate_bias, m3, gout])


RGLRU_CHUNK = 256
PROMPT_TM = 1024


def _layer(xp, xs, mem, rg_h, rg_conv, c, n, mst, ml_conv, ck, cv, w, g_final, final_norm):
    b, t, d = xp.shape
    nb = xs.shape[0]
    m = b * t
    tm = PROMPT_TM
    xp2 = xp.reshape(m, d)
    xs2 = xs.reshape(nb, d)
    w_in_args = dict(tn=1024, out_dtype=F32, w_extra=w['w_in_gate'], n_cols=D_MAIN, w_is_transposed=True)

    zs, zgs, w_in_bf = _norm_matmul(xs2, w['g_mix'], w['w_in_main'], tm=nb, emit_w_bf16=True, **w_in_args)
    ys_rnn, s_h, s_rconv, s_mconv, qs, ks = _smix_rows(
        zs, rg_h, rg_conv.reshape(nb, 3 * D_RNN), ml_conv.reshape(nb, 3 * D_MLSTM), w)

    z, zg = _norm_matmul(xp2, w['g_mix'], w_in_bf, tm=tm, **w_in_args)
    z3 = z.reshape(b, t, D_MAIN)
    zg3 = zg.reshape(b, t, N_GATE_PAD)
    h0 = jnp.zeros((b, 1, D_RNN), F32)
    conv0 = jnp.zeros((b, CONV_W - 1, D_RNN), F32)
    state_rider = _smix_state_rider(
        (b, t // RGLRU_CHUNK), c, qs.reshape(nb, 1, D_MLSTM), ks.reshape(nb, 1, D_MLSTM),
        n.reshape(nb, 1, D_MLSTM), zs.reshape(nb, 1, D_MAIN), zgs.reshape(nb, 1, N_GATE_PAD), w['gate_bias'],
        mst.reshape(nb, 1, MLSTM_HEADS), w['g_ml_out'])
    rg_grid = (b, t // RGLRU_CHUNK)
    rg_rider = _merge_riders(state_rider, _cast_rider(rg_grid, [[w['w_out'], w['w_cq']], [w['w_co']]]))
    y_rnn, p_h, p_rconv, s_c, s_n, s_m, ys_ml, w_out_cq, w_co = _rglru_prompt(
        z3, h0, conv0, w, tc=RGLRU_CHUNK, rider=rg_rider)

    xs1, qx = _matmul_res_norm_matmul([ys_rnn, ys_ml.reshape(nb, D_MLSTM).astype(BF16)], w_out_cq, xs2,
                                      w['g_xattn'], tm=nb, tn=1024, out2_dtype=F32)

    c0 = jnp.zeros((b, MLSTM_HEADS, MLSTM_HD, MLSTM_HD), F32)
    n0 = jnp.zeros((b, MLSTM_HEADS, 1, MLSTM_HD), F32)
    m0 = jnp.zeros((b, MLSTM_HEADS, 1, N_GATE_PAD), F32)
    mconv0 = jnp.zeros((b, CONV_W - 1, D_MLSTM), F32)
    y_ml, p_c, p_n, p_m, p_mconv, w_up, w_down = _mlstm_prompt(
        z3, zg3, c0, n0, m0, mconv0, w, rider=_cast_rider((b, t // CHUNK), [[w['w_up']], [w['w_down']]]))

    x1, q = _matmul_res_norm_matmul([y_rnn.reshape(m, D_RNN), y_ml.reshape(m, D_MLSTM)], w_out_cq, xp2,
                                    w['g_xattn'], tm=tm, tn=1024, out2_dtype=BF16)
    mem2 = mem.reshape(b * N_MEM, d)
    mk = _norm_matmul(mem2, w['g_mem'], w['w_mk'], tm=b * N_MEM, tn=1024, out_dtype=F32)
    mv = _norm_matmul(mem2, w['g_mem'], w['w_mv'], tm=b * N_MEM, tn=1024, out_dtype=F32)
    x2 = _xattn_prompt(q.reshape(b, t, d), mk.reshape(b, N_MEM, d), mv.reshape(b, N_MEM, d), x1.reshape(b, t, d),
                       w_co, tq=tm, tn=1024)
    ffn_tf = 512
    xp_out, os_packed = _ffn(x2.reshape(m, d), w['g_ffn'], w_up, w_down, g_final, tm=tm, tf=ffn_tf,
                             final_norm=final_norm,
                             rider=_xattn_sample_rider((m // tm, D_FF // ffn_tf), qx, ck, cv))

    os_ = _unpack_heads(os_packed).astype(BF16)
    xs2_ = _matmul_res([os_], w_co, xs1, tm=nb, tn=1024)
    xs_out, = _ffn(xs2_, w['g_ffn'], w_up, w_down, g_final, tm=nb, tf=ffn_tf, final_norm=final_norm)

    new_p = (p_h.reshape(b, D_RNN), p_rconv, p_c, p_n.reshape(b, MLSTM_HEADS, MLSTM_HD), p_m[:, :, 0, 0], p_mconv,
             mk.reshape(b, N_MEM, X_HEADS, X_HD), mv.reshape(b, N_MEM, X_HEADS, X_HD))
    new_s = (s_h, s_rconv.reshape(nb, CONV_W - 1, D_RNN), s_c, s_n.reshape(nb, MLSTM_HEADS, MLSTM_HD),
             s_m[:, 0, :MLSTM_HEADS], s_mconv.reshape(nb, CONV_W - 1, D_MLSTM))
    return xp_out.reshape(b, t, d), xs_out.reshape(nb, 1, d), new_p, new_s


def _layer_weights(l, g_mix, w_in, conv_rnn_w, conv_rnn_b, lru_wa, lru_ba, lru_wx, lru_bx, lru_lambda,
                   g_rnn_out, conv_ml_w, conv_ml_b, ml_wq, ml_wk, ml_bi, ml_bf, g_ml_out, w_out,
                   g_xattn, g_mem, w_cq, w_mk, w_mv, w_co, g_ffn, w_up, w_down):
    n_gate = 2 * MLSTM_HEADS
    w_in_t = jnp.swapaxes(w_in[l], 0, 1)
    w_gate_t = jnp.pad(w_in_t[D_MAIN:], ((0, N_GATE_PAD - n_gate), (0, 0)))
    gate_bias = jnp.pad(jnp.concatenate([ml_bi[l], ml_bf[l]]), (0, N_GATE_PAD - n_gate))
    return dict(
        g_mix=g_mix[l], w_in_main=w_in_t, w_in_gate=w_gate_t.astype(BF16),
        conv_rnn_w=conv_rnn_w[l], conv_rnn_b=conv_rnn_b[l].reshape(1, D_RNN),
        lru_wg=jnp.concatenate([lru_wa[l], lru_wx[l]], axis=-1).astype(BF16),
        lru_bg=jnp.concatenate([lru_ba[l], lru_bx[l]], axis=-1).reshape(RNN_HEADS, 1, 2 * RNN_HD),
        lru_lambda=lru_lambda[l].reshape(1, D_RNN), g_rnn_out=g_rnn_out[l].reshape(1, D_RNN),
        conv_ml_w=conv_ml_w[l], conv_ml_b=conv_ml_b[l].reshape(1, D_MLSTM),
        ml_wqk=jnp.concatenate([ml_wq[l], ml_wk[l]], axis=-1).astype(BF16),
        gate_bias=gate_bias.reshape(1, N_GATE_PAD), g_ml_out=g_ml_out[l].reshape(1, MLSTM_HD),
        w_out=w_out[l], w_cq=w_cq[l], g_xattn=g_xattn[l], g_mem=g_mem[l], w_mk=w_mk[l], w_mv=w_mv[l],
        w_co=w_co[l], g_ffn=g_ffn[l], w_up=w_up[l], w_down=w_down[l])


def kernel(x_prompt, x_sample, mem_prompt, state_rglru_h, state_rglru_conv, state_mlstm_C, state_mlstm_n, state_mlstm_m, state_mlstm_conv, cache_mem_k, cache_mem_v, g_mix, w_in, conv_rnn_w, conv_rnn_b, lru_wa, lru_ba, lru_wx, lru_bx, lru_lambda, g_rnn_out, conv_ml_w, conv_ml_b, ml_wq, ml_wk, ml_bi, ml_bf, g_ml_out, w_out, g_xattn, g_mem, w_cq, w_mk, w_mv, w_co, g_ffn, w_up, w_down, g_final):
    depth = w_in.shape[0]
    xp, xs = x_prompt, x_sample
    p_out = [[] for _ in range(8)]
    s_out = [[] for _ in range(6)]
    for l in range(depth):
        w = _layer_weights(l, g_mix, w_in, conv_rnn_w, conv_rnn_b, lru_wa, lru_ba, lru_wx, lru_bx, lru_lambda,
                           g_rnn_out, conv_ml_w, conv_ml_b, ml_wq, ml_wk, ml_bi, ml_bf, g_ml_out, w_out,
                           g_xattn, g_mem, w_cq, w_mk, w_mv, w_co, g_ffn, w_up, w_down)
        last = l == depth - 1
        xp, xs, new_p, new_s = _layer(xp, xs, mem_prompt, state_rglru_h[l], state_rglru_conv[l], state_mlstm_C[l],
                                      state_mlstm_n[l], state_mlstm_m[l], state_mlstm_conv[l], cache_mem_k[l],
                                      cache_mem_v[l], w, g_final, last)
        for j, a in enumerate(new_p):
            p_out[j].append(a)
        for j, a in enumerate(new_s):
            s_out[j].append(a)
    P = [jnp.stack(a, axis=0) for a in p_out]
    S = [jnp.stack(a, axis=0) for a in s_out]
    return (xp, xs, P[0], P[1], P[2], P[3], P[4], P[5], P[6], P[7],
            S[0], S[1], S[2], S[3], S[4], S[5])
```

```python
import functools

import jax
import jax.numpy as jnp
from jax import lax
from jax.experimental import pallas as pl
from jax.experimental.pallas import tpu as pltpu

F32 = jnp.float32
BF16 = jnp.bfloat16

D_MODEL = 2048
D_RNN = 1024
RNN_HEADS = 8
RNN_HD = 128
CONV_W = 4
LRU_C = 8.0
D_MLSTM = 1024
MLSTM_HEADS = 4
MLSTM_HD = 256
CHUNK = 128
N_MEM = 256
X_HEADS = 4
X_HD = 512
D_FF = 8192
EPS = 1e-6
D_MAIN = 5 * 1024
N_GATE_PAD = 128

V7X_VMEM_LIMIT_BYTES = 56 * 1024 * 1024
SUBLANES = 8


def _params(*sem):
    return pltpu.CompilerParams(dimension_semantics=sem, vmem_limit_bytes=V7X_VMEM_LIMIT_BYTES)


def _rms(x, g):
    ms = jnp.mean(x * x, axis=-1, keepdims=True)
    return x * lax.rsqrt(ms + EPS) * g


def _dot(a, b):
    return jnp.dot(a.astype(BF16), b.astype(BF16), preferred_element_type=F32)


def _dot_t(a, bt):
    return lax.dot_general(a.astype(BF16), bt.astype(BF16), (((1,), (1,)), ((), ())), preferred_element_type=F32)


def _with_rider(host_body, rider, n_in, n_out):
    if rider is None:
        return host_body
    r_in, r_out = len(rider['in_specs']), len(rider['out_specs'])

    def body(*refs):
        host_in, refs = refs[:n_in], refs[n_in:]
        rider_in, refs = refs[:r_in], refs[r_in:]
        host_out, refs = refs[:n_out], refs[n_out:]
        rider_out, scratch = refs[:r_out], refs[r_out:]
        host_body(*host_in, *host_out, *scratch, rider=rider['make_units'](*rider_in, *rider_out))
    return body


def _grid_steps(grid):
    n = 1
    for g in grid:
        n *= g

    def step(*idx):
        s = idx[0]
        for g, i in zip(grid[1:], idx[1:]):
            s = s * g + i
        return s
    return n, step


def _per_row_units(row_units, rows):
    return lambda *refs: [u for r in range(rows) for u in row_units(*refs, r)]


def _merge_riders(a, b):
    na_in, na_out = len(a['in_specs']), len(a['out_specs'])
    nb_in = len(b['in_specs'])

    def make_units(*refs):
        a_in, refs = refs[:na_in], refs[na_in:]
        b_in, refs = refs[:nb_in], refs[nb_in:]
        a_out, b_out = refs[:na_out], refs[na_out:]
        return a['make_units'](*a_in, *a_out) + b['make_units'](*b_in, *b_out)

    return dict(make_units=make_units, **{k: a[k] + b[k] for k in ('in_specs', 'out_specs', 'out_shape', 'args')})


def _cast_rider(grid, groups):
    n_steps, step = _grid_steps(grid)
    in_specs, out_specs, out_shape, args, widths = [], [], [], [], []
    for group in groups:
        rps = group[0].shape[0] // n_steps
        for a in group:
            in_specs.append(pl.BlockSpec((rps, a.shape[1]), lambda *g: (step(*g), 0)))
            args.append(a)
        cols = sum(a.shape[1] for a in group)
        out_specs.append(pl.BlockSpec((rps, cols), lambda *g: (step(*g), 0)))
        out_shape.append(jax.ShapeDtypeStruct((group[0].shape[0], cols), BF16))
        widths.append([a.shape[1] for a in group])

    def make_units(*refs):
        in_refs, out_refs = list(refs[:len(args)]), refs[len(args):]
        units = []
        for o_ref, ws in zip(out_refs, widths):
            srcs, in_refs = in_refs[:len(ws)], in_refs[len(ws):]

            def cast(o_ref=o_ref, srcs=srcs, ws=ws):
                c0 = 0
                for s_ref, wd in zip(srcs, ws):
                    o_ref[:, c0:c0 + wd] = s_ref[...].astype(BF16)
                    c0 += wd
            units.append(cast)
        return units

    return dict(make_units=make_units, in_specs=in_specs, out_specs=out_specs, out_shape=out_shape, args=args)


def _run_rider_share(rider, i, n):
    if rider is None:
        return
    for unit in rider[i * len(rider) // n:(i + 1) * len(rider) // n]:
        unit()


def _rider_parts(rider):
    if rider is None:
        return [], [], [], []
    return rider['in_specs'], rider['out_specs'], rider['out_shape'], rider['args']


def _norm_matmul_kernel(*refs, has_extra, w_is_transposed, emit_w_bf16):
    refs = list(refs)
    xn_ref = refs.pop()
    wb_ref = refs.pop() if emit_w_bf16 else None
    if has_extra:
        x_ref, g_ref, w_ref, we_ref, o_ref, oe_ref = refs
    else:
        x_ref, g_ref, w_ref, o_ref = refs
    dot = _dot_t if w_is_transposed else _dot

    @pl.when(pl.program_id(1) == 0)
    def _():
        xn = _rms(x_ref[...], g_ref[...]).astype(BF16)
        xn_ref[...] = xn
        if has_extra:
            oe_ref[...] = dot(xn, we_ref[...])

    wb = w_ref[...].astype(BF16)
    if emit_w_bf16:
        wb_ref[...] = wb
    o_ref[...] = dot(xn_ref[...], wb).astype(o_ref.dtype)


def _norm_matmul(x, g, w, *, tm, tn, out_dtype, w_extra=None, n_cols=None, w_is_transposed=False,
                 emit_w_bf16=False):
    m, k = x.shape
    n = n_cols or w.shape[0 if w_is_transposed else 1]
    has_extra = w_extra is not None
    w_spec = (pl.BlockSpec((tn, k), lambda i, j: (j, 0)) if w_is_transposed
              else pl.BlockSpec((k, tn), lambda i, j: (0, j)))
    in_specs = [pl.BlockSpec((tm, k), lambda i, j: (i, 0)),
                pl.BlockSpec((1, k), lambda i, j: (0, 0)),
                w_spec]
    out_specs = [pl.BlockSpec((tm, tn), lambda i, j: (i, j))]
    out_shape = [jax.ShapeDtypeStruct((m, n), out_dtype)]
    args = [x, g.reshape(1, k), w]
    if has_extra:
        ne = w_extra.shape[0 if w_is_transposed else 1]
        in_specs.append(pl.BlockSpec(w_extra.shape, lambda i, j: (0, 0)))
        out_specs.append(pl.BlockSpec((tm, ne), lambda i, j: (i, 0)))
        out_shape.append(jax.ShapeDtypeStruct((m, ne), F32))
        args.append(w_extra)
    if emit_w_bf16:
        assert m == tm
        out_specs.append(w_spec)
        out_shape.append(jax.ShapeDtypeStruct((n, k) if w_is_transposed else (k, n), BF16))
    res = pl.pallas_call(
        functools.partial(_norm_matmul_kernel, has_extra=has_extra, w_is_transposed=w_is_transposed,
                          emit_w_bf16=emit_w_bf16),
        grid=(m // tm, n // tn),
        in_specs=in_specs, out_specs=out_specs, out_shape=out_shape,
        scratch_shapes=[pltpu.VMEM((tm, k), BF16)],
        compiler_params=_params("parallel", "arbitrary"),
        name="norm_matmul",
    )(*args)
    return res if len(res) > 1 else res[0]


def _matmul_res_kernel(*refs, n_parts):
    a_refs, w_refs = refs[:n_parts], refs[n_parts:2 * n_parts]
    r_ref, o_ref = refs[2 * n_parts:]
    acc = r_ref[...]
    for a_ref, w_ref in zip(a_refs, w_refs):
        acc = acc + _dot(a_ref[...], w_ref[...])
    o_ref[...] = acc


def _matmul_res(a_parts, w, res, *, tm, tn):
    n_parts = len(a_parts)
    m, kp = a_parts[0].shape
    n = w.shape[1]
    a_specs = [pl.BlockSpec((tm, kp), lambda i, j: (i, 0)) for _ in a_parts]
    w_specs = [pl.BlockSpec((kp, tn), lambda i, j, p=p: (p, j)) for p in range(n_parts)]
    return pl.pallas_call(
        functools.partial(_matmul_res_kernel, n_parts=n_parts),
        grid=(m // tm, n // tn),
        in_specs=a_specs + w_specs + [pl.BlockSpec((tm, tn), lambda i, j: (i, j))],
        out_specs=pl.BlockSpec((tm, tn), lambda i, j: (i, j)),
        out_shape=jax.ShapeDtypeStruct((m, n), F32),
        compiler_params=_params("parallel", "arbitrary"),
        name="matmul_res",
    )(*a_parts, *([w] * n_parts), res)


def _matmul_res_norm_matmul_kernel(*refs, n_parts, n1):
    a_refs = refs[:n_parts]
    w_ref, r_ref, g_ref, x1_ref, q_ref, x1_s, xn_s = refs[n_parts:]
    j = pl.program_id(1)
    tn = w_ref.shape[1]
    kp = a_refs[0].shape[1]

    @pl.when(j < n1)
    def _():
        x1 = r_ref[...]
        for p, a_ref in enumerate(a_refs):
            x1 = x1 + _dot(a_ref[...], w_ref[p * kp:(p + 1) * kp, :])
        x1_ref[...] = x1
        x1_s[j] = x1

    @pl.when(j == n1)
    def _():
        ssq = sum(jnp.sum(jnp.square(x1_s[c]), axis=-1, keepdims=True) for c in range(n1))
        scale = lax.rsqrt(ssq * (1.0 / (n1 * tn)) + EPS)
        for c in range(n1):
            xn_s[:, c * tn:(c + 1) * tn] = (x1_s[c] * scale * g_ref[:, c * tn:(c + 1) * tn]).astype(xn_s.dtype)

    @pl.when(j >= n1)
    def _():
        q_ref[...] = _dot(xn_s[...], w_ref[...]).astype(q_ref.dtype)


def _matmul_res_norm_matmul(a_parts, w12, res, g, *, tm, tn, out2_dtype):
    n_parts = len(a_parts)
    m, kp = a_parts[0].shape
    d = res.shape[1]
    n1 = d // tn
    n2 = (w12.shape[1] - d) // tn
    return pl.pallas_call(
        functools.partial(_matmul_res_norm_matmul_kernel, n_parts=n_parts, n1=n1),
        grid=(m // tm, n1 + n2),
        in_specs=[pl.BlockSpec((tm, kp), lambda i, j: (i, 0))] * n_parts + [
            pl.BlockSpec((n_parts * kp, tn), lambda i, j: (0, j)),
            pl.BlockSpec((tm, tn), lambda i, j: (i, jnp.minimum(j, n1 - 1))),
            pl.BlockSpec((1, d), lambda i, j: (0, 0))],
        out_specs=[pl.BlockSpec((tm, tn), lambda i, j: (i, jnp.minimum(j, n1 - 1))),
                   pl.BlockSpec((tm, tn), lambda i, j: (i, jnp.maximum(j - n1, 0)))],
        out_shape=[jax.ShapeDtypeStruct((m, d), F32), jax.ShapeDtypeStruct((m, n2 * tn), out2_dtype)],
        scratch_shapes=[pltpu.VMEM((n1, tm, tn), F32), pltpu.VMEM((tm, d), BF16)],
        compiler_params=_params("parallel", "arbitrary"),
        name="matmul_res_norm_matmul",
    )(*a_parts, w12, res, g.reshape(1, d))


def _ffn_kernel(x_ref, g_ref, wu_ref, wd_ref, gf_ref, o_ref, xn_ref, *, final_norm, rider=None):
    f = pl.program_id(1)

    @pl.when(f == 0)
    def _():
        xn_ref[...] = _rms(x_ref[...], g_ref[...]).astype(BF16)
        o_ref[...] = jnp.zeros_like(o_ref)

    _run_rider_share(rider, 0, 1)
    h = _dot(xn_ref[...], wu_ref[...])
    h = jnp.square(jnp.maximum(h, 0.0)).astype(BF16)
    o_ref[...] += _dot(h, wd_ref[...])

    @pl.when(f == pl.num_programs(1) - 1)
    def _():
        y = x_ref[...] + o_ref[...]
        if final_norm:
            y = _rms(y, gf_ref[...])
        o_ref[...] = y


def _ffn(x, g, w_up, w_down, g_final, *, tm, tf, final_norm, rider=None):
    m, d = x.shape
    dff = w_up.shape[1]
    r_in, r_out, r_shape, r_args = _rider_parts(rider)
    in_specs = [pl.BlockSpec((tm, d), lambda i, f: (i, 0), pipeline_mode=pl.Buffered(1)),
                pl.BlockSpec((1, d), lambda i, f: (0, 0)),
                pl.BlockSpec((d, tf), lambda i, f: (0, f)),
                pl.BlockSpec((tf, d), lambda i, f: (f, 0)),
                pl.BlockSpec((1, d), lambda i, f: (0, 0))]
    out_specs = [pl.BlockSpec((tm, d), lambda i, f: (i, 0))]
    return pl.pallas_call(
        _with_rider(functools.partial(_ffn_kernel, final_norm=final_norm), rider, len(in_specs), len(out_specs)),
        grid=(m // tm, dff // tf),
        in_specs=in_specs + r_in, out_specs=out_specs + r_out,
        out_shape=[jax.ShapeDtypeStruct((m, d), F32)] + r_shape,
        scratch_shapes=[pltpu.VMEM((tm, d), BF16)],
        compiler_params=_params("parallel", "arbitrary"),
        name="ffn",
    )(x, g.reshape(1, d), w_up, w_down, g_final.reshape(1, d), *r_args)


def _lru_gates(xc, wg, bg, lam):
    g = _dot(xc.astype(BF16), wg) + bg
    r = jax.nn.sigmoid(g[:, :RNN_HD])
    i = jax.nn.sigmoid(g[:, RNN_HD:])
    log_a = -LRU_C * r * jax.nn.softplus(-lam)
    a = jnp.exp(log_a)
    u = jnp.sqrt(-jnp.tanh(log_a) * (a * a + 1.0)) * (i * xc)
    return a, u


def _rglru_kernel(xr_ref, gr_ref, h0_ref, c0_ref, cw_ref, cb_ref, wg_ref, bg_ref, lam_ref, gout_ref,
                  y_ref, hn_ref, cn_ref, xbuf, hc, ybuf, *, tc, rider=None):
    c = pl.program_id(1)

    @pl.when(c == 0)
    def _():
        xbuf[0:8, :] = jnp.zeros((8, D_RNN), F32)
        xbuf[5:8, :] = c0_ref[0]
        hc[...] = h0_ref[0]

    @pl.when(c > 0)
    def _():
        xbuf[0:8, :] = xbuf[tc:tc + 8, :]

    xbuf[8:8 + tc, :] = xr_ref[0]
    ng = tc // SUBLANES
    sub = lax.broadcasted_iota(jnp.int32, (ng, SUBLANES, RNN_HD), 1)
    ssq = jnp.zeros((tc, 1), F32)
    for h in range(RNN_HEADS):
        _run_rider_share(rider, h, RNN_HEADS)
        sl = slice(h * RNN_HD, (h + 1) * RNN_HD)
        xc = cb_ref[:, sl]
        for j in range(CONV_W):
            xc = xc + cw_ref[j:j + 1, sl] * xbuf[5 + j:5 + j + tc, sl]
        a, u = _lru_gates(xc, wg_ref[h], bg_ref[h], lam_ref[:, sl])
        a = a.reshape(ng, SUBLANES, RNN_HD)
        u = u.reshape(ng, SUBLANES, RNN_HD)
        d = 1
        while d < SUBLANES:
            keep = sub >= d
            a_prev = jnp.where(keep, pltpu.roll(a, d, 1), 1.0)
            u_prev = jnp.where(keep, pltpu.roll(u, d, 1), 0.0)
            u = u + a * u_prev
            a = a * a_prev
            d *= 2
        carry = hc[:, sl]
        for g in range(ng):
            hg = u[g] + a[g] * carry
            carry = hg[SUBLANES - 1:SUBLANES, :]
            ybuf[g * SUBLANES:(g + 1) * SUBLANES, sl] = hg
        hc[:, sl] = carry
        yv = ybuf[:, sl] * jax.nn.gelu(gr_ref[0, :, sl])
        ssq = ssq + jnp.sum(yv * yv, axis=-1, keepdims=True)
        ybuf[:, sl] = yv
    y = ybuf[...] * lax.rsqrt(ssq * (1.0 / D_RNN) + EPS) * gout_ref[...]
    y_ref[0] = y.astype(y_ref.dtype)

    @pl.when(c == pl.num_programs(1) - 1)
    def _():
        hn_ref[0] = hc[...]
        cn_ref[0] = xbuf[tc + 5:tc + 8, :]


def _rglru_prompt(z3, h0, conv0, w, *, tc, rider=None):
    b, t, _ = z3.shape
    full2 = lambda bi, ci: (0, 0)
    r_in, r_out, r_shape, r_args = _rider_parts(rider)
    in_specs = [pl.BlockSpec((1, tc, D_RNN), lambda bi, ci: (bi, ci, 0)),
                pl.BlockSpec((1, tc, D_RNN), lambda bi, ci: (bi, ci, 1)),
                pl.BlockSpec((1, 1, D_RNN), lambda bi, ci: (bi, 0, 0)),
                pl.BlockSpec((1, CONV_W - 1, D_RNN), lambda bi, ci: (bi, 0, 0)),
                pl.BlockSpec((CONV_W, D_RNN), full2),
                pl.BlockSpec((1, D_RNN), full2),
                pl.BlockSpec((RNN_HEADS, RNN_HD, 2 * RNN_HD), lambda bi, ci: (0, 0, 0)),
                pl.BlockSpec((RNN_HEADS, 1, 2 * RNN_HD), lambda bi, ci: (0, 0, 0)),
                pl.BlockSpec((1, D_RNN), full2),
                pl.BlockSpec((1, D_RNN), full2)]
    out_specs = [pl.BlockSpec((1, tc, D_RNN), lambda bi, ci: (bi, ci, 0)),
                 pl.BlockSpec((1, 1, D_RNN), lambda bi, ci: (bi, 0, 0)),
                 pl.BlockSpec((1, CONV_W - 1, D_RNN), lambda bi, ci: (bi, 0, 0))]
    out_shape = [jax.ShapeDtypeStruct((b, t, D_RNN), BF16),
                 jax.ShapeDtypeStruct((b, 1, D_RNN), F32),
                 jax.ShapeDtypeStruct((b, CONV_W - 1, D_RNN), F32)]
    return pl.pallas_call(
        _with_rider(functools.partial(_rglru_kernel, tc=tc), rider, len(in_specs), len(out_specs)),
        grid=(b, t // tc),
        in_specs=in_specs + r_in, out_specs=out_specs + r_out, out_shape=out_shape + r_shape,
        scratch_shapes=[pltpu.VMEM((tc + 8, D_RNN), F32),
                        pltpu.VMEM((1, D_RNN), F32),
                        pltpu.VMEM((tc, D_RNN), F32)],
        compiler_params=_params("parallel", "arbitrary"),
        name="rglru_prompt",
    )(z3, z3, h0, conv0, w['conv_rnn_w'], w['conv_rnn_b'], w['lru_wg'], w['lru_bg'],
      w['lru_lambda'], w['g_rnn_out'], *r_args)


def _mlstm_kernel(u_ref, v_ref, og_ref, zg_ref, gb_ref, c0_ref, n0_ref, m0_ref, cv0_ref,
                  cw_ref, cb_ref, wqk_ref, gout_ref,
                  y_ref, cn_ref, nn_ref, mn_ref, cvn_ref, ubuf, cst, nst, mst, rider=None):
    c = pl.program_id(1)
    L = CHUNK

    @pl.when(c == 0)
    def _():
        ubuf[0:8, :] = jnp.zeros((8, D_MLSTM), F32)
        ubuf[5:8, :] = cv0_ref[0]
        cst[...] = c0_ref[0]
        nst[...] = n0_ref[0]
        mst[...] = m0_ref[0]

    @pl.when(c > 0)
    def _():
        ubuf[0:8, :] = ubuf[L:L + 8, :]

    ubuf[8:8 + L, :] = u_ref[0]

    zg = zg_ref[0] + gb_ref[...]
    zgt = zg.T
    ti = lax.broadcasted_iota(jnp.int32, (L, L), 0)
    si = lax.broadcasted_iota(jnp.int32, (L, L), 1)
    causal = si <= ti
    anti = ti <= si

    for h in range(MLSTM_HEADS):
        _run_rider_share(rider, h, MLSTM_HEADS)
        sl = slice(h * MLSTM_HD, (h + 1) * MLSTM_HD)
        uc = cb_ref[:, sl]
        for j in range(CONV_W):
            uc = uc + cw_ref[j:j + 1, sl] * ubuf[5 + j:5 + j + L, sl]
        uc = jax.nn.silu(uc)
        qk = _dot(uc.astype(BF16), wqk_ref[h])
        q = qk[:, :MLSTM_HD]
        k = qk[:, MLSTM_HD:] * (MLSTM_HD ** -0.5)
        v = v_ref[0, :, sl]
        qb, kb = q.astype(BF16), k.astype(BF16)

        icol = zg[:, h:h + 1]
        fcol = jax.nn.log_sigmoid(zg[:, MLSTM_HEADS + h:MLSTM_HEADS + h + 1])
        irow = zgt[h:h + 1, :]
        frow = jax.nn.log_sigmoid(zgt[MLSTM_HEADS + h:MLSTM_HEADS + h + 1, :])
        bcol = jnp.sum(jnp.where(causal, frow, 0.0), axis=1, keepdims=True)
        brow = jnp.sum(jnp.where(anti, fcol, 0.0), axis=0, keepdims=True)
        dmat = jnp.where(causal, irow + bcol - brow, -jnp.inf)
        m_prev = mst[h, :, 0:1]
        inter = bcol + m_prev
        m_t = jnp.maximum(inter, jnp.max(dmat, axis=1, keepdims=True))
        w_inter = jnp.exp(inter - m_t)
        s = lax.dot_general(qb, kb, (((1,), (1,)), ((), ())), preferred_element_type=F32) * jnp.exp(dmat - m_t)
        cmat = cst[h]
        nrow = nst[h]
        cq = lax.dot_general(qb, cmat.astype(BF16), (((1,), (1,)), ((), ())), preferred_element_type=F32)
        num = w_inter * cq + _dot(s.astype(BF16), v.astype(BF16))
        den = w_inter * jnp.sum(q * nrow, axis=1, keepdims=True) + jnp.sum(s, axis=1, keepdims=True)
        hm = num / jnp.maximum(jnp.abs(den), jnp.exp(-m_t))

        m_new = m_t[L - 1:L, :]
        b_last = bcol[L - 1:L, :]
        g_state = jnp.exp(b_last + m_prev - m_new)
        g_in = jnp.exp(icol + b_last - bcol - m_new)
        cst[h] = g_state * cmat + lax.dot_general((g_in * v).astype(BF16), kb, (((0,), (0,)), ((), ())),
                                                  preferred_element_type=F32)
        nst[h] = g_state * nrow + jnp.sum(g_in * k, axis=0, keepdims=True)
        mst[h] = jnp.broadcast_to(m_new, (1, N_GATE_PAD))

        y = _rms(hm, gout_ref[...]) * jax.nn.sigmoid(og_ref[0, :, sl])
        y_ref[0, :, sl] = y.astype(y_ref.dtype)

    @pl.when(c == pl.num_programs(1) - 1)
    def _():
        cn_ref[0] = cst[...]
        nn_ref[0] = nst[...]
        mn_ref[0] = mst[...]
        cvn_ref[0] = ubuf[L + 5:L + 8, :]


def _mlstm_prompt(z3, zg3, c0, n0, m0, conv0, w, *, rider=None):
    b, t, _ = z3.shape
    nc = t // CHUNK
    u_blk, v_blk, o_blk = 2 * D_RNN // D_MLSTM, 3 * D_RNN // D_MLSTM, 4 * D_RNN // D_MLSTM
    zcol = lambda blk: pl.BlockSpec((1, CHUNK, D_MLSTM), lambda bi, ci: (bi, ci, blk))
    state4 = lambda *minor: pl.BlockSpec((1, MLSTM_HEADS) + minor, lambda bi, ci: (bi, 0, 0, 0))
    conv_spec = pl.BlockSpec((1, CONV_W - 1, D_MLSTM), lambda bi, ci: (bi, 0, 0))
    const2 = lambda shape: pl.BlockSpec(shape, lambda bi, ci: (0, 0))
    r_in, r_out, r_shape, r_args = _rider_parts(rider)
    in_specs = [zcol(u_blk), zcol(v_blk), zcol(o_blk),
                pl.BlockSpec((1, CHUNK, N_GATE_PAD), lambda bi, ci: (bi, ci, 0)),
                const2((1, N_GATE_PAD)),
                state4(MLSTM_HD, MLSTM_HD), state4(1, MLSTM_HD), state4(1, N_GATE_PAD), conv_spec,
                const2((CONV_W, D_MLSTM)), const2((1, D_MLSTM)),
                pl.BlockSpec((MLSTM_HEADS, MLSTM_HD, 2 * MLSTM_HD), lambda bi, ci: (0, 0, 0)),
                const2((1, MLSTM_HD))]
    out_specs = [pl.BlockSpec((1, CHUNK, D_MLSTM), lambda bi, ci: (bi, ci, 0)),
                 state4(MLSTM_HD, MLSTM_HD), state4(1, MLSTM_HD), state4(1, N_GATE_PAD), conv_spec]
    out_shape = [jax.ShapeDtypeStruct((b, t, D_MLSTM), BF16),
                 jax.ShapeDtypeStruct((b, MLSTM_HEADS, MLSTM_HD, MLSTM_HD), F32),
                 jax.ShapeDtypeStruct((b, MLSTM_HEADS, 1, MLSTM_HD), F32),
                 jax.ShapeDtypeStruct((b, MLSTM_HEADS, 1, N_GATE_PAD), F32),
                 jax.ShapeDtypeStruct((b, CONV_W - 1, D_MLSTM), F32)]
    return pl.pallas_call(
        _with_rider(_mlstm_kernel, rider, len(in_specs), len(out_specs)),
        grid=(b, nc),
        in_specs=in_specs + r_in, out_specs=out_specs + r_out, out_shape=out_shape + r_shape,
        scratch_shapes=[pltpu.VMEM((CHUNK + 8, D_MLSTM), F32),
                        pltpu.VMEM((MLSTM_HEADS, MLSTM_HD, MLSTM_HD), F32),
                        pltpu.VMEM((MLSTM_HEADS, 1, MLSTM_HD), F32),
                        pltpu.VMEM((MLSTM_HEADS, 1, N_GATE_PAD), F32)],
        compiler_params=_params("parallel", "arbitrary"),
        name="mlstm_prompt",
    )(z3, z3, z3, zg3, w['gate_bias'], c0, n0, m0, conv0, w['conv_ml_w'], w['conv_ml_b'],
      w['ml_wqk'], w['g_ml_out'], *r_args)


def _xattn_kernel(q_ref, k_ref, v_ref, x_ref, wo_ref, o_ref, att, rider=None):
    @pl.when(pl.program_id(2) == 0)
    def _():
        for h in range(X_HEADS):
            sl = slice(h * X_HD, (h + 1) * X_HD)
            s = _dot_t(q_ref[0, :, sl], k_ref[0, :, sl]) * (X_HD ** -0.5)
            e = jnp.exp(s - jnp.max(s, axis=-1, keepdims=True))
            p = e / jnp.sum(e, axis=-1, keepdims=True)
            att[:, sl] = _dot(p, v_ref[0, :, sl]).astype(att.dtype)

    _run_rider_share(rider, 0, 1)
    o_ref[0] = x_ref[0] + _dot(att[...], wo_ref[...])


def _xattn_prompt(q3, mk3, mv3, x3, w_co, *, tq, tn, rider=None):
    b, t, d = q3.shape
    mem = pl.BlockSpec((1, N_MEM, d), lambda bi, ti, j: (bi, 0, 0))
    cols = pl.BlockSpec((1, tq, tn), lambda bi, ti, j: (bi, ti, j))
    r_in, r_out, r_shape, r_args = _rider_parts(rider)
    in_specs = [pl.BlockSpec((1, tq, d), lambda bi, ti, j: (bi, ti, 0)), mem, mem, cols,
                pl.BlockSpec((d, tn), lambda bi, ti, j: (0, j))]
    return pl.pallas_call(
        _with_rider(_xattn_kernel, rider, len(in_specs), 1),
        grid=(b, t // tq, d // tn),
        in_specs=in_specs + r_in, out_specs=[cols] + r_out,
        out_shape=[jax.ShapeDtypeStruct((b, t, d), F32)] + r_shape,
        scratch_shapes=[pltpu.VMEM((tq, d), BF16)],
        compiler_params=_params("parallel", "parallel", "arbitrary"),
        name="xattn_prompt",
    )(q3, mk3, mv3, x3, w_co, *r_args)


XS_SUB = 2 * X_HEADS
XS_PAIRS = X_HD // (2 * 128)


def _pack_heads(x):
    lead = x.shape[:-1]
    x = x.reshape(*lead, X_HEADS, 2 * XS_PAIRS, 128)
    x = jnp.swapaxes(x, -3, -2)
    return x.reshape(*lead, XS_PAIRS, XS_SUB, 128)


def _unpack_heads(x):
    lead = x.shape[:-3]
    x = x.reshape(*lead, 2 * XS_PAIRS, X_HEADS, 128)
    x = jnp.swapaxes(x, -3, -2)
    return x.reshape(*lead, X_HEADS * X_HD)


def _xattn_sample_units(q_ref, k_ref, v_ref, o_ref, r):
    def row():
        t = jnp.sum(k_ref[r] * q_ref[r], axis=1)
        t = t + pltpu.roll(t, X_HEADS, 1)
        s = jnp.sum(t, axis=-1, keepdims=True) * (X_HD ** -0.5)
        e = jnp.exp(s - jnp.max(s, axis=0, keepdims=True))
        p = e / jnp.sum(e, axis=0, keepdims=True)
        o_ref[r] = jnp.sum(p[:, None] * v_ref[r], axis=0)

    return [row]


def _xattn_sample_rider(grid, q, ck, cv):
    b = q.shape[0]
    n_steps, step = _grid_steps(grid)
    rows = b // n_steps
    kv_spec = pl.BlockSpec((rows, N_MEM, XS_PAIRS, XS_SUB, 128), lambda *g: (step(*g), 0, 0, 0, 0))
    q_spec = pl.BlockSpec((rows, XS_PAIRS, XS_SUB, 128), lambda *g: (step(*g), 0, 0, 0))
    return dict(make_units=_per_row_units(_xattn_sample_units, rows),
                in_specs=[q_spec, kv_spec, kv_spec], out_specs=[q_spec],
                out_shape=[jax.ShapeDtypeStruct((b, XS_PAIRS, XS_SUB, 128), F32)],
                args=[_pack_heads(q), _pack_heads(ck.reshape(b, N_MEM, X_HEADS * X_HD)),
                      _pack_heads(cv.reshape(b, N_MEM, X_HEADS * X_HD))])


def _smix_rows_kernel(z_ref, h0_ref, rc_ref, mc_ref, cwr_ref, cbr_ref, wg_ref, bg_ref, lam_ref, gout_ref,
                      cwm_ref, cbm_ref, wqk_ref,
                      yr_ref, hn_ref, rcn_ref, mcn_ref, q_ref, k_ref, ybuf):
    nb = z_ref.shape[0]
    ssq = jnp.zeros((nb, 1), F32)
    for h in range(RNN_HEADS):
        sl = slice(h * RNN_HD, (h + 1) * RNN_HD)
        xc = cbr_ref[:, sl] + cwr_ref[CONV_W - 1:CONV_W, sl] * z_ref[:, sl]
        for j in range(CONV_W - 1):
            xc = xc + cwr_ref[j:j + 1, sl] * rc_ref[:, j * D_RNN + h * RNN_HD:j * D_RNN + (h + 1) * RNN_HD]
        a, u = _lru_gates(xc, wg_ref[h], bg_ref[h], lam_ref[:, sl])
        hs = a * h0_ref[:, sl] + u
        hn_ref[:, sl] = hs
        yv = hs * jax.nn.gelu(z_ref[:, D_RNN + h * RNN_HD:D_RNN + (h + 1) * RNN_HD])
        ssq = ssq + jnp.sum(yv * yv, axis=-1, keepdims=True)
        ybuf[:, sl] = yv
    yr_ref[...] = (ybuf[...] * lax.rsqrt(ssq * (1.0 / D_RNN) + EPS) * gout_ref[...]).astype(yr_ref.dtype)
    rcn_ref[:, 0:2 * D_RNN] = rc_ref[:, D_RNN:3 * D_RNN]
    rcn_ref[:, 2 * D_RNN:3 * D_RNN] = z_ref[:, 0:D_RNN]

    for h in range(MLSTM_HEADS):
        sl = slice(h * MLSTM_HD, (h + 1) * MLSTM_HD)
        uc = cbm_ref[:, sl] + cwm_ref[CONV_W - 1:CONV_W, sl] * z_ref[:, 2 * D_RNN + h * MLSTM_HD:2 * D_RNN + (h + 1) * MLSTM_HD]
        for j in range(CONV_W - 1):
            uc = uc + cwm_ref[j:j + 1, sl] * mc_ref[:, j * D_MLSTM + h * MLSTM_HD:j * D_MLSTM + (h + 1) * MLSTM_HD]
        uc = jax.nn.silu(uc)
        qk = _dot(uc.astype(BF16), wqk_ref[h])
        q_ref[:, sl] = qk[:, :MLSTM_HD]
        k_ref[:, sl] = qk[:, MLSTM_HD:] * (MLSTM_HD ** -0.5)
    mcn_ref[:, 0:2 * D_MLSTM] = mc_ref[:, D_MLSTM:3 * D_MLSTM]
    mcn_ref[:, 2 * D_MLSTM:3 * D_MLSTM] = z_ref[:, 2 * D_RNN:2 * D_RNN + D_MLSTM]


def _smix_rows(z, h0, rconv, mconv, w):
    nb = z.shape[0]
    outs = [jax.ShapeDtypeStruct((nb, D_RNN), BF16),
            jax.ShapeDtypeStruct((nb, D_RNN), F32),
            jax.ShapeDtypeStruct((nb, 3 * D_RNN), F32),
            jax.ShapeDtypeStruct((nb, 3 * D_MLSTM), F32),
            jax.ShapeDtypeStruct((nb, D_MLSTM), F32),
            jax.ShapeDtypeStruct((nb, D_MLSTM), F32)]
    return pl.pallas_call(
        _smix_rows_kernel,
        out_shape=outs,
        scratch_shapes=[pltpu.VMEM((nb, D_RNN), F32)],
        compiler_params=pltpu.CompilerParams(vmem_limit_bytes=V7X_VMEM_LIMIT_BYTES),
        name="smix_rows",
    )(z, h0, rconv, mconv, w['conv_rnn_w'], w['conv_rnn_b'], w['lru_wg'], w['lru_bg'], w['lru_lambda'],
      w['g_rnn_out'], w['conv_ml_w'], w['conv_ml_b'], w['ml_wqk'])


MXU_MIN_ROWS = 8


def _smix_state_units(c_ref, q_ref, k_ref, n_ref, v_ref, og_ref, zg_ref, gb_ref, m_ref, gout_ref,
                      cn_ref, nn_ref, mn_ref, y_ref, r):
    lane_g = lax.broadcasted_iota(jnp.int32, (1, N_GATE_PAD), 1)
    first_row = lax.broadcasted_iota(jnp.int32, (MXU_MIN_ROWS, MLSTM_HD), 0) == 0

    def head(h, zg, m_out):
        sl = slice(h * MLSTM_HD, (h + 1) * MLSTM_HD)
        q = q_ref[r, :, sl]
        k = k_ref[r, :, sl]
        n = n_ref[r, :, sl]
        v = v_ref[r, :, sl]
        ig = zg[:, h:h + 1]
        lf = jax.nn.log_sigmoid(zg[:, MLSTM_HEADS + h:MLSTM_HEADS + h + 1])
        m_prev = m_ref[r, :, h:h + 1]
        m_t = jnp.maximum(lf + m_prev, ig)
        w_inter = jnp.exp(lf + m_prev - m_t)
        g_in = jnp.exp(ig - m_t)
        s = jnp.sum(q * k, axis=1, keepdims=True) * g_in
        den = w_inter * jnp.sum(n * q, axis=1, keepdims=True) + s
        denom = jnp.maximum(jnp.abs(den), jnp.exp(-m_t))
        cmat = c_ref[r, h]
        q8 = jnp.broadcast_to(q, (MXU_MIN_ROWS, MLSTM_HD)).astype(BF16)
        cq = lax.dot_general(q8, cmat.astype(BF16), (((1,), (1,)), ((), ())),
                             preferred_element_type=F32)[0:1, :]
        hm = (w_inter * cq + s * v) / denom
        gv8 = jnp.where(first_row, jnp.broadcast_to(g_in * v, (MXU_MIN_ROWS, MLSTM_HD)), 0.0).astype(BF16)
        k8 = jnp.broadcast_to(k, (MXU_MIN_ROWS, MLSTM_HD)).astype(BF16)
        outer = lax.dot_general(gv8, k8, (((0,), (0,)), ((), ())), preferred_element_type=F32)
        cn_ref[r, h] = w_inter * cmat + outer
        nn_ref[r, :, sl] = w_inter * n + g_in * k
        y = _rms(hm, gout_ref[...]) * jax.nn.sigmoid(og_ref[r, :, sl])
        y_ref[r, :, sl] = y.astype(y_ref.dtype)
        return jnp.where(lane_g == h, m_t, m_out)

    def row():
        zg = zg_ref[r] + gb_ref[...]
        m_out = jnp.zeros((1, N_GATE_PAD), F32)
        for h in range(MLSTM_HEADS):
            m_out = head(h, zg, m_out)
        mn_ref[r] = m_out

    return [row]


def _smix_state_rider(grid, c, q3, k3, n3, z3, zg3, gate_bias, m3, gout):
    nb = c.shape[0]
    n_steps, step = _grid_steps(grid)
    rows = nb // n_steps
    row = lambda width, col=0: pl.BlockSpec((rows, 1, width), lambda *g: (step(*g), 0, col))
    cblk = pl.BlockSpec((rows, MLSTM_HEADS, MLSTM_HD, MLSTM_HD), lambda *g: (step(*g), 0, 0, 0))
    return dict(
        make_units=_per_row_units(_smix_state_units, rows),
        in_specs=[cblk, row(D_MLSTM), row(D_MLSTM), row(D_MLSTM),
                  row(D_MLSTM, 3 * D_RNN // D_MLSTM), row(D_MLSTM, 4 * D_RNN // D_MLSTM),
                  row(N_GATE_PAD),
                  pl.BlockSpec((1, N_GATE_PAD), lambda *g: (0, 0)),
                  row(MLSTM_HEADS),
                  pl.BlockSpec((1, MLSTM_HD), lambda *g: (0, 0))],
        out_specs=[cblk, row(D_MLSTM), row(N_GATE_PAD), row(D_MLSTM)],
        out_shape=[jax.ShapeDtypeStruct(c.shape, F32),
                   jax.ShapeDtypeStruct((nb, 1, D_MLSTM), F32),
                   jax.ShapeDtypeStruct((nb, 1, N_GATE_PAD), F32),
                   jax.ShapeDtypeStruct((nb, 1, D_MLSTM), F32)],
        args=[c, q3, k3, n3, z3, z3, zg3, gate_bias, m3, gout])


RGLRU_CHUNK = 256
PROMPT_TM = 1024


def _layer(xp, xs, mem, rg_h, rg_conv, c, n, mst, ml_conv, ck, cv, w, g_final, final_norm):
    b, t, d = xp.shape
    nb = xs.shape[0]
    m = b * t
    tm = PROMPT_TM
    xp2 = xp.reshape(m, d)
    xs2 = xs.reshape(nb, d)
    w_in_args = dict(tn=1024, out_dtype=F32, w_extra=w['w_in_gate'], n_cols=D_MAIN, w_is_transposed=True)

    zs, zgs, w_in_bf = _norm_matmul(xs2, w['g_mix'], w['w_in_main'], tm=nb, emit_w_bf16=True, **w_in_args)
    ys_rnn, s_h, s_rconv, s_mconv, qs, ks = _smix_rows(
        zs, rg_h, rg_conv.reshape(nb, 3 * D_RNN), ml_conv.reshape(nb, 3 * D_MLSTM), w)

    z, zg = _norm_matmul(xp2, w['g_mix'], w_in_bf, tm=tm, **w_in_args)
    z3 = z.reshape(b, t, D_MAIN)
    zg3 = zg.reshape(b, t, N_GATE_PAD)
    h0 = jnp.zeros((b, 1, D_RNN), F32)
    conv0 = jnp.zeros((b, CONV_W - 1, D_RNN), F32)
    y_rnn, p_h, p_rconv, w_out_cq, w_co = _rglru_prompt(
        z3, h0, conv0, w, tc=RGLRU_CHUNK,
        rider=_cast_rider((b, t // RGLRU_CHUNK), [[w['w_out'], w['w_cq']], [w['w_co']]]))

    c0 = jnp.zeros((b, MLSTM_HEADS, MLSTM_HD, MLSTM_HD), F32)
    n0 = jnp.zeros((b, MLSTM_HEADS, 1, MLSTM_HD), F32)
    m0 = jnp.zeros((b, MLSTM_HEADS, 1, N_GATE_PAD), F32)
    mconv0 = jnp.zeros((b, CONV_W - 1, D_MLSTM), F32)
    y_ml, p_c, p_n, p_m, p_mconv, w_up, w_down = _mlstm_prompt(
        z3, zg3, c0, n0, m0, mconv0, w, rider=_cast_rider((b, t // CHUNK), [[w['w_up']], [w['w_down']]]))

    x1, q = _matmul_res_norm_matmul([y_rnn.reshape(m, D_RNN), y_ml.reshape(m, D_MLSTM)], w_out_cq, xp2,
                                    w['g_xattn'], tm=tm, tn=1024, out2_dtype=BF16)
    mem2 = mem.reshape(b * N_MEM, d)
    mk = _norm_matmul(mem2, w['g_mem'], w['w_mk'], tm=b * N_MEM, tn=1024, out_dtype=F32)
    mv = _norm_matmul(mem2, w['g_mem'], w['w_mv'], tm=b * N_MEM, tn=1024, out_dtype=F32)
    xattn_tn = 512
    x2, s_c, s_n, s_m, ys_ml = _xattn_prompt(
        q.reshape(b, t, d), mk.reshape(b, N_MEM, d), mv.reshape(b, N_MEM, d), x1.reshape(b, t, d), w_co,
        tq=tm, tn=xattn_tn,
        rider=_smix_state_rider(
            (b, t // tm, d // xattn_tn), c, qs.reshape(nb, 1, D_MLSTM), ks.reshape(nb, 1, D_MLSTM),
            n.reshape(nb, 1, D_MLSTM), zs.reshape(nb, 1, D_MAIN), zgs.reshape(nb, 1, N_GATE_PAD), w['gate_bias'],
            mst.reshape(nb, 1, MLSTM_HEADS), w['g_ml_out']))

    xs1, qx = _matmul_res_norm_matmul([ys_rnn, ys_ml.reshape(nb, D_MLSTM).astype(BF16)], w_out_cq, xs2,
                                      w['g_xattn'], tm=nb, tn=1024, out2_dtype=F32)
    ffn_tf = 512
    xp_out, os_packed = _ffn(x2.reshape(m, d), w['g_ffn'], w_up, w_down, g_final, tm=tm, tf=ffn_tf,
                             final_norm=final_norm,
                             rider=_xattn_sample_rider((m // tm, D_FF // ffn_tf), qx, ck, cv))

    os_ = _unpack_heads(os_packed).astype(BF16)
    xs2_ = _matmul_res([os_], w_co, xs1, tm=nb, tn=1024)
    xs_out, = _ffn(xs2_, w['g_ffn'], w_up, w_down, g_final, tm=nb, tf=ffn_tf, final_norm=final_norm)

    new_p = (p_h.reshape(b, D_RNN), p_rconv, p_c, p_n.reshape(b, MLSTM_HEADS, MLSTM_HD), p_m[:, :, 0, 0], p_mconv,
             mk.reshape(b, N_MEM, X_HEADS, X_HD), mv.reshape(b, N_MEM, X_HEADS, X_HD))
    new_s = (s_h, s_rconv.reshape(nb, CONV_W - 1, D_RNN), s_c, s_n.reshape(nb, MLSTM_HEADS, MLSTM_HD),
             s_m[:, 0, :MLSTM_HEADS], s_mconv.reshape(nb, CONV_W - 1, D_MLSTM))
    return xp_out.reshape(b, t, d), xs_out.reshape(nb, 1, d), new_p, new_s


def _layer_weights(l, g_mix, w_in, conv_rnn_w, conv_rnn_b, lru_wa, lru_ba, lru_wx, lru_bx, lru_lambda,
                   g_rnn_out, conv_ml_w, conv_ml_b, ml_wq, ml_wk, ml_bi, ml_bf, g_ml_out, w_out,
                   g_xattn, g_mem, w_cq, w_mk, w_mv, w_co, g_ffn, w_up, w_down):
    n_gate = 2 * MLSTM_HEADS
    w_in_t = jnp.swapaxes(w_in[l], 0, 1)
    w_gate_t = jnp.pad(w_in_t[D_MAIN:], ((0, N_GATE_PAD - n_gate), (0, 0)))
    gate_bias = jnp.pad(jnp.concatenate([ml_bi[l], ml_bf[l]]), (0, N_GATE_PAD - n_gate))
    return dict(
        g_mix=g_mix[l], w_in_main=w_in_t, w_in_gate=w_gate_t.astype(BF16),
        conv_rnn_w=conv_rnn_w[l], conv_rnn_b=conv_rnn_b[l].reshape(1, D_RNN),
        lru_wg=jnp.concatenate([lru_wa[l], lru_wx[l]], axis=-1).astype(BF16),
        lru_bg=jnp.concatenate([lru_ba[l], lru_bx[l]], axis=-1).reshape(RNN_HEADS, 1, 2 * RNN_HD),
        lru_lambda=lru_lambda[l].reshape(1, D_RNN), g_rnn_out=g_rnn_out[l].reshape(1, D_RNN),
        conv_ml_w=conv_ml_w[l], conv_ml_b=conv_ml_b[l].reshape(1, D_MLSTM),
        ml_wqk=jnp.concatenate([ml_wq[l], ml_wk[l]], axis=-1).astype(BF16),
        gate_bias=gate_bias.reshape(1, N_GATE_PAD), g_ml_out=g_ml_out[l].reshape(1, MLSTM_HD),
        w_out=w_out[l], w_cq=w_cq[l], g_xattn=g_xattn[l], g_mem=g_mem[l], w_mk=w_mk[l], w_mv=w_mv[l],
        w_co=w_co[l], g_ffn=g_ffn[l], w_up=w_up[l], w_down=w_down[l])


def kernel(x_prompt, x_sample, mem_prompt, state_rglru_h, state_rglru_conv, state_mlstm_C, state_mlstm_n, state_mlstm_m, state_mlstm_conv, cache_mem_k, cache_mem_v, g_mix, w_in, conv_rnn_w, conv_rnn_b, lru_wa, lru_ba, lru_wx, lru_bx, lru_lambda, g_rnn_out, conv_ml_w, conv_ml_b, ml_wq, ml_wk, ml_bi, ml_bf, g_ml_out, w_out, g_xattn, g_mem, w_cq, w_mk, w_mv, w_co, g_ffn, w_up, w_down, g_final):
    depth = w_in.shape[0]
    xp, xs = x_prompt, x_sample
    p_out = [[] for _ in range(8)]
    s_out = [[] for _ in range(6)]
    for l in range(depth):
        w = _layer_weights(l, g_mix, w_in, conv_rnn_w, conv_rnn_b, lru_wa, lru_ba, lru_wx, lru_bx, lru_lambda,
                           g_rnn_out, conv_ml_w, conv_ml_b, ml_wq, ml_wk, ml_bi, ml_bf, g_ml_out, w_out,
                           g_xattn, g_mem, w_cq, w_mk, w_mv, w_co, g_ffn, w_up, w_down)
        last = l == depth - 1
        xp, xs, new_p, new_s = _layer(xp, xs, mem_prompt, state_rglru_h[l], state_rglru_conv[l], state_mlstm_C[l],
                                      state_mlstm_n[l], state_mlstm_m[l], state_mlstm_conv[l], cache_mem_k[l],
                                      cache_mem_v[l], w, g_final, last)
        for j, a in enumerate(new_p):
            p_out[j].append(a)
        for j, a in enumerate(new_s):
            s_out[j].append(a)
    P = [jnp.stack(a, axis=0) for a in p_out]
    S = [jnp.stack(a, axis=0) for a in s_out]
    return (xp, xs, P[0], P[1], P[2], P[3], P[4], P[5], P[6], P[7],
            S[0], S[1], S[2], S[3], S[4], S[5])
```

```python
import functools

import jax
import jax.numpy as jnp
from jax import lax
from jax.experimental import pallas as pl
from jax.experimental.pallas import tpu as pltpu

F32 = jnp.float32
BF16 = jnp.bfloat16

D_MODEL = 2048
D_RNN = 1024
RNN_HEADS = 8
RNN_HD = 128
CONV_W = 4
LRU_C = 8.0
D_MLSTM = 1024
MLSTM_HEADS = 4
MLSTM_HD = 256
CHUNK = 128
N_MEM = 256
X_HEADS = 4
X_HD = 512
D_FF = 8192
EPS = 1e-6
D_MAIN = 5 * 1024
N_GATE_PAD = 128

V7X_VMEM_LIMIT_BYTES = 56 * 1024 * 1024
V7X_VMEM_LIMIT_FFN_BYTES = 63 * 1024 * 1024
SUBLANES = 8


def _params(*sem, vmem_limit_bytes=V7X_VMEM_LIMIT_BYTES):
    return pltpu.CompilerParams(dimension_semantics=sem, vmem_limit_bytes=vmem_limit_bytes)


def _rms(x, g):
    ms = jnp.mean(x * x, axis=-1, keepdims=True)
    return x * lax.rsqrt(ms + EPS) * g


def _dot(a, b):
    return jnp.dot(a.astype(BF16), b.astype(BF16), preferred_element_type=F32)


def _dot_t(a, bt):
    return lax.dot_general(a.astype(BF16), bt.astype(BF16), (((1,), (1,)), ((), ())), preferred_element_type=F32)


def _with_rider(host_body, rider, n_in, n_out):
    if rider is None:
        return host_body
    r_in, r_out = len(rider['in_specs']), len(rider['out_specs'])

    def body(*refs):
        host_in, refs = refs[:n_in], refs[n_in:]
        rider_in, refs = refs[:r_in], refs[r_in:]
        host_out, refs = refs[:n_out], refs[n_out:]
        rider_out, scratch = refs[:r_out], refs[r_out:]
        host_body(*host_in, *host_out, *scratch, rider=rider['make_units'](*rider_in, *rider_out))
    return body


def _grid_steps(grid):
    n = 1
    for g in grid:
        n *= g

    def step(*idx):
        s = idx[0]
        for g, i in zip(grid[1:], idx[1:]):
            s = s * g + i
        return s
    return n, step


def _per_row_units(row_units, rows):
    return lambda *refs: [u for r in range(rows) for u in row_units(*refs, r)]


def _merge_riders(a, b):
    na_in, na_out = len(a['in_specs']), len(a['out_specs'])
    nb_in = len(b['in_specs'])

    def make_units(*refs):
        a_in, refs = refs[:na_in], refs[na_in:]
        b_in, refs = refs[:nb_in], refs[nb_in:]
        a_out, b_out = refs[:na_out], refs[na_out:]
        return a['make_units'](*a_in, *a_out) + b['make_units'](*b_in, *b_out)

    return dict(make_units=make_units, **{k: a[k] + b[k] for k in ('in_specs', 'out_specs', 'out_shape', 'args')})


def _cast_rider(grid, groups):
    n_steps, step = _grid_steps(grid)
    in_specs, out_specs, out_shape, args, widths = [], [], [], [], []
    for group in groups:
        rps = group[0].shape[0] // n_steps
        for a in group:
            in_specs.append(pl.BlockSpec((rps, a.shape[1]), lambda *g: (step(*g), 0)))
            args.append(a)
        cols = sum(a.shape[1] for a in group)
        out_specs.append(pl.BlockSpec((rps, cols), lambda *g: (step(*g), 0)))
        out_shape.append(jax.ShapeDtypeStruct((group[0].shape[0], cols), BF16))
        widths.append([a.shape[1] for a in group])

    def make_units(*refs):
        in_refs, out_refs = list(refs[:len(args)]), refs[len(args):]
        units = []
        for o_ref, ws in zip(out_refs, widths):
            srcs, in_refs = in_refs[:len(ws)], in_refs[len(ws):]

            def cast(o_ref=o_ref, srcs=srcs, ws=ws):
                c0 = 0
                for s_ref, wd in zip(srcs, ws):
                    o_ref[:, c0:c0 + wd] = s_ref[...].astype(BF16)
                    c0 += wd
            units.append(cast)
        return units

    return dict(make_units=make_units, in_specs=in_specs, out_specs=out_specs, out_shape=out_shape, args=args)


def _run_rider_share(rider, i, n):
    if rider is None:
        return
    for unit in rider[i * len(rider) // n:(i + 1) * len(rider) // n]:
        unit()


def _rider_parts(rider):
    if rider is None:
        return [], [], [], []
    return rider['in_specs'], rider['out_specs'], rider['out_shape'], rider['args']


def _norm_matmul_kernel(*refs, has_extra, w_is_transposed, emit_w_bf16):
    refs = list(refs)
    xn_ref = refs.pop()
    wb_ref = refs.pop() if emit_w_bf16 else None
    if has_extra:
        x_ref, g_ref, w_ref, we_ref, o_ref, oe_ref = refs
    else:
        x_ref, g_ref, w_ref, o_ref = refs
    dot = _dot_t if w_is_transposed else _dot

    @pl.when(pl.program_id(1) == 0)
    def _():
        xn = _rms(x_ref[...], g_ref[...]).astype(BF16)
        xn_ref[...] = xn
        if has_extra:
            oe_ref[...] = dot(xn, we_ref[...])

    wb = w_ref[...].astype(BF16)
    if emit_w_bf16:
        wb_ref[...] = wb
    o_ref[...] = dot(xn_ref[...], wb).astype(o_ref.dtype)


def _norm_matmul(x, g, w, *, tm, tn, out_dtype, w_extra=None, n_cols=None, w_is_transposed=False,
                 emit_w_bf16=False):
    m, k = x.shape
    n = n_cols or w.shape[0 if w_is_transposed else 1]
    has_extra = w_extra is not None
    w_spec = (pl.BlockSpec((tn, k), lambda i, j: (j, 0)) if w_is_transposed
              else pl.BlockSpec((k, tn), lambda i, j: (0, j)))
    in_specs = [pl.BlockSpec((tm, k), lambda i, j: (i, 0)),
                pl.BlockSpec((1, k), lambda i, j: (0, 0)),
                w_spec]
    out_specs = [pl.BlockSpec((tm, tn), lambda i, j: (i, j))]
    out_shape = [jax.ShapeDtypeStruct((m, n), out_dtype)]
    args = [x, g.reshape(1, k), w]
    if has_extra:
        ne = w_extra.shape[0 if w_is_transposed else 1]
        in_specs.append(pl.BlockSpec(w_extra.shape, lambda i, j: (0, 0)))
        out_specs.append(pl.BlockSpec((tm, ne), lambda i, j: (i, 0)))
        out_shape.append(jax.ShapeDtypeStruct((m, ne), F32))
        args.append(w_extra)
    if emit_w_bf16:
        assert m == tm
        out_specs.append(w_spec)
        out_shape.append(jax.ShapeDtypeStruct((n, k) if w_is_transposed else (k, n), BF16))
    res = pl.pallas_call(
        functools.partial(_norm_matmul_kernel, has_extra=has_extra, w_is_transposed=w_is_transposed,
                          emit_w_bf16=emit_w_bf16),
        grid=(m // tm, n // tn),
        in_specs=in_specs, out_specs=out_specs, out_shape=out_shape,
        scratch_shapes=[pltpu.VMEM((tm, k), BF16)],
        compiler_params=_params("parallel", "arbitrary"),
        name="norm_matmul",
    )(*args)
    return res if len(res) > 1 else res[0]


def _matmul_res_kernel(*refs, n_parts):
    a_refs, w_refs = refs[:n_parts], refs[n_parts:2 * n_parts]
    r_ref, o_ref = refs[2 * n_parts:]
    acc = r_ref[...]
    for a_ref, w_ref in zip(a_refs, w_refs):
        acc = acc + _dot(a_ref[...], w_ref[...])
    o_ref[...] = acc


def _matmul_res(a_parts, w, res, *, tm, tn):
    n_parts = len(a_parts)
    m, kp = a_parts[0].shape
    n = w.shape[1]
    a_specs = [pl.BlockSpec((tm, kp), lambda i, j: (i, 0)) for _ in a_parts]
    w_specs = [pl.BlockSpec((kp, tn), lambda i, j, p=p: (p, j)) for p in range(n_parts)]
    return pl.pallas_call(
        functools.partial(_matmul_res_kernel, n_parts=n_parts),
        grid=(m // tm, n // tn),
        in_specs=a_specs + w_specs + [pl.BlockSpec((tm, tn), lambda i, j: (i, j))],
        out_specs=pl.BlockSpec((tm, tn), lambda i, j: (i, j)),
        out_shape=jax.ShapeDtypeStruct((m, n), F32),
        compiler_params=_params("parallel", "arbitrary"),
        name="matmul_res",
    )(*a_parts, *([w] * n_parts), res)


def _matmul_res_norm_matmul_kernel(*refs, n_parts, n1):
    a_refs = refs[:n_parts]
    w_ref, r_ref, g_ref, x1_ref, q_ref, x1_s, xn_s = refs[n_parts:]
    j = pl.program_id(1)
    tn = w_ref.shape[1]
    kp = a_refs[0].shape[1]

    @pl.when(j < n1)
    def _():
        x1 = r_ref[...]
        for p, a_ref in enumerate(a_refs):
            x1 = x1 + _dot(a_ref[...], w_ref[p * kp:(p + 1) * kp, :])
        x1_ref[...] = x1
        x1_s[j] = x1

    @pl.when(j == n1)
    def _():
        ssq = sum(jnp.sum(jnp.square(x1_s[c]), axis=-1, keepdims=True) for c in range(n1))
        scale = lax.rsqrt(ssq * (1.0 / (n1 * tn)) + EPS)
        for c in range(n1):
            xn_s[:, c * tn:(c + 1) * tn] = (x1_s[c] * scale * g_ref[:, c * tn:(c + 1) * tn]).astype(xn_s.dtype)

    @pl.when(j >= n1)
    def _():
        q_ref[...] = _dot(xn_s[...], w_ref[...]).astype(q_ref.dtype)


def _matmul_res_norm_matmul(a_parts, w12, res, g, *, tm, tn, out2_dtype):
    n_parts = len(a_parts)
    m, kp = a_parts[0].shape
    d = res.shape[1]
    n1 = d // tn
    n2 = (w12.shape[1] - d) // tn
    return pl.pallas_call(
        functools.partial(_matmul_res_norm_matmul_kernel, n_parts=n_parts, n1=n1),
        grid=(m // tm, n1 + n2),
        in_specs=[pl.BlockSpec((tm, kp), lambda i, j: (i, 0))] * n_parts + [
            pl.BlockSpec((n_parts * kp, tn), lambda i, j: (0, j)),
            pl.BlockSpec((tm, tn), lambda i, j: (i, jnp.minimum(j, n1 - 1))),
            pl.BlockSpec((1, d), lambda i, j: (0, 0))],
        out_specs=[pl.BlockSpec((tm, tn), lambda i, j: (i, jnp.minimum(j, n1 - 1))),
                   pl.BlockSpec((tm, tn), lambda i, j: (i, jnp.maximum(j - n1, 0)))],
        out_shape=[jax.ShapeDtypeStruct((m, d), F32), jax.ShapeDtypeStruct((m, n2 * tn), out2_dtype)],
        scratch_shapes=[pltpu.VMEM((n1, tm, tn), F32), pltpu.VMEM((tm, d), BF16)],
        compiler_params=_params("parallel", "arbitrary"),
        name="matmul_res_norm_matmul",
    )(*a_parts, w12, res, g.reshape(1, d))


def _ffn_kernel(x_ref, g_ref, wu_ref, wd_ref, gf_ref, o_ref, xn_ref, *, final_norm, rider=None):
    f = pl.program_id(1)

    @pl.when(f == 0)
    def _():
        xn_ref[...] = _rms(x_ref[...], g_ref[...]).astype(BF16)
        o_ref[...] = jnp.zeros_like(o_ref)

    _run_rider_share(rider, 0, 1)
    h = _dot(xn_ref[...], wu_ref[...])
    h = jnp.square(jnp.maximum(h, 0.0)).astype(BF16)
    o_ref[...] += _dot(h, wd_ref[...])

    @pl.when(f == pl.num_programs(1) - 1)
    def _():
        y = x_ref[...] + o_ref[...]
        if final_norm:
            y = _rms(y, gf_ref[...])
        o_ref[...] = y


def _ffn(x, g, w_up, w_down, g_final, *, tm, tf, final_norm, rider=None):
    m, d = x.shape
    dff = w_up.shape[1]
    r_in, r_out, r_shape, r_args = _rider_parts(rider)
    in_specs = [pl.BlockSpec((tm, d), lambda i, f: (i, 0)),
                pl.BlockSpec((1, d), lambda i, f: (0, 0)),
                pl.BlockSpec((d, tf), lambda i, f: (0, f)),
                pl.BlockSpec((tf, d), lambda i, f: (f, 0)),
                pl.BlockSpec((1, d), lambda i, f: (0, 0))]
    out_specs = [pl.BlockSpec((tm, d), lambda i, f: (i, 0))]
    return pl.pallas_call(
        _with_rider(functools.partial(_ffn_kernel, final_norm=final_norm), rider, len(in_specs), len(out_specs)),
        grid=(m // tm, dff // tf),
        in_specs=in_specs + r_in, out_specs=out_specs + r_out,
        out_shape=[jax.ShapeDtypeStruct((m, d), F32)] + r_shape,
        scratch_shapes=[pltpu.VMEM((tm, d), BF16)],
        compiler_params=_params("parallel", "arbitrary", vmem_limit_bytes=V7X_VMEM_LIMIT_FFN_BYTES),
        name="ffn",
    )(x, g.reshape(1, d), w_up, w_down, g_final.reshape(1, d), *r_args)


def _lru_gates(xc, wg, bg, lam):
    g = _dot(xc.astype(BF16), wg) + bg
    r = jax.nn.sigmoid(g[:, :RNN_HD])
    i = jax.nn.sigmoid(g[:, RNN_HD:])
    log_a = -LRU_C * r * jax.nn.softplus(-lam)
    a = jnp.exp(log_a)
    u = jnp.sqrt(-jnp.tanh(log_a) * (a * a + 1.0)) * (i * xc)
    return a, u


def _rglru_kernel(xr_ref, gr_ref, h0_ref, c0_ref, cw_ref, cb_ref, wg_ref, bg_ref, lam_ref, gout_ref,
                  y_ref, hn_ref, cn_ref, xbuf, hc, ybuf, *, tc, rider=None):
    c = pl.program_id(1)

    @pl.when(c == 0)
    def _():
        xbuf[0:8, :] = jnp.zeros((8, D_RNN), F32)
        xbuf[5:8, :] = c0_ref[0]
        hc[...] = h0_ref[0]

    @pl.when(c > 0)
    def _():
        xbuf[0:8, :] = xbuf[tc:tc + 8, :]

    xbuf[8:8 + tc, :] = xr_ref[0]
    ng = tc // SUBLANES
    sub = lax.broadcasted_iota(jnp.int32, (ng, SUBLANES, RNN_HD), 1)
    ssq = jnp.zeros((tc, 1), F32)
    for h in range(RNN_HEADS):
        _run_rider_share(rider, h, RNN_HEADS)
        sl = slice(h * RNN_HD, (h + 1) * RNN_HD)
        xc = cb_ref[:, sl]
        for j in range(CONV_W):
            xc = xc + cw_ref[j:j + 1, sl] * xbuf[5 + j:5 + j + tc, sl]
        a, u = _lru_gates(xc, wg_ref[h], bg_ref[h], lam_ref[:, sl])
        a = a.reshape(ng, SUBLANES, RNN_HD)
        u = u.reshape(ng, SUBLANES, RNN_HD)
        d = 1
        while d < SUBLANES:
            keep = sub >= d
            a_prev = jnp.where(keep, pltpu.roll(a, d, 1), 1.0)
            u_prev = jnp.where(keep, pltpu.roll(u, d, 1), 0.0)
            u = u + a * u_prev
            a = a * a_prev
            d *= 2
        carry = hc[:, sl]
        for g in range(ng):
            hg = u[g] + a[g] * carry
            carry = hg[SUBLANES - 1:SUBLANES, :]
            ybuf[g * SUBLANES:(g + 1) * SUBLANES, sl] = hg
        hc[:, sl] = carry
        yv = ybuf[:, sl] * jax.nn.gelu(gr_ref[0, :, sl])
        ssq = ssq + jnp.sum(yv * yv, axis=-1, keepdims=True)
        ybuf[:, sl] = yv
    y = ybuf[...] * lax.rsqrt(ssq * (1.0 / D_RNN) + EPS) * gout_ref[...]
    y_ref[0] = y.astype(y_ref.dtype)

    @pl.when(c == pl.num_programs(1) - 1)
    def _():
        hn_ref[0] = hc[...]
        cn_ref[0] = xbuf[tc + 5:tc + 8, :]


def _rglru_prompt(z3, h0, conv0, w, *, tc, rider=None):
    b, t, _ = z3.shape
    full2 = lambda bi, ci: (0, 0)
    r_in, r_out, r_shape, r_args = _rider_parts(rider)
    in_specs = [pl.BlockSpec((1, tc, D_RNN), lambda bi, ci: (bi, ci, 0)),
                pl.BlockSpec((1, tc, D_RNN), lambda bi, ci: (bi, ci, 1)),
                pl.BlockSpec((1, 1, D_RNN), lambda bi, ci: (bi, 0, 0)),
                pl.BlockSpec((1, CONV_W - 1, D_RNN), lambda bi, ci: (bi, 0, 0)),
                pl.BlockSpec((CONV_W, D_RNN), full2),
                pl.BlockSpec((1, D_RNN), full2),
                pl.BlockSpec((RNN_HEADS, RNN_HD, 2 * RNN_HD), lambda bi, ci: (0, 0, 0)),
                pl.BlockSpec((RNN_HEADS, 1, 2 * RNN_HD), lambda bi, ci: (0, 0, 0)),
                pl.BlockSpec((1, D_RNN), full2),
                pl.BlockSpec((1, D_RNN), full2)]
    out_specs = [pl.BlockSpec((1, tc, D_RNN), lambda bi, ci: (bi, ci, 0)),
                 pl.BlockSpec((1, 1, D_RNN), lambda bi, ci: (bi, 0, 0)),
                 pl.BlockSpec((1, CONV_W - 1, D_RNN), lambda bi, ci: (bi, 0, 0))]
    out_shape = [jax.ShapeDtypeStruct((b, t, D_RNN), BF16),
                 jax.ShapeDtypeStruct((b, 1, D_RNN), F32),
                 jax.ShapeDtypeStruct((b, CONV_W - 1, D_RNN), F32)]
    return pl.pallas_call(
        _with_rider(functools.partial(_rglru_kernel, tc=tc), rider, len(in_specs), len(out_specs)),
        grid=(b, t // tc),
        in_specs=in_specs + r_in, out_specs=out_specs + r_out, out_shape=out_shape + r_shape,
        scratch_shapes=[pltpu.VMEM((tc + 8, D_RNN), F32),
                        pltpu.VMEM((1, D_RNN), F32),
                        pltpu.VMEM((tc, D_RNN), F32)],
        compiler_params=_params("parallel", "arbitrary"),
        name="rglru_prompt",
    )(z3, z3, h0, conv0, w['conv_rnn_w'], w['conv_rnn_b'], w['lru_wg'], w['lru_bg'],
      w['lru_lambda'], w['g_rnn_out'], *r_args)


def _mlstm_kernel(u_ref, v_ref, og_ref, zg_ref, gb_ref, c0_ref, n0_ref, m0_ref, cv0_ref,
                  cw_ref, cb_ref, wqk_ref, gout_ref,
                  y_ref, cn_ref, nn_ref, mn_ref, cvn_ref, ubuf, cst, nst, mst, rider=None):
    c = pl.program_id(1)
    L = CHUNK
    n_seq = u_ref.shape[0]

    @pl.when(c == 0)
    def _():
        ubuf[:, 0:8, :] = jnp.zeros((n_seq, 8, D_MLSTM), F32)
        ubuf[:, 5:8, :] = cv0_ref[...]
        cst[...] = c0_ref[...]
        nst[...] = n0_ref[...]
        mst[...] = m0_ref[...]

    @pl.when(c > 0)
    def _():
        ubuf[:, 0:8, :] = ubuf[:, L:L + 8, :]

    ubuf[:, 8:8 + L, :] = u_ref[...]

    ti = lax.broadcasted_iota(jnp.int32, (L, L), 0)
    si = lax.broadcasted_iota(jnp.int32, (L, L), 1)
    causal = si <= ti
    anti = ti <= si
    zgs = [zg_ref[s] + gb_ref[...] for s in range(n_seq)]
    zgts = [zg.T for zg in zgs]

    for h in range(MLSTM_HEADS):
        _run_rider_share(rider, h, MLSTM_HEADS)
        sl = slice(h * MLSTM_HD, (h + 1) * MLSTM_HD)
        for s in range(n_seq):
            zg, zgt = zgs[s], zgts[s]
            uc = cb_ref[:, sl]
            for j in range(CONV_W):
                uc = uc + cw_ref[j:j + 1, sl] * ubuf[s, 5 + j:5 + j + L, sl]
            uc = jax.nn.silu(uc)
            qk = _dot(uc.astype(BF16), wqk_ref[h])
            q = qk[:, :MLSTM_HD]
            k = qk[:, MLSTM_HD:] * (MLSTM_HD ** -0.5)
            v = v_ref[s, :, sl]
            qb, kb = q.astype(BF16), k.astype(BF16)

            icol = zg[:, h:h + 1]
            fcol = jax.nn.log_sigmoid(zg[:, MLSTM_HEADS + h:MLSTM_HEADS + h + 1])
            irow = zgt[h:h + 1, :]
            frow = jax.nn.log_sigmoid(zgt[MLSTM_HEADS + h:MLSTM_HEADS + h + 1, :])
            bcol = jnp.sum(jnp.where(causal, frow, 0.0), axis=1, keepdims=True)
            brow = jnp.sum(jnp.where(anti, fcol, 0.0), axis=0, keepdims=True)
            dmat = jnp.where(causal, irow + bcol - brow, -jnp.inf)
            m_prev = mst[s, h, :, 0:1]
            inter = bcol + m_prev
            m_t = jnp.maximum(inter, jnp.max(dmat, axis=1, keepdims=True))
            w_inter = jnp.exp(inter - m_t)
            sc = lax.dot_general(qb, kb, (((1,), (1,)), ((), ())), preferred_element_type=F32) * jnp.exp(dmat - m_t)
            cmat = cst[s, h]
            nrow = nst[s, h]
            cq = lax.dot_general(qb, cmat.astype(BF16), (((1,), (1,)), ((), ())), preferred_element_type=F32)
            num = w_inter * cq + _dot(sc.astype(BF16), v.astype(BF16))
            den = w_inter * jnp.sum(q * nrow, axis=1, keepdims=True) + jnp.sum(sc, axis=1, keepdims=True)
            hm = num / jnp.maximum(jnp.abs(den), jnp.exp(-m_t))

            m_new = m_t[L - 1:L, :]
            b_last = bcol[L - 1:L, :]
            g_state = jnp.exp(b_last + m_prev - m_new)
            g_in = jnp.exp(icol + b_last - bcol - m_new)
            cst[s, h] = g_state * cmat + lax.dot_general((g_in * v).astype(BF16), kb, (((0,), (0,)), ((), ())),
                                                         preferred_element_type=F32)
            nst[s, h] = g_state * nrow + jnp.sum(g_in * k, axis=0, keepdims=True)
            mst[s, h] = jnp.broadcast_to(m_new, (1, N_GATE_PAD))

            y = _rms(hm, gout_ref[...]) * jax.nn.sigmoid(og_ref[s, :, sl])
            y_ref[s, :, sl] = y.astype(y_ref.dtype)

    @pl.when(c == pl.num_programs(1) - 1)
    def _():
        cn_ref[...] = cst[...]
        nn_ref[...] = nst[...]
        mn_ref[...] = mst[...]
        cvn_ref[...] = ubuf[:, L + 5:L + 8, :]


MLSTM_SEQS = 1


def _mlstm_prompt(z3, zg3, c0, n0, m0, conv0, w, *, rider=None):
    b, t, _ = z3.shape
    nc = t // CHUNK
    ns = MLSTM_SEQS
    u_blk, v_blk, o_blk = 2 * D_RNN // D_MLSTM, 3 * D_RNN // D_MLSTM, 4 * D_RNN // D_MLSTM
    zcol = lambda blk: pl.BlockSpec((ns, CHUNK, D_MLSTM), lambda bi, ci: (bi, ci, blk))
    state4 = lambda *minor: pl.BlockSpec((ns, MLSTM_HEADS) + minor, lambda bi, ci: (bi, 0, 0, 0))
    conv_spec = pl.BlockSpec((ns, CONV_W - 1, D_MLSTM), lambda bi, ci: (bi, 0, 0))
    const2 = lambda shape: pl.BlockSpec(shape, lambda bi, ci: (0, 0))
    r_in, r_out, r_shape, r_args = _rider_parts(rider)
    in_specs = [zcol(u_blk), zcol(v_blk), zcol(o_blk),
                pl.BlockSpec((ns, CHUNK, N_GATE_PAD), lambda bi, ci: (bi, ci, 0)),
                const2((1, N_GATE_PAD)),
                state4(MLSTM_HD, MLSTM_HD), state4(1, MLSTM_HD), state4(1, N_GATE_PAD), conv_spec,
                const2((CONV_W, D_MLSTM)), const2((1, D_MLSTM)),
                pl.BlockSpec((MLSTM_HEADS, MLSTM_HD, 2 * MLSTM_HD), lambda bi, ci: (0, 0, 0)),
                const2((1, MLSTM_HD))]
    out_specs = [pl.BlockSpec((ns, CHUNK, D_MLSTM), lambda bi, ci: (bi, ci, 0)),
                 state4(MLSTM_HD, MLSTM_HD), state4(1, MLSTM_HD), state4(1, N_GATE_PAD), conv_spec]
    out_shape = [jax.ShapeDtypeStruct((b, t, D_MLSTM), BF16),
                 jax.ShapeDtypeStruct((b, MLSTM_HEADS, MLSTM_HD, MLSTM_HD), F32),
                 jax.ShapeDtypeStruct((b, MLSTM_HEADS, 1, MLSTM_HD), F32),
                 jax.ShapeDtypeStruct((b, MLSTM_HEADS, 1, N_GATE_PAD), F32),
                 jax.ShapeDtypeStruct((b, CONV_W - 1, D_MLSTM), F32)]
    return pl.pallas_call(
        _with_rider(_mlstm_kernel, rider, len(in_specs), len(out_specs)),
        grid=(b // ns, nc),
        in_specs=in_specs + r_in, out_specs=out_specs + r_out, out_shape=out_shape + r_shape,
        scratch_shapes=[pltpu.VMEM((ns, CHUNK + 8, D_MLSTM), F32),
                        pltpu.VMEM((ns, MLSTM_HEADS, MLSTM_HD, MLSTM_HD), F32),
                        pltpu.VMEM((ns, MLSTM_HEADS, 1, MLSTM_HD), F32),
                        pltpu.VMEM((ns, MLSTM_HEADS, 1, N_GATE_PAD), F32)],
        compiler_params=_params("parallel", "arbitrary"),
        name="mlstm_prompt",
    )(z3, z3, z3, zg3, w['gate_bias'], c0, n0, m0, conv0, w['conv_ml_w'], w['conv_ml_b'],
      w['ml_wqk'], w['g_ml_out'], *r_args)


def _xattn_kernel(q_ref, k_ref, v_ref, x_ref, wo_ref, o_ref, att, rider=None):
    @pl.when(pl.program_id(2) == 0)
    def _():
        for h in range(X_HEADS):
            sl = slice(h * X_HD, (h + 1) * X_HD)
            s = _dot_t(q_ref[0, :, sl], k_ref[0, :, sl]) * (X_HD ** -0.5)
            e = jnp.exp(s - jnp.max(s, axis=-1, keepdims=True))
            p = e / jnp.sum(e, axis=-1, keepdims=True)
            att[:, sl] = _dot(p, v_ref[0, :, sl]).astype(att.dtype)

    _run_rider_share(rider, 0, 1)
    o_ref[0] = x_ref[0] + _dot(att[...], wo_ref[...])


def _xattn_prompt(q3, mk3, mv3, x3, w_co, *, tq, tn, rider=None):
    b, t, d = q3.shape
    mem = pl.BlockSpec((1, N_MEM, d), lambda bi, ti, j: (bi, 0, 0))
    cols = pl.BlockSpec((1, tq, tn), lambda bi, ti, j: (bi, ti, j))
    r_in, r_out, r_shape, r_args = _rider_parts(rider)
    in_specs = [pl.BlockSpec((1, tq, d), lambda bi, ti, j: (bi, ti, 0)), mem, mem, cols,
                pl.BlockSpec((d, tn), lambda bi, ti, j: (0, j))]
    return pl.pallas_call(
        _with_rider(_xattn_kernel, rider, len(in_specs), 1),
        grid=(b, t // tq, d // tn),
        in_specs=in_specs + r_in, out_specs=[cols] + r_out,
        out_shape=[jax.ShapeDtypeStruct((b, t, d), F32)] + r_shape,
        scratch_shapes=[pltpu.VMEM((tq, d), BF16)],
        compiler_params=_params("parallel", "parallel", "arbitrary"),
        name="xattn_prompt",
    )(q3, mk3, mv3, x3, w_co, *r_args)


XS_SUB = 2 * X_HEADS
XS_PAIRS = X_HD // (2 * 128)


def _pack_heads(x):
    lead = x.shape[:-1]
    x = x.reshape(*lead, X_HEADS, 2 * XS_PAIRS, 128)
    x = jnp.swapaxes(x, -3, -2)
    return x.reshape(*lead, XS_PAIRS, XS_SUB, 128)


def _unpack_heads(x):
    lead = x.shape[:-3]
    x = x.reshape(*lead, 2 * XS_PAIRS, X_HEADS, 128)
    x = jnp.swapaxes(x, -3, -2)
    return x.reshape(*lead, X_HEADS * X_HD)


def _xattn_sample_units(q_ref, k_ref, v_ref, o_ref, r):
    def row():
        t = jnp.sum(k_ref[r] * q_ref[r], axis=1)
        t = t + pltpu.roll(t, X_HEADS, 1)
        s = jnp.sum(t, axis=-1, keepdims=True) * (X_HD ** -0.5)
        e = jnp.exp(s - jnp.max(s, axis=0, keepdims=True))
        p = e / jnp.sum(e, axis=0, keepdims=True)
        o_ref[r] = jnp.sum(p[:, None] * v_ref[r], axis=0)

    return [row]


def _xattn_sample_rider(grid, q, ck, cv):
    b = q.shape[0]
    n_steps, step = _grid_steps(grid)
    rows = b // n_steps
    kv_spec = pl.BlockSpec((rows, N_MEM, XS_PAIRS, XS_SUB, 128), lambda *g: (step(*g), 0, 0, 0, 0))
    q_spec = pl.BlockSpec((rows, XS_PAIRS, XS_SUB, 128), lambda *g: (step(*g), 0, 0, 0))
    return dict(make_units=_per_row_units(_xattn_sample_units, rows),
                in_specs=[q_spec, kv_spec, kv_spec], out_specs=[q_spec],
                out_shape=[jax.ShapeDtypeStruct((b, XS_PAIRS, XS_SUB, 128), F32)],
                args=[_pack_heads(q), _pack_heads(ck.reshape(b, N_MEM, X_HEADS * X_HD)),
                      _pack_heads(cv.reshape(b, N_MEM, X_HEADS * X_HD))])


def _smix_rows_kernel(z_ref, h0_ref, rc_ref, mc_ref, cwr_ref, cbr_ref, wg_ref, bg_ref, lam_ref, gout_ref,
                      cwm_ref, cbm_ref, wqk_ref,
                      yr_ref, hn_ref, rcn_ref, mcn_ref, q_ref, k_ref, ybuf):
    nb = z_ref.shape[0]
    ssq = jnp.zeros((nb, 1), F32)
    for h in range(RNN_HEADS):
        sl = slice(h * RNN_HD, (h + 1) * RNN_HD)
        xc = cbr_ref[:, sl] + cwr_ref[CONV_W - 1:CONV_W, sl] * z_ref[:, sl]
        for j in range(CONV_W - 1):
            xc = xc + cwr_ref[j:j + 1, sl] * rc_ref[:, j * D_RNN + h * RNN_HD:j * D_RNN + (h + 1) * RNN_HD]
        a, u = _lru_gates(xc, wg_ref[h], bg_ref[h], lam_ref[:, sl])
        hs = a * h0_ref[:, sl] + u
        hn_ref[:, sl] = hs
        yv = hs * jax.nn.gelu(z_ref[:, D_RNN + h * RNN_HD:D_RNN + (h + 1) * RNN_HD])
        ssq = ssq + jnp.sum(yv * yv, axis=-1, keepdims=True)
        ybuf[:, sl] = yv
    yr_ref[...] = (ybuf[...] * lax.rsqrt(ssq * (1.0 / D_RNN) + EPS) * gout_ref[...]).astype(yr_ref.dtype)
    rcn_ref[:, 0:2 * D_RNN] = rc_ref[:, D_RNN:3 * D_RNN]
    rcn_ref[:, 2 * D_RNN:3 * D_RNN] = z_ref[:, 0:D_RNN]

    for h in range(MLSTM_HEADS):
        sl = slice(h * MLSTM_HD, (h + 1) * MLSTM_HD)
        uc = cbm_ref[:, sl] + cwm_ref[CONV_W - 1:CONV_W, sl] * z_ref[:, 2 * D_RNN + h * MLSTM_HD:2 * D_RNN + (h + 1) * MLSTM_HD]
        for j in range(CONV_W - 1):
            uc = uc + cwm_ref[j:j + 1, sl] * mc_ref[:, j * D_MLSTM + h * MLSTM_HD:j * D_MLSTM + (h + 1) * MLSTM_HD]
        uc = jax.nn.silu(uc)
        qk = _dot(uc.astype(BF16), wqk_ref[h])
        q_ref[:, sl] = qk[:, :MLSTM_HD]
        k_ref[:, sl] = qk[:, MLSTM_HD:] * (MLSTM_HD ** -0.5)
    mcn_ref[:, 0:2 * D_MLSTM] = mc_ref[:, D_MLSTM:3 * D_MLSTM]
    mcn_ref[:, 2 * D_MLSTM:3 * D_MLSTM] = z_ref[:, 2 * D_RNN:2 * D_RNN + D_MLSTM]


def _smix_rows(z, h0, rconv, mconv, w):
    nb = z.shape[0]
    outs = [jax.ShapeDtypeStruct((nb, D_RNN), BF16),
            jax.ShapeDtypeStruct((nb, D_RNN), F32),
            jax.ShapeDtypeStruct((nb, 3 * D_RNN), F32),
            jax.ShapeDtypeStruct((nb, 3 * D_MLSTM), F32),
            jax.ShapeDtypeStruct((nb, D_MLSTM), F32),
            jax.ShapeDtypeStruct((nb, D_MLSTM), F32)]
    return pl.pallas_call(
        _smix_rows_kernel,
        out_shape=outs,
        scratch_shapes=[pltpu.VMEM((nb, D_RNN), F32)],
        compiler_params=pltpu.CompilerParams(vmem_limit_bytes=V7X_VMEM_LIMIT_BYTES),
        name="smix_rows",
    )(z, h0, rconv, mconv, w['conv_rnn_w'], w['conv_rnn_b'], w['lru_wg'], w['lru_bg'], w['lru_lambda'],
      w['g_rnn_out'], w['conv_ml_w'], w['conv_ml_b'], w['ml_wqk'])


MXU_MIN_ROWS = 8


def _smix_state_units(c_ref, q_ref, k_ref, n_ref, v_ref, og_ref, zg_ref, gb_ref, m_ref, gout_ref,
                      cn_ref, nn_ref, mn_ref, y_ref, r):
    lane_g = lax.broadcasted_iota(jnp.int32, (1, N_GATE_PAD), 1)
    first_row = lax.broadcasted_iota(jnp.int32, (MXU_MIN_ROWS, MLSTM_HD), 0) == 0

    def head(h, zg, m_out):
        sl = slice(h * MLSTM_HD, (h + 1) * MLSTM_HD)
        q = q_ref[r, :, sl]
        k = k_ref[r, :, sl]
        n = n_ref[r, :, sl]
        v = v_ref[r, :, sl]
        ig = zg[:, h:h + 1]
        lf = jax.nn.log_sigmoid(zg[:, MLSTM_HEADS + h:MLSTM_HEADS + h + 1])
        m_prev = m_ref[r, :, h:h + 1]
        m_t = jnp.maximum(lf + m_prev, ig)
        w_inter = jnp.exp(lf + m_prev - m_t)
        g_in = jnp.exp(ig - m_t)
        s = jnp.sum(q * k, axis=1, keepdims=True) * g_in
        den = w_inter * jnp.sum(n * q, axis=1, keepdims=True) + s
        denom = jnp.maximum(jnp.abs(den), jnp.exp(-m_t))
        cmat = c_ref[r, h]
        q8 = jnp.broadcast_to(q, (MXU_MIN_ROWS, MLSTM_HD)).astype(BF16)
        cq = lax.dot_general(q8, cmat.astype(BF16), (((1,), (1,)), ((), ())),
                             preferred_element_type=F32)[0:1, :]
        hm = (w_inter * cq + s * v) / denom
        gv8 = jnp.where(first_row, jnp.broadcast_to(g_in * v, (MXU_MIN_ROWS, MLSTM_HD)), 0.0).astype(BF16)
        k8 = jnp.broadcast_to(k, (MXU_MIN_ROWS, MLSTM_HD)).astype(BF16)
        outer = lax.dot_general(gv8, k8, (((0,), (0,)), ((), ())), preferred_element_type=F32)
        cn_ref[r, h] = w_inter * cmat + outer
        nn_ref[r, :, sl] = w_inter * n + g_in * k
        y = _rms(hm, gout_ref[...]) * jax.nn.sigmoid(og_ref[r, :, sl])
        y_ref[r, :, sl] = y.astype(y_ref.dtype)
        return jnp.where(lane_g == h, m_t, m_out)

    def row():
        zg = zg_ref[r] + gb_ref[...]
        m_out = jnp.zeros((1, N_GATE_PAD), F32)
        for h in range(MLSTM_HEADS):
            m_out = head(h, zg, m_out)
        mn_ref[r] = m_out

    return [row]


def _smix_state_rider(grid, c, q3, k3, n3, z3, zg3, gate_bias, m3, gout):
    nb = c.shape[0]
    n_steps, step = _grid_steps(grid)
    rows = nb // n_steps
    row = lambda width, col=0: pl.BlockSpec((rows, 1, width), lambda *g: (step(*g), 0, col))
    cblk = pl.BlockSpec((rows, MLSTM_HEADS, MLSTM_HD, MLSTM_HD), lambda *g: (step(*g), 0, 0, 0))
    return dict(
        make_units=_per_row_units(_smix_state_units, rows),
        in_specs=[cblk, row(D_MLSTM), row(D_MLSTM), row(D_MLSTM),
                  row(D_MLSTM, 3 * D_RNN // D_MLSTM), row(D_MLSTM, 4 * D_RNN // D_MLSTM),
                  row(N_GATE_PAD),
                  pl.BlockSpec((1, N_GATE_PAD), lambda *g: (0, 0)),
                  row(MLSTM_HEADS),
                  pl.BlockSpec((1, MLSTM_HD), lambda *g: (0, 0))],
        out_specs=[cblk, row(D_MLSTM), row(N_GATE_PAD), row(D_MLSTM)],
        out_shape=[jax.ShapeDtypeStruct(c.shape, F32),
                   jax.ShapeDtypeStruct((nb, 1, D_MLSTM), F32),
                   jax.ShapeDtypeStruct((nb, 1, N_GATE_PAD), F32),
                   jax.ShapeDtypeStruct((nb, 1, D_MLSTM), F32)],
        args=[c, q3, k3, n3, z3, z3, zg3, gate_bias, m3, gout])


RGLRU_CHUNK = 256
PROMPT_TM = 1024


def _layer(xp, xs, mem, rg_h, rg_conv, c, n, mst, ml_conv, ck, cv, w, g_final, final_norm):
    b, t, d = xp.shape
    nb = xs.shape[0]
    m = b * t
    tm = PROMPT_TM
    xp2 = xp.reshape(m, d)
    xs2 = xs.reshape(nb, d)
    w_in_args = dict(tn=1024, out_dtype=F32, w_extra=w['w_in_gate'], n_cols=D_MAIN, w_is_transposed=True)

    zs, zgs, w_in_bf = _norm_matmul(xs2, w['g_mix'], w['w_in_main'], tm=nb, emit_w_bf16=True, **w_in_args)
    ys_rnn, s_h, s_rconv, s_mconv, qs, ks = _smix_rows(
        zs, rg_h, rg_conv.reshape(nb, 3 * D_RNN), ml_conv.reshape(nb, 3 * D_MLSTM), w)

    z, zg = _norm_matmul(xp2, w['g_mix'], w_in_bf, tm=tm, **w_in_args)
    z3 = z.reshape(b, t, D_MAIN)
    zg3 = zg.reshape(b, t, N_GATE_PAD)
    h0 = jnp.zeros((b, 1, D_RNN), F32)
    conv0 = jnp.zeros((b, CONV_W - 1, D_RNN), F32)
    rg_grid = (b, t // RGLRU_CHUNK)
    state_rider = _smix_state_rider(
        rg_grid, c, qs.reshape(nb, 1, D_MLSTM), ks.reshape(nb, 1, D_MLSTM), n.reshape(nb, 1, D_MLSTM),
        zs.reshape(nb, 1, D_MAIN), zgs.reshape(nb, 1, N_GATE_PAD), w['gate_bias'],
        mst.reshape(nb, 1, MLSTM_HEADS), w['g_ml_out'])
    rg_rider = _merge_riders(state_rider, _cast_rider(rg_grid, [[w['w_out'], w['w_cq']], [w['w_co']]]))
    y_rnn, p_h, p_rconv, s_c, s_n, s_m, ys_ml, w_out_cq, w_co = _rglru_prompt(
        z3, h0, conv0, w, tc=RGLRU_CHUNK, rider=rg_rider)

    xs1, qx = _matmul_res_norm_matmul([ys_rnn, ys_ml.reshape(nb, D_MLSTM).astype(BF16)], w_out_cq, xs2,
                                      w['g_xattn'], tm=nb, tn=1024, out2_dtype=F32)

    c0 = jnp.zeros((b, MLSTM_HEADS, MLSTM_HD, MLSTM_HD), F32)
    n0 = jnp.zeros((b, MLSTM_HEADS, 1, MLSTM_HD), F32)
    m0 = jnp.zeros((b, MLSTM_HEADS, 1, N_GATE_PAD), F32)
    mconv0 = jnp.zeros((b, CONV_W - 1, D_MLSTM), F32)
    y_ml, p_c, p_n, p_m, p_mconv, w_up, w_down = _mlstm_prompt(
        z3, zg3, c0, n0, m0, mconv0, w,
        rider=_cast_rider((b // MLSTM_SEQS, t // CHUNK), [[w['w_up']], [w['w_down']]]))

    x1, q = _matmul_res_norm_matmul([y_rnn.reshape(m, D_RNN), y_ml.reshape(m, D_MLSTM)], w_out_cq, xp2,
                                    w['g_xattn'], tm=tm, tn=1024, out2_dtype=BF16)
    mem2 = mem.reshape(b * N_MEM, d)
    mk = _norm_matmul(mem2, w['g_mem'], w['w_mk'], tm=b * N_MEM, tn=1024, out_dtype=F32)
    mv = _norm_matmul(mem2, w['g_mem'], w['w_mv'], tm=b * N_MEM, tn=1024, out_dtype=F32)
    x2, = _xattn_prompt(q.reshape(b, t, d), mk.reshape(b, N_MEM, d), mv.reshape(b, N_MEM, d), x1.reshape(b, t, d),
                        w_co, tq=tm, tn=1024)
    ffn_tf = 512
    xp_out, os_packed = _ffn(x2.reshape(m, d), w['g_ffn'], w_up, w_down, g_final, tm=tm, tf=ffn_tf,
                             final_norm=final_norm,
                             rider=_xattn_sample_rider((m // tm, D_FF // ffn_tf), qx, ck, cv))

    os_ = _unpack_heads(os_packed).astype(BF16)
    xs2_ = _matmul_res([os_], w_co, xs1, tm=nb, tn=1024)
    xs_out, = _ffn(xs2_, w['g_ffn'], w_up, w_down, g_final, tm=nb, tf=ffn_tf, final_norm=final_norm)

    new_p = (p_h.reshape(b, D_RNN), p_rconv, p_c, p_n.reshape(b, MLSTM_HEADS, MLSTM_HD), p_m[:, :, 0, 0], p_mconv,
             mk.reshape(b, N_MEM, X_HEADS, X_HD), mv.reshape(b, N_MEM, X_HEADS, X_HD))
    new_s = (s_h, s_rconv.reshape(nb, CONV_W - 1, D_RNN), s_c, s_n.reshape(nb, MLSTM_HEADS, MLSTM_HD),
             s_m[:, 0, :MLSTM_HEADS], s_mconv.reshape(nb, CONV_W - 1, D_MLSTM))
    return xp_out.reshape(b, t, d), xs_out.reshape(nb, 1, d), new_p, new_s


def _layer_weights(l, g_mix, w_in, conv_rnn_w, conv_rnn_b, lru_wa, lru_ba, lru_wx, lru_bx, lru_lambda,
                   g_rnn_out, conv_ml_w, conv_ml_b, ml_wq, ml_wk, ml_bi, ml_bf, g_ml_out, w_out,
                   g_xattn, g_mem, w_cq, w_mk, w_mv, w_co, g_ffn, w_up, w_down):
    n_gate = 2 * MLSTM_HEADS
    w_in_t = jnp.swapaxes(w_in[l], 0, 1)
    w_gate_t = jnp.pad(w_in_t[D_MAIN:], ((0, N_GATE_PAD - n_gate), (0, 0)))
    gate_bias = jnp.pad(jnp.concatenate([ml_bi[l], ml_bf[l]]), (0, N_GATE_PAD - n_gate))
    return dict(
        g_mix=g_mix[l], w_in_main=w_in_t, w_in_gate=w_gate_t.astype(BF16),
        conv_rnn_w=conv_rnn_w[l], conv_rnn_b=conv_rnn_b[l].reshape(1, D_RNN),
        lru_wg=jnp.concatenate([lru_wa[l], lru_wx[l]], axis=-1).astype(BF16),
        lru_bg=jnp.concatenate([lru_ba[l], lru_bx[l]], axis=-1).reshape(RNN_HEADS, 1, 2 * RNN_HD),
        lru_lambda=lru_lambda[l].reshape(1, D_RNN), g_rnn_out=g_rnn_out[l].reshape(1, D_RNN),
        conv_ml_w=conv_ml_w[l], conv_ml_b=conv_ml_b[l].reshape(1, D_MLSTM),
        ml_wqk=jnp.concatenate([ml_wq[l], ml_wk[l]], axis=-1).astype(BF16),
        gate_bias=gate_bias.reshape(1, N_GATE_PAD), g_ml_out=g_ml_out[l].reshape(1, MLSTM_HD),
        w_out=w_out[l], w_cq=w_cq[l], g_xattn=g_xattn[l], g_mem=g_mem[l], w_mk=w_mk[l], w_mv=w_mv[l],
        w_co=w_co[l], g_ffn=g_ffn[l], w_up=w_up[l], w_down=w_down[l])


def kernel(x_prompt, x_sample, mem_prompt, state_rglru_h, state_rglru_conv, state_mlstm_C, state_mlstm_n, state_mlstm_m, state_mlstm_conv, cache_mem_k, cache_mem_v, g_mix, w_in, conv_rnn_w, conv_rnn_b, lru_wa, lru_ba, lru_wx, lru_bx, lru_lambda, g_rnn_out, conv_ml_w, conv_ml_b, ml_wq, ml_wk, ml_bi, ml_bf, g_ml_out, w_out, g_xattn, g_mem, w_cq, w_mk, w_mv, w_co, g_ffn, w_up, w_down, g_final):
    depth = w_in.shape[0]
    xp, xs = x_prompt, x_sample
    p_out = [[] for _ in range(8)]
    s_out = [[] for _ in range(6)]
    for l in range(depth):
        w = _layer_weights(l, g_mix, w_in, conv_rnn_w, conv_rnn_b, lru_wa, lru_ba, lru_wx, lru_bx, lru_lambda,
                           g_rnn_out, conv_ml_w, conv_ml_b, ml_wq, ml_wk, ml_bi, ml_bf, g_ml_out, w_out,
                           g_xattn, g_mem, w_cq, w_mk, w_mv, w_co, g_ffn, w_up, w_down)
        last = l == depth - 1
        xp, xs, new_p, new_s = _layer(xp, xs, mem_prompt, state_rglru_h[l], state_rglru_conv[l], state_mlstm_C[l],
                                      state_mlstm_n[l], state_mlstm_m[l], state_mlstm_conv[l], cache_mem_k[l],
                                      cache_mem_v[l], w, g_final, last)
        for j, a in enumerate(new_p):
            p_out[j].append(a)
        for j, a in enumerate(new_s):
            s_out[j].append(a)
    P = [jnp.stack(a, axis=0) for a in p_out]
    S = [jnp.stack(a, axis=0) for a in s_out]
    return (xp, xs, P[0], P[1], P[2], P[3], P[4], P[5], P[6], P[7],
            S[0], S[1], S[2], S[3], S[4], S[5])
```

```python
import functools

import jax
import jax.numpy as jnp
from jax import lax
from jax.experimental import pallas as pl
from jax.experimental.pallas import tpu as pltpu

F32 = jnp.float32
BF16 = jnp.bfloat16

D_MODEL = 2048
D_RNN = 1024
RNN_HEADS = 8
RNN_HD = 128
CONV_W = 4
LRU_C = 8.0
D_MLSTM = 1024
MLSTM_HEADS = 4
MLSTM_HD = 256
CHUNK = 128
N_MEM = 256
X_HEADS = 4
X_HD = 512
D_FF = 8192
EPS = 1e-6
D_MAIN = 5 * 1024
N_GATE_PAD = 128

V7X_VMEM_LIMIT_BYTES = 56 * 1024 * 1024
V7X_VMEM_LIMIT_FFN_BYTES = 63 * 1024 * 1024
SUBLANES = 8


def _params(*sem, vmem_limit_bytes=V7X_VMEM_LIMIT_BYTES):
    return pltpu.CompilerParams(dimension_semantics=sem, vmem_limit_bytes=vmem_limit_bytes)


def _rms(x, g):
    ms = jnp.mean(x * x, axis=-1, keepdims=True)
    return x * lax.rsqrt(ms + EPS) * g


def _dot(a, b):
    return jnp.dot(a.astype(BF16), b.astype(BF16), preferred_element_type=F32)


def _dot_t(a, bt):
    return lax.dot_general(a.astype(BF16), bt.astype(BF16), (((1,), (1,)), ((), ())), preferred_element_type=F32)


def _with_rider(host_body, rider, n_in, n_out):
    if rider is None:
        return host_body
    r_in, r_out = len(rider['in_specs']), len(rider['out_specs'])

    def body(*refs):
        host_in, refs = refs[:n_in], refs[n_in:]
        rider_in, refs = refs[:r_in], refs[r_in:]
        host_out, refs = refs[:n_out], refs[n_out:]
        rider_out, scratch = refs[:r_out], refs[r_out:]
        host_body(*host_in, *host_out, *scratch, rider=rider['make_units'](*rider_in, *rider_out))
    return body


def _grid_steps(grid):
    n = 1
    for g in grid:
        n *= g

    def step(*idx):
        s = idx[0]
        for g, i in zip(grid[1:], idx[1:]):
            s = s * g + i
        return s
    return n, step


def _per_row_units(row_units, rows):
    return lambda *refs: [u for r in range(rows) for u in row_units(*refs, r)]


def _merge_riders(a, b):
    na_in, na_out = len(a['in_specs']), len(a['out_specs'])
    nb_in = len(b['in_specs'])

    def make_units(*refs):
        a_in, refs = refs[:na_in], refs[na_in:]
        b_in, refs = refs[:nb_in], refs[nb_in:]
        a_out, b_out = refs[:na_out], refs[na_out:]
        return a['make_units'](*a_in, *a_out) + b['make_units'](*b_in, *b_out)

    return dict(make_units=make_units, **{k: a[k] + b[k] for k in ('in_specs', 'out_specs', 'out_shape', 'args')})


def _cast_rider(grid, groups):
    n_steps, step = _grid_steps(grid)
    in_specs, out_specs, out_shape, args, widths = [], [], [], [], []
    for group in groups:
        rps = group[0].shape[0] // n_steps
        for a in group:
            in_specs.append(pl.BlockSpec((rps, a.shape[1]), lambda *g: (step(*g), 0)))
            args.append(a)
        cols = sum(a.shape[1] for a in group)
        out_specs.append(pl.BlockSpec((rps, cols), lambda *g: (step(*g), 0)))
        out_shape.append(jax.ShapeDtypeStruct((group[0].shape[0], cols), BF16))
        widths.append([a.shape[1] for a in group])

    def make_units(*refs):
        in_refs, out_refs = list(refs[:len(args)]), refs[len(args):]
        units = []
        for o_ref, ws in zip(out_refs, widths):
            srcs, in_refs = in_refs[:len(ws)], in_refs[len(ws):]

            def cast(o_ref=o_ref, srcs=srcs, ws=ws):
                c0 = 0
                for s_ref, wd in zip(srcs, ws):
                    o_ref[:, c0:c0 + wd] = s_ref[...].astype(BF16)
                    c0 += wd
            units.append(cast)
        return units

    return dict(make_units=make_units, in_specs=in_specs, out_specs=out_specs, out_shape=out_shape, args=args)


def _run_rider_share(rider, i, n):
    if rider is None:
        return
    for unit in rider[i * len(rider) // n:(i + 1) * len(rider) // n]:
        unit()


def _rider_parts(rider):
    if rider is None:
        return [], [], [], []
    return rider['in_specs'], rider['out_specs'], rider['out_shape'], rider['args']


def _norm_matmul_kernel(*refs, has_extra, w_is_transposed, emit_w_bf16):
    refs = list(refs)
    xn_ref = refs.pop()
    wb_ref = refs.pop() if emit_w_bf16 else None
    if has_extra:
        x_ref, g_ref, w_ref, we_ref, o_ref, oe_ref = refs
    else:
        x_ref, g_ref, w_ref, o_ref = refs
    dot = _dot_t if w_is_transposed else _dot

    @pl.when(pl.program_id(1) == 0)
    def _():
        xn = _rms(x_ref[...], g_ref[...]).astype(BF16)
        xn_ref[...] = xn
        if has_extra:
            oe_ref[...] = dot(xn, we_ref[...])

    wb = w_ref[...].astype(BF16)
    if emit_w_bf16:
        wb_ref[...] = wb
    o_ref[...] = dot(xn_ref[...], wb).astype(o_ref.dtype)


def _norm_matmul(x, g, w, *, tm, tn, out_dtype, w_extra=None, n_cols=None, w_is_transposed=False,
                 emit_w_bf16=False):
    m, k = x.shape
    n = n_cols or w.shape[0 if w_is_transposed else 1]
    has_extra = w_extra is not None
    w_spec = (pl.BlockSpec((tn, k), lambda i, j: (j, 0)) if w_is_transposed
              else pl.BlockSpec((k, tn), lambda i, j: (0, j)))
    in_specs = [pl.BlockSpec((tm, k), lambda i, j: (i, 0)),
                pl.BlockSpec((1, k), lambda i, j: (0, 0)),
                w_spec]
    out_specs = [pl.BlockSpec((tm, tn), lambda i, j: (i, j))]
    out_shape = [jax.ShapeDtypeStruct((m, n), out_dtype)]
    args = [x, g.reshape(1, k), w]
    if has_extra:
        ne = w_extra.shape[0 if w_is_transposed else 1]
        in_specs.append(pl.BlockSpec(w_extra.shape, lambda i, j: (0, 0)))
        out_specs.append(pl.BlockSpec((tm, ne), lambda i, j: (i, 0)))
        out_shape.append(jax.ShapeDtypeStruct((m, ne), F32))
        args.append(w_extra)
    if emit_w_bf16:
        assert m == tm
        out_specs.append(w_spec)
        out_shape.append(jax.ShapeDtypeStruct((n, k) if w_is_transposed else (k, n), BF16))
    res = pl.pallas_call(
        functools.partial(_norm_matmul_kernel, has_extra=has_extra, w_is_transposed=w_is_transposed,
                          emit_w_bf16=emit_w_bf16),
        grid=(m // tm, n // tn),
        in_specs=in_specs, out_specs=out_specs, out_shape=out_shape,
        scratch_shapes=[pltpu.VMEM((tm, k), BF16)],
        compiler_params=_params("parallel", "arbitrary"),
        name="norm_matmul",
    )(*args)
    return res if len(res) > 1 else res[0]


def _matmul_res_kernel(*refs, n_parts):
    a_refs, w_refs = refs[:n_parts], refs[n_parts:2 * n_parts]
    r_ref, o_ref = refs[2 * n_parts:]
    acc = r_ref[...]
    for a_ref, w_ref in zip(a_refs, w_refs):
        acc = acc + _dot(a_ref[...], w_ref[...])
    o_ref[...] = acc


def _matmul_res(a_parts, w, res, *, tm, tn):
    n_parts = len(a_parts)
    m, kp = a_parts[0].shape
    n = w.shape[1]
    a_specs = [pl.BlockSpec((tm, kp), lambda i, j: (i, 0)) for _ in a_parts]
    w_specs = [pl.BlockSpec((kp, tn), lambda i, j, p=p: (p, j)) for p in range(n_parts)]
    return pl.pallas_call(
        functools.partial(_matmul_res_kernel, n_parts=n_parts),
        grid=(m // tm, n // tn),
        in_specs=a_specs + w_specs + [pl.BlockSpec((tm, tn), lambda i, j: (i, j))],
        out_specs=pl.BlockSpec((tm, tn), lambda i, j: (i, j)),
        out_shape=jax.ShapeDtypeStruct((m, n), F32),
        compiler_params=_params("parallel", "arbitrary"),
        name="matmul_res",
    )(*a_parts, *([w] * n_parts), res)


def _matmul_res_norm_matmul_kernel(*refs, n_parts, n1):
    a_refs = refs[:n_parts]
    w_ref, r_ref, g_ref, x1_ref, q_ref, x1_s, xn_s = refs[n_parts:]
    j = pl.program_id(1)
    tn = w_ref.shape[1]
    kp = a_refs[0].shape[1]

    @pl.when(j < n1)
    def _():
        x1 = r_ref[...]
        for p, a_ref in enumerate(a_refs):
            x1 = x1 + _dot(a_ref[...], w_ref[p * kp:(p + 1) * kp, :])
        x1_ref[...] = x1
        x1_s[j] = x1

    @pl.when(j == n1)
    def _():
        ssq = sum(jnp.sum(jnp.square(x1_s[c]), axis=-1, keepdims=True) for c in range(n1))
        scale = lax.rsqrt(ssq * (1.0 / (n1 * tn)) + EPS)
        for c in range(n1):
            xn_s[:, c * tn:(c + 1) * tn] = (x1_s[c] * scale * g_ref[:, c * tn:(c + 1) * tn]).astype(xn_s.dtype)

    @pl.when(j >= n1)
    def _():
        q_ref[...] = _dot(xn_s[...], w_ref[...]).astype(q_ref.dtype)


def _matmul_res_norm_matmul(a_parts, w12, res, g, *, tm, tn, out2_dtype):
    n_parts = len(a_parts)
    m, kp = a_parts[0].shape
    d = res.shape[1]
    n1 = d // tn
    n2 = (w12.shape[1] - d) // tn
    return pl.pallas_call(
        functools.partial(_matmul_res_norm_matmul_kernel, n_parts=n_parts, n1=n1),
        grid=(m // tm, n1 + n2),
        in_specs=[pl.BlockSpec((tm, kp), lambda i, j: (i, 0))] * n_parts + [
            pl.BlockSpec((n_parts * kp, tn), lambda i, j: (0, j)),
            pl.BlockSpec((tm, tn), lambda i, j: (i, jnp.minimum(j, n1 - 1))),
            pl.BlockSpec((1, d), lambda i, j: (0, 0))],
        out_specs=[pl.BlockSpec((tm, tn), lambda i, j: (i, jnp.minimum(j, n1 - 1))),
                   pl.BlockSpec((tm, tn), lambda i, j: (i, jnp.maximum(j - n1, 0)))],
        out_shape=[jax.ShapeDtypeStruct((m, d), F32), jax.ShapeDtypeStruct((m, n2 * tn), out2_dtype)],
        scratch_shapes=[pltpu.VMEM((n1, tm, tn), F32), pltpu.VMEM((tm, d), BF16)],
        compiler_params=_params("parallel", "arbitrary"),
        name="matmul_res_norm_matmul",
    )(*a_parts, w12, res, g.reshape(1, d))


def _ffn_kernel(x_ref, g_ref, wu_ref, wd_ref, gf_ref, o_ref, xn_ref, *, final_norm, rider=None):
    f = pl.program_id(1)

    @pl.when(f == 0)
    def _():
        xn_ref[...] = _rms(x_ref[...], g_ref[...]).astype(BF16)
        o_ref[...] = jnp.zeros_like(o_ref)

    _run_rider_share(rider, 0, 1)
    h = _dot(xn_ref[...], wu_ref[...])
    h = jnp.square(jnp.maximum(h, 0.0)).astype(BF16)
    o_ref[...] += _dot(h, wd_ref[...])

    @pl.when(f == pl.num_programs(1) - 1)
    def _():
        y = x_ref[...] + o_ref[...]
        if final_norm:
            y = _rms(y, gf_ref[...])
        o_ref[...] = y


def _ffn(x, g, w_up, w_down, g_final, *, tm, tf, final_norm, rider=None):
    m, d = x.shape
    dff = w_up.shape[1]
    r_in, r_out, r_shape, r_args = _rider_parts(rider)
    in_specs = [pl.BlockSpec((tm, d), lambda i, f: (i, 0)),
                pl.BlockSpec((1, d), lambda i, f: (0, 0)),
                pl.BlockSpec((d, tf), lambda i, f: (0, f)),
                pl.BlockSpec((tf, d), lambda i, f: (f, 0)),
                pl.BlockSpec((1, d), lambda i, f: (0, 0))]
    out_specs = [pl.BlockSpec((tm, d), lambda i, f: (i, 0))]
    return pl.pallas_call(
        _with_rider(functools.partial(_ffn_kernel, final_norm=final_norm), rider, len(in_specs), len(out_specs)),
        grid=(m // tm, dff // tf),
        in_specs=in_specs + r_in, out_specs=out_specs + r_out,
        out_shape=[jax.ShapeDtypeStruct((m, d), F32)] + r_shape,
        scratch_shapes=[pltpu.VMEM((tm, d), BF16)],
        compiler_params=_params("parallel", "arbitrary", vmem_limit_bytes=V7X_VMEM_LIMIT_FFN_BYTES),
        name="ffn",
    )(x, g.reshape(1, d), w_up, w_down, g_final.reshape(1, d), *r_args)


def _lru_gates(xc, wg, bg, lam):
    g = _dot(xc.astype(BF16), wg) + bg
    r = jax.nn.sigmoid(g[:, :RNN_HD])
    i = jax.nn.sigmoid(g[:, RNN_HD:])
    log_a = -LRU_C * r * jax.nn.softplus(-lam)
    a = jnp.exp(log_a)
    u = jnp.sqrt(-jnp.tanh(log_a) * (a * a + 1.0)) * (i * xc)
    return a, u


def _rglru_kernel(xr_ref, gr_ref, h0_ref, c0_ref, cw_ref, cb_ref, wg_ref, bg_ref, lam_ref, gout_ref,
                  y_ref, hn_ref, cn_ref, xbuf, hc, ybuf, *, tc, rider=None):
    c = pl.program_id(1)

    @pl.when(c == 0)
    def _():
        xbuf[0:8, :] = jnp.zeros((8, D_RNN), F32)
        xbuf[5:8, :] = c0_ref[0]
        hc[...] = h0_ref[0]

    @pl.when(c > 0)
    def _():
        xbuf[0:8, :] = xbuf[tc:tc + 8, :]

    xbuf[8:8 + tc, :] = xr_ref[0]
    ng = tc // SUBLANES
    sub = lax.broadcasted_iota(jnp.int32, (ng, SUBLANES, RNN_HD), 1)
    ssq = jnp.zeros((tc, 1), F32)
    for h in range(RNN_HEADS):
        _run_rider_share(rider, h, RNN_HEADS)
        sl = slice(h * RNN_HD, (h + 1) * RNN_HD)
        xc = cb_ref[:, sl]
        for j in range(CONV_W):
            xc = xc + cw_ref[j:j + 1, sl] * xbuf[5 + j:5 + j + tc, sl]
        a, u = _lru_gates(xc, wg_ref[h], bg_ref[h], lam_ref[:, sl])
        a = a.reshape(ng, SUBLANES, RNN_HD)
        u = u.reshape(ng, SUBLANES, RNN_HD)
        d = 1
        while d < SUBLANES:
            keep = sub >= d
            a_prev = jnp.where(keep, pltpu.roll(a, d, 1), 1.0)
            u_prev = jnp.where(keep, pltpu.roll(u, d, 1), 0.0)
            u = u + a * u_prev
            a = a * a_prev
            d *= 2
        carry = hc[:, sl]
        for g in range(ng):
            hg = u[g] + a[g] * carry
            carry = hg[SUBLANES - 1:SUBLANES, :]
            ybuf[g * SUBLANES:(g + 1) * SUBLANES, sl] = hg
        hc[:, sl] = carry
        yv = ybuf[:, sl] * jax.nn.gelu(gr_ref[0, :, sl])
        ssq = ssq + jnp.sum(yv * yv, axis=-1, keepdims=True)
        ybuf[:, sl] = yv
    y = ybuf[...] * lax.rsqrt(ssq * (1.0 / D_RNN) + EPS) * gout_ref[...]
    y_ref[0] = y.astype(y_ref.dtype)

    @pl.when(c == pl.num_programs(1) - 1)
    def _():
        hn_ref[0] = hc[...]
        cn_ref[0] = xbuf[tc + 5:tc + 8, :]


def _rglru_prompt(z3, h0, conv0, w, *, tc, rider=None):
    b, t, _ = z3.shape
    full2 = lambda bi, ci: (0, 0)
    r_in, r_out, r_shape, r_args = _rider_parts(rider)
    in_specs = [pl.BlockSpec((1, tc, D_RNN), lambda bi, ci: (bi, ci, 0)),
                pl.BlockSpec((1, tc, D_RNN), lambda bi, ci: (bi, ci, 1)),
                pl.BlockSpec((1, 1, D_RNN), lambda bi, ci: (bi, 0, 0)),
                pl.BlockSpec((1, CONV_W - 1, D_RNN), lambda bi, ci: (bi, 0, 0)),
                pl.BlockSpec((CONV_W, D_RNN), full2),
                pl.BlockSpec((1, D_RNN), full2),
                pl.BlockSpec((RNN_HEADS, RNN_HD, 2 * RNN_HD), lambda bi, ci: (0, 0, 0)),
                pl.BlockSpec((RNN_HEADS, 1, 2 * RNN_HD), lambda bi, ci: (0, 0, 0)),
                pl.BlockSpec((1, D_RNN), full2),
                pl.BlockSpec((1, D_RNN), full2)]
    out_specs = [pl.BlockSpec((1, tc, D_RNN), lambda bi, ci: (bi, ci, 0)),
                 pl.BlockSpec((1, 1, D_RNN), lambda bi, ci: (bi, 0, 0)),
                 pl.BlockSpec((1, CONV_W - 1, D_RNN), lambda bi, ci: (bi, 0, 0))]
    out_shape = [jax.ShapeDtypeStruct((b, t, D_RNN), BF16),
                 jax.ShapeDtypeStruct((b, 1, D_RNN), F32),
                 jax.ShapeDtypeStruct((b, CONV_W - 1, D_RNN), F32)]
    return pl.pallas_call(
        _with_rider(functools.partial(_rglru_kernel, tc=tc), rider, len(in_specs), len(out_specs)),
        grid=(b, t // tc),
        in_specs=in_specs + r_in, out_specs=out_specs + r_out, out_shape=out_shape + r_shape,
        scratch_shapes=[pltpu.VMEM((tc + 8, D_RNN), F32),
                        pltpu.VMEM((1, D_RNN), F32),
                        pltpu.VMEM((tc, D_RNN), F32)],
        compiler_params=_params("parallel", "arbitrary"),
        name="rglru_prompt",
    )(z3, z3, h0, conv0, w['conv_rnn_w'], w['conv_rnn_b'], w['lru_wg'], w['lru_bg'],
      w['lru_lambda'], w['g_rnn_out'], *r_args)


def _mlstm_kernel(u_ref, v_ref, og_ref, zg_ref, gb_ref, c0_ref, n0_ref, m0_ref, cv0_ref,
                  cw_ref, cb_ref, wqk_ref, gout_ref,
                  y_ref, cn_ref, nn_ref, mn_ref, cvn_ref, ubuf, cst, nst, mst, rider=None):
    c = pl.program_id(1)
    L = CHUNK
    span = u_ref.shape[1]

    @pl.when(c == 0)
    def _():
        ubuf[0:8, :] = jnp.zeros((8, D_MLSTM), F32)
        ubuf[5:8, :] = cv0_ref[0]
        cst[...] = c0_ref[0]
        nst[...] = n0_ref[0]
        mst[...] = m0_ref[0]

    @pl.when(c > 0)
    def _():
        ubuf[0:8, :] = ubuf[span:span + 8, :]

    ubuf[8:8 + span, :] = u_ref[0]

    ti = lax.broadcasted_iota(jnp.int32, (L, L), 0)
    si = lax.broadcasted_iota(jnp.int32, (L, L), 1)
    causal = si <= ti
    anti = ti <= si
    n_sub = span // L

    for sub in range(n_sub):
        t0 = sub * L
        zg = zg_ref[0, t0:t0 + L, :] + gb_ref[...]
        zgt = zg.T
        for h in range(MLSTM_HEADS):
            _run_rider_share(rider, sub * MLSTM_HEADS + h, n_sub * MLSTM_HEADS)
            sl = slice(h * MLSTM_HD, (h + 1) * MLSTM_HD)
            uc = cb_ref[:, sl]
            for j in range(CONV_W):
                uc = uc + cw_ref[j:j + 1, sl] * ubuf[t0 + 5 + j:t0 + 5 + j + L, sl]
            uc = jax.nn.silu(uc)
            qk = _dot(uc.astype(BF16), wqk_ref[h])
            q = qk[:, :MLSTM_HD]
            k = qk[:, MLSTM_HD:] * (MLSTM_HD ** -0.5)
            v = v_ref[0, t0:t0 + L, sl]
            qb, kb = q.astype(BF16), k.astype(BF16)

            icol = zg[:, h:h + 1]
            fcol = jax.nn.log_sigmoid(zg[:, MLSTM_HEADS + h:MLSTM_HEADS + h + 1])
            irow = zgt[h:h + 1, :]
            frow = jax.nn.log_sigmoid(zgt[MLSTM_HEADS + h:MLSTM_HEADS + h + 1, :])
            bcol = jnp.sum(jnp.where(causal, frow, 0.0), axis=1, keepdims=True)
            brow = jnp.sum(jnp.where(anti, fcol, 0.0), axis=0, keepdims=True)
            dmat = jnp.where(causal, irow + bcol - brow, -jnp.inf)
            m_prev = mst[h, :, 0:1]
            inter = bcol + m_prev
            m_t = jnp.maximum(inter, jnp.max(dmat, axis=1, keepdims=True))
            w_inter = jnp.exp(inter - m_t)
            sc = lax.dot_general(qb, kb, (((1,), (1,)), ((), ())), preferred_element_type=F32) * jnp.exp(dmat - m_t)
            cmat = cst[h]
            nrow = nst[h]
            cq = lax.dot_general(qb, cmat.astype(BF16), (((1,), (1,)), ((), ())), preferred_element_type=F32)
            num = w_inter * cq + _dot(sc.astype(BF16), v.astype(BF16))
            den = w_inter * jnp.sum(q * nrow, axis=1, keepdims=True) + jnp.sum(sc, axis=1, keepdims=True)
            hm = num / jnp.maximum(jnp.abs(den), jnp.exp(-m_t))

            m_new = m_t[L - 1:L, :]
            b_last = bcol[L - 1:L, :]
            g_state = jnp.exp(b_last + m_prev - m_new)
            g_in = jnp.exp(icol + b_last - bcol - m_new)
            cst[h] = g_state * cmat + lax.dot_general((g_in * v).astype(BF16), kb, (((0,), (0,)), ((), ())),
                                                      preferred_element_type=F32)
            nst[h] = g_state * nrow + jnp.sum(g_in * k, axis=0, keepdims=True)
            mst[h] = jnp.broadcast_to(m_new, (1, N_GATE_PAD))

            y = _rms(hm, gout_ref[...]) * jax.nn.sigmoid(og_ref[0, t0:t0 + L, sl])
            y_ref[0, t0:t0 + L, sl] = y.astype(y_ref.dtype)

    @pl.when(c == pl.num_programs(1) - 1)
    def _():
        cn_ref[0] = cst[...]
        nn_ref[0] = nst[...]
        mn_ref[0] = mst[...]
        cvn_ref[0] = ubuf[span + 5:span + 8, :]


MLSTM_CHUNKS_PER_STEP = 2


def _mlstm_prompt(z3, zg3, c0, n0, m0, conv0, w, *, rider=None):
    b, t, _ = z3.shape
    span = CHUNK * MLSTM_CHUNKS_PER_STEP
    u_blk, v_blk, o_blk = 2 * D_RNN // D_MLSTM, 3 * D_RNN // D_MLSTM, 4 * D_RNN // D_MLSTM
    zcol = lambda blk: pl.BlockSpec((1, span, D_MLSTM), lambda bi, ci: (bi, ci, blk))
    state4 = lambda *minor: pl.BlockSpec((1, MLSTM_HEADS) + minor, lambda bi, ci: (bi, 0, 0, 0))
    conv_spec = pl.BlockSpec((1, CONV_W - 1, D_MLSTM), lambda bi, ci: (bi, 0, 0))
    const2 = lambda shape: pl.BlockSpec(shape, lambda bi, ci: (0, 0))
    r_in, r_out, r_shape, r_args = _rider_parts(rider)
    in_specs = [zcol(u_blk), zcol(v_blk), zcol(o_blk),
                pl.BlockSpec((1, span, N_GATE_PAD), lambda bi, ci: (bi, ci, 0)),
                const2((1, N_GATE_PAD)),
                state4(MLSTM_HD, MLSTM_HD), state4(1, MLSTM_HD), state4(1, N_GATE_PAD), conv_spec,
                const2((CONV_W, D_MLSTM)), const2((1, D_MLSTM)),
                pl.BlockSpec((MLSTM_HEADS, MLSTM_HD, 2 * MLSTM_HD), lambda bi, ci: (0, 0, 0)),
                const2((1, MLSTM_HD))]
    out_specs = [pl.BlockSpec((1, span, D_MLSTM), lambda bi, ci: (bi, ci, 0)),
                 state4(MLSTM_HD, MLSTM_HD), state4(1, MLSTM_HD), state4(1, N_GATE_PAD), conv_spec]
    out_shape = [jax.ShapeDtypeStruct((b, t, D_MLSTM), BF16),
                 jax.ShapeDtypeStruct((b, MLSTM_HEADS, MLSTM_HD, MLSTM_HD), F32),
                 jax.ShapeDtypeStruct((b, MLSTM_HEADS, 1, MLSTM_HD), F32),
                 jax.ShapeDtypeStruct((b, MLSTM_HEADS, 1, N_GATE_PAD), F32),
                 jax.ShapeDtypeStruct((b, CONV_W - 1, D_MLSTM), F32)]
    return pl.pallas_call(
        _with_rider(_mlstm_kernel, rider, len(in_specs), len(out_specs)),
        grid=(b, t // span),
        in_specs=in_specs + r_in, out_specs=out_specs + r_out, out_shape=out_shape + r_shape,
        scratch_shapes=[pltpu.VMEM((span + 8, D_MLSTM), F32),
                        pltpu.VMEM((MLSTM_HEADS, MLSTM_HD, MLSTM_HD), F32),
                        pltpu.VMEM((MLSTM_HEADS, 1, MLSTM_HD), F32),
                        pltpu.VMEM((MLSTM_HEADS, 1, N_GATE_PAD), F32)],
        compiler_params=_params("parallel", "arbitrary"),
        name="mlstm_prompt",
    )(z3, z3, z3, zg3, w['gate_bias'], c0, n0, m0, conv0, w['conv_ml_w'], w['conv_ml_b'],
      w['ml_wqk'], w['g_ml_out'], *r_args)


def _xattn_kernel(q_ref, k_ref, v_ref, x_ref, wo_ref, o_ref, att, rider=None):
    @pl.when(pl.program_id(2) == 0)
    def _():
        for h in range(X_HEADS):
            sl = slice(h * X_HD, (h + 1) * X_HD)
            s = _dot_t(q_ref[0, :, sl], k_ref[0, :, sl]) * (X_HD ** -0.5)
            e = jnp.exp(s - jnp.max(s, axis=-1, keepdims=True))
            p = e / jnp.sum(e, axis=-1, keepdims=True)
            att[:, sl] = _dot(p, v_ref[0, :, sl]).astype(att.dtype)

    _run_rider_share(rider, 0, 1)
    o_ref[0] = x_ref[0] + _dot(att[...], wo_ref[...])


def _xattn_prompt(q3, mk3, mv3, x3, w_co, *, tq, tn, rider=None):
    b, t, d = q3.shape
    mem = pl.BlockSpec((1, N_MEM, d), lambda bi, ti, j: (bi, 0, 0))
    cols = pl.BlockSpec((1, tq, tn), lambda bi, ti, j: (bi, ti, j))
    r_in, r_out, r_shape, r_args = _rider_parts(rider)
    in_specs = [pl.BlockSpec((1, tq, d), lambda bi, ti, j: (bi, ti, 0)), mem, mem, cols,
                pl.BlockSpec((d, tn), lambda bi, ti, j: (0, j))]
    return pl.pallas_call(
        _with_rider(_xattn_kernel, rider, len(in_specs), 1),
        grid=(b, t // tq, d // tn),
        in_specs=in_specs + r_in, out_specs=[cols] + r_out,
        out_shape=[jax.ShapeDtypeStruct((b, t, d), F32)] + r_shape,
        scratch_shapes=[pltpu.VMEM((tq, d), BF16)],
        compiler_params=_params("parallel", "parallel", "arbitrary"),
        name="xattn_prompt",
    )(q3, mk3, mv3, x3, w_co, *r_args)


XS_SUB = 2 * X_HEADS
XS_PAIRS = X_HD // (2 * 128)


def _pack_heads(x):
    lead = x.shape[:-1]
    x = x.reshape(*lead, X_HEADS, 2 * XS_PAIRS, 128)
    x = jnp.swapaxes(x, -3, -2)
    return x.reshape(*lead, XS_PAIRS, XS_SUB, 128)


def _unpack_heads(x):
    lead = x.shape[:-3]
    x = x.reshape(*lead, 2 * XS_PAIRS, X_HEADS, 128)
    x = jnp.swapaxes(x, -3, -2)
    return x.reshape(*lead, X_HEADS * X_HD)


def _xattn_sample_units(q_ref, k_ref, v_ref, o_ref, r):
    def row():
        t = jnp.sum(k_ref[r] * q_ref[r], axis=1)
        t = t + pltpu.roll(t, X_HEADS, 1)
        s = jnp.sum(t, axis=-1, keepdims=True) * (X_HD ** -0.5)
        e = jnp.exp(s - jnp.max(s, axis=0, keepdims=True))
        p = e / jnp.sum(e, axis=0, keepdims=True)
        o_ref[r] = jnp.sum(p[:, None] * v_ref[r], axis=0)

    return [row]


def _xattn_sample_rider(grid, q, ck, cv):
    b = q.shape[0]
    n_steps, step = _grid_steps(grid)
    rows = b // n_steps
    kv_spec = pl.BlockSpec((rows, N_MEM, XS_PAIRS, XS_SUB, 128), lambda *g: (step(*g), 0, 0, 0, 0))
    q_spec = pl.BlockSpec((rows, XS_PAIRS, XS_SUB, 128), lambda *g: (step(*g), 0, 0, 0))
    return dict(make_units=_per_row_units(_xattn_sample_units, rows),
                in_specs=[q_spec, kv_spec, kv_spec], out_specs=[q_spec],
                out_shape=[jax.ShapeDtypeStruct((b, XS_PAIRS, XS_SUB, 128), F32)],
                args=[_pack_heads(q), _pack_heads(ck.reshape(b, N_MEM, X_HEADS * X_HD)),
                      _pack_heads(cv.reshape(b, N_MEM, X_HEADS * X_HD))])


def _smix_rows_kernel(z_ref, h0_ref, rc_ref, mc_ref, cwr_ref, cbr_ref, wg_ref, bg_ref, lam_ref, gout_ref,
                      cwm_ref, cbm_ref, wqk_ref,
                      yr_ref, hn_ref, rcn_ref, mcn_ref, q_ref, k_ref, ybuf):
    nb = z_ref.shape[0]
    ssq = jnp.zeros((nb, 1), F32)
    for h in range(RNN_HEADS):
        sl = slice(h * RNN_HD, (h + 1) * RNN_HD)
        xc = cbr_ref[:, sl] + cwr_ref[CONV_W - 1:CONV_W, sl] * z_ref[:, sl]
        for j in range(CONV_W - 1):
            xc = xc + cwr_ref[j:j + 1, sl] * rc_ref[:, j * D_RNN + h * RNN_HD:j * D_RNN + (h + 1) * RNN_HD]
        a, u = _lru_gates(xc, wg_ref[h], bg_ref[h], lam_ref[:, sl])
        hs = a * h0_ref[:, sl] + u
        hn_ref[:, sl] = hs
        yv = hs * jax.nn.gelu(z_ref[:, D_RNN + h * RNN_HD:D_RNN + (h + 1) * RNN_HD])
        ssq = ssq + jnp.sum(yv * yv, axis=-1, keepdims=True)
        ybuf[:, sl] = yv
    yr_ref[...] = (ybuf[...] * lax.rsqrt(ssq * (1.0 / D_RNN) + EPS) * gout_ref[...]).astype(yr_ref.dtype)
    rcn_ref[:, 0:2 * D_RNN] = rc_ref[:, D_RNN:3 * D_RNN]
    rcn_ref[:, 2 * D_RNN:3 * D_RNN] = z_ref[:, 0:D_RNN]

    for h in range(MLSTM_HEADS):
        sl = slice(h * MLSTM_HD, (h + 1) * MLSTM_HD)
        uc = cbm_ref[:, sl] + cwm_ref[CONV_W - 1:CONV_W, sl] * z_ref[:, 2 * D_RNN + h * MLSTM_HD:2 * D_RNN + (h + 1) * MLSTM_HD]
        for j in range(CONV_W - 1):
            uc = uc + cwm_ref[j:j + 1, sl] * mc_ref[:, j * D_MLSTM + h * MLSTM_HD:j * D_MLSTM + (h + 1) * MLSTM_HD]
        uc = jax.nn.silu(uc)
        qk = _dot(uc.astype(BF16), wqk_ref[h])
        q_ref[:, sl] = qk[:, :MLSTM_HD]
        k_ref[:, sl] = qk[:, MLSTM_HD:] * (MLSTM_HD ** -0.5)
    mcn_ref[:, 0:2 * D_MLSTM] = mc_ref[:, D_MLSTM:3 * D_MLSTM]
    mcn_ref[:, 2 * D_MLSTM:3 * D_MLSTM] = z_ref[:, 2 * D_RNN:2 * D_RNN + D_MLSTM]


def _smix_rows(z, h0, rconv, mconv, w):
    nb = z.shape[0]
    outs = [jax.ShapeDtypeStruct((nb, D_RNN), BF16),
            jax.ShapeDtypeStruct((nb, D_RNN), F32),
            jax.ShapeDtypeStruct((nb, 3 * D_RNN), F32),
            jax.ShapeDtypeStruct((nb, 3 * D_MLSTM), F32),
            jax.ShapeDtypeStruct((nb, D_MLSTM), F32),
            jax.ShapeDtypeStruct((nb, D_MLSTM), F32)]
    return pl.pallas_call(
        _smix_rows_kernel,
        out_shape=outs,
        scratch_shapes=[pltpu.VMEM((nb, D_RNN), F32)],
        compiler_params=pltpu.CompilerParams(vmem_limit_bytes=V7X_VMEM_LIMIT_BYTES),
        name="smix_rows",
    )(z, h0, rconv, mconv, w['conv_rnn_w'], w['conv_rnn_b'], w['lru_wg'], w['lru_bg'], w['lru_lambda'],
      w['g_rnn_out'], w['conv_ml_w'], w['conv_ml_b'], w['ml_wqk'])


MXU_MIN_ROWS = 8


def _smix_state_units(c_ref, q_ref, k_ref, n_ref, v_ref, og_ref, zg_ref, gb_ref, m_ref, gout_ref,
                      cn_ref, nn_ref, mn_ref, y_ref, r):
    lane_g = lax.broadcasted_iota(jnp.int32, (1, N_GATE_PAD), 1)
    first_row = lax.broadcasted_iota(jnp.int32, (MXU_MIN_ROWS, MLSTM_HD), 0) == 0

    def head(h, zg, m_out):
        sl = slice(h * MLSTM_HD, (h + 1) * MLSTM_HD)
        q = q_ref[r, :, sl]
        k = k_ref[r, :, sl]
        n = n_ref[r, :, sl]
        v = v_ref[r, :, sl]
        ig = zg[:, h:h + 1]
        lf = jax.nn.log_sigmoid(zg[:, MLSTM_HEADS + h:MLSTM_HEADS + h + 1])
        m_prev = m_ref[r, :, h:h + 1]
        m_t = jnp.maximum(lf + m_prev, ig)
        w_inter = jnp.exp(lf + m_prev - m_t)
        g_in = jnp.exp(ig - m_t)
        s = jnp.sum(q * k, axis=1, keepdims=True) * g_in
        den = w_inter * jnp.sum(n * q, axis=1, keepdims=True) + s
        denom = jnp.maximum(jnp.abs(den), jnp.exp(-m_t))
        cmat = c_ref[r, h]
        q8 = jnp.broadcast_to(q, (MXU_MIN_ROWS, MLSTM_HD)).astype(BF16)
        cq = lax.dot_general(q8, cmat.astype(BF16), (((1,), (1,)), ((), ())),
                             preferred_element_type=F32)[0:1, :]
        hm = (w_inter * cq + s * v) / denom
        gv8 = jnp.where(first_row, jnp.broadcast_to(g_in * v, (MXU_MIN_ROWS, MLSTM_HD)), 0.0).astype(BF16)
        k8 = jnp.broadcast_to(k, (MXU_MIN_ROWS, MLSTM_HD)).astype(BF16)
        outer = lax.dot_general(gv8, k8, (((0,), (0,)), ((), ())), preferred_element_type=F32)
        cn_ref[r, h] = w_inter * cmat + outer
        nn_ref[r, :, sl] = w_inter * n + g_in * k
        y = _rms(hm, gout_ref[...]) * jax.nn.sigmoid(og_ref[r, :, sl])
        y_ref[r, :, sl] = y.astype(y_ref.dtype)
        return jnp.where(lane_g == h, m_t, m_out)

    def row():
        zg = zg_ref[r] + gb_ref[...]
        m_out = jnp.zeros((1, N_GATE_PAD), F32)
        for h in range(MLSTM_HEADS):
            m_out = head(h, zg, m_out)
        mn_ref[r] = m_out

    return [row]


def _smix_state_rider(grid, c, q3, k3, n3, z3, zg3, gate_bias, m3, gout):
    nb = c.shape[0]
    n_steps, step = _grid_steps(grid)
    rows = nb // n_steps
    row = lambda width, col=0: pl.BlockSpec((rows, 1, width), lambda *g: (step(*g), 0, col))
    cblk = pl.BlockSpec((rows, MLSTM_HEADS, MLSTM_HD, MLSTM_HD), lambda *g: (step(*g), 0, 0, 0))
    return dict(
        make_units=_per_row_units(_smix_state_units, rows),
        in_specs=[cblk, row(D_MLSTM), row(D_MLSTM), row(D_MLSTM),
                  row(D_MLSTM, 3 * D_RNN // D_MLSTM), row(D_MLSTM, 4 * D_RNN // D_MLSTM),
                  row(N_GATE_PAD),
                  pl.BlockSpec((1, N_GATE_PAD), lambda *g: (0, 0)),
                  row(MLSTM_HEADS),
                  pl.BlockSpec((1, MLSTM_HD), lambda *g: (0, 0))],
        out_specs=[cblk, row(D_MLSTM), row(N_GATE_PAD), row(D_MLSTM)],
        out_shape=[jax.ShapeDtypeStruct(c.shape, F32),
                   jax.ShapeDtypeStruct((nb, 1, D_MLSTM), F32),
                   jax.ShapeDtypeStruct((nb, 1, N_GATE_PAD), F32),
                   jax.ShapeDtypeStruct((nb, 1, D_MLSTM), F32)],
        args=[c, q3, k3, n3, z3, z3, zg3, gate_bias, m3, gout])


RGLRU_CHUNK = 256
PROMPT_TM = 1024


def _layer(xp, xs, mem, rg_h, rg_conv, c, n, mst, ml_conv, ck, cv, w, g_final, final_norm):
    b, t, d = xp.shape
    nb = xs.shape[0]
    m = b * t
    tm = PROMPT_TM
    xp2 = xp.reshape(m, d)
    xs2 = xs.reshape(nb, d)
    w_in_args = dict(tn=1024, out_dtype=F32, w_extra=w['w_in_gate'], n_cols=D_MAIN, w_is_transposed=True)

    zs, zgs, w_in_bf = _norm_matmul(xs2, w['g_mix'], w['w_in_main'], tm=nb, emit_w_bf16=True, **w_in_args)
    ys_rnn, s_h, s_rconv, s_mconv, qs, ks = _smix_rows(
        zs, rg_h, rg_conv.reshape(nb, 3 * D_RNN), ml_conv.reshape(nb, 3 * D_MLSTM), w)

    z, zg = _norm_matmul(xp2, w['g_mix'], w_in_bf, tm=tm, **w_in_args)
    z3 = z.reshape(b, t, D_MAIN)
    zg3 = zg.reshape(b, t, N_GATE_PAD)
    h0 = jnp.zeros((b, 1, D_RNN), F32)
    conv0 = jnp.zeros((b, CONV_W - 1, D_RNN), F32)
    rg_grid = (b, t // RGLRU_CHUNK)
    state_rider = _smix_state_rider(
        rg_grid, c, qs.reshape(nb, 1, D_MLSTM), ks.reshape(nb, 1, D_MLSTM), n.reshape(nb, 1, D_MLSTM),
        zs.reshape(nb, 1, D_MAIN), zgs.reshape(nb, 1, N_GATE_PAD), w['gate_bias'],
        mst.reshape(nb, 1, MLSTM_HEADS), w['g_ml_out'])
    rg_rider = _merge_riders(state_rider, _cast_rider(rg_grid, [[w['w_out'], w['w_cq']], [w['w_co']]]))
    y_rnn, p_h, p_rconv, s_c, s_n, s_m, ys_ml, w_out_cq, w_co = _rglru_prompt(
        z3, h0, conv0, w, tc=RGLRU_CHUNK, rider=rg_rider)

    xs1, qx = _matmul_res_norm_matmul([ys_rnn, ys_ml.reshape(nb, D_MLSTM).astype(BF16)], w_out_cq, xs2,
                                      w['g_xattn'], tm=nb, tn=1024, out2_dtype=F32)

    c0 = jnp.zeros((b, MLSTM_HEADS, MLSTM_HD, MLSTM_HD), F32)
    n0 = jnp.zeros((b, MLSTM_HEADS, 1, MLSTM_HD), F32)
    m0 = jnp.zeros((b, MLSTM_HEADS, 1, N_GATE_PAD), F32)
    mconv0 = jnp.zeros((b, CONV_W - 1, D_MLSTM), F32)
    y_ml, p_c, p_n, p_m, p_mconv, w_up, w_down = _mlstm_prompt(
        z3, zg3, c0, n0, m0, mconv0, w,
        rider=_cast_rider((b, t // (CHUNK * MLSTM_CHUNKS_PER_STEP)), [[w['w_up']], [w['w_down']]]))

    x1, q = _matmul_res_norm_matmul([y_rnn.reshape(m, D_RNN), y_ml.reshape(m, D_MLSTM)], w_out_cq, xp2,
                                    w['g_xattn'], tm=tm, tn=1024, out2_dtype=BF16)
    mem2 = mem.reshape(b * N_MEM, d)
    mk = _norm_matmul(mem2, w['g_mem'], w['w_mk'], tm=b * N_MEM, tn=1024, out_dtype=F32)
    mv = _norm_matmul(mem2, w['g_mem'], w['w_mv'], tm=b * N_MEM, tn=1024, out_dtype=F32)
    x2, = _xattn_prompt(q.reshape(b, t, d), mk.reshape(b, N_MEM, d), mv.reshape(b, N_MEM, d), x1.reshape(b, t, d),
                        w_co, tq=tm, tn=1024)
    ffn_tf = 512
    xp_out, os_packed = _ffn(x2.reshape(m, d), w['g_ffn'], w_up, w_down, g_final, tm=tm, tf=ffn_tf,
                             final_norm=final_norm,
                             rider=_xattn_sample_rider((m // tm, D_FF // ffn_tf), qx, ck, cv))

    os_ = _unpack_heads(os_packed).astype(BF16)
    xs2_ = _matmul_res([os_], w_co, xs1, tm=nb, tn=1024)
    xs_out, = _ffn(xs2_, w['g_ffn'], w_up, w_down, g_final, tm=nb, tf=ffn_tf, final_norm=final_norm)

    new_p = (p_h.reshape(b, D_RNN), p_rconv, p_c, p_n.reshape(b, MLSTM_HEADS, MLSTM_HD), p_m[:, :, 0, 0], p_mconv,
             mk.reshape(b, N_MEM, X_HEADS, X_HD), mv.reshape(b, N_MEM, X_HEADS, X_HD))
    new_s = (s_h, s_rconv.reshape(nb, CONV_W - 1, D_RNN), s_c, s_n.reshape(nb, MLSTM_HEADS, MLSTM_HD),
             s_m[:, 0, :MLSTM_HEADS], s_mconv.reshape(nb, CONV_W - 1, D_MLSTM))
    return xp_out.reshape(b, t, d), xs_out.reshape(nb, 1, d), new_p, new_s


def _layer_weights(l, g_mix, w_in, conv_rnn_w, conv_rnn_b, lru_wa, lru_ba, lru_wx, lru_bx, lru_lambda,
                   g_rnn_out, conv_ml_w, conv_ml_b, ml_wq, ml_wk, ml_bi, ml_bf, g_ml_out, w_out,
                   g_xattn, g_mem, w_cq, w_mk, w_mv, w_co, g_ffn, w_up, w_down):
    n_gate = 2 * MLSTM_HEADS
    w_in_t = jnp.swapaxes(w_in[l], 0, 1)
    w_gate_t = jnp.pad(w_in_t[D_MAIN:], ((0, N_GATE_PAD - n_gate), (0, 0)))
    gate_bias = jnp.pad(jnp.concatenate([ml_bi[l], ml_bf[l]]), (0, N_GATE_PAD - n_gate))
    return dict(
        g_mix=g_mix[l], w_in_main=w_in_t, w_in_gate=w_gate_t.astype(BF16),
        conv_rnn_w=conv_rnn_w[l], conv_rnn_b=conv_rnn_b[l].reshape(1, D_RNN),
        lru_wg=jnp.concatenate([lru_wa[l], lru_wx[l]], axis=-1).astype(BF16),
        lru_bg=jnp.concatenate([lru_ba[l], lru_bx[l]], axis=-1).reshape(RNN_HEADS, 1, 2 * RNN_HD),
        lru_lambda=lru_lambda[l].reshape(1, D_RNN), g_rnn_out=g_rnn_out[l].reshape(1, D_RNN),
        conv_ml_w=conv_ml_w[l], conv_ml_b=conv_ml_b[l].reshape(1, D_MLSTM),
        ml_wqk=jnp.concatenate([ml_wq[l], ml_wk[l]], axis=-1).astype(BF16),
        gate_bias=gate_bias.reshape(1, N_GATE_PAD), g_ml_out=g_ml_out[l].reshape(1, MLSTM_HD),
        w_out=w_out[l], w_cq=w_cq[l], g_xattn=g_xattn[l], g_mem=g_mem[l], w_mk=w_mk[l], w_mv=w_mv[l],
        w_co=w_co[l], g_ffn=g_ffn[l], w_up=w_up[l], w_down=w_down[l])


def kernel(x_prompt, x_sample, mem_prompt, state_rglru_h, state_rglru_conv, state_mlstm_C, state_mlstm_n, state_mlstm_m, state_mlstm_conv, cache_mem_k, cache_mem_v, g_mix, w_in, conv_rnn_w, conv_rnn_b, lru_wa, lru_ba, lru_wx, lru_bx, lru_lambda, g_rnn_out, conv_ml_w, conv_ml_b, ml_wq, ml_wk, ml_bi, ml_bf, g_ml_out, w_out, g_xattn, g_mem, w_cq, w_mk, w_mv, w_co, g_ffn, w_up, w_down, g_final):
    depth = w_in.shape[0]
    xp, xs = x_prompt, x_sample
    p_out = [[] for _ in range(8)]
    s_out = [[] for _ in range(6)]
    for l in range(depth):
        w = _layer_weights(l, g_mix, w_in, conv_rnn_w, conv_rnn_b, lru_wa, lru_ba, lru_wx, lru_bx, lru_lambda,
                           g_rnn_out, conv_ml_w, conv_ml_b, ml_wq, ml_wk, ml_bi, ml_bf, g_ml_out, w_out,
                           g_xattn, g_mem, w_cq, w_mk, w_mv, w_co, g_ffn, w_up, w_down)
        last = l == depth - 1
        xp, xs, new_p, new_s = _layer(xp, xs, mem_prompt, state_rglru_h[l], state_rglru_conv[l], state_mlstm_C[l],
                                      state_mlstm_n[l], state_mlstm_m[l], state_mlstm_conv[l], cache_mem_k[l],
                                      cache_mem_v[l], w, g_final, last)
        for j, a in enumerate(new_p):
            p_out[j].append(a)
        for j, a in enumerate(new_s):
            s_out[j].append(a)
    P = [jnp.stack(a, axis=0) for a in p_out]
    S = [jnp.stack(a, axis=0) for a in s_out]
    return (xp, xs, P[0], P[1], P[2], P[3], P[4], P[5], P[6], P[7],
            S[0], S[1], S[2], S[3], S[4], S[5])
```

```python
import functools

import jax
import jax.numpy as jnp
from jax import lax
from jax.experimental import pallas as pl
from jax.experimental.pallas import tpu as pltpu

F32 = jnp.float32
BF16 = jnp.bfloat16

D_MODEL = 2048
D_RNN = 1024
RNN_HEADS = 8
RNN_HD = 128
CONV_W = 4
LRU_C = 8.0
D_MLSTM = 1024
MLSTM_HEADS = 4
MLSTM_HD = 256
CHUNK = 128
N_MEM = 256
X_HEADS = 4
X_HD = 512
D_FF = 8192
EPS = 1e-6
D_MAIN = 5 * 1024
N_GATE_PAD = 128

V7X_VMEM_LIMIT_BYTES = 56 * 1024 * 1024
V7X_VMEM_LIMIT_FFN_BYTES = 63 * 1024 * 1024
SUBLANES = 8


def _params(*sem, vmem_limit_bytes=V7X_VMEM_LIMIT_BYTES):
    return pltpu.CompilerParams(dimension_semantics=sem, vmem_limit_bytes=vmem_limit_bytes)


def _rms(x, g):
    ms = jnp.mean(x * x, axis=-1, keepdims=True)
    return x * lax.rsqrt(ms + EPS) * g


def _dot(a, b):
    return jnp.dot(a.astype(BF16), b.astype(BF16), preferred_element_type=F32)


def _dot_t(a, bt):
    return lax.dot_general(a.astype(BF16), bt.astype(BF16), (((1,), (1,)), ((), ())), preferred_element_type=F32)


def _with_rider(host_body, rider, n_in, n_out):
    if rider is None:
        return host_body
    r_in, r_out = len(rider['in_specs']), len(rider['out_specs'])

    def body(*refs):
        host_in, refs = refs[:n_in], refs[n_in:]
        rider_in, refs = refs[:r_in], refs[r_in:]
        host_out, refs = refs[:n_out], refs[n_out:]
        rider_out, scratch = refs[:r_out], refs[r_out:]
        host_body(*host_in, *host_out, *scratch, rider=rider['make_units'](*rider_in, *rider_out))
    return body


def _grid_steps(grid):
    n = 1
    for g in grid:
        n *= g

    def step(*idx):
        s = idx[0]
        for g, i in zip(grid[1:], idx[1:]):
            s = s * g + i
        return s
    return n, step


def _per_row_units(row_units, rows):
    return lambda *refs: [u for r in range(rows) for u in row_units(*refs, r)]


def _merge_riders(a, b):
    na_in, na_out = len(a['in_specs']), len(a['out_specs'])
    nb_in = len(b['in_specs'])

    def make_units(*refs):
        a_in, refs = refs[:na_in], refs[na_in:]
        b_in, refs = refs[:nb_in], refs[nb_in:]
        a_out, b_out = refs[:na_out], refs[na_out:]
        return a['make_units'](*a_in, *a_out) + b['make_units'](*b_in, *b_out)

    return dict(make_units=make_units, **{k: a[k] + b[k] for k in ('in_specs', 'out_specs', 'out_shape', 'args')})


def _cast_rider(grid, groups):
    n_steps, step = _grid_steps(grid)
    in_specs, out_specs, out_shape, args, widths = [], [], [], [], []
    for group in groups:
        rps = group[0].shape[0] // n_steps
        for a in group:
            in_specs.append(pl.BlockSpec((rps, a.shape[1]), lambda *g: (step(*g), 0)))
            args.append(a)
        cols = sum(a.shape[1] for a in group)
        out_specs.append(pl.BlockSpec((rps, cols), lambda *g: (step(*g), 0)))
        out_shape.append(jax.ShapeDtypeStruct((group[0].shape[0], cols), BF16))
        widths.append([a.shape[1] for a in group])

    def make_units(*refs):
        in_refs, out_refs = list(refs[:len(args)]), refs[len(args):]
        units = []
        for o_ref, ws in zip(out_refs, widths):
            srcs, in_refs = in_refs[:len(ws)], in_refs[len(ws):]

            def cast(o_ref=o_ref, srcs=srcs, ws=ws):
                c0 = 0
                for s_ref, wd in zip(srcs, ws):
                    o_ref[:, c0:c0 + wd] = s_ref[...].astype(BF16)
                    c0 += wd
            units.append(cast)
        return units

    return dict(make_units=make_units, in_specs=in_specs, out_specs=out_specs, out_shape=out_shape, args=args)


def _run_rider_share(rider, i, n):
    if rider is None:
        return
    for unit in rider[i * len(rider) // n:(i + 1) * len(rider) // n]:
        unit()


def _rider_parts(rider):
    if rider is None:
        return [], [], [], []
    return rider['in_specs'], rider['out_specs'], rider['out_shape'], rider['args']


def _norm_matmul_kernel(*refs, has_extra, w_is_transposed, emit_w_bf16):
    refs = list(refs)
    xn_ref = refs.pop()
    wb_ref = refs.pop() if emit_w_bf16 else None
    if has_extra:
        x_ref, g_ref, w_ref, we_ref, o_ref, oe_ref = refs
    else:
        x_ref, g_ref, w_ref, o_ref = refs
    dot = _dot_t if w_is_transposed else _dot

    @pl.when(pl.program_id(1) == 0)
    def _():
        xn = _rms(x_ref[...], g_ref[...]).astype(BF16)
        xn_ref[...] = xn
        if has_extra:
            oe_ref[...] = dot(xn, we_ref[...])

    wb = w_ref[...].astype(BF16)
    if emit_w_bf16:
        wb_ref[...] = wb
    o_ref[...] = dot(xn_ref[...], wb).astype(o_ref.dtype)


def _norm_matmul(x, g, w, *, tm, tn, out_dtype, w_extra=None, n_cols=None, w_is_transposed=False,
                 emit_w_bf16=False):
    m, k = x.shape
    n = n_cols or w.shape[0 if w_is_transposed else 1]
    has_extra = w_extra is not None
    w_spec = (pl.BlockSpec((tn, k), lambda i, j: (j, 0)) if w_is_transposed
              else pl.BlockSpec((k, tn), lambda i, j: (0, j)))
    in_specs = [pl.BlockSpec((tm, k), lambda i, j: (i, 0)),
                pl.BlockSpec((1, k), lambda i, j: (0, 0)),
                w_spec]
    out_specs = [pl.BlockSpec((tm, tn), lambda i, j: (i, j))]
    out_shape = [jax.ShapeDtypeStruct((m, n), out_dtype)]
    args = [x, g.reshape(1, k), w]
    if has_extra:
        ne = w_extra.shape[0 if w_is_transposed else 1]
        in_specs.append(pl.BlockSpec(w_extra.shape, lambda i, j: (0, 0)))
        out_specs.append(pl.BlockSpec((tm, ne), lambda i, j: (i, 0)))
        out_shape.append(jax.ShapeDtypeStruct((m, ne), F32))
        args.append(w_extra)
    if emit_w_bf16:
        assert m == tm
        out_specs.append(w_spec)
        out_shape.append(jax.ShapeDtypeStruct((n, k) if w_is_transposed else (k, n), BF16))
    res = pl.pallas_call(
        functools.partial(_norm_matmul_kernel, has_extra=has_extra, w_is_transposed=w_is_transposed,
                          emit_w_bf16=emit_w_bf16),
        grid=(m // tm, n // tn),
        in_specs=in_specs, out_specs=out_specs, out_shape=out_shape,
        scratch_shapes=[pltpu.VMEM((tm, k), BF16)],
        compiler_params=_params("parallel", "arbitrary"),
        name="norm_matmul",
    )(*args)
    return res if len(res) > 1 else res[0]


def _norm_matmul_pair_kernel(x_ref, g_ref, wa_ref, wb_ref, oa_ref, ob_ref, xn_ref):
    @pl.when(pl.program_id(0) == 0)
    def _():
        xn_ref[...] = _rms(x_ref[...], g_ref[...]).astype(BF16)

    oa_ref[...] = _dot(xn_ref[...], wa_ref[...])
    ob_ref[...] = _dot(xn_ref[...], wb_ref[...])


def _norm_matmul_pair(x, g, wa, wb, *, tn):
    m, k = x.shape
    n = wa.shape[1]
    cols = lambda rows: pl.BlockSpec((rows, tn), lambda j: (0, j))
    return pl.pallas_call(
        _norm_matmul_pair_kernel,
        grid=(n // tn,),
        in_specs=[pl.BlockSpec((m, k), lambda j: (0, 0)), pl.BlockSpec((1, k), lambda j: (0, 0)), cols(k), cols(k)],
        out_specs=[cols(m), cols(m)],
        out_shape=[jax.ShapeDtypeStruct((m, n), F32)] * 2,
        scratch_shapes=[pltpu.VMEM((m, k), BF16)],
        compiler_params=_params("arbitrary"),
        name="norm_matmul_pair",
    )(x, g.reshape(1, k), wa, wb)


def _matmul_res_kernel(*refs, n_parts):
    a_refs, w_refs = refs[:n_parts], refs[n_parts:2 * n_parts]
    r_ref, o_ref = refs[2 * n_parts:]
    acc = r_ref[...]
    for a_ref, w_ref in zip(a_refs, w_refs):
        acc = acc + _dot(a_ref[...], w_ref[...])
    o_ref[...] = acc


def _matmul_res(a_parts, w, res, *, tm, tn):
    n_parts = len(a_parts)
    m, kp = a_parts[0].shape
    n = w.shape[1]
    a_specs = [pl.BlockSpec((tm, kp), lambda i, j: (i, 0)) for _ in a_parts]
    w_specs = [pl.BlockSpec((kp, tn), lambda i, j, p=p: (p, j)) for p in range(n_parts)]
    return pl.pallas_call(
        functools.partial(_matmul_res_kernel, n_parts=n_parts),
        grid=(m // tm, n // tn),
        in_specs=a_specs + w_specs + [pl.BlockSpec((tm, tn), lambda i, j: (i, j))],
        out_specs=pl.BlockSpec((tm, tn), lambda i, j: (i, j)),
        out_shape=jax.ShapeDtypeStruct((m, n), F32),
        compiler_params=_params("parallel", "arbitrary"),
        name="matmul_res",
    )(*a_parts, *([w] * n_parts), res)


def _matmul_res_norm_matmul_kernel(*refs, n_parts, n1):
    a_refs = refs[:n_parts]
    w_ref, r_ref, g_ref, x1_ref, q_ref, x1_s, xn_s = refs[n_parts:]
    j = pl.program_id(1)
    tn = w_ref.shape[1]
    kp = a_refs[0].shape[1]

    @pl.when(j < n1)
    def _():
        x1 = r_ref[...]
        for p, a_ref in enumerate(a_refs):
            x1 = x1 + _dot(a_ref[...], w_ref[p * kp:(p + 1) * kp, :])
        x1_ref[...] = x1
        x1_s[j] = x1

    @pl.when(j == n1)
    def _():
        ssq = sum(jnp.sum(jnp.square(x1_s[c]), axis=-1, keepdims=True) for c in range(n1))
        scale = lax.rsqrt(ssq * (1.0 / (n1 * tn)) + EPS)
        for c in range(n1):
            xn_s[:, c * tn:(c + 1) * tn] = (x1_s[c] * scale * g_ref[:, c * tn:(c + 1) * tn]).astype(xn_s.dtype)

    @pl.when(j >= n1)
    def _():
        q_ref[...] = _dot(xn_s[...], w_ref[...]).astype(q_ref.dtype)


def _matmul_res_norm_matmul(a_parts, w12, res, g, *, tm, tn, out2_dtype):
    n_parts = len(a_parts)
    m, kp = a_parts[0].shape
    d = res.shape[1]
    n1 = d // tn
    n2 = (w12.shape[1] - d) // tn
    return pl.pallas_call(
        functools.partial(_matmul_res_norm_matmul_kernel, n_parts=n_parts, n1=n1),
        grid=(m // tm, n1 + n2),
        in_specs=[pl.BlockSpec((tm, kp), lambda i, j: (i, 0))] * n_parts + [
            pl.BlockSpec((n_parts * kp, tn), lambda i, j: (0, j)),
            pl.BlockSpec((tm, tn), lambda i, j: (i, jnp.minimum(j, n1 - 1))),
            pl.BlockSpec((1, d), lambda i, j: (0, 0))],
        out_specs=[pl.BlockSpec((tm, tn), lambda i, j: (i, jnp.minimum(j, n1 - 1))),
                   pl.BlockSpec((tm, tn), lambda i, j: (i, jnp.maximum(j - n1, 0)))],
        out_shape=[jax.ShapeDtypeStruct((m, d), F32), jax.ShapeDtypeStruct((m, n2 * tn), out2_dtype)],
        scratch_shapes=[pltpu.VMEM((n1, tm, tn), F32), pltpu.VMEM((tm, d), BF16)],
        compiler_params=_params("parallel", "arbitrary"),
        name="matmul_res_norm_matmul",
    )(*a_parts, w12, res, g.reshape(1, d))


def _ffn_kernel(x_ref, g_ref, wu_ref, wd_ref, gf_ref, o_ref, xn_ref, *, final_norm, rider=None):
    f = pl.program_id(1)

    @pl.when(f == 0)
    def _():
        x = x_ref[...]
        xn_ref[...] = _rms(x, g_ref[...]).astype(BF16)
        o_ref[...] = x

    _run_rider_share(rider, 0, 1)
    h = _dot(xn_ref[...], wu_ref[...])
    h = jnp.square(jnp.maximum(h, 0.0)).astype(BF16)
    o_ref[...] += _dot(h, wd_ref[...])

    if final_norm:
        @pl.when(f == pl.num_programs(1) - 1)
        def _():
            o_ref[...] = _rms(o_ref[...], gf_ref[...])


def _ffn(x, g, w_up, w_down, g_final, *, tm, tf, final_norm, rider=None):
    m, d = x.shape
    dff = w_up.shape[1]
    r_in, r_out, r_shape, r_args = _rider_parts(rider)
    in_specs = [pl.BlockSpec((tm, d), lambda i, f: (i, 0)),
                pl.BlockSpec((1, d), lambda i, f: (0, 0)),
                pl.BlockSpec((d, tf), lambda i, f: (0, f)),
                pl.BlockSpec((tf, d), lambda i, f: (f, 0)),
                pl.BlockSpec((1, d), lambda i, f: (0, 0))]
    out_specs = [pl.BlockSpec((tm, d), lambda i, f: (i, 0))]
    return pl.pallas_call(
        _with_rider(functools.partial(_ffn_kernel, final_norm=final_norm), rider, len(in_specs), len(out_specs)),
        grid=(m // tm, dff // tf),
        in_specs=in_specs + r_in, out_specs=out_specs + r_out,
        out_shape=[jax.ShapeDtypeStruct((m, d), F32)] + r_shape,
        scratch_shapes=[pltpu.VMEM((tm, d), BF16)],
        compiler_params=_params("parallel", "arbitrary", vmem_limit_bytes=V7X_VMEM_LIMIT_FFN_BYTES),
        name="ffn",
    )(x, g.reshape(1, d), w_up, w_down, g_final.reshape(1, d), *r_args)


def _lru_gates(xc, wg, bg, lam):
    g = _dot(xc.astype(BF16), wg) + bg
    r = jax.nn.sigmoid(g[:, :RNN_HD])
    i = jax.nn.sigmoid(g[:, RNN_HD:])
    log_a = -LRU_C * r * jax.nn.softplus(-lam)
    a = jnp.exp(log_a)
    u = jnp.sqrt(-jnp.tanh(log_a) * (a * a + 1.0)) * (i * xc)
    return a, u


def _rglru_kernel(xr_ref, gr_ref, h0_ref, c0_ref, cw_ref, cb_ref, wg_ref, bg_ref, lam_ref, gout_ref,
                  y_ref, hn_ref, cn_ref, xbuf, hc, ybuf, *, tc, rider=None):
    c = pl.program_id(1)

    @pl.when(c == 0)
    def _():
        xbuf[0:8, :] = jnp.zeros((8, D_RNN), F32)
        xbuf[5:8, :] = c0_ref[0]
        hc[...] = h0_ref[0]

    @pl.when(c > 0)
    def _():
        xbuf[0:8, :] = xbuf[tc:tc + 8, :]

    xbuf[8:8 + tc, :] = xr_ref[0]
    ng = tc // SUBLANES
    sub = lax.broadcasted_iota(jnp.int32, (ng, SUBLANES, RNN_HD), 1)
    ssq = jnp.zeros((tc, 1), F32)
    for h in range(RNN_HEADS):
        _run_rider_share(rider, h, RNN_HEADS)
        sl = slice(h * RNN_HD, (h + 1) * RNN_HD)
        xc = cb_ref[:, sl]
        for j in range(CONV_W):
            xc = xc + cw_ref[j:j + 1, sl] * xbuf[5 + j:5 + j + tc, sl]
        a, u = _lru_gates(xc, wg_ref[h], bg_ref[h], lam_ref[:, sl])
        a = a.reshape(ng, SUBLANES, RNN_HD)
        u = u.reshape(ng, SUBLANES, RNN_HD)
        d = 1
        while d < SUBLANES:
            keep = sub >= d
            a_prev = jnp.where(keep, pltpu.roll(a, d, 1), 1.0)
            u_prev = jnp.where(keep, pltpu.roll(u, d, 1), 0.0)
            u = u + a * u_prev
            a = a * a_prev
            d *= 2
        carry = hc[:, sl]
        for g in range(ng):
            hg = u[g] + a[g] * carry
            carry = hg[SUBLANES - 1:SUBLANES, :]
            ybuf[g * SUBLANES:(g + 1) * SUBLANES, sl] = hg
        hc[:, sl] = carry
        yv = ybuf[:, sl] * jax.nn.gelu(gr_ref[0, :, sl])
        ssq = ssq + jnp.sum(yv * yv, axis=-1, keepdims=True)
        ybuf[:, sl] = yv
    y = ybuf[...] * lax.rsqrt(ssq * (1.0 / D_RNN) + EPS) * gout_ref[...]
    y_ref[0] = y.astype(y_ref.dtype)

    @pl.when(c == pl.num_programs(1) - 1)
    def _():
        hn_ref[0] = hc[...]
        cn_ref[0] = xbuf[tc + 5:tc + 8, :]


def _rglru_prompt(z3, h0, conv0, w, *, tc, rider=None):
    b, t, _ = z3.shape
    full2 = lambda bi, ci: (0, 0)
    r_in, r_out, r_shape, r_args = _rider_parts(rider)
    in_specs = [pl.BlockSpec((1, tc, D_RNN), lambda bi, ci: (bi, ci, 0)),
                pl.BlockSpec((1, tc, D_RNN), lambda bi, ci: (bi, ci, 1)),
                pl.BlockSpec((1, 1, D_RNN), lambda bi, ci: (bi, 0, 0)),
                pl.BlockSpec((1, CONV_W - 1, D_RNN), lambda bi, ci: (bi, 0, 0)),
                pl.BlockSpec((CONV_W, D_RNN), full2),
                pl.BlockSpec((1, D_RNN), full2),
                pl.BlockSpec((RNN_HEADS, RNN_HD, 2 * RNN_HD), lambda bi, ci: (0, 0, 0)),
                pl.BlockSpec((RNN_HEADS, 1, 2 * RNN_HD), lambda bi, ci: (0, 0, 0)),
                pl.BlockSpec((1, D_RNN), full2),
                pl.BlockSpec((1, D_RNN), full2)]
    out_specs = [pl.BlockSpec((1, tc, D_RNN), lambda bi, ci: (bi, ci, 0)),
                 pl.BlockSpec((1, 1, D_RNN), lambda bi, ci: (bi, 0, 0)),
                 pl.BlockSpec((1, CONV_W - 1, D_RNN), lambda bi, ci: (bi, 0, 0))]
    out_shape = [jax.ShapeDtypeStruct((b, t, D_RNN), BF16),
                 jax.ShapeDtypeStruct((b, 1, D_RNN), F32),
                 jax.ShapeDtypeStruct((b, CONV_W - 1, D_RNN), F32)]
    return pl.pallas_call(
        _with_rider(functools.partial(_rglru_kernel, tc=tc), rider, len(in_specs), len(out_specs)),
        grid=(b, t // tc),
        in_specs=in_specs + r_in, out_specs=out_specs + r_out, out_shape=out_shape + r_shape,
        scratch_shapes=[pltpu.VMEM((tc + 8, D_RNN), F32),
                        pltpu.VMEM((1, D_RNN), F32),
                        pltpu.VMEM((tc, D_RNN), F32)],
        compiler_params=_params("parallel", "arbitrary"),
        name="rglru_prompt",
    )(z3, z3, h0, conv0, w['conv_rnn_w'], w['conv_rnn_b'], w['lru_wg'], w['lru_bg'],
      w['lru_lambda'], w['g_rnn_out'], *r_args)


def _mlstm_kernel(u_ref, v_ref, og_ref, zg_ref, gb_ref, c0_ref, n0_ref, m0_ref, cv0_ref,
                  cw_ref, cb_ref, wqk_ref, gout_ref,
                  y_ref, cn_ref, nn_ref, mn_ref, cvn_ref, ubuf, cst, nst, mst, rider=None):
    c = pl.program_id(1)
    L = CHUNK
    n_seq = u_ref.shape[0]

    @pl.when(c == 0)
    def _():
        ubuf[:, 0:8, :] = jnp.zeros((n_seq, 8, D_MLSTM), F32)
        ubuf[:, 5:8, :] = cv0_ref[...]
        cst[...] = c0_ref[...]
        nst[...] = n0_ref[...]
        mst[...] = m0_ref[...]

    @pl.when(c > 0)
    def _():
        ubuf[:, 0:8, :] = ubuf[:, L:L + 8, :]

    ubuf[:, 8:8 + L, :] = u_ref[...]

    ti = lax.broadcasted_iota(jnp.int32, (L, L), 0)
    si = lax.broadcasted_iota(jnp.int32, (L, L), 1)
    causal = si <= ti
    anti = ti <= si
    zgs = [zg_ref[s] + gb_ref[...] for s in range(n_seq)]
    zgts = [zg.T for zg in zgs]

    for h in range(MLSTM_HEADS):
        _run_rider_share(rider, h, MLSTM_HEADS)
        sl = slice(h * MLSTM_HD, (h + 1) * MLSTM_HD)
        for s in range(n_seq):
            zg, zgt = zgs[s], zgts[s]
            uc = cb_ref[:, sl]
            for j in range(CONV_W):
                uc = uc + cw_ref[j:j + 1, sl] * ubuf[s, 5 + j:5 + j + L, sl]
            uc = jax.nn.silu(uc)
            qk = _dot(uc.astype(BF16), wqk_ref[h])
            q = qk[:, :MLSTM_HD]
            k = qk[:, MLSTM_HD:] * (MLSTM_HD ** -0.5)
            v = v_ref[s, :, sl]
            qb, kb = q.astype(BF16), k.astype(BF16)

            icol = zg[:, h:h + 1]
            fcol = jax.nn.log_sigmoid(zg[:, MLSTM_HEADS + h:MLSTM_HEADS + h + 1])
            irow = zgt[h:h + 1, :]
            frow = jax.nn.log_sigmoid(zgt[MLSTM_HEADS + h:MLSTM_HEADS + h + 1, :])
            bcol = jnp.sum(jnp.where(causal, frow, 0.0), axis=1, keepdims=True)
            brow = jnp.sum(jnp.where(anti, fcol, 0.0), axis=0, keepdims=True)
            dmat = jnp.where(causal, irow + bcol - brow, -jnp.inf)
            m_prev = mst[s, h, :, 0:1]
            inter = bcol + m_prev
            m_t = jnp.maximum(inter, jnp.max(dmat, axis=1, keepdims=True))
            w_inter = jnp.exp(inter - m_t)
            sc = lax.dot_general(qb, kb, (((1,), (1,)), ((), ())), preferred_element_type=F32) * jnp.exp(dmat - m_t)
            cmat = cst[s, h]
            nrow = nst[s, h]
            cq = lax.dot_general(qb, cmat.astype(BF16), (((1,), (1,)), ((), ())), preferred_element_type=F32)
            num = w_inter * cq + _dot(sc.astype(BF16), v.astype(BF16))
            den = w_inter * jnp.sum(q * nrow, axis=1, keepdims=True) + jnp.sum(sc, axis=1, keepdims=True)
            hm = num / jnp.maximum(jnp.abs(den), jnp.exp(-m_t))

            m_new = m_t[L - 1:L, :]
            b_last = bcol[L - 1:L, :]
            g_state = jnp.exp(b_last + m_prev - m_new)
            g_in = jnp.exp(icol + b_last - bcol - m_new)
            cst[s, h] = g_state * cmat + lax.dot_general((g_in * v).astype(BF16), kb, (((0,), (0,)), ((), ())),
                                                         preferred_element_type=F32)
            nst[s, h] = g_state * nrow + jnp.sum(g_in * k, axis=0, keepdims=True)
            mst[s, h] = jnp.broadcast_to(m_new, (1, N_GATE_PAD))

            y = _rms(hm, gout_ref[...]) * jax.nn.sigmoid(og_ref[s, :, sl])
            y_ref[s, :, sl] = y.astype(y_ref.dtype)

    @pl.when(c == pl.num_programs(1) - 1)
    def _():
        cn_ref[...] = cst[...]
        nn_ref[...] = nst[...]
        mn_ref[...] = mst[...]
        cvn_ref[...] = ubuf[:, L + 5:L + 8, :]


MLSTM_SEQS = 1


def _mlstm_prompt(z3, zg3, c0, n0, m0, conv0, w, *, rider=None):
    b, t, _ = z3.shape
    nc = t // CHUNK
    ns = MLSTM_SEQS
    u_blk, v_blk, o_blk = 2 * D_RNN // D_MLSTM, 3 * D_RNN // D_MLSTM, 4 * D_RNN // D_MLSTM
    zcol = lambda blk: pl.BlockSpec((ns, CHUNK, D_MLSTM), lambda bi, ci: (bi, ci, blk))
    state4 = lambda *minor: pl.BlockSpec((ns, MLSTM_HEADS) + minor, lambda bi, ci: (bi, 0, 0, 0))
    conv_spec = pl.BlockSpec((ns, CONV_W - 1, D_MLSTM), lambda bi, ci: (bi, 0, 0))
    const2 = lambda shape: pl.BlockSpec(shape, lambda bi, ci: (0, 0))
    r_in, r_out, r_shape, r_args = _rider_parts(rider)
    in_specs = [zcol(u_blk), zcol(v_blk), zcol(o_blk),
                pl.BlockSpec((ns, CHUNK, N_GATE_PAD), lambda bi, ci: (bi, ci, 0)),
                const2((1, N_GATE_PAD)),
                state4(MLSTM_HD, MLSTM_HD), state4(1, MLSTM_HD), state4(1, N_GATE_PAD), conv_spec,
                const2((CONV_W, D_MLSTM)), const2((1, D_MLSTM)),
                pl.BlockSpec((MLSTM_HEADS, MLSTM_HD, 2 * MLSTM_HD), lambda bi, ci: (0, 0, 0)),
                const2((1, MLSTM_HD))]
    out_specs = [pl.BlockSpec((ns, CHUNK, D_MLSTM), lambda bi, ci: (bi, ci, 0)),
                 state4(MLSTM_HD, MLSTM_HD), state4(1, MLSTM_HD), state4(1, N_GATE_PAD), conv_spec]
    out_shape = [jax.ShapeDtypeStruct((b, t, D_MLSTM), BF16),
                 jax.ShapeDtypeStruct((b, MLSTM_HEADS, MLSTM_HD, MLSTM_HD), F32),
                 jax.ShapeDtypeStruct((b, MLSTM_HEADS, 1, MLSTM_HD), F32),
                 jax.ShapeDtypeStruct((b, MLSTM_HEADS, 1, N_GATE_PAD), F32),
                 jax.ShapeDtypeStruct((b, CONV_W - 1, D_MLSTM), F32)]
    return pl.pallas_call(
        _with_rider(_mlstm_kernel, rider, len(in_specs), len(out_specs)),
        grid=(b // ns, nc),
        in_specs=in_specs + r_in, out_specs=out_specs + r_out, out_shape=out_shape + r_shape,
        scratch_shapes=[pltpu.VMEM((ns, CHUNK + 8, D_MLSTM), F32),
                        pltpu.VMEM((ns, MLSTM_HEADS, MLSTM_HD, MLSTM_HD), F32),
                        pltpu.VMEM((ns, MLSTM_HEADS, 1, MLSTM_HD), F32),
                        pltpu.VMEM((ns, MLSTM_HEADS, 1, N_GATE_PAD), F32)],
        compiler_params=_params("parallel", "arbitrary"),
        name="mlstm_prompt",
    )(z3, z3, z3, zg3, w['gate_bias'], c0, n0, m0, conv0, w['conv_ml_w'], w['conv_ml_b'],
      w['ml_wqk'], w['g_ml_out'], *r_args)


def _xattn_kernel(q_ref, k_ref, v_ref, x_ref, wo_ref, o_ref, att, rider=None):
    @pl.when(pl.program_id(2) == 0)
    def _():
        for h in range(X_HEADS):
            sl = slice(h * X_HD, (h + 1) * X_HD)
            s = _dot_t(q_ref[0, :, sl], k_ref[0, :, sl]) * (X_HD ** -0.5)
            e = jnp.exp(s - jnp.max(s, axis=-1, keepdims=True))
            p = e / jnp.sum(e, axis=-1, keepdims=True)
            att[:, sl] = _dot(p, v_ref[0, :, sl]).astype(att.dtype)

    _run_rider_share(rider, 0, 1)
    o_ref[0] = x_ref[0] + _dot(att[...], wo_ref[...])


def _xattn_prompt(q3, mk3, mv3, x3, w_co, *, tq, tn, rider=None):
    b, t, d = q3.shape
    mem = pl.BlockSpec((1, N_MEM, d), lambda bi, ti, j: (bi, 0, 0))
    cols = pl.BlockSpec((1, tq, tn), lambda bi, ti, j: (bi, ti, j))
    r_in, r_out, r_shape, r_args = _rider_parts(rider)
    in_specs = [pl.BlockSpec((1, tq, d), lambda bi, ti, j: (bi, ti, 0)), mem, mem, cols,
                pl.BlockSpec((d, tn), lambda bi, ti, j: (0, j))]
    return pl.pallas_call(
        _with_rider(_xattn_kernel, rider, len(in_specs), 1),
        grid=(b, t // tq, d // tn),
        in_specs=in_specs + r_in, out_specs=[cols] + r_out,
        out_shape=[jax.ShapeDtypeStruct((b, t, d), F32)] + r_shape,
        scratch_shapes=[pltpu.VMEM((tq, d), BF16)],
        compiler_params=_params("parallel", "parallel", "arbitrary"),
        name="xattn_prompt",
    )(q3, mk3, mv3, x3, w_co, *r_args)


XS_SUB = 2 * X_HEADS
XS_PAIRS = X_HD // (2 * 128)


def _pack_heads(x):
    lead = x.shape[:-1]
    x = x.reshape(*lead, X_HEADS, 2 * XS_PAIRS, 128)
    x = jnp.swapaxes(x, -3, -2)
    return x.reshape(*lead, XS_PAIRS, XS_SUB, 128)


def _unpack_heads(x):
    lead = x.shape[:-3]
    x = x.reshape(*lead, 2 * XS_PAIRS, X_HEADS, 128)
    x = jnp.swapaxes(x, -3, -2)
    return x.reshape(*lead, X_HEADS * X_HD)


def _xattn_sample_units(q_ref, k_ref, v_ref, o_ref, r):
    def row():
        t = jnp.sum(k_ref[r] * q_ref[r], axis=1)
        t = t + pltpu.roll(t, X_HEADS, 1)
        s = jnp.sum(t, axis=-1, keepdims=True) * (X_HD ** -0.5)
        e = jnp.exp(s - jnp.max(s, axis=0, keepdims=True))
        p = e / jnp.sum(e, axis=0, keepdims=True)
        o_ref[r] = jnp.sum(p[:, None] * v_ref[r], axis=0)

    return [row]


def _xattn_sample_rider(grid, q, ck, cv):
    b = q.shape[0]
    n_steps, step = _grid_steps(grid)
    rows = b // n_steps
    kv_spec = pl.BlockSpec((rows, N_MEM, XS_PAIRS, XS_SUB, 128), lambda *g: (step(*g), 0, 0, 0, 0))
    q_spec = pl.BlockSpec((rows, XS_PAIRS, XS_SUB, 128), lambda *g: (step(*g), 0, 0, 0))
    return dict(make_units=_per_row_units(_xattn_sample_units, rows),
                in_specs=[q_spec, kv_spec, kv_spec], out_specs=[q_spec],
                out_shape=[jax.ShapeDtypeStruct((b, XS_PAIRS, XS_SUB, 128), F32)],
                args=[_pack_heads(q), _pack_heads(ck.reshape(b, N_MEM, X_HEADS * X_HD)),
                      _pack_heads(cv.reshape(b, N_MEM, X_HEADS * X_HD))])


def _smix_rows_kernel(z_ref, h0_ref, rc_ref, mc_ref, cwr_ref, cbr_ref, wg_ref, bg_ref, lam_ref, gout_ref,
                      cwm_ref, cbm_ref, wqk_ref,
                      yr_ref, hn_ref, rcn_ref, mcn_ref, q_ref, k_ref, ybuf):
    nb = z_ref.shape[0]
    ssq = jnp.zeros((nb, 1), F32)
    for h in range(RNN_HEADS):
        sl = slice(h * RNN_HD, (h + 1) * RNN_HD)
        xc = cbr_ref[:, sl] + cwr_ref[CONV_W - 1:CONV_W, sl] * z_ref[:, sl]
        for j in range(CONV_W - 1):
            xc = xc + cwr_ref[j:j + 1, sl] * rc_ref[:, j * D_RNN + h * RNN_HD:j * D_RNN + (h + 1) * RNN_HD]
        a, u = _lru_gates(xc, wg_ref[h], bg_ref[h], lam_ref[:, sl])
        hs = a * h0_ref[:, sl] + u
        hn_ref[:, sl] = hs
        yv = hs * jax.nn.gelu(z_ref[:, D_RNN + h * RNN_HD:D_RNN + (h + 1) * RNN_HD])
        ssq = ssq + jnp.sum(yv * yv, axis=-1, keepdims=True)
        ybuf[:, sl] = yv
    yr_ref[...] = (ybuf[...] * lax.rsqrt(ssq * (1.0 / D_RNN) + EPS) * gout_ref[...]).astype(yr_ref.dtype)
    rcn_ref[:, 0:2 * D_RNN] = rc_ref[:, D_RNN:3 * D_RNN]
    rcn_ref[:, 2 * D_RNN:3 * D_RNN] = z_ref[:, 0:D_RNN]

    for h in range(MLSTM_HEADS):
        sl = slice(h * MLSTM_HD, (h + 1) * MLSTM_HD)
        uc = cbm_ref[:, sl] + cwm_ref[CONV_W - 1:CONV_W, sl] * z_ref[:, 2 * D_RNN + h * MLSTM_HD:2 * D_RNN + (h + 1) * MLSTM_HD]
        for j in range(CONV_W - 1):
            uc = uc + cwm_ref[j:j + 1, sl] * mc_ref[:, j * D_MLSTM + h * MLSTM_HD:j * D_MLSTM + (h + 1) * MLSTM_HD]
        uc = jax.nn.silu(uc)
        qk = _dot(uc.astype(BF16), wqk_ref[h])
        q_ref[:, sl] = qk[:, :MLSTM_HD]
        k_ref[:, sl] = qk[:, MLSTM_HD:] * (MLSTM_HD ** -0.5)
    mcn_ref[:, 0:2 * D_MLSTM] = mc_ref[:, D_MLSTM:3 * D_MLSTM]
    mcn_ref[:, 2 * D_MLSTM:3 * D_MLSTM] = z_ref[:, 2 * D_RNN:2 * D_RNN + D_MLSTM]


def _smix_rows(z, h0, rconv, mconv, w):
    nb = z.shape[0]
    outs = [jax.ShapeDtypeStruct((nb, D_RNN), BF16),
            jax.ShapeDtypeStruct((nb, D_RNN), F32),
            jax.ShapeDtypeStruct((nb, 3 * D_RNN), F32),
            jax.ShapeDtypeStruct((nb, 3 * D_MLSTM), F32),
            jax.ShapeDtypeStruct((nb, D_MLSTM), F32),
            jax.ShapeDtypeStruct((nb, D_MLSTM), F32)]
    return pl.pallas_call(
        _smix_rows_kernel,
        out_shape=outs,
        scratch_shapes=[pltpu.VMEM((nb, D_RNN), F32)],
        compiler_params=pltpu.CompilerParams(vmem_limit_bytes=V7X_VMEM_LIMIT_BYTES),
        name="smix_rows",
    )(z, h0, rconv, mconv, w['conv_rnn_w'], w['conv_rnn_b'], w['lru_wg'], w['lru_bg'], w['lru_lambda'],
      w['g_rnn_out'], w['conv_ml_w'], w['conv_ml_b'], w['ml_wqk'])


MXU_MIN_ROWS = 8


def _smix_state_units(c_ref, q_ref, k_ref, n_ref, v_ref, og_ref, zg_ref, gb_ref, m_ref, gout_ref,
                      cn_ref, nn_ref, mn_ref, y_ref, r):
    lane_g = lax.broadcasted_iota(jnp.int32, (1, N_GATE_PAD), 1)
    first_row = lax.broadcasted_iota(jnp.int32, (MXU_MIN_ROWS, MLSTM_HD), 0) == 0

    def head(h, zg, m_out):
        sl = slice(h * MLSTM_HD, (h + 1) * MLSTM_HD)
        q = q_ref[r, :, sl]
        k = k_ref[r, :, sl]
        n = n_ref[r, :, sl]
        v = v_ref[r, :, sl]
        ig = zg[:, h:h + 1]
        lf = jax.nn.log_sigmoid(zg[:, MLSTM_HEADS + h:MLSTM_HEADS + h + 1])
        m_prev = m_ref[r, :, h:h + 1]
        m_t = jnp.maximum(lf + m_prev, ig)
        w_inter = jnp.exp(lf + m_prev - m_t)
        g_in = jnp.exp(ig - m_t)
        s = jnp.sum(q * k, axis=1, keepdims=True) * g_in
        den = w_inter * jnp.sum(n * q, axis=1, keepdims=True) + s
        denom = jnp.maximum(jnp.abs(den), jnp.exp(-m_t))
        cmat = c_ref[r, h]
        q8 = jnp.broadcast_to(q, (MXU_MIN_ROWS, MLSTM_HD)).astype(BF16)
        cq = lax.dot_general(q8, cmat.astype(BF16), (((1,), (1,)), ((), ())),
                             preferred_element_type=F32)[0:1, :]
        hm = (w_inter * cq + s * v) / denom
        gv8 = jnp.where(first_row, jnp.broadcast_to(g_in * v, (MXU_MIN_ROWS, MLSTM_HD)), 0.0).astype(BF16)
        k8 = jnp.broadcast_to(k, (MXU_MIN_ROWS, MLSTM_HD)).astype(BF16)
        outer = lax.dot_general(gv8, k8, (((0,), (0,)), ((), ())), preferred_element_type=F32)
        cn_ref[r, h] = w_inter * cmat + outer
        nn_ref[r, :, sl] = w_inter * n + g_in * k
        y = _rms(hm, gout_ref[...]) * jax.nn.sigmoid(og_ref[r, :, sl])
        y_ref[r, :, sl] = y.astype(y_ref.dtype)
        return jnp.where(lane_g == h, m_t, m_out)

    def row():
        zg = zg_ref[r] + gb_ref[...]
        m_out = jnp.zeros((1, N_GATE_PAD), F32)
        for h in range(MLSTM_HEADS):
            m_out = head(h, zg, m_out)
        mn_ref[r] = m_out

    return [row]


def _smix_state_rider(grid, c, q3, k3, n3, z3, zg3, gate_bias, m3, gout):
    nb = c.shape[0]
    n_steps, step = _grid_steps(grid)
    rows = nb // n_steps
    row = lambda width, col=0: pl.BlockSpec((rows, 1, width), lambda *g: (step(*g), 0, col))
    cblk = pl.BlockSpec((rows, MLSTM_HEADS, MLSTM_HD, MLSTM_HD), lambda *g: (step(*g), 0, 0, 0))
    return dict(
        make_units=_per_row_units(_smix_state_units, rows),
        in_specs=[cblk, row(D_MLSTM), row(D_MLSTM), row(D_MLSTM),
                  row(D_MLSTM, 3 * D_RNN // D_MLSTM), row(D_MLSTM, 4 * D_RNN // D_MLSTM),
                  row(N_GATE_PAD),
                  pl.BlockSpec((1, N_GATE_PAD), lambda *g: (0, 0)),
                  row(MLSTM_HEADS),
                  pl.BlockSpec((1, MLSTM_HD), lambda *g: (0, 0))],
        out_specs=[cblk, row(D_MLSTM), row(N_GATE_PAD), row(D_MLSTM)],
        out_shape=[jax.ShapeDtypeStruct(c.shape, F32),
                   jax.ShapeDtypeStruct((nb, 1, D_MLSTM), F32),
                   jax.ShapeDtypeStruct((nb, 1, N_GATE_PAD), F32),
                   jax.ShapeDtypeStruct((nb, 1, D_MLSTM), F32)],
        args=[c, q3, k3, n3, z3, z3, zg3, gate_bias, m3, gout])


RGLRU_CHUNK = 256
PROMPT_TM = 1024
PROMPT_W_IN_TN = 1280


def _layer(xp, xs, mem, rg_h, rg_conv, c, n, mst, ml_conv, ck, cv, w, g_final, final_norm):
    b, t, d = xp.shape
    nb = xs.shape[0]
    m = b * t
    tm = PROMPT_TM
    xp2 = xp.reshape(m, d)
    xs2 = xs.reshape(nb, d)
    w_in_args = dict(tn=1024, out_dtype=F32, w_extra=w['w_in_gate'], n_cols=D_MAIN, w_is_transposed=True)

    zs, zgs, w_in_bf = _norm_matmul(xs2, w['g_mix'], w['w_in_main'], tm=nb, emit_w_bf16=True, **w_in_args)
    ys_rnn, s_h, s_rconv, s_mconv, qs, ks = _smix_rows(
        zs, rg_h, rg_conv.reshape(nb, 3 * D_RNN), ml_conv.reshape(nb, 3 * D_MLSTM), w)

    z, zg = _norm_matmul(xp2, w['g_mix'], w_in_bf, tm=tm, **dict(w_in_args, tn=PROMPT_W_IN_TN))
    z3 = z.reshape(b, t, D_MAIN)
    zg3 = zg.reshape(b, t, N_GATE_PAD)
    h0 = jnp.zeros((b, 1, D_RNN), F32)
    conv0 = jnp.zeros((b, CONV_W - 1, D_RNN), F32)
    rg_grid = (b, t // RGLRU_CHUNK)
    state_rider = _smix_state_rider(
        rg_grid, c, qs.reshape(nb, 1, D_MLSTM), ks.reshape(nb, 1, D_MLSTM), n.reshape(nb, 1, D_MLSTM),
        zs.reshape(nb, 1, D_MAIN), zgs.reshape(nb, 1, N_GATE_PAD), w['gate_bias'],
        mst.reshape(nb, 1, MLSTM_HEADS), w['g_ml_out'])
    rg_rider = _merge_riders(state_rider, _cast_rider(rg_grid, [[w['w_out'], w['w_cq']], [w['w_co']]]))
    y_rnn, p_h, p_rconv, s_c, s_n, s_m, ys_ml, w_out_cq, w_co = _rglru_prompt(
        z3, h0, conv0, w, tc=RGLRU_CHUNK, rider=rg_rider)

    xs1, qx = _matmul_res_norm_matmul([ys_rnn, ys_ml.reshape(nb, D_MLSTM).astype(BF16)], w_out_cq, xs2,
                                      w['g_xattn'], tm=nb, tn=1024, out2_dtype=F32)

    c0 = jnp.zeros((b, MLSTM_HEADS, MLSTM_HD, MLSTM_HD), F32)
    n0 = jnp.zeros((b, MLSTM_HEADS, 1, MLSTM_HD), F32)
    m0 = jnp.zeros((b, MLSTM_HEADS, 1, N_GATE_PAD), F32)
    mconv0 = jnp.zeros((b, CONV_W - 1, D_MLSTM), F32)
    y_ml, p_c, p_n, p_m, p_mconv, w_up, w_down = _mlstm_prompt(
        z3, zg3, c0, n0, m0, mconv0, w,
        rider=_cast_rider((b // MLSTM_SEQS, t // CHUNK), [[w['w_up']], [w['w_down']]]))

    x1, q = _matmul_res_norm_matmul([y_rnn.reshape(m, D_RNN), y_ml.reshape(m, D_MLSTM)], w_out_cq, xp2,
                                    w['g_xattn'], tm=tm, tn=1024, out2_dtype=BF16)
    mem2 = mem.reshape(b * N_MEM, d)
    mk, mv = _norm_matmul_pair(mem2, w['g_mem'], w['w_mk'], w['w_mv'], tn=512)
    x2, = _xattn_prompt(q.reshape(b, t, d), mk.reshape(b, N_MEM, d), mv.reshape(b, N_MEM, d), x1.reshape(b, t, d),
                        w_co, tq=tm, tn=1024)
    ffn_tf = 512
    xp_out, os_packed = _ffn(x2.reshape(m, d), w['g_ffn'], w_up, w_down, g_final, tm=tm, tf=ffn_tf,
                             final_norm=final_norm,
                             rider=_xattn_sample_rider((m // tm, D_FF // ffn_tf), qx, ck, cv))

    os_ = _unpack_heads(os_packed).astype(BF16)
    xs2_ = _matmul_res([os_], w_co, xs1, tm=nb, tn=1024)
    xs_out, = _ffn(xs2_, w['g_ffn'], w_up, w_down, g_final, tm=nb, tf=ffn_tf, final_norm=final_norm)

    new_p = (p_h.reshape(b, D_RNN), p_rconv, p_c, p_n.reshape(b, MLSTM_HEADS, MLSTM_HD), p_m[:, :, 0, 0], p_mconv,
             mk.reshape(b, N_MEM, X_HEADS, X_HD), mv.reshape(b, N_MEM, X_HEADS, X_HD))
    new_s = (s_h, s_rconv.reshape(nb, CONV_W - 1, D_RNN), s_c, s_n.reshape(nb, MLSTM_HEADS, MLSTM_HD),
             s_m[:, 0, :MLSTM_HEADS], s_mconv.reshape(nb, CONV_W - 1, D_MLSTM))
    return xp_out.reshape(b, t, d), xs_out.reshape(nb, 1, d), new_p, new_s


def _layer_weights(l, g_mix, w_in, conv_rnn_w, conv_rnn_b, lru_wa, lru_ba, lru_wx, lru_bx, lru_lambda,
                   g_rnn_out, conv_ml_w, conv_ml_b, ml_wq, ml_wk, ml_bi, ml_bf, g_ml_out, w_out,
                   g_xattn, g_mem, w_cq, w_mk, w_mv, w_co, g_ffn, w_up, w_down):
    n_gate = 2 * MLSTM_HEADS
    w_in_t = jnp.swapaxes(w_in[l], 0, 1)
    w_gate_t = jnp.pad(w_in_t[D_MAIN:], ((0, N_GATE_PAD - n_gate), (0, 0)))
    gate_bias = jnp.pad(jnp.concatenate([ml_bi[l], ml_bf[l]]), (0, N_GATE_PAD - n_gate))
    return dict(
        g_mix=g_mix[l], w_in_main=w_in_t, w_in_gate=w_gate_t.astype(BF16),
        conv_rnn_w=conv_rnn_w[l], conv_rnn_b=conv_rnn_b[l].reshape(1, D_RNN),
        lru_wg=jnp.concatenate([lru_wa[l], lru_wx[l]], axis=-1).astype(BF16),
        lru_bg=jnp.concatenate([lru_ba[l], lru_bx[l]], axis=-1).reshape(RNN_HEADS, 1, 2 * RNN_HD),
        lru_lambda=lru_lambda[l].reshape(1, D_RNN), g_rnn_out=g_rnn_out[l].reshape(1, D_RNN),
        conv_ml_w=conv_ml_w[l], conv_ml_b=conv_ml_b[l].reshape(1, D_MLSTM),
        ml_wqk=jnp.concatenate([ml_wq[l], ml_wk[l]], axis=-1).astype(BF16),
        gate_bias=gate_bias.reshape(1, N_GATE_PAD), g_ml_out=g_ml_out[l].reshape(1, MLSTM_HD),
        w_out=w_out[l], w_cq=w_cq[l], g_xattn=g_xattn[l], g_mem=g_mem[l], w_mk=w_mk[l], w_mv=w_mv[l],
        w_co=w_co[l], g_ffn=g_ffn[l], w_up=w_up[l], w_down=w_down[l])


def kernel(x_prompt, x_sample, mem_prompt, state_rglru_h, state_rglru_conv, state_mlstm_C, state_mlstm_n, state_mlstm_m, state_mlstm_conv, cache_mem_k, cache_mem_v, g_mix, w_in, conv_rnn_w, conv_rnn_b, lru_wa, lru_ba, lru_wx, lru_bx, lru_lambda, g_rnn_out, conv_ml_w, conv_ml_b, ml_wq, ml_wk, ml_bi, ml_bf, g_ml_out, w_out, g_xattn, g_mem, w_cq, w_mk, w_mv, w_co, g_ffn, w_up, w_down, g_final):
    depth = w_in.shape[0]
    xp, xs = x_prompt, x_sample
    p_out = [[] for _ in range(8)]
    s_out = [[] for _ in range(6)]
    for l in range(depth):
        w = _layer_weights(l, g_mix, w_in, conv_rnn_w, conv_rnn_b, lru_wa, lru_ba, lru_wx, lru_bx, lru_lambda,
                           g_rnn_out, conv_ml_w, conv_ml_b, ml_wq, ml_wk, ml_bi, ml_bf, g_ml_out, w_out,
                           g_xattn, g_mem, w_cq, w_mk, w_mv, w_co, g_ffn, w_up, w_down)
        last = l == depth - 1
        xp, xs, new_p, new_s = _layer(xp, xs, mem_prompt, state_rglru_h[l], state_rglru_conv[l], state_mlstm_C[l],
                                      state_mlstm_n[l], state_mlstm_m[l], state_mlstm_conv[l], cache_mem_k[l],
                                      cache_mem_v[l], w, g_final, last)
        for j, a in enumerate(new_p):
            p_out[j].append(a)
        for j, a in enumerate(new_s):
            s_out[j].append(a)
    P = [jnp.stack(a, axis=0) for a in p_out]
    S = [jnp.stack(a, axis=0) for a in s_out]
    return (xp, xs, P[0], P[1], P[2], P[3], P[4], P[5], P[6], P[7],
            S[0], S[1], S[2], S[3], S[4], S[5])
```

```python
import functools

import jax
import jax.numpy as jnp
from jax import lax
from jax.experimental import pallas as pl
from jax.experimental.pallas import tpu as pltpu

F32 = jnp.float32
BF16 = jnp.bfloat16

D_MODEL = 2048
D_RNN = 1024
RNN_HEADS = 8
RNN_HD = 128
CONV_W = 4
LRU_C = 8.0
D_MLSTM = 1024
MLSTM_HEADS = 4
MLSTM_HD = 256
CHUNK = 128
N_MEM = 256
X_HEADS = 4
X_HD = 512
D_FF = 8192
EPS = 1e-6
D_MAIN = 5 * 1024
N_GATE_PAD = 128

V7X_VMEM_LIMIT_BYTES = 56 * 1024 * 1024
V7X_VMEM_LIMIT_HIGH_BYTES = 63 * 1024 * 1024
SUBLANES = 8


def _params(*sem, vmem_limit_bytes=V7X_VMEM_LIMIT_BYTES):
    return pltpu.CompilerParams(dimension_semantics=sem, vmem_limit_bytes=vmem_limit_bytes)


def _rms(x, g):
    ms = jnp.mean(x * x, axis=-1, keepdims=True)
    return x * lax.rsqrt(ms + EPS) * g


def _dot(a, b):
    return jnp.dot(a.astype(BF16), b.astype(BF16), preferred_element_type=F32)


def _dot_t(a, bt):
    return lax.dot_general(a.astype(BF16), bt.astype(BF16), (((1,), (1,)), ((), ())), preferred_element_type=F32)


def _with_rider(host_body, rider, n_in, n_out):
    if rider is None:
        return host_body
    r_in, r_out = len(rider['in_specs']), len(rider['out_specs'])

    def body(*refs):
        host_in, refs = refs[:n_in], refs[n_in:]
        rider_in, refs = refs[:r_in], refs[r_in:]
        host_out, refs = refs[:n_out], refs[n_out:]
        rider_out, scratch = refs[:r_out], refs[r_out:]
        host_body(*host_in, *host_out, *scratch, rider=rider['make_units'](*rider_in, *rider_out))
    return body


def _grid_steps(grid):
    n = 1
    for g in grid:
        n *= g

    def step(*idx):
        s = idx[0]
        for g, i in zip(grid[1:], idx[1:]):
            s = s * g + i
        return s
    return n, step


def _per_row_units(row_units, rows):
    return lambda *refs: [u for r in range(rows) for u in row_units(*refs, r)]


def _merge_riders(a, b):
    na_in, na_out = len(a['in_specs']), len(a['out_specs'])
    nb_in = len(b['in_specs'])

    def make_units(*refs):
        a_in, refs = refs[:na_in], refs[na_in:]
        b_in, refs = refs[:nb_in], refs[nb_in:]
        a_out, b_out = refs[:na_out], refs[na_out:]
        return a['make_units'](*a_in, *a_out) + b['make_units'](*b_in, *b_out)

    return dict(make_units=make_units, **{k: a[k] + b[k] for k in ('in_specs', 'out_specs', 'out_shape', 'args')})


def _cast_rider(grid, groups):
    n_steps, step = _grid_steps(grid)
    in_specs, out_specs, out_shape, args, widths = [], [], [], [], []
    for group in groups:
        rps = group[0].shape[0] // n_steps
        for a in group:
            in_specs.append(pl.BlockSpec((rps, a.shape[1]), lambda *g: (step(*g), 0)))
            args.append(a)
        cols = sum(a.shape[1] for a in group)
        out_specs.append(pl.BlockSpec((rps, cols), lambda *g: (step(*g), 0)))
        out_shape.append(jax.ShapeDtypeStruct((group[0].shape[0], cols), BF16))
        widths.append([a.shape[1] for a in group])

    def make_units(*refs):
        in_refs, out_refs = list(refs[:len(args)]), refs[len(args):]
        units = []
        for o_ref, ws in zip(out_refs, widths):
            srcs, in_refs = in_refs[:len(ws)], in_refs[len(ws):]

            def cast(o_ref=o_ref, srcs=srcs, ws=ws):
                c0 = 0
                for s_ref, wd in zip(srcs, ws):
                    o_ref[:, c0:c0 + wd] = s_ref[...].astype(BF16)
                    c0 += wd
            units.append(cast)
        return units

    return dict(make_units=make_units, in_specs=in_specs, out_specs=out_specs, out_shape=out_shape, args=args)


def _run_rider_share(rider, i, n):
    if rider is None:
        return
    for unit in rider[i * len(rider) // n:(i + 1) * len(rider) // n]:
        unit()


def _rider_parts(rider):
    if rider is None:
        return [], [], [], []
    return rider['in_specs'], rider['out_specs'], rider['out_shape'], rider['args']


def _norm_matmul_kernel(*refs, has_extra, w_is_transposed, emit_w_bf16):
    refs = list(refs)
    xn_ref = refs.pop()
    wb_ref = refs.pop() if emit_w_bf16 else None
    if has_extra:
        x_ref, g_ref, w_ref, we_ref, o_ref, oe_ref = refs
    else:
        x_ref, g_ref, w_ref, o_ref = refs
    dot = _dot_t if w_is_transposed else _dot

    @pl.when(pl.program_id(1) == 0)
    def _():
        xn = _rms(x_ref[...], g_ref[...]).astype(BF16)
        xn_ref[...] = xn
        if has_extra:
            oe_ref[...] = dot(xn, we_ref[...])

    wb = w_ref[...].astype(BF16)
    if emit_w_bf16:
        wb_ref[...] = wb
    o_ref[...] = dot(xn_ref[...], wb).astype(o_ref.dtype)


def _norm_matmul(x, g, w, *, tm, tn, out_dtype, w_extra=None, n_cols=None, w_is_transposed=False,
                 emit_w_bf16=False):
    m, k = x.shape
    n = n_cols or w.shape[0 if w_is_transposed else 1]
    has_extra = w_extra is not None
    w_spec = (pl.BlockSpec((tn, k), lambda i, j: (j, 0)) if w_is_transposed
              else pl.BlockSpec((k, tn), lambda i, j: (0, j)))
    in_specs = [pl.BlockSpec((tm, k), lambda i, j: (i, 0)),
                pl.BlockSpec((1, k), lambda i, j: (0, 0)),
                w_spec]
    out_specs = [pl.BlockSpec((tm, tn), lambda i, j: (i, j))]
    out_shape = [jax.ShapeDtypeStruct((m, n), out_dtype)]
    args = [x, g.reshape(1, k), w]
    if has_extra:
        ne = w_extra.shape[0 if w_is_transposed else 1]
        in_specs.append(pl.BlockSpec(w_extra.shape, lambda i, j: (0, 0)))
        out_specs.append(pl.BlockSpec((tm, ne), lambda i, j: (i, 0)))
        out_shape.append(jax.ShapeDtypeStruct((m, ne), F32))
        args.append(w_extra)
    if emit_w_bf16:
        assert m == tm
        out_specs.append(w_spec)
        out_shape.append(jax.ShapeDtypeStruct((n, k) if w_is_transposed else (k, n), BF16))
    res = pl.pallas_call(
        functools.partial(_norm_matmul_kernel, has_extra=has_extra, w_is_transposed=w_is_transposed,
                          emit_w_bf16=emit_w_bf16),
        grid=(m // tm, n // tn),
        in_specs=in_specs, out_specs=out_specs, out_shape=out_shape,
        scratch_shapes=[pltpu.VMEM((tm, k), BF16)],
        compiler_params=_params("parallel", "arbitrary"),
        name="norm_matmul",
    )(*args)
    return res if len(res) > 1 else res[0]


def _norm_matmul_pair_kernel(x_ref, g_ref, wa_ref, wb_ref, oa_ref, ob_ref, xn_ref):
    @pl.when(pl.program_id(0) == 0)
    def _():
        xn_ref[...] = _rms(x_ref[...], g_ref[...]).astype(BF16)

    oa_ref[...] = _dot(xn_ref[...], wa_ref[...])
    ob_ref[...] = _dot(xn_ref[...], wb_ref[...])


def _norm_matmul_pair(x, g, wa, wb, *, tn):
    m, k = x.shape
    n = wa.shape[1]
    cols = lambda rows: pl.BlockSpec((rows, tn), lambda j: (0, j))
    return pl.pallas_call(
        _norm_matmul_pair_kernel,
        grid=(n // tn,),
        in_specs=[pl.BlockSpec((m, k), lambda j: (0, 0)), pl.BlockSpec((1, k), lambda j: (0, 0)), cols(k), cols(k)],
        out_specs=[cols(m), cols(m)],
        out_shape=[jax.ShapeDtypeStruct((m, n), F32)] * 2,
        scratch_shapes=[pltpu.VMEM((m, k), BF16)],
        compiler_params=_params("arbitrary"),
        name="norm_matmul_pair",
    )(x, g.reshape(1, k), wa, wb)


def _matmul_res_kernel(*refs, n_parts):
    a_refs, w_refs = refs[:n_parts], refs[n_parts:2 * n_parts]
    r_ref, o_ref = refs[2 * n_parts:]
    acc = r_ref[...]
    for a_ref, w_ref in zip(a_refs, w_refs):
        acc = acc + _dot(a_ref[...], w_ref[...])
    o_ref[...] = acc


def _matmul_res(a_parts, w, res, *, tm, tn):
    n_parts = len(a_parts)
    m, kp = a_parts[0].shape
    n = w.shape[1]
    a_specs = [pl.BlockSpec((tm, kp), lambda i, j: (i, 0)) for _ in a_parts]
    w_specs = [pl.BlockSpec((kp, tn), lambda i, j, p=p: (p, j)) for p in range(n_parts)]
    return pl.pallas_call(
        functools.partial(_matmul_res_kernel, n_parts=n_parts),
        grid=(m // tm, n // tn),
        in_specs=a_specs + w_specs + [pl.BlockSpec((tm, tn), lambda i, j: (i, j))],
        out_specs=pl.BlockSpec((tm, tn), lambda i, j: (i, j)),
        out_shape=jax.ShapeDtypeStruct((m, n), F32),
        compiler_params=_params("parallel", "arbitrary"),
        name="matmul_res",
    )(*a_parts, *([w] * n_parts), res)


def _matmul_res_norm_matmul_kernel(*refs, n_parts, n1):
    a_refs = refs[:n_parts]
    w_ref, r_ref, g_ref, x1_ref, q_ref, x1_s, xn_s = refs[n_parts:]
    j = pl.program_id(1)
    tn = w_ref.shape[1]
    kp = a_refs[0].shape[1]

    @pl.when(j < n1)
    def _():
        x1 = r_ref[...]
        for p, a_ref in enumerate(a_refs):
            x1 = x1 + _dot(a_ref[...], w_ref[p * kp:(p + 1) * kp, :])
        x1_ref[...] = x1
        x1_s[j] = x1

    @pl.when(j == n1)
    def _():
        ssq = sum(jnp.sum(jnp.square(x1_s[c]), axis=-1, keepdims=True) for c in range(n1))
        scale = lax.rsqrt(ssq * (1.0 / (n1 * tn)) + EPS)
        for c in range(n1):
            xn_s[:, c * tn:(c + 1) * tn] = (x1_s[c] * scale * g_ref[:, c * tn:(c + 1) * tn]).astype(xn_s.dtype)

    @pl.when(j >= n1)
    def _():
        q_ref[...] = _dot(xn_s[...], w_ref[...]).astype(q_ref.dtype)


def _matmul_res_norm_matmul(a_parts, w12, res, g, *, tm, tn, out2_dtype):
    n_parts = len(a_parts)
    m, kp = a_parts[0].shape
    d = res.shape[1]
    n1 = d // tn
    n2 = (w12.shape[1] - d) // tn
    return pl.pallas_call(
        functools.partial(_matmul_res_norm_matmul_kernel, n_parts=n_parts, n1=n1),
        grid=(m // tm, n1 + n2),
        in_specs=[pl.BlockSpec((tm, kp), lambda i, j: (i, 0))] * n_parts + [
            pl.BlockSpec((n_parts * kp, tn), lambda i, j: (0, j)),
            pl.BlockSpec((tm, tn), lambda i, j: (i, jnp.minimum(j, n1 - 1))),
            pl.BlockSpec((1, d), lambda i, j: (0, 0))],
        out_specs=[pl.BlockSpec((tm, tn), lambda i, j: (i, jnp.minimum(j, n1 - 1))),
                   pl.BlockSpec((tm, tn), lambda i, j: (i, jnp.maximum(j - n1, 0)))],
        out_shape=[jax.ShapeDtypeStruct((m, d), F32), jax.ShapeDtypeStruct((m, n2 * tn), out2_dtype)],
        scratch_shapes=[pltpu.VMEM((n1, tm, tn), F32), pltpu.VMEM((tm, d), BF16)],
        compiler_params=_params("parallel", "arbitrary"),
        name="matmul_res_norm_matmul",
    )(*a_parts, w12, res, g.reshape(1, d))


def _ffn_kernel(x_ref, g_ref, wu_ref, wd_ref, gf_ref, o_ref, xn_ref, *, final_norm, rider=None):
    f = pl.program_id(1)

    @pl.when(f == 0)
    def _():
        x = x_ref[...]
        xn_ref[...] = _rms(x, g_ref[...]).astype(BF16)
        o_ref[...] = x

    _run_rider_share(rider, 0, 1)
    h = _dot(xn_ref[...], wu_ref[...])
    h = jnp.square(jnp.maximum(h, 0.0)).astype(BF16)
    o_ref[...] += _dot(h, wd_ref[...])

    if final_norm:
        @pl.when(f == pl.num_programs(1) - 1)
        def _():
            o_ref[...] = _rms(o_ref[...], gf_ref[...])


def _ffn(x, g, w_up, w_down, g_final, *, tm, tf, final_norm, rider=None):
    m, d = x.shape
    dff = w_up.shape[1]
    r_in, r_out, r_shape, r_args = _rider_parts(rider)
    in_specs = [pl.BlockSpec((tm, d), lambda i, f: (i, 0)),
                pl.BlockSpec((1, d), lambda i, f: (0, 0)),
                pl.BlockSpec((d, tf), lambda i, f: (0, f)),
                pl.BlockSpec((tf, d), lambda i, f: (f, 0)),
                pl.BlockSpec((1, d), lambda i, f: (0, 0))]
    out_specs = [pl.BlockSpec((tm, d), lambda i, f: (i, 0))]
    return pl.pallas_call(
        _with_rider(functools.partial(_ffn_kernel, final_norm=final_norm), rider, len(in_specs), len(out_specs)),
        grid=(m // tm, dff // tf),
        in_specs=in_specs + r_in, out_specs=out_specs + r_out,
        out_shape=[jax.ShapeDtypeStruct((m, d), F32)] + r_shape,
        scratch_shapes=[pltpu.VMEM((tm, d), BF16)],
        compiler_params=_params("parallel", "arbitrary", vmem_limit_bytes=V7X_VMEM_LIMIT_HIGH_BYTES),
        name="ffn",
    )(x, g.reshape(1, d), w_up, w_down, g_final.reshape(1, d), *r_args)


def _lru_gates(xc, wg, bg, lam):
    g = _dot(xc.astype(BF16), wg) + bg
    r = jax.nn.sigmoid(g[:, :RNN_HD])
    i = jax.nn.sigmoid(g[:, RNN_HD:])
    log_a = -LRU_C * r * jax.nn.softplus(-lam)
    a = jnp.exp(log_a)
    u = jnp.sqrt(-jnp.tanh(log_a) * (a * a + 1.0)) * (i * xc)
    return a, u


def _rglru_kernel(xr_ref, gr_ref, h0_ref, c0_ref, cw_ref, cb_ref, wg_ref, bg_ref, lam_ref, gout_ref,
                  y_ref, hn_ref, cn_ref, xbuf, hc, ybuf, *, tc, rider=None):
    c = pl.program_id(1)

    @pl.when(c == 0)
    def _():
        xbuf[0:8, :] = jnp.zeros((8, D_RNN), F32)
        xbuf[5:8, :] = c0_ref[0]
        hc[...] = h0_ref[0]

    @pl.when(c > 0)
    def _():
        xbuf[0:8, :] = xbuf[tc:tc + 8, :]

    xbuf[8:8 + tc, :] = xr_ref[0]
    ng = tc // SUBLANES
    sub = lax.broadcasted_iota(jnp.int32, (ng, SUBLANES, RNN_HD), 1)
    ssq = jnp.zeros((tc, 1), F32)
    for h in range(RNN_HEADS):
        _run_rider_share(rider, h, RNN_HEADS)
        sl = slice(h * RNN_HD, (h + 1) * RNN_HD)
        xc = cb_ref[:, sl]
        for j in range(CONV_W):
            xc = xc + cw_ref[j:j + 1, sl] * xbuf[5 + j:5 + j + tc, sl]
        a, u = _lru_gates(xc, wg_ref[h], bg_ref[h], lam_ref[:, sl])
        a = a.reshape(ng, SUBLANES, RNN_HD)
        u = u.reshape(ng, SUBLANES, RNN_HD)
        d = 1
        while d < SUBLANES:
            keep = sub >= d
            a_prev = jnp.where(keep, pltpu.roll(a, d, 1), 1.0)
            u_prev = jnp.where(keep, pltpu.roll(u, d, 1), 0.0)
            u = u + a * u_prev
            a = a * a_prev
            d *= 2
        carry = hc[:, sl]
        for g in range(ng):
            hg = u[g] + a[g] * carry
            carry = hg[SUBLANES - 1:SUBLANES, :]
            ybuf[g * SUBLANES:(g + 1) * SUBLANES, sl] = hg
        hc[:, sl] = carry
        yv = ybuf[:, sl] * jax.nn.gelu(gr_ref[0, :, sl])
        ssq = ssq + jnp.sum(yv * yv, axis=-1, keepdims=True)
        ybuf[:, sl] = yv
    y = ybuf[...] * lax.rsqrt(ssq * (1.0 / D_RNN) + EPS) * gout_ref[...]
    y_ref[0] = y.astype(y_ref.dtype)

    @pl.when(c == pl.num_programs(1) - 1)
    def _():
        hn_ref[0] = hc[...]
        cn_ref[0] = xbuf[tc + 5:tc + 8, :]


def _rglru_prompt(z3, h0, conv0, w, *, tc, rider=None, vmem_limit_bytes=V7X_VMEM_LIMIT_BYTES):
    b, t, _ = z3.shape
    full2 = lambda bi, ci: (0, 0)
    r_in, r_out, r_shape, r_args = _rider_parts(rider)
    in_specs = [pl.BlockSpec((1, tc, D_RNN), lambda bi, ci: (bi, ci, 0)),
                pl.BlockSpec((1, tc, D_RNN), lambda bi, ci: (bi, ci, 1)),
                pl.BlockSpec((1, 1, D_RNN), lambda bi, ci: (bi, 0, 0)),
                pl.BlockSpec((1, CONV_W - 1, D_RNN), lambda bi, ci: (bi, 0, 0)),
                pl.BlockSpec((CONV_W, D_RNN), full2),
                pl.BlockSpec((1, D_RNN), full2),
                pl.BlockSpec((RNN_HEADS, RNN_HD, 2 * RNN_HD), lambda bi, ci: (0, 0, 0)),
                pl.BlockSpec((RNN_HEADS, 1, 2 * RNN_HD), lambda bi, ci: (0, 0, 0)),
                pl.BlockSpec((1, D_RNN), full2),
                pl.BlockSpec((1, D_RNN), full2)]
    out_specs = [pl.BlockSpec((1, tc, D_RNN), lambda bi, ci: (bi, ci, 0)),
                 pl.BlockSpec((1, 1, D_RNN), lambda bi, ci: (bi, 0, 0)),
                 pl.BlockSpec((1, CONV_W - 1, D_RNN), lambda bi, ci: (bi, 0, 0))]
    out_shape = [jax.ShapeDtypeStruct((b, t, D_RNN), BF16),
                 jax.ShapeDtypeStruct((b, 1, D_RNN), F32),
                 jax.ShapeDtypeStruct((b, CONV_W - 1, D_RNN), F32)]
    return pl.pallas_call(
        _with_rider(functools.partial(_rglru_kernel, tc=tc), rider, len(in_specs), len(out_specs)),
        grid=(b, t // tc),
        in_specs=in_specs + r_in, out_specs=out_specs + r_out, out_shape=out_shape + r_shape,
        scratch_shapes=[pltpu.VMEM((tc + 8, D_RNN), F32),
                        pltpu.VMEM((1, D_RNN), F32),
                        pltpu.VMEM((tc, D_RNN), F32)],
        compiler_params=_params("parallel", "arbitrary", vmem_limit_bytes=vmem_limit_bytes),
        name="rglru_prompt",
    )(z3, z3, h0, conv0, w['conv_rnn_w'], w['conv_rnn_b'], w['lru_wg'], w['lru_bg'],
      w['lru_lambda'], w['g_rnn_out'], *r_args)


def _mlstm_kernel(u_ref, v_ref, og_ref, zg_ref, gb_ref, c0_ref, n0_ref, m0_ref, cv0_ref,
                  cw_ref, cb_ref, wqk_ref, gout_ref,
                  y_ref, cn_ref, nn_ref, mn_ref, cvn_ref, ubuf, cst, nst, mst, rider=None):
    c = pl.program_id(1)
    L = CHUNK
    n_seq = u_ref.shape[0]

    @pl.when(c == 0)
    def _():
        ubuf[:, 0:8, :] = jnp.zeros((n_seq, 8, D_MLSTM), F32)
        ubuf[:, 5:8, :] = cv0_ref[...]
        cst[...] = c0_ref[...]
        nst[...] = n0_ref[...]
        mst[...] = m0_ref[...]

    @pl.when(c > 0)
    def _():
        ubuf[:, 0:8, :] = ubuf[:, L:L + 8, :]

    ubuf[:, 8:8 + L, :] = u_ref[...]

    ti = lax.broadcasted_iota(jnp.int32, (L, L), 0)
    si = lax.broadcasted_iota(jnp.int32, (L, L), 1)
    causal = si <= ti
    anti = ti <= si
    zgs = [zg_ref[s] + gb_ref[...] for s in range(n_seq)]
    zgts = [zg.T for zg in zgs]

    for h in range(MLSTM_HEADS):
        _run_rider_share(rider, h, MLSTM_HEADS)
        sl = slice(h * MLSTM_HD, (h + 1) * MLSTM_HD)
        for s in range(n_seq):
            zg, zgt = zgs[s], zgts[s]
            uc = cb_ref[:, sl]
            for j in range(CONV_W):
                uc = uc + cw_ref[j:j + 1, sl] * ubuf[s, 5 + j:5 + j + L, sl]
            uc = jax.nn.silu(uc)
            qk = _dot(uc.astype(BF16), wqk_ref[h])
            q = qk[:, :MLSTM_HD]
            k = qk[:, MLSTM_HD:] * (MLSTM_HD ** -0.5)
            v = v_ref[s, :, sl]
            qb, kb = q.astype(BF16), k.astype(BF16)

            icol = zg[:, h:h + 1]
            fcol = jax.nn.log_sigmoid(zg[:, MLSTM_HEADS + h:MLSTM_HEADS + h + 1])
            irow = zgt[h:h + 1, :]
            frow = jax.nn.log_sigmoid(zgt[MLSTM_HEADS + h:MLSTM_HEADS + h + 1, :])
            bcol = jnp.sum(jnp.where(causal, frow, 0.0), axis=1, keepdims=True)
            brow = jnp.sum(jnp.where(anti, fcol, 0.0), axis=0, keepdims=True)
            dmat = jnp.where(causal, irow + bcol - brow, -jnp.inf)
            m_prev = mst[s, h, :, 0:1]
            inter = bcol + m_prev
            m_t = jnp.maximum(inter, jnp.max(dmat, axis=1, keepdims=True))
            w_inter = jnp.exp(inter - m_t)
            sc = lax.dot_general(qb, kb, (((1,), (1,)), ((), ())), preferred_element_type=F32) * jnp.exp(dmat - m_t)
            cmat = cst[s, h]
            nrow = nst[s, h]
            cq = lax.dot_general(qb, cmat.astype(BF16), (((1,), (1,)), ((), ())), preferred_element_type=F32)
            num = w_inter * cq + _dot(sc.astype(BF16), v.astype(BF16))
            den = w_inter * jnp.sum(q * nrow, axis=1, keepdims=True) + jnp.sum(sc, axis=1, keepdims=True)
            hm = num / jnp.maximum(jnp.abs(den), jnp.exp(-m_t))

            m_new = m_t[L - 1:L, :]
            b_last = bcol[L - 1:L, :]
            g_state = jnp.exp(b_last + m_prev - m_new)
            g_in = jnp.exp(icol + b_last - bcol - m_new)
            cst[s, h] = g_state * cmat + lax.dot_general((g_in * v).astype(BF16), kb, (((0,), (0,)), ((), ())),
                                                         preferred_element_type=F32)
            nst[s, h] = g_state * nrow + jnp.sum(g_in * k, axis=0, keepdims=True)
            mst[s, h] = jnp.broadcast_to(m_new, (1, N_GATE_PAD))

            y = _rms(hm, gout_ref[...]) * jax.nn.sigmoid(og_ref[s, :, sl])
            y_ref[s, :, sl] = y.astype(y_ref.dtype)

    @pl.when(c == pl.num_programs(1) - 1)
    def _():
        cn_ref[...] = cst[...]
        nn_ref[...] = nst[...]
        mn_ref[...] = mst[...]
        cvn_ref[...] = ubuf[:, L + 5:L + 8, :]


MLSTM_SEQS = 1


def _mlstm_prompt(z3, zg3, c0, n0, m0, conv0, w, *, rider=None):
    b, t, _ = z3.shape
    nc = t // CHUNK
    ns = MLSTM_SEQS
    u_blk, v_blk, o_blk = 2 * D_RNN // D_MLSTM, 3 * D_RNN // D_MLSTM, 4 * D_RNN // D_MLSTM
    zcol = lambda blk: pl.BlockSpec((ns, CHUNK, D_MLSTM), lambda bi, ci: (bi, ci, blk))
    state4 = lambda *minor: pl.BlockSpec((ns, MLSTM_HEADS) + minor, lambda bi, ci: (bi, 0, 0, 0))
    conv_spec = pl.BlockSpec((ns, CONV_W - 1, D_MLSTM), lambda bi, ci: (bi, 0, 0))
    const2 = lambda shape: pl.BlockSpec(shape, lambda bi, ci: (0, 0))
    r_in, r_out, r_shape, r_args = _rider_parts(rider)
    in_specs = [zcol(u_blk), zcol(v_blk), zcol(o_blk),
                pl.BlockSpec((ns, CHUNK, N_GATE_PAD), lambda bi, ci: (bi, ci, 0)),
                const2((1, N_GATE_PAD)),
                state4(MLSTM_HD, MLSTM_HD), state4(1, MLSTM_HD), state4(1, N_GATE_PAD), conv_spec,
                const2((CONV_W, D_MLSTM)), const2((1, D_MLSTM)),
                pl.BlockSpec((MLSTM_HEADS, MLSTM_HD, 2 * MLSTM_HD), lambda bi, ci: (0, 0, 0)),
                const2((1, MLSTM_HD))]
    out_specs = [pl.BlockSpec((ns, CHUNK, D_MLSTM), lambda bi, ci: (bi, ci, 0)),
                 state4(MLSTM_HD, MLSTM_HD), state4(1, MLSTM_HD), state4(1, N_GATE_PAD), conv_spec]
    out_shape = [jax.ShapeDtypeStruct((b, t, D_MLSTM), BF16),
                 jax.ShapeDtypeStruct((b, MLSTM_HEADS, MLSTM_HD, MLSTM_HD), F32),
                 jax.ShapeDtypeStruct((b, MLSTM_HEADS, 1, MLSTM_HD), F32),
                 jax.ShapeDtypeStruct((b, MLSTM_HEADS, 1, N_GATE_PAD), F32),
                 jax.ShapeDtypeStruct((b, CONV_W - 1, D_MLSTM), F32)]
    return pl.pallas_call(
        _with_rider(_mlstm_kernel, rider, len(in_specs), len(out_specs)),
        grid=(b // ns, nc),
        in_specs=in_specs + r_in, out_specs=out_specs + r_out, out_shape=out_shape + r_shape,
        scratch_shapes=[pltpu.VMEM((ns, CHUNK + 8, D_MLSTM), F32),
                        pltpu.VMEM((ns, MLSTM_HEADS, MLSTM_HD, MLSTM_HD), F32),
                        pltpu.VMEM((ns, MLSTM_HEADS, 1, MLSTM_HD), F32),
                        pltpu.VMEM((ns, MLSTM_HEADS, 1, N_GATE_PAD), F32)],
        compiler_params=_params("parallel", "arbitrary"),
        name="mlstm_prompt",
    )(z3, z3, z3, zg3, w['gate_bias'], c0, n0, m0, conv0, w['conv_ml_w'], w['conv_ml_b'],
      w['ml_wqk'], w['g_ml_out'], *r_args)


def _xattn_kernel(q_ref, k_ref, v_ref, x_ref, wo_ref, o_ref, att, rider=None):
    @pl.when(pl.program_id(2) == 0)
    def _():
        for h in range(X_HEADS):
            sl = slice(h * X_HD, (h + 1) * X_HD)
            s = _dot_t(q_ref[0, :, sl], k_ref[0, :, sl]) * (X_HD ** -0.5)
            e = jnp.exp(s - jnp.max(s, axis=-1, keepdims=True))
            p = e / jnp.sum(e, axis=-1, keepdims=True)
            att[:, sl] = _dot(p, v_ref[0, :, sl]).astype(att.dtype)

    _run_rider_share(rider, 0, 1)
    o_ref[0] = x_ref[0] + _dot(att[...], wo_ref[...])


def _xattn_prompt(q3, mk3, mv3, x3, w_co, *, tq, tn, rider=None):
    b, t, d = q3.shape
    mem = pl.BlockSpec((1, N_MEM, d), lambda bi, ti, j: (bi, 0, 0))
    cols = pl.BlockSpec((1, tq, tn), lambda bi, ti, j: (bi, ti, j))
    r_in, r_out, r_shape, r_args = _rider_parts(rider)
    in_specs = [pl.BlockSpec((1, tq, d), lambda bi, ti, j: (bi, ti, 0)), mem, mem, cols,
                pl.BlockSpec((d, tn), lambda bi, ti, j: (0, j))]
    return pl.pallas_call(
        _with_rider(_xattn_kernel, rider, len(in_specs), 1),
        grid=(b, t // tq, d // tn),
        in_specs=in_specs + r_in, out_specs=[cols] + r_out,
        out_shape=[jax.ShapeDtypeStruct((b, t, d), F32)] + r_shape,
        scratch_shapes=[pltpu.VMEM((tq, d), BF16)],
        compiler_params=_params("parallel", "parallel", "arbitrary"),
        name="xattn_prompt",
    )(q3, mk3, mv3, x3, w_co, *r_args)


XS_SUB = 2 * X_HEADS
XS_PAIRS = X_HD // (2 * 128)


def _pack_heads(x):
    lead = x.shape[:-1]
    x = x.reshape(*lead, X_HEADS, 2 * XS_PAIRS, 128)
    x = jnp.swapaxes(x, -3, -2)
    return x.reshape(*lead, XS_PAIRS, XS_SUB, 128)


def _unpack_heads(x):
    lead = x.shape[:-3]
    x = x.reshape(*lead, 2 * XS_PAIRS, X_HEADS, 128)
    x = jnp.swapaxes(x, -3, -2)
    return x.reshape(*lead, X_HEADS * X_HD)


def _xattn_sample_units(q_ref, k_ref, v_ref, o_ref, r):
    def row():
        t = jnp.sum(k_ref[r] * q_ref[r], axis=1)
        t = t + pltpu.roll(t, X_HEADS, 1)
        s = jnp.sum(t, axis=-1, keepdims=True) * (X_HD ** -0.5)
        e = jnp.exp(s - jnp.max(s, axis=0, keepdims=True))
        p = e / jnp.sum(e, axis=0, keepdims=True)
        o_ref[r] = jnp.sum(p[:, None] * v_ref[r], axis=0)

    return [row]


def _xattn_sample_rider(grid, q, ck, cv):
    b = q.shape[0]
    n_steps, step = _grid_steps(grid)
    rows = b // n_steps
    kv_spec = pl.BlockSpec((rows, N_MEM, XS_PAIRS, XS_SUB, 128), lambda *g: (step(*g), 0, 0, 0, 0))
    q_spec = pl.BlockSpec((rows, XS_PAIRS, XS_SUB, 128), lambda *g: (step(*g), 0, 0, 0))
    return dict(make_units=_per_row_units(_xattn_sample_units, rows),
                in_specs=[q_spec, kv_spec, kv_spec], out_specs=[q_spec],
                out_shape=[jax.ShapeDtypeStruct((b, XS_PAIRS, XS_SUB, 128), F32)],
                args=[_pack_heads(q), _pack_heads(ck.reshape(b, N_MEM, X_HEADS * X_HD)),
                      _pack_heads(cv.reshape(b, N_MEM, X_HEADS * X_HD))])


def _smix_rows_kernel(z_ref, h0_ref, rc_ref, mc_ref, cwr_ref, cbr_ref, wg_ref, bg_ref, lam_ref, gout_ref,
                      cwm_ref, cbm_ref, wqk_ref,
                      yr_ref, hn_ref, rcn_ref, mcn_ref, q_ref, k_ref, ybuf):
    nb = z_ref.shape[0]
    ssq = jnp.zeros((nb, 1), F32)
    for h in range(RNN_HEADS):
        sl = slice(h * RNN_HD, (h + 1) * RNN_HD)
        xc = cbr_ref[:, sl] + cwr_ref[CONV_W - 1:CONV_W, sl] * z_ref[:, sl]
        for j in range(CONV_W - 1):
            xc = xc + cwr_ref[j:j + 1, sl] * rc_ref[:, j * D_RNN + h * RNN_HD:j * D_RNN + (h + 1) * RNN_HD]
        a, u = _lru_gates(xc, wg_ref[h], bg_ref[h], lam_ref[:, sl])
        hs = a * h0_ref[:, sl] + u
        hn_ref[:, sl] = hs
        yv = hs * jax.nn.gelu(z_ref[:, D_RNN + h * RNN_HD:D_RNN + (h + 1) * RNN_HD])
        ssq = ssq + jnp.sum(yv * yv, axis=-1, keepdims=True)
        ybuf[:, sl] = yv
    yr_ref[...] = (ybuf[...] * lax.rsqrt(ssq * (1.0 / D_RNN) + EPS) * gout_ref[...]).astype(yr_ref.dtype)
    rcn_ref[:, 0:2 * D_RNN] = rc_ref[:, D_RNN:3 * D_RNN]
    rcn_ref[:, 2 * D_RNN:3 * D_RNN] = z_ref[:, 0:D_RNN]

    for h in range(MLSTM_HEADS):
        sl = slice(h * MLSTM_HD, (h + 1) * MLSTM_HD)
        uc = cbm_ref[:, sl] + cwm_ref[CONV_W - 1:CONV_W, sl] * z_ref[:, 2 * D_RNN + h * MLSTM_HD:2 * D_RNN + (h + 1) * MLSTM_HD]
        for j in range(CONV_W - 1):
            uc = uc + cwm_ref[j:j + 1, sl] * mc_ref[:, j * D_MLSTM + h * MLSTM_HD:j * D_MLSTM + (h + 1) * MLSTM_HD]
        uc = jax.nn.silu(uc)
        qk = _dot(uc.astype(BF16), wqk_ref[h])
        q_ref[:, sl] = qk[:, :MLSTM_HD]
        k_ref[:, sl] = qk[:, MLSTM_HD:] * (MLSTM_HD ** -0.5)
    mcn_ref[:, 0:2 * D_MLSTM] = mc_ref[:, D_MLSTM:3 * D_MLSTM]
    mcn_ref[:, 2 * D_MLSTM:3 * D_MLSTM] = z_ref[:, 2 * D_RNN:2 * D_RNN + D_MLSTM]


def _smix_rows(z, h0, rconv, mconv, w):
    nb = z.shape[0]
    outs = [jax.ShapeDtypeStruct((nb, D_RNN), BF16),
            jax.ShapeDtypeStruct((nb, D_RNN), F32),
            jax.ShapeDtypeStruct((nb, 3 * D_RNN), F32),
            jax.ShapeDtypeStruct((nb, 3 * D_MLSTM), F32),
            jax.ShapeDtypeStruct((nb, D_MLSTM), F32),
            jax.ShapeDtypeStruct((nb, D_MLSTM), F32)]
    return pl.pallas_call(
        _smix_rows_kernel,
        out_shape=outs,
        scratch_shapes=[pltpu.VMEM((nb, D_RNN), F32)],
        compiler_params=pltpu.CompilerParams(vmem_limit_bytes=V7X_VMEM_LIMIT_BYTES),
        name="smix_rows",
    )(z, h0, rconv, mconv, w['conv_rnn_w'], w['conv_rnn_b'], w['lru_wg'], w['lru_bg'], w['lru_lambda'],
      w['g_rnn_out'], w['conv_ml_w'], w['conv_ml_b'], w['ml_wqk'])


MXU_MIN_ROWS = 8


def _smix_state_units(c_ref, q_ref, k_ref, n_ref, v_ref, og_ref, zg_ref, gb_ref, m_ref, gout_ref,
                      cn_ref, nn_ref, mn_ref, y_ref, r):
    lane_g = lax.broadcasted_iota(jnp.int32, (1, N_GATE_PAD), 1)
    first_row = lax.broadcasted_iota(jnp.int32, (MXU_MIN_ROWS, MLSTM_HD), 0) == 0
    rr = slice(r, r + 1)

    def head(h, zg, m_out):
        sl = slice(h * MLSTM_HD, (h + 1) * MLSTM_HD)
        q = q_ref[rr, sl]
        k = k_ref[rr, sl]
        n = n_ref[rr, sl]
        v = v_ref[rr, sl]
        ig = zg[:, h:h + 1]
        lf = jax.nn.log_sigmoid(zg[:, MLSTM_HEADS + h:MLSTM_HEADS + h + 1])
        m_prev = m_ref[rr, h:h + 1]
        m_t = jnp.maximum(lf + m_prev, ig)
        w_inter = jnp.exp(lf + m_prev - m_t)
        g_in = jnp.exp(ig - m_t)
        s = jnp.sum(q * k, axis=1, keepdims=True) * g_in
        den = w_inter * jnp.sum(n * q, axis=1, keepdims=True) + s
        denom = jnp.maximum(jnp.abs(den), jnp.exp(-m_t))
        cmat = c_ref[r, h]
        q8 = jnp.broadcast_to(q, (MXU_MIN_ROWS, MLSTM_HD)).astype(BF16)
        cq = lax.dot_general(q8, cmat.astype(BF16), (((1,), (1,)), ((), ())),
                             preferred_element_type=F32)[0:1, :]
        hm = (w_inter * cq + s * v) / denom
        gv8 = jnp.where(first_row, jnp.broadcast_to(g_in * v, (MXU_MIN_ROWS, MLSTM_HD)), 0.0).astype(BF16)
        k8 = jnp.broadcast_to(k, (MXU_MIN_ROWS, MLSTM_HD)).astype(BF16)
        outer = lax.dot_general(gv8, k8, (((0,), (0,)), ((), ())), preferred_element_type=F32)
        cn_ref[r, h] = w_inter * cmat + outer
        nn_ref[rr, sl] = w_inter * n + g_in * k
        y = _rms(hm, gout_ref[...]) * jax.nn.sigmoid(og_ref[rr, sl])
        y_ref[rr, sl] = y.astype(y_ref.dtype)
        return jnp.where(lane_g == h, m_t, m_out)

    def row():
        zg = zg_ref[rr, :] + gb_ref[...]
        m_out = jnp.zeros((1, N_GATE_PAD), F32)
        for h in range(MLSTM_HEADS):
            m_out = head(h, zg, m_out)
        mn_ref[rr, :] = m_out

    return [row]


def _smix_state_rider(grid, c, q, k, n, z, zg, gate_bias, m, gout):
    nb = c.shape[0]
    n_steps, step = _grid_steps(grid)
    rows = nb // n_steps
    assert rows % SUBLANES == 0
    row = lambda width, col=0: pl.BlockSpec((rows, width), lambda *g: (step(*g), col))
    cblk = pl.BlockSpec((rows, MLSTM_HEADS, MLSTM_HD, MLSTM_HD), lambda *g: (step(*g), 0, 0, 0))
    return dict(
        make_units=_per_row_units(_smix_state_units, rows),
        in_specs=[cblk, row(D_MLSTM), row(D_MLSTM), row(D_MLSTM),
                  row(D_MLSTM, 3 * D_RNN // D_MLSTM), row(D_MLSTM, 4 * D_RNN // D_MLSTM),
                  row(N_GATE_PAD),
                  pl.BlockSpec((1, N_GATE_PAD), lambda *g: (0, 0)),
                  row(MLSTM_HEADS),
                  pl.BlockSpec((1, MLSTM_HD), lambda *g: (0, 0))],
        out_specs=[cblk, row(D_MLSTM), row(N_GATE_PAD), row(D_MLSTM)],
        out_shape=[jax.ShapeDtypeStruct(c.shape, F32),
                   jax.ShapeDtypeStruct((nb, D_MLSTM), F32),
                   jax.ShapeDtypeStruct((nb, N_GATE_PAD), F32),
                   jax.ShapeDtypeStruct((nb, D_MLSTM), F32)],
        args=[c, q, k, n, z, z, zg, gate_bias, m, gout])


RGLRU_CHUNK = 512
PROMPT_TM = 1024
PROMPT_W_IN_TN = 1280


def _layer(xp, xs, mem, rg_h, rg_conv, c, n, mst, ml_conv, ck, cv, w, g_final, final_norm):
    b, t, d = xp.shape
    nb = xs.shape[0]
    m = b * t
    tm = PROMPT_TM
    xp2 = xp.reshape(m, d)
    xs2 = xs.reshape(nb, d)
    w_in_args = dict(tn=1024, out_dtype=F32, w_extra=w['w_in_gate'], n_cols=D_MAIN, w_is_transposed=True)

    zs, zgs, w_in_bf = _norm_matmul(xs2, w['g_mix'], w['w_in_main'], tm=nb, emit_w_bf16=True, **w_in_args)
    ys_rnn, s_h, s_rconv, s_mconv, qs, ks = _smix_rows(
        zs, rg_h, rg_conv.reshape(nb, 3 * D_RNN), ml_conv.reshape(nb, 3 * D_MLSTM), w)

    z, zg = _norm_matmul(xp2, w['g_mix'], w_in_bf, tm=tm, **dict(w_in_args, tn=PROMPT_W_IN_TN))
    z3 = z.reshape(b, t, D_MAIN)
    zg3 = zg.reshape(b, t, N_GATE_PAD)
    h0 = jnp.zeros((b, 1, D_RNN), F32)
    conv0 = jnp.zeros((b, CONV_W - 1, D_RNN), F32)
    rg_grid = (b, t // RGLRU_CHUNK)
    state_rider = _smix_state_rider(rg_grid, c, qs, ks, n.reshape(nb, D_MLSTM), zs, zgs, w['gate_bias'], mst,
                                    w['g_ml_out'])
    rg_rider = _merge_riders(state_rider, _cast_rider(rg_grid, [[w['w_out'], w['w_cq']], [w['w_co']]]))
    y_rnn, p_h, p_rconv, s_c, s_n, s_m, ys_ml, w_out_cq, w_co = _rglru_prompt(
        z3, h0, conv0, w, tc=RGLRU_CHUNK, rider=rg_rider, vmem_limit_bytes=V7X_VMEM_LIMIT_HIGH_BYTES)

    xs1, qx = _matmul_res_norm_matmul([ys_rnn, ys_ml.astype(BF16)], w_out_cq, xs2,
                                      w['g_xattn'], tm=nb, tn=1024, out2_dtype=F32)

    c0 = jnp.zeros((b, MLSTM_HEADS, MLSTM_HD, MLSTM_HD), F32)
    n0 = jnp.zeros((b, MLSTM_HEADS, 1, MLSTM_HD), F32)
    m0 = jnp.zeros((b, MLSTM_HEADS, 1, N_GATE_PAD), F32)
    mconv0 = jnp.zeros((b, CONV_W - 1, D_MLSTM), F32)
    y_ml, p_c, p_n, p_m, p_mconv, w_up, w_down = _mlstm_prompt(
        z3, zg3, c0, n0, m0, mconv0, w,
        rider=_cast_rider((b // MLSTM_SEQS, t // CHUNK), [[w['w_up']], [w['w_down']]]))

    x1, q = _matmul_res_norm_matmul([y_rnn.reshape(m, D_RNN), y_ml.reshape(m, D_MLSTM)], w_out_cq, xp2,
                                    w['g_xattn'], tm=tm, tn=1024, out2_dtype=BF16)
    mem2 = mem.reshape(b * N_MEM, d)
    mk, mv = _norm_matmul_pair(mem2, w['g_mem'], w['w_mk'], w['w_mv'], tn=512)
    x2, = _xattn_prompt(q.reshape(b, t, d), mk.reshape(b, N_MEM, d), mv.reshape(b, N_MEM, d), x1.reshape(b, t, d),
                        w_co, tq=tm, tn=1024)
    ffn_tf = 512
    xp_out, os_packed = _ffn(x2.reshape(m, d), w['g_ffn'], w_up, w_down, g_final, tm=tm, tf=ffn_tf,
                             final_norm=final_norm,
                             rider=_xattn_sample_rider((m // tm, D_FF // ffn_tf), qx, ck, cv))

    os_ = _unpack_heads(os_packed).astype(BF16)
    xs2_ = _matmul_res([os_], w_co, xs1, tm=nb, tn=1024)
    xs_out, = _ffn(xs2_, w['g_ffn'], w_up, w_down, g_final, tm=nb, tf=ffn_tf, final_norm=final_norm)

    new_p = (p_h.reshape(b, D_RNN), p_rconv, p_c, p_n.reshape(b, MLSTM_HEADS, MLSTM_HD), p_m[:, :, 0, 0], p_mconv,
             mk.reshape(b, N_MEM, X_HEADS, X_HD), mv.reshape(b, N_MEM, X_HEADS, X_HD))
    new_s = (s_h, s_rconv.reshape(nb, CONV_W - 1, D_RNN), s_c, s_n.reshape(nb, MLSTM_HEADS, MLSTM_HD),
             s_m[:, :MLSTM_HEADS], s_mconv.reshape(nb, CONV_W - 1, D_MLSTM))
    return xp_out.reshape(b, t, d), xs_out.reshape(nb, 1, d), new_p, new_s


def _layer_weights(l, g_mix, w_in, conv_rnn_w, conv_rnn_b, lru_wa, lru_ba, lru_wx, lru_bx, lru_lambda,
                   g_rnn_out, conv_ml_w, conv_ml_b, ml_wq, ml_wk, ml_bi, ml_bf, g_ml_out, w_out,
                   g_xattn, g_mem, w_cq, w_mk, w_mv, w_co, g_ffn, w_up, w_down):
    n_gate = 2 * MLSTM_HEADS
    w_in_t = jnp.swapaxes(w_in[l], 0, 1)
    w_gate_t = jnp.pad(w_in_t[D_MAIN:], ((0, N_GATE_PAD - n_gate), (0, 0)))
    gate_bias = jnp.pad(jnp.concatenate([ml_bi[l], ml_bf[l]]), (0, N_GATE_PAD - n_gate))
    return dict(
        g_mix=g_mix[l], w_in_main=w_in_t, w_in_gate=w_gate_t.astype(BF16),
        conv_rnn_w=conv_rnn_w[l], conv_rnn_b=conv_rnn_b[l].reshape(1, D_RNN),
        lru_wg=jnp.concatenate([lru_wa[l], lru_wx[l]], axis=-1).astype(BF16),
        lru_bg=jnp.concatenate([lru_ba[l], lru_bx[l]], axis=-1).reshape(RNN_HEADS, 1, 2 * RNN_HD),
        lru_lambda=lru_lambda[l].reshape(1, D_RNN), g_rnn_out=g_rnn_out[l].reshape(1, D_RNN),
        conv_ml_w=conv_ml_w[l], conv_ml_b=conv_ml_b[l].reshape(1, D_MLSTM),
        ml_wqk=jnp.concatenate([ml_wq[l], ml_wk[l]], axis=-1).astype(BF16),
        gate_bias=gate_bias.reshape(1, N_GATE_PAD), g_ml_out=g_ml_out[l].reshape(1, MLSTM_HD),
        w_out=w_out[l], w_cq=w_cq[l], g_xattn=g_xattn[l], g_mem=g_mem[l], w_mk=w_mk[l], w_mv=w_mv[l],
        w_co=w_co[l], g_ffn=g_ffn[l], w_up=w_up[l], w_down=w_down[l])


def kernel(x_prompt, x_sample, mem_prompt, state_rglru_h, state_rglru_conv, state_mlstm_C, state_mlstm_n, state_mlstm_m, state_mlstm_conv, cache_mem_k, cache_mem_v, g_mix, w_in, conv_rnn_w, conv_rnn_b, lru_wa, lru_ba, lru_wx, lru_bx, lru_lambda, g_rnn_out, conv_ml_w, conv_ml_b, ml_wq, ml_wk, ml_bi, ml_bf, g_ml_out, w_out, g_xattn, g_mem, w_cq, w_mk, w_mv, w_co, g_ffn, w_up, w_down, g_final):
    depth = w_in.shape[0]
    xp, xs = x_prompt, x_sample
    p_out = [[] for _ in range(8)]
    s_out = [[] for _ in range(6)]
    for l in range(depth):
        w = _layer_weights(l, g_mix, w_in, conv_rnn_w, conv_rnn_b, lru_wa, lru_ba, lru_wx, lru_bx, lru_lambda,
                           g_rnn_out, conv_ml_w, conv_ml_b, ml_wq, ml_wk, ml_bi, ml_bf, g_ml_out, w_out,
                           g_xattn, g_mem, w_cq, w_mk, w_mv, w_co, g_ffn, w_up, w_down)
        last = l == depth - 1
        xp, xs, new_p, new_s = _layer(xp, xs, mem_prompt, state_rglru_h[l], state_rglru_conv[l], state_mlstm_C[l],
                                      state_mlstm_n[l], state_mlstm_m[l], state_mlstm_conv[l], cache_mem_k[l],
                                      cache_mem_v[l], w, g_final, last)
        for j, a in enumerate(new_p):
            p_out[j].append(a)
        for j, a in enumerate(new_s):
            s_out[j].append(a)
    P = [jnp.stack(a, axis=0) for a in p_out]
    S = [jnp.stack(a, axis=0) for a in s_out]
    return (xp, xs, P[0], P[1], P[2], P[3], P[4], P[5], P[6], P[7],
            S[0], S[1], S[2], S[3], S[4], S[5])
```

```python
import functools

import jax
import jax.numpy as jnp
from jax import lax
from jax.experimental import pallas as pl
from jax.experimental.pallas import tpu as pltpu

F32 = jnp.float32
BF16 = jnp.bfloat16

D_MODEL = 2048
D_RNN = 1024
RNN_HEADS = 8
RNN_HD = 128
CONV_W = 4
LRU_C = 8.0
D_MLSTM = 1024
MLSTM_HEADS = 4
MLSTM_HD = 256
CHUNK = 128
N_MEM = 256
X_HEADS = 4
X_HD = 512
D_FF = 8192
EPS = 1e-6
D_MAIN = 5 * 1024
N_GATE_PAD = 128

V7X_VMEM_LIMIT_BYTES = 56 * 1024 * 1024
V7X_VMEM_LIMIT_FFN_BYTES = 63 * 1024 * 1024
SUBLANES = 8


def _params(*sem, vmem_limit_bytes=V7X_VMEM_LIMIT_BYTES):
    return pltpu.CompilerParams(dimension_semantics=sem, vmem_limit_bytes=vmem_limit_bytes)


def _rms(x, g):
    ms = jnp.mean(x * x, axis=-1, keepdims=True)
    return x * lax.rsqrt(ms + EPS) * g


def _dot(a, b):
    return jnp.dot(a.astype(BF16), b.astype(BF16), preferred_element_type=F32)


def _dot_t(a, bt):
    return lax.dot_general(a.astype(BF16), bt.astype(BF16), (((1,), (1,)), ((), ())), preferred_element_type=F32)


def _with_rider(host_body, rider, n_in, n_out):
    if rider is None:
        return host_body
    r_in, r_out = len(rider['in_specs']), len(rider['out_specs'])

    def body(*refs):
        host_in, refs = refs[:n_in], refs[n_in:]
        rider_in, refs = refs[:r_in], refs[r_in:]
        host_out, refs = refs[:n_out], refs[n_out:]
        rider_out, scratch = refs[:r_out], refs[r_out:]
        host_body(*host_in, *host_out, *scratch, rider=rider['make_units'](*rider_in, *rider_out))
    return body


def _grid_steps(grid):
    n = 1
    for g in grid:
        n *= g

    def step(*idx):
        s = idx[0]
        for g, i in zip(grid[1:], idx[1:]):
            s = s * g + i
        return s
    return n, step


def _per_row_units(row_units, rows):
    return lambda *refs: [u for r in range(rows) for u in row_units(*refs, r)]


def _merge_riders(a, b):
    na_in, na_out = len(a['in_specs']), len(a['out_specs'])
    nb_in = len(b['in_specs'])

    def make_units(*refs):
        a_in, refs = refs[:na_in], refs[na_in:]
        b_in, refs = refs[:nb_in], refs[nb_in:]
        a_out, b_out = refs[:na_out], refs[na_out:]
        return a['make_units'](*a_in, *a_out) + b['make_units'](*b_in, *b_out)

    return dict(make_units=make_units, **{k: a[k] + b[k] for k in ('in_specs', 'out_specs', 'out_shape', 'args')})


def _cast_rider(grid, groups):
    n_steps, step = _grid_steps(grid)
    in_specs, out_specs, out_shape, args, widths = [], [], [], [], []
    for group in groups:
        rps = group[0].shape[0] // n_steps
        for a in group:
            in_specs.append(pl.BlockSpec((rps, a.shape[1]), lambda *g: (step(*g), 0)))
            args.append(a)
        cols = sum(a.shape[1] for a in group)
        out_specs.append(pl.BlockSpec((rps, cols), lambda *g: (step(*g), 0)))
        out_shape.append(jax.ShapeDtypeStruct((group[0].shape[0], cols), BF16))
        widths.append([a.shape[1] for a in group])

    def make_units(*refs):
        in_refs, out_refs = list(refs[:len(args)]), refs[len(args):]
        units = []
        for o_ref, ws in zip(out_refs, widths):
            srcs, in_refs = in_refs[:len(ws)], in_refs[len(ws):]

            def cast(o_ref=o_ref, srcs=srcs, ws=ws):
                c0 = 0
                for s_ref, wd in zip(srcs, ws):
                    o_ref[:, c0:c0 + wd] = s_ref[...].astype(BF16)
                    c0 += wd
            units.append(cast)
        return units

    return dict(make_units=make_units, in_specs=in_specs, out_specs=out_specs, out_shape=out_shape, args=args)


def _run_rider_share(rider, i, n):
    if rider is None:
        return
    for unit in rider[i * len(rider) // n:(i + 1) * len(rider) // n]:
        unit()


def _rider_parts(rider):
    if rider is None:
        return [], [], [], []
    return rider['in_specs'], rider['out_specs'], rider['out_shape'], rider['args']


def _norm_matmul_kernel(*refs, has_extra, w_is_transposed, emit_w_bf16):
    refs = list(refs)
    xn_ref = refs.pop()
    wb_ref = refs.pop() if emit_w_bf16 else None
    if has_extra:
        x_ref, g_ref, w_ref, we_ref, o_ref, oe_ref = refs
    else:
        x_ref, g_ref, w_ref, o_ref = refs
    dot = _dot_t if w_is_transposed else _dot

    @pl.when(pl.program_id(1) == 0)
    def _():
        xn = _rms(x_ref[...], g_ref[...]).astype(BF16)
        xn_ref[...] = xn
        if has_extra:
            oe_ref[...] = dot(xn, we_ref[...])

    wb = w_ref[...].astype(BF16)
    if emit_w_bf16:
        wb_ref[...] = wb
    o_ref[...] = dot(xn_ref[...], wb).astype(o_ref.dtype)


def _norm_matmul(x, g, w, *, tm, tn, out_dtype, w_extra=None, n_cols=None, w_is_transposed=False,
                 emit_w_bf16=False):
    m, k = x.shape
    n = n_cols or w.shape[0 if w_is_transposed else 1]
    has_extra = w_extra is not None
    w_spec = (pl.BlockSpec((tn, k), lambda i, j: (j, 0)) if w_is_transposed
              else pl.BlockSpec((k, tn), lambda i, j: (0, j)))
    in_specs = [pl.BlockSpec((tm, k), lambda i, j: (i, 0)),
                pl.BlockSpec((1, k), lambda i, j: (0, 0)),
                w_spec]
    out_specs = [pl.BlockSpec((tm, tn), lambda i, j: (i, j))]
    out_shape = [jax.ShapeDtypeStruct((m, n), out_dtype)]
    args = [x, g.reshape(1, k), w]
    if has_extra:
        ne = w_extra.shape[0 if w_is_transposed else 1]
        in_specs.append(pl.BlockSpec(w_extra.shape, lambda i, j: (0, 0)))
        out_specs.append(pl.BlockSpec((tm, ne), lambda i, j: (i, 0)))
        out_shape.append(jax.ShapeDtypeStruct((m, ne), F32))
        args.append(w_extra)
    if emit_w_bf16:
        assert m == tm
        out_specs.append(w_spec)
        out_shape.append(jax.ShapeDtypeStruct((n, k) if w_is_transposed else (k, n), BF16))
    res = pl.pallas_call(
        functools.partial(_norm_matmul_kernel, has_extra=has_extra, w_is_transposed=w_is_transposed,
                          emit_w_bf16=emit_w_bf16),
        grid=(m // tm, n // tn),
        in_specs=in_specs, out_specs=out_specs, out_shape=out_shape,
        scratch_shapes=[pltpu.VMEM((tm, k), BF16)],
        compiler_params=_params("parallel", "arbitrary"),
        name="norm_matmul",
    )(*args)
    return res if len(res) > 1 else res[0]


def _norm_matmul_pair_kernel(x_ref, g_ref, wa_ref, wb_ref, oa_ref, ob_ref, xn_ref):
    @pl.when(pl.program_id(0) == 0)
    def _():
        xn_ref[...] = _rms(x_ref[...], g_ref[...]).astype(BF16)

    oa_ref[...] = _dot(xn_ref[...], wa_ref[...])
    ob_ref[...] = _dot(xn_ref[...], wb_ref[...])


def _norm_matmul_pair(x, g, wa, wb, *, tn):
    m, k = x.shape
    n = wa.shape[1]
    cols = lambda rows: pl.BlockSpec((rows, tn), lambda j: (0, j))
    return pl.pallas_call(
        _norm_matmul_pair_kernel,
        grid=(n // tn,),
        in_specs=[pl.BlockSpec((m, k), lambda j: (0, 0)), pl.BlockSpec((1, k), lambda j: (0, 0)), cols(k), cols(k)],
        out_specs=[cols(m), cols(m)],
        out_shape=[jax.ShapeDtypeStruct((m, n), F32)] * 2,
        scratch_shapes=[pltpu.VMEM((m, k), BF16)],
        compiler_params=_params("arbitrary"),
        name="norm_matmul_pair",
    )(x, g.reshape(1, k), wa, wb)


def _matmul_res_kernel(*refs, n_parts):
    a_refs, w_refs = refs[:n_parts], refs[n_parts:2 * n_parts]
    r_ref, o_ref = refs[2 * n_parts:]
    acc = r_ref[...]
    for a_ref, w_ref in zip(a_refs, w_refs):
        acc = acc + _dot(a_ref[...], w_ref[...])
    o_ref[...] = acc


def _matmul_res(a_parts, w, res, *, tm, tn):
    n_parts = len(a_parts)
    m, kp = a_parts[0].shape
    n = w.shape[1]
    a_specs = [pl.BlockSpec((tm, kp), lambda i, j: (i, 0)) for _ in a_parts]
    w_specs = [pl.BlockSpec((kp, tn), lambda i, j, p=p: (p, j)) for p in range(n_parts)]
    return pl.pallas_call(
        functools.partial(_matmul_res_kernel, n_parts=n_parts),
        grid=(m // tm, n // tn),
        in_specs=a_specs + w_specs + [pl.BlockSpec((tm, tn), lambda i, j: (i, j))],
        out_specs=pl.BlockSpec((tm, tn), lambda i, j: (i, j)),
        out_shape=jax.ShapeDtypeStruct((m, n), F32),
        compiler_params=_params("parallel", "arbitrary"),
        name="matmul_res",
    )(*a_parts, *([w] * n_parts), res)


def _matmul_res_norm_matmul_kernel(*refs, n_parts, n1):
    a_refs = refs[:n_parts]
    w_ref, r_ref, g_ref, x1_ref, q_ref, x1_s, xn_s = refs[n_parts:]
    j = pl.program_id(1)
    tn = w_ref.shape[1]
    kp = a_refs[0].shape[1]

    @pl.when(j < n1)
    def _():
        x1 = r_ref[...]
        for p, a_ref in enumerate(a_refs):
            x1 = x1 + _dot(a_ref[...], w_ref[p * kp:(p + 1) * kp, :])
        x1_ref[...] = x1
        x1_s[j] = x1

    @pl.when(j == n1)
    def _():
        ssq = sum(jnp.sum(jnp.square(x1_s[c]), axis=-1, keepdims=True) for c in range(n1))
        scale = lax.rsqrt(ssq * (1.0 / (n1 * tn)) + EPS)
        for c in range(n1):
            xn_s[:, c * tn:(c + 1) * tn] = (x1_s[c] * scale * g_ref[:, c * tn:(c + 1) * tn]).astype(xn_s.dtype)

    @pl.when(j >= n1)
    def _():
        q_ref[...] = _dot(xn_s[...], w_ref[...]).astype(q_ref.dtype)


def _matmul_res_norm_matmul(a_parts, w12, res, g, *, tm, tn, out2_dtype):
    n_parts = len(a_parts)
    m, kp = a_parts[0].shape
    d = res.shape[1]
    n1 = d // tn
    n2 = (w12.shape[1] - d) // tn
    return pl.pallas_call(
        functools.partial(_matmul_res_norm_matmul_kernel, n_parts=n_parts, n1=n1),
        grid=(m // tm, n1 + n2),
        in_specs=[pl.BlockSpec((tm, kp), lambda i, j: (i, 0))] * n_parts + [
            pl.BlockSpec((n_parts * kp, tn), lambda i, j: (0, j)),
            pl.BlockSpec((tm, tn), lambda i, j: (i, jnp.minimum(j, n1 - 1))),
            pl.BlockSpec((1, d), lambda i, j: (0, 0))],
        out_specs=[pl.BlockSpec((tm, tn), lambda i, j: (i, jnp.minimum(j, n1 - 1))),
                   pl.BlockSpec((tm, tn), lambda i, j: (i, jnp.maximum(j - n1, 0)))],
        out_shape=[jax.ShapeDtypeStruct((m, d), F32), jax.ShapeDtypeStruct((m, n2 * tn), out2_dtype)],
        scratch_shapes=[pltpu.VMEM((n1, tm, tn), F32), pltpu.VMEM((tm, d), BF16)],
        compiler_params=_params("parallel", "arbitrary"),
        name="matmul_res_norm_matmul",
    )(*a_parts, w12, res, g.reshape(1, d))


def _ffn_kernel(x_ref, g_ref, wu_ref, wd_ref, gf_ref, o_ref, xn_ref, *, final_norm, rider=None):
    f = pl.program_id(1)

    @pl.when(f == 0)
    def _():
        x = x_ref[...]
        xn_ref[...] = _rms(x, g_ref[...]).astype(BF16)
        o_ref[...] = x

    _run_rider_share(rider, 0, 1)
    h = _dot(xn_ref[...], wu_ref[...])
    h = jnp.square(jnp.maximum(h, 0.0)).astype(BF16)
    o_ref[...] += _dot(h, wd_ref[...])

    if final_norm:
        @pl.when(f == pl.num_programs(1) - 1)
        def _():
            o_ref[...] = _rms(o_ref[...], gf_ref[...])


def _ffn(x, g, w_up, w_down, g_final, *, tm, tf, final_norm, rider=None):
    m, d = x.shape
    dff = w_up.shape[1]
    r_in, r_out, r_shape, r_args = _rider_parts(rider)
    in_specs = [pl.BlockSpec((tm, d), lambda i, f: (i, 0)),
                pl.BlockSpec((1, d), lambda i, f: (0, 0)),
                pl.BlockSpec((d, tf), lambda i, f: (0, f)),
                pl.BlockSpec((tf, d), lambda i, f: (f, 0)),
                pl.BlockSpec((1, d), lambda i, f: (0, 0))]
    out_specs = [pl.BlockSpec((tm, d), lambda i, f: (i, 0))]
    return pl.pallas_call(
        _with_rider(functools.partial(_ffn_kernel, final_norm=final_norm), rider, len(in_specs), len(out_specs)),
        grid=(m // tm, dff // tf),
        in_specs=in_specs + r_in, out_specs=out_specs + r_out,
        out_shape=[jax.ShapeDtypeStruct((m, d), F32)] + r_shape,
        scratch_shapes=[pltpu.VMEM((tm, d), BF16)],
        compiler_params=_params("parallel", "arbitrary", vmem_limit_bytes=V7X_VMEM_LIMIT_FFN_BYTES),
        name="ffn",
    )(x, g.reshape(1, d), w_up, w_down, g_final.reshape(1, d), *r_args)


def _lru_gates(xc, wg, bg, lam):
    g = _dot(xc.astype(BF16), wg) + bg
    r = jax.nn.sigmoid(g[:, :RNN_HD])
    i = jax.nn.sigmoid(g[:, RNN_HD:])
    log_a = -LRU_C * r * jax.nn.softplus(-lam)
    a = jnp.exp(log_a)
    u = jnp.sqrt(-jnp.tanh(log_a) * (a * a + 1.0)) * (i * xc)
    return a, u


def _rglru_kernel(xr_ref, gr_ref, h0_ref, c0_ref, cw_ref, cb_ref, wg_ref, bg_ref, lam_ref, gout_ref,
                  y_ref, hn_ref, cn_ref, xbuf, hc, ybuf, *, tc, rider=None):
    c = pl.program_id(1)

    @pl.when(c == 0)
    def _():
        xbuf[0:8, :] = jnp.zeros((8, D_RNN), F32)
        xbuf[5:8, :] = c0_ref[0]
        hc[...] = h0_ref[0]

    @pl.when(c > 0)
    def _():
        xbuf[0:8, :] = xbuf[tc:tc + 8, :]

    xbuf[8:8 + tc, :] = xr_ref[0]
    ng = tc // SUBLANES
    sub = lax.broadcasted_iota(jnp.int32, (ng, SUBLANES, RNN_HD), 1)
    ssq = jnp.zeros((tc, 1), F32)
    for h in range(RNN_HEADS):
        _run_rider_share(rider, h, RNN_HEADS)
        sl = slice(h * RNN_HD, (h + 1) * RNN_HD)
        xc = cb_ref[:, sl]
        for j in range(CONV_W):
            xc = xc + cw_ref[j:j + 1, sl] * xbuf[5 + j:5 + j + tc, sl]
        a, u = _lru_gates(xc, wg_ref[h], bg_ref[h], lam_ref[:, sl])
        a = a.reshape(ng, SUBLANES, RNN_HD)
        u = u.reshape(ng, SUBLANES, RNN_HD)
        d = 1
        while d < SUBLANES:
            keep = sub >= d
            a_prev = jnp.where(keep, pltpu.roll(a, d, 1), 1.0)
            u_prev = jnp.where(keep, pltpu.roll(u, d, 1), 0.0)
            u = u + a * u_prev
            a = a * a_prev
            d *= 2
        carry = hc[:, sl]
        for g in range(ng):
            hg = u[g] + a[g] * carry
            carry = hg[SUBLANES - 1:SUBLANES, :]
            ybuf[g * SUBLANES:(g + 1) * SUBLANES, sl] = hg
        hc[:, sl] = carry
        yv = ybuf[:, sl] * jax.nn.gelu(gr_ref[0, :, sl])
        ssq = ssq + jnp.sum(yv * yv, axis=-1, keepdims=True)
        ybuf[:, sl] = yv
    y = ybuf[...] * lax.rsqrt(ssq * (1.0 / D_RNN) + EPS) * gout_ref[...]
    y_ref[0] = y.astype(y_ref.dtype)

    @pl.when(c == pl.num_programs(1) - 1)
    def _():
        hn_ref[0] = hc[...]
        cn_ref[0] = xbuf[tc + 5:tc + 8, :]


def _rglru_prompt(z3, h0, conv0, w, *, tc, rider=None):
    b, t, _ = z3.shape
    full2 = lambda bi, ci: (0, 0)
    r_in, r_out, r_shape, r_args = _rider_parts(rider)
    in_specs = [pl.BlockSpec((1, tc, D_RNN), lambda bi, ci: (bi, ci, 0)),
                pl.BlockSpec((1, tc, D_RNN), lambda bi, ci: (bi, ci, 1)),
                pl.BlockSpec((1, 1, D_RNN), lambda bi, ci: (bi, 0, 0)),
                pl.BlockSpec((1, CONV_W - 1, D_RNN), lambda bi, ci: (bi, 0, 0)),
                pl.BlockSpec((CONV_W, D_RNN), full2),
                pl.BlockSpec((1, D_RNN), full2),
                pl.BlockSpec((RNN_HEADS, RNN_HD, 2 * RNN_HD), lambda bi, ci: (0, 0, 0)),
                pl.BlockSpec((RNN_HEADS, 1, 2 * RNN_HD), lambda bi, ci: (0, 0, 0)),
                pl.BlockSpec((1, D_RNN), full2),
                pl.BlockSpec((1, D_RNN), full2)]
    out_specs = [pl.BlockSpec((1, tc, D_RNN), lambda bi, ci: (bi, ci, 0)),
                 pl.BlockSpec((1, 1, D_RNN), lambda bi, ci: (bi, 0, 0)),
                 pl.BlockSpec((1, CONV_W - 1, D_RNN), lambda bi, ci: (bi, 0, 0))]
    out_shape = [jax.ShapeDtypeStruct((b, t, D_RNN), BF16),
                 jax.ShapeDtypeStruct((b, 1, D_RNN), F32),
                 jax.ShapeDtypeStruct((b, CONV_W - 1, D_RNN), F32)]
    return pl.pallas_call(
        _with_rider(functools.partial(_rglru_kernel, tc=tc), rider, len(in_specs), len(out_specs)),
        grid=(b, t // tc),
        in_specs=in_specs + r_in, out_specs=out_specs + r_out, out_shape=out_shape + r_shape,
        scratch_shapes=[pltpu.VMEM((tc + 8, D_RNN), F32),
                        pltpu.VMEM((1, D_RNN), F32),
                        pltpu.VMEM((tc, D_RNN), F32)],
        compiler_params=_params("parallel", "arbitrary"),
        name="rglru_prompt",
    )(z3, z3, h0, conv0, w['conv_rnn_w'], w['conv_rnn_b'], w['lru_wg'], w['lru_bg'],
      w['lru_lambda'], w['g_rnn_out'], *r_args)


def _mlstm_kernel(u_ref, v_ref, og_ref, zg_ref, gb_ref, c0_ref, n0_ref, m0_ref, cv0_ref,
                  cw_ref, cb_ref, wqk_ref, gout_ref,
                  y_ref, cn_ref, nn_ref, mn_ref, cvn_ref, ubuf, cst, nst, mst, rider=None):
    c = pl.program_id(1)
    L = CHUNK
    n_seq = u_ref.shape[0]

    @pl.when(c == 0)
    def _():
        ubuf[:, 0:8, :] = jnp.zeros((n_seq, 8, D_MLSTM), F32)
        ubuf[:, 5:8, :] = cv0_ref[...]
        cst[...] = c0_ref[...]
        nst[...] = n0_ref[...]
        mst[...] = m0_ref[...]

    @pl.when(c > 0)
    def _():
        ubuf[:, 0:8, :] = ubuf[:, L:L + 8, :]

    ubuf[:, 8:8 + L, :] = u_ref[...]

    ti = lax.broadcasted_iota(jnp.int32, (L, L), 0)
    si = lax.broadcasted_iota(jnp.int32, (L, L), 1)
    causal = si <= ti
    anti = ti <= si
    zgs = [zg_ref[s] + gb_ref[...] for s in range(n_seq)]
    zgts = [zg.T for zg in zgs]

    for h in range(MLSTM_HEADS):
        _run_rider_share(rider, h, MLSTM_HEADS)
        sl = slice(h * MLSTM_HD, (h + 1) * MLSTM_HD)
        for s in range(n_seq):
            zg, zgt = zgs[s], zgts[s]
            uc = cb_ref[:, sl]
            for j in range(CONV_W):
                uc = uc + cw_ref[j:j + 1, sl] * ubuf[s, 5 + j:5 + j + L, sl]
            uc = jax.nn.silu(uc)
            qk = _dot(uc.astype(BF16), wqk_ref[h])
            q = qk[:, :MLSTM_HD]
            k = qk[:, MLSTM_HD:] * (MLSTM_HD ** -0.5)
            v = v_ref[s, :, sl]
            qb, kb = q.astype(BF16), k.astype(BF16)

            icol = zg[:, h:h + 1]
            fcol = jax.nn.log_sigmoid(zg[:, MLSTM_HEADS + h:MLSTM_HEADS + h + 1])
            irow = zgt[h:h + 1, :]
            frow = jax.nn.log_sigmoid(zgt[MLSTM_HEADS + h:MLSTM_HEADS + h + 1, :])
            bcol = jnp.sum(jnp.where(causal, frow, 0.0), axis=1, keepdims=True)
            brow = jnp.sum(jnp.where(anti, fcol, 0.0), axis=0, keepdims=True)
            dmat = jnp.where(causal, irow + bcol - brow, -jnp.inf)
            m_prev = mst[s, h, :, 0:1]
            inter = bcol + m_prev
            m_t = jnp.maximum(inter, jnp.max(dmat, axis=1, keepdims=True))
            w_inter = jnp.exp(inter - m_t)
            sc = lax.dot_general(qb, kb, (((1,), (1,)), ((), ())), preferred_element_type=F32) * jnp.exp(dmat - m_t)
            cmat = cst[s, h]
            nrow = nst[s, h]
            cq = lax.dot_general(qb, cmat.astype(BF16), (((1,), (1,)), ((), ())), preferred_element_type=F32)
            num = w_inter * cq + _dot(sc.astype(BF16), v.astype(BF16))
            den = w_inter * jnp.sum(q * nrow, axis=1, keepdims=True) + jnp.sum(sc, axis=1, keepdims=True)
            hm = num / jnp.maximum(jnp.abs(den), jnp.exp(-m_t))

            m_new = m_t[L - 1:L, :]
            b_last = bcol[L - 1:L, :]
            g_state = jnp.exp(b_last + m_prev - m_new)
            g_in = jnp.exp(icol + b_last - bcol - m_new)
            cst[s, h] = g_state * cmat + lax.dot_general((g_in * v).astype(BF16), kb, (((0,), (0,)), ((), ())),
                                                         preferred_element_type=F32)
            nst[s, h] = g_state * nrow + jnp.sum(g_in * k, axis=0, keepdims=True)
            mst[s, h] = jnp.broadcast_to(m_new, (1, N_GATE_PAD))

            y = _rms(hm, gout_ref[...]) * jax.nn.sigmoid(og_ref[s, :, sl])
            y_ref[s, :, sl] = y.astype(y_ref.dtype)

    @pl.when(c == pl.num_programs(1) - 1)
    def _():
        cn_ref[...] = cst[...]
        nn_ref[...] = nst[...]
        mn_ref[...] = mst[...]
        cvn_ref[...] = ubuf[:, L + 5:L + 8, :]


MLSTM_SEQS = 1


def _mlstm_prompt(z3, zg3, c0, n0, m0, conv0, w, *, rider=None):
    b, t, _ = z3.shape
    nc = t // CHUNK
    ns = MLSTM_SEQS
    u_blk, v_blk, o_blk = 2 * D_RNN // D_MLSTM, 3 * D_RNN // D_MLSTM, 4 * D_RNN // D_MLSTM
    zcol = lambda blk: pl.BlockSpec((ns, CHUNK, D_MLSTM), lambda bi, ci: (bi, ci, blk))
    state4 = lambda *minor: pl.BlockSpec((ns, MLSTM_HEADS) + minor, lambda bi, ci: (bi, 0, 0, 0))
    conv_spec = pl.BlockSpec((ns, CONV_W - 1, D_MLSTM), lambda bi, ci: (bi, 0, 0))
    const2 = lambda shape: pl.BlockSpec(shape, lambda bi, ci: (0, 0))
    r_in, r_out, r_shape, r_args = _rider_parts(rider)
    in_specs = [zcol(u_blk), zcol(v_blk), zcol(o_blk),
                pl.BlockSpec((ns, CHUNK, N_GATE_PAD), lambda bi, ci: (bi, ci, 0)),
                const2((1, N_GATE_PAD)),
                state4(MLSTM_HD, MLSTM_HD), state4(1, MLSTM_HD), state4(1, N_GATE_PAD), conv_spec,
                const2((CONV_W, D_MLSTM)), const2((1, D_MLSTM)),
                pl.BlockSpec((MLSTM_HEADS, MLSTM_HD, 2 * MLSTM_HD), lambda bi, ci: (0, 0, 0)),
                const2((1, MLSTM_HD))]
    out_specs = [pl.BlockSpec((ns, CHUNK, D_MLSTM), lambda bi, ci: (bi, ci, 0)),
                 state4(MLSTM_HD, MLSTM_HD), state4(1, MLSTM_HD), state4(1, N_GATE_PAD), conv_spec]
    out_shape = [jax.ShapeDtypeStruct((b, t, D_MLSTM), BF16),
                 jax.ShapeDtypeStruct((b, MLSTM_HEADS, MLSTM_HD, MLSTM_HD), F32),
                 jax.ShapeDtypeStruct((b, MLSTM_HEADS, 1, MLSTM_HD), F32),
                 jax.ShapeDtypeStruct((b, MLSTM_HEADS, 1, N_GATE_PAD), F32),
                 jax.ShapeDtypeStruct((b, CONV_W - 1, D_MLSTM), F32)]
    return pl.pallas_call(
        _with_rider(_mlstm_kernel, rider, len(in_specs), len(out_specs)),
        grid=(b // ns, nc),
        in_specs=in_specs + r_in, out_specs=out_specs + r_out, out_shape=out_shape + r_shape,
        scratch_shapes=[pltpu.VMEM((ns, CHUNK + 8, D_MLSTM), F32),
                        pltpu.VMEM((ns, MLSTM_HEADS, MLSTM_HD, MLSTM_HD), F32),
                        pltpu.VMEM((ns, MLSTM_HEADS, 1, MLSTM_HD), F32),
                        pltpu.VMEM((ns, MLSTM_HEADS, 1, N_GATE_PAD), F32)],
        compiler_params=_params("parallel", "arbitrary"),
        name="mlstm_prompt",
    )(z3, z3, z3, zg3, w['gate_bias'], c0, n0, m0, conv0, w['conv_ml_w'], w['conv_ml_b'],
      w['ml_wqk'], w['g_ml_out'], *r_args)


def _xattn_kernel(q_ref, k_ref, v_ref, x_ref, wo_ref, o_ref, att, rider=None):
    @pl.when(pl.program_id(2) == 0)
    def _():
        for h in range(X_HEADS):
            sl = slice(h * X_HD, (h + 1) * X_HD)
            s = _dot_t(q_ref[0, :, sl], k_ref[0, :, sl]) * (X_HD ** -0.5)
            e = jnp.exp(s - jnp.max(s, axis=-1, keepdims=True))
            p = e / jnp.sum(e, axis=-1, keepdims=True)
            att[:, sl] = _dot(p, v_ref[0, :, sl]).astype(att.dtype)

    _run_rider_share(rider, 0, 1)
    o_ref[0] = x_ref[0] + _dot(att[...], wo_ref[...])


def _xattn_prompt(q3, mk3, mv3, x3, w_co, *, tq, tn, rider=None):
    b, t, d = q3.shape
    mem = pl.BlockSpec((1, N_MEM, d), lambda bi, ti, j: (bi, 0, 0))
    cols = pl.BlockSpec((1, tq, tn), lambda bi, ti, j: (bi, ti, j))
    r_in, r_out, r_shape, r_args = _rider_parts(rider)
    in_specs = [pl.BlockSpec((1, tq, d), lambda bi, ti, j: (bi, ti, 0)), mem, mem, cols,
                pl.BlockSpec((d, tn), lambda bi, ti, j: (0, j))]
    return pl.pallas_call(
        _with_rider(_xattn_kernel, rider, len(in_specs), 1),
        grid=(b, t // tq, d // tn),
        in_specs=in_specs + r_in, out_specs=[cols] + r_out,
        out_shape=[jax.ShapeDtypeStruct((b, t, d), F32)] + r_shape,
        scratch_shapes=[pltpu.VMEM((tq, d), BF16)],
        compiler_params=_params("parallel", "parallel", "arbitrary"),
        name="xattn_prompt",
    )(q3, mk3, mv3, x3, w_co, *r_args)


XS_SUB = 2 * X_HEADS
XS_PAIRS = X_HD // (2 * 128)


def _pack_heads(x):
    lead = x.shape[:-1]
    x = x.reshape(*lead, X_HEADS, 2 * XS_PAIRS, 128)
    x = jnp.swapaxes(x, -3, -2)
    return x.reshape(*lead, XS_PAIRS, XS_SUB, 128)


def _unpack_heads(x):
    lead = x.shape[:-3]
    x = x.reshape(*lead, 2 * XS_PAIRS, X_HEADS, 128)
    x = jnp.swapaxes(x, -3, -2)
    return x.reshape(*lead, X_HEADS * X_HD)


def _xattn_sample_units(q_ref, k_ref, v_ref, o_ref, r):
    def row():
        t = jnp.sum(k_ref[r] * q_ref[r], axis=1)
        t = t + pltpu.roll(t, X_HEADS, 1)
        s = jnp.sum(t, axis=-1, keepdims=True) * (X_HD ** -0.5)
        e = jnp.exp(s - jnp.max(s, axis=0, keepdims=True))
        p = e / jnp.sum(e, axis=0, keepdims=True)
        o_ref[r] = jnp.sum(p[:, None] * v_ref[r], axis=0)

    return [row]


def _xattn_sample_rider(grid, q, ck, cv):
    b = q.shape[0]
    n_steps, step = _grid_steps(grid)
    rows = b // n_steps
    kv_spec = pl.BlockSpec((rows, N_MEM, XS_PAIRS, XS_SUB, 128), lambda *g: (step(*g), 0, 0, 0, 0))
    q_spec = pl.BlockSpec((rows, XS_PAIRS, XS_SUB, 128), lambda *g: (step(*g), 0, 0, 0))
    return dict(make_units=_per_row_units(_xattn_sample_units, rows),
                in_specs=[q_spec, kv_spec, kv_spec], out_specs=[q_spec],
                out_shape=[jax.ShapeDtypeStruct((b, XS_PAIRS, XS_SUB, 128), F32)],
                args=[_pack_heads(q), _pack_heads(ck.reshape(b, N_MEM, X_HEADS * X_HD)),
                      _pack_heads(cv.reshape(b, N_MEM, X_HEADS * X_HD))])


def _smix_rows_kernel(z_ref, h0_ref, rc_ref, mc_ref, cwr_ref, cbr_ref, wg_ref, bg_ref, lam_ref, gout_ref,
                      cwm_ref, cbm_ref, wqk_ref,
                      yr_ref, hn_ref, rcn_ref, mcn_ref, q_ref, k_ref, ybuf):
    nb = z_ref.shape[0]
    ssq = jnp.zeros((nb, 1), F32)
    for h in range(RNN_HEADS):
        sl = slice(h * RNN_HD, (h + 1) * RNN_HD)
        xc = cbr_ref[:, sl] + cwr_ref[CONV_W - 1:CONV_W, sl] * z_ref[:, sl]
        for j in range(CONV_W - 1):
            xc = xc + cwr_ref[j:j + 1, sl] * rc_ref[:, j * D_RNN + h * RNN_HD:j * D_RNN + (h + 1) * RNN_HD]
        a, u = _lru_gates(xc, wg_ref[h], bg_ref[h], lam_ref[:, sl])
        hs = a * h0_ref[:, sl] + u
        hn_ref[:, sl] = hs
        yv = hs * jax.nn.gelu(z_ref[:, D_RNN + h * RNN_HD:D_RNN + (h + 1) * RNN_HD])
        ssq = ssq + jnp.sum(yv * yv, axis=-1, keepdims=True)
        ybuf[:, sl] = yv
    yr_ref[...] = (ybuf[...] * lax.rsqrt(ssq * (1.0 / D_RNN) + EPS) * gout_ref[...]).astype(yr_ref.dtype)
    rcn_ref[:, 0:2 * D_RNN] = rc_ref[:, D_RNN:3 * D_RNN]
    rcn_ref[:, 2 * D_RNN:3 * D_RNN] = z_ref[:, 0:D_RNN]

    for h in range(MLSTM_HEADS):
        sl = slice(h * MLSTM_HD, (h + 1) * MLSTM_HD)
        uc = cbm_ref[:, sl] + cwm_ref[CONV_W - 1:CONV_W, sl] * z_ref[:, 2 * D_RNN + h * MLSTM_HD:2 * D_RNN + (h + 1) * MLSTM_HD]
        for j in range(CONV_W - 1):
            uc = uc + cwm_ref[j:j + 1, sl] * mc_ref[:, j * D_MLSTM + h * MLSTM_HD:j * D_MLSTM + (h + 1) * MLSTM_HD]
        uc = jax.nn.silu(uc)
        qk = _dot(uc.astype(BF16), wqk_ref[h])
        q_ref[:, sl] = qk[:, :MLSTM_HD]
        k_ref[:, sl] = qk[:, MLSTM_HD:] * (MLSTM_HD ** -0.5)
    mcn_ref[:, 0:2 * D_MLSTM] = mc_ref[:, D_MLSTM:3 * D_MLSTM]
    mcn_ref[:, 2 * D_MLSTM:3 * D_MLSTM] = z_ref[:, 2 * D_RNN:2 * D_RNN + D_MLSTM]


def _smix_rows(z, h0, rconv, mconv, w):
    nb = z.shape[0]
    outs = [jax.ShapeDtypeStruct((nb, D_RNN), BF16),
            jax.ShapeDtypeStruct((nb, D_RNN), F32),
            jax.ShapeDtypeStruct((nb, 3 * D_RNN), F32),
            jax.ShapeDtypeStruct((nb, 3 * D_MLSTM), F32),
            jax.ShapeDtypeStruct((nb, D_MLSTM), F32),
            jax.ShapeDtypeStruct((nb, D_MLSTM), F32)]
    return pl.pallas_call(
        _smix_rows_kernel,
        out_shape=outs,
        scratch_shapes=[pltpu.VMEM((nb, D_RNN), F32)],
        compiler_params=pltpu.CompilerParams(vmem_limit_bytes=V7X_VMEM_LIMIT_BYTES),
        name="smix_rows",
    )(z, h0, rconv, mconv, w['conv_rnn_w'], w['conv_rnn_b'], w['lru_wg'], w['lru_bg'], w['lru_lambda'],
      w['g_rnn_out'], w['conv_ml_w'], w['conv_ml_b'], w['ml_wqk'])


MXU_MIN_ROWS = 8


def _smix_state_units(c_ref, q_ref, k_ref, n_ref, v_ref, og_ref, zg_ref, gb_ref, m_ref, gout_ref,
                      cn_ref, nn_ref, mn_ref, y_ref, r):
    lane_g = lax.broadcasted_iota(jnp.int32, (1, N_GATE_PAD), 1)
    first_row = lax.broadcasted_iota(jnp.int32, (MXU_MIN_ROWS, MLSTM_HD), 0) == 0

    def head(h, zg, m_out):
        sl = slice(h * MLSTM_HD, (h + 1) * MLSTM_HD)
        q = q_ref[r, :, sl]
        k = k_ref[r, :, sl]
        n = n_ref[r, :, sl]
        v = v_ref[r, :, sl]
        ig = zg[:, h:h + 1]
        lf = jax.nn.log_sigmoid(zg[:, MLSTM_HEADS + h:MLSTM_HEADS + h + 1])
        m_prev = m_ref[r, :, h:h + 1]
        m_t = jnp.maximum(lf + m_prev, ig)
        w_inter = jnp.exp(lf + m_prev - m_t)
        g_in = jnp.exp(ig - m_t)
        s = jnp.sum(q * k, axis=1, keepdims=True) * g_in
        den = w_inter * jnp.sum(n * q, axis=1, keepdims=True) + s
        denom = jnp.maximum(jnp.abs(den), jnp.exp(-m_t))
        cmat = c_ref[r, h]
        q8 = jnp.broadcast_to(q, (MXU_MIN_ROWS, MLSTM_HD)).astype(BF16)
        cq = lax.dot_general(q8, cmat.astype(BF16), (((1,), (1,)), ((), ())),
                             preferred_element_type=F32)[0:1, :]
        hm = (w_inter * cq + s * v) / denom
        gv8 = jnp.where(first_row, jnp.broadcast_to(g_in * v, (MXU_MIN_ROWS, MLSTM_HD)), 0.0).astype(BF16)
        k8 = jnp.broadcast_to(k, (MXU_MIN_ROWS, MLSTM_HD)).astype(BF16)
        outer = lax.dot_general(gv8, k8, (((0,), (0,)), ((), ())), preferred_element_type=F32)
        cn_ref[r, h] = w_inter * cmat + outer
        nn_ref[r, :, sl] = w_inter * n + g_in * k
        y = _rms(hm, gout_ref[...]) * jax.nn.sigmoid(og_ref[r, :, sl])
        y_ref[r, :, sl] = y.astype(y_ref.dtype)
        return jnp.where(lane_g == h, m_t, m_out)

    def row():
        zg = zg_ref[r] + gb_ref[...]
        m_out = jnp.zeros((1, N_GATE_PAD), F32)
        for h in range(MLSTM_HEADS):
            m_out = head(h, zg, m_out)
        mn_ref[r] = m_out

    return [row]


def _smix_state_rider(grid, c, q3, k3, n3, z3, zg3, gate_bias, m3, gout):
    nb = c.shape[0]
    n_steps, step = _grid_steps(grid)
    rows = nb // n_steps
    row = lambda width, col=0: pl.BlockSpec((rows, 1, width), lambda *g: (step(*g), 0, col))
    cblk = pl.BlockSpec((rows, MLSTM_HEADS, MLSTM_HD, MLSTM_HD), lambda *g: (step(*g), 0, 0, 0))
    return dict(
        make_units=_per_row_units(_smix_state_units, rows),
        in_specs=[cblk, row(D_MLSTM), row(D_MLSTM), row(D_MLSTM),
                  row(D_MLSTM, 3 * D_RNN // D_MLSTM), row(D_MLSTM, 4 * D_RNN // D_MLSTM),
                  row(N_GATE_PAD),
                  pl.BlockSpec((1, N_GATE_PAD), lambda *g: (0, 0)),
                  row(MLSTM_HEADS),
                  pl.BlockSpec((1, MLSTM_HD), lambda *g: (0, 0))],
        out_specs=[cblk, row(D_MLSTM), row(N_GATE_PAD), row(D_MLSTM)],
        out_shape=[jax.ShapeDtypeStruct(c.shape, F32),
                   jax.ShapeDtypeStruct((nb, 1, D_MLSTM), F32),
                   jax.ShapeDtypeStruct((nb, 1, N_GATE_PAD), F32),
                   jax.ShapeDtypeStruct((nb, 1, D_MLSTM), F32)],
        args=[c, q3, k3, n3, z3, z3, zg3, gate_bias, m3, gout])


RGLRU_CHUNK = 256
PROMPT_TM = 1024
PROMPT_W_IN_TN = 1280
SAMPLE_FFN_TF = 2048


def _layer(xp, xs, mem, rg_h, rg_conv, c, n, mst, ml_conv, ck, cv, w, g_final, final_norm):
    b, t, d = xp.shape
    nb = xs.shape[0]
    m = b * t
    tm = PROMPT_TM
    xp2 = xp.reshape(m, d)
    xs2 = xs.reshape(nb, d)
    w_in_args = dict(tn=1024, out_dtype=F32, w_extra=w['w_in_gate'], n_cols=D_MAIN, w_is_transposed=True)

    zs, zgs, w_in_bf = _norm_matmul(xs2, w['g_mix'], w['w_in_main'], tm=nb, emit_w_bf16=True, **w_in_args)
    ys_rnn, s_h, s_rconv, s_mconv, qs, ks = _smix_rows(
        zs, rg_h, rg_conv.reshape(nb, 3 * D_RNN), ml_conv.reshape(nb, 3 * D_MLSTM), w)

    z, zg = _norm_matmul(xp2, w['g_mix'], w_in_bf, tm=tm, **dict(w_in_args, tn=PROMPT_W_IN_TN))
    z3 = z.reshape(b, t, D_MAIN)
    zg3 = zg.reshape(b, t, N_GATE_PAD)
    h0 = jnp.zeros((b, 1, D_RNN), F32)
    conv0 = jnp.zeros((b, CONV_W - 1, D_RNN), F32)
    rg_grid = (b, t // RGLRU_CHUNK)
    state_rider = _smix_state_rider(
        rg_grid, c, qs.reshape(nb, 1, D_MLSTM), ks.reshape(nb, 1, D_MLSTM), n.reshape(nb, 1, D_MLSTM),
        zs.reshape(nb, 1, D_MAIN), zgs.reshape(nb, 1, N_GATE_PAD), w['gate_bias'],
        mst.reshape(nb, 1, MLSTM_HEADS), w['g_ml_out'])
    rg_rider = _merge_riders(state_rider, _cast_rider(rg_grid, [[w['w_out'], w['w_cq']], [w['w_co']]]))
    y_rnn, p_h, p_rconv, s_c, s_n, s_m, ys_ml, w_out_cq, w_co = _rglru_prompt(
        z3, h0, conv0, w, tc=RGLRU_CHUNK, rider=rg_rider)

    xs1, qx = _matmul_res_norm_matmul([ys_rnn, ys_ml.reshape(nb, D_MLSTM).astype(BF16)], w_out_cq, xs2,
                                      w['g_xattn'], tm=nb, tn=1024, out2_dtype=F32)

    c0 = jnp.zeros((b, MLSTM_HEADS, MLSTM_HD, MLSTM_HD), F32)
    n0 = jnp.zeros((b, MLSTM_HEADS, 1, MLSTM_HD), F32)
    m0 = jnp.zeros((b, MLSTM_HEADS, 1, N_GATE_PAD), F32)
    mconv0 = jnp.zeros((b, CONV_W - 1, D_MLSTM), F32)
    y_ml, p_c, p_n, p_m, p_mconv, w_up, w_down = _mlstm_prompt(
        z3, zg3, c0, n0, m0, mconv0, w,
        rider=_cast_rider((b // MLSTM_SEQS, t // CHUNK), [[w['w_up']], [w['w_down']]]))

    x1, q = _matmul_res_norm_matmul([y_rnn.reshape(m, D_RNN), y_ml.reshape(m, D_MLSTM)], w_out_cq, xp2,
                                    w['g_xattn'], tm=tm, tn=1024, out2_dtype=BF16)
    mem2 = mem.reshape(b * N_MEM, d)
    mk, mv = _norm_matmul_pair(mem2, w['g_mem'], w['w_mk'], w['w_mv'], tn=512)
    x2, = _xattn_prompt(q.reshape(b, t, d), mk.reshape(b, N_MEM, d), mv.reshape(b, N_MEM, d), x1.reshape(b, t, d),
                        w_co, tq=tm, tn=1024)
    ffn_tf = 512
    xp_out, os_packed = _ffn(x2.reshape(m, d), w['g_ffn'], w_up, w_down, g_final, tm=tm, tf=ffn_tf,
                             final_norm=final_norm,
                             rider=_xattn_sample_rider((m // tm, D_FF // ffn_tf), qx, ck, cv))

    os_ = _unpack_heads(os_packed).astype(BF16)
    xs2_ = _matmul_res([os_], w_co, xs1, tm=nb, tn=1024)
    xs_out, = _ffn(xs2_, w['g_ffn'], w_up, w_down, g_final, tm=nb, tf=SAMPLE_FFN_TF, final_norm=final_norm)

    new_p = (p_h.reshape(b, D_RNN), p_rconv, p_c, p_n.reshape(b, MLSTM_HEADS, MLSTM_HD), p_m[:, :, 0, 0], p_mconv,
             mk.reshape(b, N_MEM, X_HEADS, X_HD), mv.reshape(b, N_MEM, X_HEADS, X_HD))
    new_s = (s_h, s_rconv.reshape(nb, CONV_W - 1, D_RNN), s_c, s_n.reshape(nb, MLSTM_HEADS, MLSTM_HD),
             s_m[:, 0, :MLSTM_HEADS], s_mconv.reshape(nb, CONV_W - 1, D_MLSTM))
    return xp_out.reshape(b, t, d), xs_out.reshape(nb, 1, d), new_p, new_s


def _layer_weights(l, g_mix, w_in, conv_rnn_w, conv_rnn_b, lru_wa, lru_ba, lru_wx, lru_bx, lru_lambda,
                   g_rnn_out, conv_ml_w, conv_ml_b, ml_wq, ml_wk, ml_bi, ml_bf, g_ml_out, w_out,
                   g_xattn, g_mem, w_cq, w_mk, w_mv, w_co, g_ffn, w_up, w_down):
    n_gate = 2 * MLSTM_HEADS
    w_in_t = jnp.swapaxes(w_in[l], 0, 1)
    w_gate_t = jnp.pad(w_in_t[D_MAIN:], ((0, N_GATE_PAD - n_gate), (0, 0)))
    gate_bias = jnp.pad(jnp.concatenate([ml_bi[l], ml_bf[l]]), (0, N_GATE_PAD - n_gate))
    return dict(
        g_mix=g_mix[l], w_in_main=w_in_t, w_in_gate=w_gate_t.astype(BF16),
        conv_rnn_w=conv_rnn_w[l], conv_rnn_b=conv_rnn_b[l].reshape(1, D_RNN),
        lru_wg=jnp.concatenate([lru_wa[l], lru_wx[l]], axis=-1).astype(BF16),
        lru_bg=jnp.concatenate([lru_ba[l], lru_bx[l]], axis=-1).reshape(RNN_HEADS, 1, 2 * RNN_HD),
        lru_lambda=lru_lambda[l].reshape(1, D_RNN), g_rnn_out=g_rnn_out[l].reshape(1, D_RNN),
        conv_ml_w=conv_ml_w[l], conv_ml_b=conv_ml_b[l].reshape(1, D_MLSTM),
        ml_wqk=jnp.concatenate([ml_wq[l], ml_wk[l]], axis=-1).astype(BF16),
        gate_bias=gate_bias.reshape(1, N_GATE_PAD), g_ml_out=g_ml_out[l].reshape(1, MLSTM_HD),
        w_out=w_out[l], w_cq=w_cq[l], g_xattn=g_xattn[l], g_mem=g_mem[l], w_mk=w_mk[l], w_mv=w_mv[l],
        w_co=w_co[l], g_ffn=g_ffn[l], w_up=w_up[l], w_down=w_down[l])


def kernel(x_prompt, x_sample, mem_prompt, state_rglru_h, state_rglru_conv, state_mlstm_C, state_mlstm_n, state_mlstm_m, state_mlstm_conv, cache_mem_k, cache_mem_v, g_mix, w_in, conv_rnn_w, conv_rnn_b, lru_wa, lru_ba, lru_wx, lru_bx, lru_lambda, g_rnn_out, conv_ml_w, conv_ml_b, ml_wq, ml_wk, ml_bi, ml_bf, g_ml_out, w_out, g_xattn, g_mem, w_cq, w_mk, w_mv, w_co, g_ffn, w_up, w_down, g_final):
    depth = w_in.shape[0]
    xp, xs = x_prompt, x_sample
    p_out = [[] for _ in range(8)]
    s_out = [[] for _ in range(6)]
    for l in range(depth):
        w = _layer_weights(l, g_mix, w_in, conv_rnn_w, conv_rnn_b, lru_wa, lru_ba, lru_wx, lru_bx, lru_lambda,
                           g_rnn_out, conv_ml_w, conv_ml_b, ml_wq, ml_wk, ml_bi, ml_bf, g_ml_out, w_out,
                           g_xattn, g_mem, w_cq, w_mk, w_mv, w_co, g_ffn, w_up, w_down)
        last = l == depth - 1
        xp, xs, new_p, new_s = _layer(xp, xs, mem_prompt, state_rglru_h[l], state_rglru_conv[l], state_mlstm_C[l],
                                      state_mlstm_n[l], state_mlstm_m[l], state_mlstm_conv[l], cache_mem_k[l],
                                      cache_mem_v[l], w, g_final, last)
        for j, a in enumerate(new_p):
            p_out[j].append(a)
        for j, a in enumerate(new_s):
            s_out[j].append(a)
    P = [jnp.stack(a, axis=0) for a in p_out]
    S = [jnp.stack(a, axis=0) for a in s_out]
    return (xp, xs, P[0], P[1], P[2], P[3], P[4], P[5], P[6], P[7],
            S[0], S[1], S[2], S[3], S[4], S[5])
```

```python
import functools

import jax
import jax.numpy as jnp
from jax import lax
from jax.experimental import pallas as pl
from jax.experimental.pallas import tpu as pltpu

F32 = jnp.float32
BF16 = jnp.bfloat16

D_MODEL = 2048
D_RNN = 1024
RNN_HEADS = 8
RNN_HD = 128
CONV_W = 4
LRU_C = 8.0
D_MLSTM = 1024
MLSTM_HEADS = 4
MLSTM_HD = 256
CHUNK = 128
N_MEM = 256
X_HEADS = 4
X_HD = 512
D_FF = 8192
EPS = 1e-6
D_MAIN = 5 * 1024
N_GATE_PAD = 128

V7X_VMEM_LIMIT_BYTES = 56 * 1024 * 1024
V7X_VMEM_LIMIT_FFN_BYTES = 63 * 1024 * 1024
SUBLANES = 8


def _params(*sem, vmem_limit_bytes=V7X_VMEM_LIMIT_BYTES):
    return pltpu.CompilerParams(dimension_semantics=sem, vmem_limit_bytes=vmem_limit_bytes)


def _rms(x, g):
    ms = jnp.mean(x * x, axis=-1, keepdims=True)
    return x * lax.rsqrt(ms + EPS) * g


def _dot(a, b):
    return jnp.dot(a.astype(BF16), b.astype(BF16), preferred_element_type=F32)


def _dot_t(a, bt):
    return lax.dot_general(a.astype(BF16), bt.astype(BF16), (((1,), (1,)), ((), ())), preferred_element_type=F32)


def _with_rider(host_body, rider, n_in, n_out):
    if rider is None:
        return host_body
    r_in, r_out = len(rider['in_specs']), len(rider['out_specs'])

    def body(*refs):
        host_in, refs = refs[:n_in], refs[n_in:]
        rider_in, refs = refs[:r_in], refs[r_in:]
        host_out, refs = refs[:n_out], refs[n_out:]
        rider_out, scratch = refs[:r_out], refs[r_out:]
        host_body(*host_in, *host_out, *scratch, rider=rider['make_units'](*rider_in, *rider_out))
    return body


def _grid_steps(grid):
    n = 1
    for g in grid:
        n *= g

    def step(*idx):
        s = idx[0]
        for g, i in zip(grid[1:], idx[1:]):
            s = s * g + i
        return s
    return n, step


def _per_row_units(row_units, rows):
    return lambda *refs: [u for r in range(rows) for u in row_units(*refs, r)]


def _merge_riders(a, b):
    na_in, na_out = len(a['in_specs']), len(a['out_specs'])
    nb_in = len(b['in_specs'])

    def make_units(*refs):
        a_in, refs = refs[:na_in], refs[na_in:]
        b_in, refs = refs[:nb_in], refs[nb_in:]
        a_out, b_out = refs[:na_out], refs[na_out:]
        return a['make_units'](*a_in, *a_out) + b['make_units'](*b_in, *b_out)

    return dict(make_units=make_units, **{k: a[k] + b[k] for k in ('in_specs', 'out_specs', 'out_shape', 'args')})


def _cast_rider(grid, groups):
    n_steps, step = _grid_steps(grid)
    in_specs, out_specs, out_shape, args, widths = [], [], [], [], []
    for group in groups:
        rps = group[0].shape[0] // n_steps
        for a in group:
            in_specs.append(pl.BlockSpec((rps, a.shape[1]), lambda *g: (step(*g), 0)))
            args.append(a)
        cols = sum(a.shape[1] for a in group)
        out_specs.append(pl.BlockSpec((rps, cols), lambda *g: (step(*g), 0)))
        out_shape.append(jax.ShapeDtypeStruct((group[0].shape[0], cols), BF16))
        widths.append([a.shape[1] for a in group])

    def make_units(*refs):
        in_refs, out_refs = list(refs[:len(args)]), refs[len(args):]
        units = []
        for o_ref, ws in zip(out_refs, widths):
            srcs, in_refs = in_refs[:len(ws)], in_refs[len(ws):]

            def cast(o_ref=o_ref, srcs=srcs, ws=ws):
                c0 = 0
                for s_ref, wd in zip(srcs, ws):
                    o_ref[:, c0:c0 + wd] = s_ref[...].astype(BF16)
                    c0 += wd
            units.append(cast)
        return units

    return dict(make_units=make_units, in_specs=in_specs, out_specs=out_specs, out_shape=out_shape, args=args)


def _run_rider_share(rider, i, n):
    if rider is None:
        return
    for unit in rider[i * len(rider) // n:(i + 1) * len(rider) // n]:
        unit()


def _rider_parts(rider):
    if rider is None:
        return [], [], [], []
    return rider['in_specs'], rider['out_specs'], rider['out_shape'], rider['args']


def _norm_matmul_kernel(*refs, has_extra, w_is_transposed, emit_w_bf16):
    refs = list(refs)
    xn_ref = refs.pop()
    wb_ref = refs.pop() if emit_w_bf16 else None
    if has_extra:
        x_ref, g_ref, w_ref, we_ref, o_ref, oe_ref = refs
    else:
        x_ref, g_ref, w_ref, o_ref = refs
    dot = _dot_t if w_is_transposed else _dot

    @pl.when(pl.program_id(1) == 0)
    def _():
        xn = _rms(x_ref[...], g_ref[...]).astype(BF16)
        xn_ref[...] = xn
        if has_extra:
            oe_ref[...] = dot(xn, we_ref[...])

    wb = w_ref[...].astype(BF16)
    if emit_w_bf16:
        wb_ref[...] = wb
    o_ref[...] = dot(xn_ref[...], wb).astype(o_ref.dtype)


def _norm_matmul(x, g, w, *, tm, tn, out_dtype, w_extra=None, n_cols=None, w_is_transposed=False,
                 emit_w_bf16=False):
    m, k = x.shape
    n = n_cols or w.shape[0 if w_is_transposed else 1]
    has_extra = w_extra is not None
    w_spec = (pl.BlockSpec((tn, k), lambda i, j: (j, 0)) if w_is_transposed
              else pl.BlockSpec((k, tn), lambda i, j: (0, j)))
    in_specs = [pl.BlockSpec((tm, k), lambda i, j: (i, 0)),
                pl.BlockSpec((1, k), lambda i, j: (0, 0)),
                w_spec]
    out_specs = [pl.BlockSpec((tm, tn), lambda i, j: (i, j))]
    out_shape = [jax.ShapeDtypeStruct((m, n), out_dtype)]
    args = [x, g.reshape(1, k), w]
    if has_extra:
        ne = w_extra.shape[0 if w_is_transposed else 1]
        in_specs.append(pl.BlockSpec(w_extra.shape, lambda i, j: (0, 0)))
        out_specs.append(pl.BlockSpec((tm, ne), lambda i, j: (i, 0)))
        out_shape.append(jax.ShapeDtypeStruct((m, ne), F32))
        args.append(w_extra)
    if emit_w_bf16:
        assert m == tm
        out_specs.append(w_spec)
        out_shape.append(jax.ShapeDtypeStruct((n, k) if w_is_transposed else (k, n), BF16))
    res = pl.pallas_call(
        functools.partial(_norm_matmul_kernel, has_extra=has_extra, w_is_transposed=w_is_transposed,
                          emit_w_bf16=emit_w_bf16),
        grid=(m // tm, n // tn),
        in_specs=in_specs, out_specs=out_specs, out_shape=out_shape,
        scratch_shapes=[pltpu.VMEM((tm, k), BF16)],
        compiler_params=_params("parallel", "arbitrary"),
        name="norm_matmul",
    )(*args)
    return res if len(res) > 1 else res[0]


def _norm_matmul_pair_kernel(x_ref, g_ref, wa_ref, wb_ref, oa_ref, ob_ref, xn_ref):
    @pl.when(pl.program_id(0) == 0)
    def _():
        xn_ref[...] = _rms(x_ref[...], g_ref[...]).astype(BF16)

    oa_ref[...] = _dot(xn_ref[...], wa_ref[...])
    ob_ref[...] = _dot(xn_ref[...], wb_ref[...])


def _norm_matmul_pair(x, g, wa, wb, *, tn):
    m, k = x.shape
    n = wa.shape[1]
    cols = lambda rows: pl.BlockSpec((rows, tn), lambda j: (0, j))
    return pl.pallas_call(
        _norm_matmul_pair_kernel,
        grid=(n // tn,),
        in_specs=[pl.BlockSpec((m, k), lambda j: (0, 0)), pl.BlockSpec((1, k), lambda j: (0, 0)), cols(k), cols(k)],
        out_specs=[cols(m), cols(m)],
        out_shape=[jax.ShapeDtypeStruct((m, n), F32)] * 2,
        scratch_shapes=[pltpu.VMEM((m, k), BF16)],
        compiler_params=_params("arbitrary"),
        name="norm_matmul_pair",
    )(x, g.reshape(1, k), wa, wb)


def _matmul_res_kernel(*refs, n_parts):
    a_refs, w_refs = refs[:n_parts], refs[n_parts:2 * n_parts]
    r_ref, o_ref = refs[2 * n_parts:]
    acc = r_ref[...]
    for a_ref, w_ref in zip(a_refs, w_refs):
        acc = acc + _dot(a_ref[...], w_ref[...])
    o_ref[...] = acc


def _matmul_res(a_parts, w, res, *, tm, tn):
    n_parts = len(a_parts)
    m, kp = a_parts[0].shape
    n = w.shape[1]
    a_specs = [pl.BlockSpec((tm, kp), lambda i, j: (i, 0)) for _ in a_parts]
    w_specs = [pl.BlockSpec((kp, tn), lambda i, j, p=p: (p, j)) for p in range(n_parts)]
    return pl.pallas_call(
        functools.partial(_matmul_res_kernel, n_parts=n_parts),
        grid=(m // tm, n // tn),
        in_specs=a_specs + w_specs + [pl.BlockSpec((tm, tn), lambda i, j: (i, j))],
        out_specs=pl.BlockSpec((tm, tn), lambda i, j: (i, j)),
        out_shape=jax.ShapeDtypeStruct((m, n), F32),
        compiler_params=_params("parallel", "arbitrary"),
        name="matmul_res",
    )(*a_parts, *([w] * n_parts), res)


def _matmul_res_norm_matmul_kernel(*refs, n_parts, n1):
    a_refs = refs[:n_parts]
    w_ref, r_ref, g_ref, x1_ref, q_ref, x1_s, xn_s, ssq_s = refs[n_parts:]
    j = pl.program_id(1)
    tn = w_ref.shape[1]
    kp = a_refs[0].shape[1]

    @pl.when(j < n1)
    def _():
        x1 = r_ref[...]
        for p, a_ref in enumerate(a_refs):
            x1 = x1 + _dot(a_ref[...], w_ref[p * kp:(p + 1) * kp, :])
        x1_ref[...] = x1
        x1_s[j] = x1
        row_ss = jnp.sum(x1 * x1, axis=-1, keepdims=True)

        @pl.when(j == 0)
        def _():
            ssq_s[...] = row_ss

        @pl.when(j > 0)
        def _():
            ssq_s[...] += row_ss

    @pl.when(j == n1)
    def _():
        scale = lax.rsqrt(ssq_s[...] * (1.0 / (n1 * tn)) + EPS)
        for c in range(n1):
            xn_s[:, c * tn:(c + 1) * tn] = (x1_s[c] * scale * g_ref[:, c * tn:(c + 1) * tn]).astype(xn_s.dtype)

    @pl.when(j >= n1)
    def _():
        q_ref[...] = _dot(xn_s[...], w_ref[...]).astype(q_ref.dtype)


def _matmul_res_norm_matmul(a_parts, w12, res, g, *, tm, tn, out2_dtype):
    n_parts = len(a_parts)
    m, kp = a_parts[0].shape
    d = res.shape[1]
    n1 = d // tn
    n2 = (w12.shape[1] - d) // tn
    return pl.pallas_call(
        functools.partial(_matmul_res_norm_matmul_kernel, n_parts=n_parts, n1=n1),
        grid=(m // tm, n1 + n2),
        in_specs=[pl.BlockSpec((tm, kp), lambda i, j: (i, 0))] * n_parts + [
            pl.BlockSpec((n_parts * kp, tn), lambda i, j: (0, j)),
            pl.BlockSpec((tm, tn), lambda i, j: (i, jnp.minimum(j, n1 - 1))),
            pl.BlockSpec((1, d), lambda i, j: (0, 0))],
        out_specs=[pl.BlockSpec((tm, tn), lambda i, j: (i, jnp.minimum(j, n1 - 1))),
                   pl.BlockSpec((tm, tn), lambda i, j: (i, jnp.maximum(j - n1, 0)))],
        out_shape=[jax.ShapeDtypeStruct((m, d), F32), jax.ShapeDtypeStruct((m, n2 * tn), out2_dtype)],
        scratch_shapes=[pltpu.VMEM((n1, tm, tn), F32), pltpu.VMEM((tm, d), BF16), pltpu.VMEM((tm, 1), F32)],
        compiler_params=_params("parallel", "arbitrary"),
        name="matmul_res_norm_matmul",
    )(*a_parts, w12, res, g.reshape(1, d))


def _ffn_kernel(x_ref, g_ref, wu_ref, wd_ref, gf_ref, o_ref, xn_ref, *, final_norm, rider=None):
    f = pl.program_id(1)

    @pl.when(f == 0)
    def _():
        x = x_ref[...]
        xn_ref[...] = _rms(x, g_ref[...]).astype(BF16)
        o_ref[...] = x

    _run_rider_share(rider, 0, 1)
    h = _dot(xn_ref[...], wu_ref[...])
    h = jnp.square(jnp.maximum(h, 0.0)).astype(BF16)
    o_ref[...] += _dot(h, wd_ref[...])

    if final_norm:
        @pl.when(f == pl.num_programs(1) - 1)
        def _():
            o_ref[...] = _rms(o_ref[...], gf_ref[...])


def _ffn(x, g, w_up, w_down, g_final, *, tm, tf, final_norm, rider=None):
    m, d = x.shape
    dff = w_up.shape[1]
    r_in, r_out, r_shape, r_args = _rider_parts(rider)
    in_specs = [pl.BlockSpec((tm, d), lambda i, f: (i, 0)),
                pl.BlockSpec((1, d), lambda i, f: (0, 0)),
                pl.BlockSpec((d, tf), lambda i, f: (0, f)),
                pl.BlockSpec((tf, d), lambda i, f: (f, 0)),
                pl.BlockSpec((1, d), lambda i, f: (0, 0))]
    out_specs = [pl.BlockSpec((tm, d), lambda i, f: (i, 0))]
    return pl.pallas_call(
        _with_rider(functools.partial(_ffn_kernel, final_norm=final_norm), rider, len(in_specs), len(out_specs)),
        grid=(m // tm, dff // tf),
        in_specs=in_specs + r_in, out_specs=out_specs + r_out,
        out_shape=[jax.ShapeDtypeStruct((m, d), F32)] + r_shape,
        scratch_shapes=[pltpu.VMEM((tm, d), BF16)],
        compiler_params=_params("parallel", "arbitrary", vmem_limit_bytes=V7X_VMEM_LIMIT_FFN_BYTES),
        name="ffn",
    )(x, g.reshape(1, d), w_up, w_down, g_final.reshape(1, d), *r_args)


def _lru_gates(xc, wg, bg, lam):
    g = _dot(xc.astype(BF16), wg) + bg
    r = jax.nn.sigmoid(g[:, :RNN_HD])
    i = jax.nn.sigmoid(g[:, RNN_HD:])
    log_a = -LRU_C * r * jax.nn.softplus(-lam)
    a = jnp.exp(log_a)
    u = jnp.sqrt(-jnp.tanh(log_a) * (a * a + 1.0)) * (i * xc)
    return a, u


def _rglru_kernel(xr_ref, gr_ref, h0_ref, c0_ref, cw_ref, cb_ref, wg_ref, bg_ref, lam_ref, gout_ref,
                  y_ref, hn_ref, cn_ref, xbuf, hc, ybuf, *, tc, rider=None):
    c = pl.program_id(1)

    @pl.when(c == 0)
    def _():
        xbuf[0:8, :] = jnp.zeros((8, D_RNN), F32)
        xbuf[5:8, :] = c0_ref[0]
        hc[...] = h0_ref[0]

    @pl.when(c > 0)
    def _():
        xbuf[0:8, :] = xbuf[tc:tc + 8, :]

    xbuf[8:8 + tc, :] = xr_ref[0]
    ng = tc // SUBLANES
    sub = lax.broadcasted_iota(jnp.int32, (ng, SUBLANES, RNN_HD), 1)
    ssq = jnp.zeros((tc, 1), F32)
    for h in range(RNN_HEADS):
        _run_rider_share(rider, h, RNN_HEADS)
        sl = slice(h * RNN_HD, (h + 1) * RNN_HD)
        xc = cb_ref[:, sl]
        for j in range(CONV_W):
            xc = xc + cw_ref[j:j + 1, sl] * xbuf[5 + j:5 + j + tc, sl]
        a, u = _lru_gates(xc, wg_ref[h], bg_ref[h], lam_ref[:, sl])
        a = a.reshape(ng, SUBLANES, RNN_HD)
        u = u.reshape(ng, SUBLANES, RNN_HD)
        d = 1
        while d < SUBLANES:
            keep = sub >= d
            a_prev = jnp.where(keep, pltpu.roll(a, d, 1), 1.0)
            u_prev = jnp.where(keep, pltpu.roll(u, d, 1), 0.0)
            u = u + a * u_prev
            a = a * a_prev
            d *= 2
        carry = hc[:, sl]
        for g in range(ng):
            hg = u[g] + a[g] * carry
            carry = hg[SUBLANES - 1:SUBLANES, :]
            ybuf[g * SUBLANES:(g + 1) * SUBLANES, sl] = hg
        hc[:, sl] = carry
        yv = ybuf[:, sl] * jax.nn.gelu(gr_ref[0, :, sl])
        ssq = ssq + jnp.sum(yv * yv, axis=-1, keepdims=True)
        ybuf[:, sl] = yv
    y = ybuf[...] * lax.rsqrt(ssq * (1.0 / D_RNN) + EPS) * gout_ref[...]
    y_ref[0] = y.astype(y_ref.dtype)

    @pl.when(c == pl.num_programs(1) - 1)
    def _():
        hn_ref[0] = hc[...]
        cn_ref[0] = xbuf[tc + 5:tc + 8, :]


def _rglru_prompt(z3, h0, conv0, w, *, tc, rider=None):
    b, t, _ = z3.shape
    full2 = lambda bi, ci: (0, 0)
    r_in, r_out, r_shape, r_args = _rider_parts(rider)
    in_specs = [pl.BlockSpec((1, tc, D_RNN), lambda bi, ci: (bi, ci, 0)),
                pl.BlockSpec((1, tc, D_RNN), lambda bi, ci: (bi, ci, 1)),
                pl.BlockSpec((1, 1, D_RNN), lambda bi, ci: (bi, 0, 0)),
                pl.BlockSpec((1, CONV_W - 1, D_RNN), lambda bi, ci: (bi, 0, 0)),
                pl.BlockSpec((CONV_W, D_RNN), full2),
                pl.BlockSpec((1, D_RNN), full2),
                pl.BlockSpec((RNN_HEADS, RNN_HD, 2 * RNN_HD), lambda bi, ci: (0, 0, 0)),
                pl.BlockSpec((RNN_HEADS, 1, 2 * RNN_HD), lambda bi, ci: (0, 0, 0)),
                pl.BlockSpec((1, D_RNN), full2),
                pl.BlockSpec((1, D_RNN), full2)]
    out_specs = [pl.BlockSpec((1, tc, D_RNN), lambda bi, ci: (bi, ci, 0)),
                 pl.BlockSpec((1, 1, D_RNN), lambda bi, ci: (bi, 0, 0)),
                 pl.BlockSpec((1, CONV_W - 1, D_RNN), lambda bi, ci: (bi, 0, 0))]
    out_shape = [jax.ShapeDtypeStruct((b, t, D_RNN), BF16),
                 jax.ShapeDtypeStruct((b, 1, D_RNN), F32),
                 jax.ShapeDtypeStruct((b, CONV_W - 1, D_RNN), F32)]
    return pl.pallas_call(
        _with_rider(functools.partial(_rglru_kernel, tc=tc), rider, len(in_specs), len(out_specs)),
        grid=(b, t // tc),
        in_specs=in_specs + r_in, out_specs=out_specs + r_out, out_shape=out_shape + r_shape,
        scratch_shapes=[pltpu.VMEM((tc + 8, D_RNN), F32),
                        pltpu.VMEM((1, D_RNN), F32),
                        pltpu.VMEM((tc, D_RNN), F32)],
        compiler_params=_params("parallel", "arbitrary"),
        name="rglru_prompt",
    )(z3, z3, h0, conv0, w['conv_rnn_w'], w['conv_rnn_b'], w['lru_wg'], w['lru_bg'],
      w['lru_lambda'], w['g_rnn_out'], *r_args)


def _mlstm_kernel(u_ref, v_ref, og_ref, zg_ref, gb_ref, c0_ref, n0_ref, m0_ref, cv0_ref,
                  cw_ref, cb_ref, wqk_ref, gout_ref,
                  y_ref, cn_ref, nn_ref, mn_ref, cvn_ref, ubuf, cst, nst, mst, rider=None):
    c = pl.program_id(1)
    L = CHUNK
    n_seq = u_ref.shape[0]

    @pl.when(c == 0)
    def _():
        ubuf[:, 0:8, :] = jnp.zeros((n_seq, 8, D_MLSTM), F32)
        ubuf[:, 5:8, :] = cv0_ref[...]
        cst[...] = c0_ref[...]
        nst[...] = n0_ref[...]
        mst[...] = m0_ref[...]

    @pl.when(c > 0)
    def _():
        ubuf[:, 0:8, :] = ubuf[:, L:L + 8, :]

    ubuf[:, 8:8 + L, :] = u_ref[...]

    ti = lax.broadcasted_iota(jnp.int32, (L, L), 0)
    si = lax.broadcasted_iota(jnp.int32, (L, L), 1)
    causal = si <= ti
    anti = ti <= si
    zgs = [zg_ref[s] + gb_ref[...] for s in range(n_seq)]
    zgts = [zg.T for zg in zgs]

    for h in range(MLSTM_HEADS):
        _run_rider_share(rider, h, MLSTM_HEADS)
        sl = slice(h * MLSTM_HD, (h + 1) * MLSTM_HD)
        for s in range(n_seq):
            zg, zgt = zgs[s], zgts[s]
            uc = cb_ref[:, sl]
            for j in range(CONV_W):
                uc = uc + cw_ref[j:j + 1, sl] * ubuf[s, 5 + j:5 + j + L, sl]
            uc = jax.nn.silu(uc)
            qk = _dot(uc.astype(BF16), wqk_ref[h])
            q = qk[:, :MLSTM_HD]
            k = qk[:, MLSTM_HD:] * (MLSTM_HD ** -0.5)
            v = v_ref[s, :, sl]
            qb, kb = q.astype(BF16), k.astype(BF16)

            icol = zg[:, h:h + 1]
            fcol = jax.nn.log_sigmoid(zg[:, MLSTM_HEADS + h:MLSTM_HEADS + h + 1])
            irow = zgt[h:h + 1, :]
            frow = jax.nn.log_sigmoid(zgt[MLSTM_HEADS + h:MLSTM_HEADS + h + 1, :])
            bcol = jnp.sum(jnp.where(causal, frow, 0.0), axis=1, keepdims=True)
            brow = jnp.sum(jnp.where(anti, fcol, 0.0), axis=0, keepdims=True)
            dmat = jnp.where(causal, irow + bcol - brow, -jnp.inf)
            m_prev = mst[s, h, :, 0:1]
            inter = bcol + m_prev
            m_t = jnp.maximum(inter, jnp.max(dmat, axis=1, keepdims=True))
            w_inter = jnp.exp(inter - m_t)
            sc = lax.dot_general(qb, kb, (((1,), (1,)), ((), ())), preferred_element_type=F32) * jnp.exp(dmat - m_t)
            cmat = cst[s, h]
            nrow = nst[s, h]
            cq = lax.dot_general(qb, cmat.astype(BF16), (((1,), (1,)), ((), ())), preferred_element_type=F32)
            num = w_inter * cq + _dot(sc.astype(BF16), v.astype(BF16))
            den = w_inter * jnp.sum(q * nrow, axis=1, keepdims=True) + jnp.sum(sc, axis=1, keepdims=True)
            hm = num / jnp.maximum(jnp.abs(den), jnp.exp(-m_t))

            m_new = m_t[L - 1:L, :]
            b_last = bcol[L - 1:L, :]
            g_state = jnp.exp(b_last + m_prev - m_new)
            g_in = jnp.exp(icol + b_last - bcol - m_new)
            cst[s, h] = g_state * cmat + lax.dot_general((g_in * v).astype(BF16), kb, (((0,), (0,)), ((), ())),
                                                         preferred_element_type=F32)
            nst[s, h] = g_state * nrow + jnp.sum(g_in * k, axis=0, keepdims=True)
            mst[s, h] = jnp.broadcast_to(m_new, (1, N_GATE_PAD))

            y = _rms(hm, gout_ref[...]) * jax.nn.sigmoid(og_ref[s, :, sl])
            y_ref[s, :, sl] = y.astype(y_ref.dtype)

    @pl.when(c == pl.num_programs(1) - 1)
    def _():
        cn_ref[...] = cst[...]
        nn_ref[...] = nst[...]
        mn_ref[...] = mst[...]
        cvn_ref[...] = ubuf[:, L + 5:L + 8, :]


MLSTM_SEQS = 1


def _mlstm_prompt(z3, zg3, c0, n0, m0, conv0, w, *, rider=None):
    b, t, _ = z3.shape
    nc = t // CHUNK
    ns = MLSTM_SEQS
    u_blk, v_blk, o_blk = 2 * D_RNN // D_MLSTM, 3 * D_RNN // D_MLSTM, 4 * D_RNN // D_MLSTM
    zcol = lambda blk: pl.BlockSpec((ns, CHUNK, D_MLSTM), lambda bi, ci: (bi, ci, blk))
    state4 = lambda *minor: pl.BlockSpec((ns, MLSTM_HEADS) + minor, lambda bi, ci: (bi, 0, 0, 0))
    conv_spec = pl.BlockSpec((ns, CONV_W - 1, D_MLSTM), lambda bi, ci: (bi, 0, 0))
    const2 = lambda shape: pl.BlockSpec(shape, lambda bi, ci: (0, 0))
    r_in, r_out, r_shape, r_args = _rider_parts(rider)
    in_specs = [zcol(u_blk), zcol(v_blk), zcol(o_blk),
                pl.BlockSpec((ns, CHUNK, N_GATE_PAD), lambda bi, ci: (bi, ci, 0)),
                const2((1, N_GATE_PAD)),
                state4(MLSTM_HD, MLSTM_HD), state4(1, MLSTM_HD), state4(1, N_GATE_PAD), conv_spec,
                const2((CONV_W, D_MLSTM)), const2((1, D_MLSTM)),
                pl.BlockSpec((MLSTM_HEADS, MLSTM_HD, 2 * MLSTM_HD), lambda bi, ci: (0, 0, 0)),
                const2((1, MLSTM_HD))]
    out_specs = [pl.BlockSpec((ns, CHUNK, D_MLSTM), lambda bi, ci: (bi, ci, 0)),
                 state4(MLSTM_HD, MLSTM_HD), state4(1, MLSTM_HD), state4(1, N_GATE_PAD), conv_spec]
    out_shape = [jax.ShapeDtypeStruct((b, t, D_MLSTM), BF16),
                 jax.ShapeDtypeStruct((b, MLSTM_HEADS, MLSTM_HD, MLSTM_HD), F32),
                 jax.ShapeDtypeStruct((b, MLSTM_HEADS, 1, MLSTM_HD), F32),
                 jax.ShapeDtypeStruct((b, MLSTM_HEADS, 1, N_GATE_PAD), F32),
                 jax.ShapeDtypeStruct((b, CONV_W - 1, D_MLSTM), F32)]
    return pl.pallas_call(
        _with_rider(_mlstm_kernel, rider, len(in_specs), len(out_specs)),
        grid=(b // ns, nc),
        in_specs=in_specs + r_in, out_specs=out_specs + r_out, out_shape=out_shape + r_shape,
        scratch_shapes=[pltpu.VMEM((ns, CHUNK + 8, D_MLSTM), F32),
                        pltpu.VMEM((ns, MLSTM_HEADS, MLSTM_HD, MLSTM_HD), F32),
                        pltpu.VMEM((ns, MLSTM_HEADS, 1, MLSTM_HD), F32),
                        pltpu.VMEM((ns, MLSTM_HEADS, 1, N_GATE_PAD), F32)],
        compiler_params=_params("parallel", "arbitrary"),
        name="mlstm_prompt",
    )(z3, z3, z3, zg3, w['gate_bias'], c0, n0, m0, conv0, w['conv_ml_w'], w['conv_ml_b'],
      w['ml_wqk'], w['g_ml_out'], *r_args)


def _xattn_kernel(q_ref, k_ref, v_ref, x_ref, wo_ref, o_ref, att):
    @pl.when(pl.program_id(2) == 0)
    def _():
        for h in range(X_HEADS):
            sl = slice(h * X_HD, (h + 1) * X_HD)
            s = _dot_t(q_ref[0, :, sl], k_ref[0, :, sl]) * (X_HD ** -0.5)
            e = jnp.exp(s - jnp.max(s, axis=-1, keepdims=True))
            p = e * (1.0 / jnp.sum(e, axis=-1, keepdims=True))
            att[:, sl] = _dot(p, v_ref[0, :, sl]).astype(att.dtype)

    o_ref[0] = x_ref[0] + _dot(att[...], wo_ref[...])


def _xattn_prompt(q3, mk3, mv3, x3, w_co, *, tq, tn):
    b, t, d = q3.shape
    mem = pl.BlockSpec((1, N_MEM, d), lambda bi, ti, j: (bi, 0, 0))
    cols = pl.BlockSpec((1, tq, tn), lambda bi, ti, j: (bi, ti, j))
    return pl.pallas_call(
        _xattn_kernel,
        grid=(b, t // tq, d // tn),
        in_specs=[pl.BlockSpec((1, tq, d), lambda bi, ti, j: (bi, ti, 0)), mem, mem, cols,
                  pl.BlockSpec((d, tn), lambda bi, ti, j: (0, j))],
        out_specs=cols,
        out_shape=jax.ShapeDtypeStruct((b, t, d), F32),
        scratch_shapes=[pltpu.VMEM((tq, d), BF16)],
        compiler_params=_params("parallel", "parallel", "arbitrary"),
        name="xattn_prompt",
    )(q3, mk3, mv3, x3, w_co)


XS_SUB = 2 * X_HEADS
XS_PAIRS = X_HD // (2 * 128)


def _pack_heads(x):
    lead = x.shape[:-1]
    x = x.reshape(*lead, X_HEADS, 2 * XS_PAIRS, 128)
    x = jnp.swapaxes(x, -3, -2)
    return x.reshape(*lead, XS_PAIRS, XS_SUB, 128)


def _unpack_heads(x):
    lead = x.shape[:-3]
    x = x.reshape(*lead, 2 * XS_PAIRS, X_HEADS, 128)
    x = jnp.swapaxes(x, -3, -2)
    return x.reshape(*lead, X_HEADS * X_HD)


def _xattn_sample_units(q_ref, k_ref, v_ref, o_ref, r):
    def row():
        t = jnp.sum(k_ref[r] * q_ref[r], axis=1)
        t = t + pltpu.roll(t, X_HEADS, 1)
        s = jnp.sum(t, axis=-1, keepdims=True) * (X_HD ** -0.5)
        e = jnp.exp(s - jnp.max(s, axis=0, keepdims=True))
        p = e / jnp.sum(e, axis=0, keepdims=True)
        o_ref[r] = jnp.sum(p[:, None] * v_ref[r], axis=0)

    return [row]


def _xattn_sample_rider(grid, q, ck, cv):
    b = q.shape[0]
    n_steps, step = _grid_steps(grid)
    rows = b // n_steps
    kv_spec = pl.BlockSpec((rows, N_MEM, XS_PAIRS, XS_SUB, 128), lambda *g: (step(*g), 0, 0, 0, 0))
    q_spec = pl.BlockSpec((rows, XS_PAIRS, XS_SUB, 128), lambda *g: (step(*g), 0, 0, 0))
    return dict(make_units=_per_row_units(_xattn_sample_units, rows),
                in_specs=[q_spec, kv_spec, kv_spec], out_specs=[q_spec],
                out_shape=[jax.ShapeDtypeStruct((b, XS_PAIRS, XS_SUB, 128), F32)],
                args=[_pack_heads(q), _pack_heads(ck.reshape(b, N_MEM, X_HEADS * X_HD)),
                      _pack_heads(cv.reshape(b, N_MEM, X_HEADS * X_HD))])


def _smix_rows_kernel(z_ref, h0_ref, rc_ref, mc_ref, cwr_ref, cbr_ref, wg_ref, bg_ref, lam_ref, gout_ref,
                      cwm_ref, cbm_ref, wqk_ref,
                      yr_ref, hn_ref, rcn_ref, mcn_ref, q_ref, k_ref, ybuf):
    nb = z_ref.shape[0]
    ssq = jnp.zeros((nb, 1), F32)
    for h in range(RNN_HEADS):
        sl = slice(h * RNN_HD, (h + 1) * RNN_HD)
        xc = cbr_ref[:, sl] + cwr_ref[CONV_W - 1:CONV_W, sl] * z_ref[:, sl]
        for j in range(CONV_W - 1):
            xc = xc + cwr_ref[j:j + 1, sl] * rc_ref[:, j * D_RNN + h * RNN_HD:j * D_RNN + (h + 1) * RNN_HD]
        a, u = _lru_gates(xc, wg_ref[h], bg_ref[h], lam_ref[:, sl])
        hs = a * h0_ref[:, sl] + u
        hn_ref[:, sl] = hs
        yv = hs * jax.nn.gelu(z_ref[:, D_RNN + h * RNN_HD:D_RNN + (h + 1) * RNN_HD])
        ssq = ssq + jnp.sum(yv * yv, axis=-1, keepdims=True)
        ybuf[:, sl] = yv
    yr_ref[...] = (ybuf[...] * lax.rsqrt(ssq * (1.0 / D_RNN) + EPS) * gout_ref[...]).astype(yr_ref.dtype)
    rcn_ref[:, 0:2 * D_RNN] = rc_ref[:, D_RNN:3 * D_RNN]
    rcn_ref[:, 2 * D_RNN:3 * D_RNN] = z_ref[:, 0:D_RNN]

    for h in range(MLSTM_HEADS):
        sl = slice(h * MLSTM_HD, (h + 1) * MLSTM_HD)
        uc = cbm_ref[:, sl] + cwm_ref[CONV_W - 1:CONV_W, sl] * z_ref[:, 2 * D_RNN + h * MLSTM_HD:2 * D_RNN + (h + 1) * MLSTM_HD]
        for j in range(CONV_W - 1):
            uc = uc + cwm_ref[j:j + 1, sl] * mc_ref[:, j * D_MLSTM + h * MLSTM_HD:j * D_MLSTM + (h + 1) * MLSTM_HD]
        uc = jax.nn.silu(uc)
        qk = _dot(uc.astype(BF16), wqk_ref[h])
        q_ref[:, sl] = qk[:, :MLSTM_HD]
        k_ref[:, sl] = qk[:, MLSTM_HD:] * (MLSTM_HD ** -0.5)
    mcn_ref[:, 0:2 * D_MLSTM] = mc_ref[:, D_MLSTM:3 * D_MLSTM]
    mcn_ref[:, 2 * D_MLSTM:3 * D_MLSTM] = z_ref[:, 2 * D_RNN:2 * D_RNN + D_MLSTM]


def _smix_rows(z, h0, rconv, mconv, w):
    nb = z.shape[0]
    outs = [jax.ShapeDtypeStruct((nb, D_RNN), BF16),
            jax.ShapeDtypeStruct((nb, D_RNN), F32),
            jax.ShapeDtypeStruct((nb, 3 * D_RNN), F32),
            jax.ShapeDtypeStruct((nb, 3 * D_MLSTM), F32),
            jax.ShapeDtypeStruct((nb, D_MLSTM), F32),
            jax.ShapeDtypeStruct((nb, D_MLSTM), F32)]
    return pl.pallas_call(
        _smix_rows_kernel,
        out_shape=outs,
        scratch_shapes=[pltpu.VMEM((nb, D_RNN), F32)],
        compiler_params=pltpu.CompilerParams(vmem_limit_bytes=V7X_VMEM_LIMIT_BYTES),
        name="smix_rows",
    )(z, h0, rconv, mconv, w['conv_rnn_w'], w['conv_rnn_b'], w['lru_wg'], w['lru_bg'], w['lru_lambda'],
      w['g_rnn_out'], w['conv_ml_w'], w['conv_ml_b'], w['ml_wqk'])


MXU_MIN_ROWS = 8


def _smix_state_units(c_ref, q_ref, k_ref, n_ref, v_ref, og_ref, zg_ref, gb_ref, m_ref, gout_ref,
                      cn_ref, nn_ref, mn_ref, y_ref, r):
    lane_g = lax.broadcasted_iota(jnp.int32, (1, N_GATE_PAD), 1)
    first_row = lax.broadcasted_iota(jnp.int32, (MXU_MIN_ROWS, MLSTM_HD), 0) == 0

    def head(h, zg, m_out):
        sl = slice(h * MLSTM_HD, (h + 1) * MLSTM_HD)
        q = q_ref[r, :, sl]
        k = k_ref[r, :, sl]
        n = n_ref[r, :, sl]
        v = v_ref[r, :, sl]
        ig = zg[:, h:h + 1]
        lf = jax.nn.log_sigmoid(zg[:, MLSTM_HEADS + h:MLSTM_HEADS + h + 1])
        m_prev = m_ref[r, :, h:h + 1]
        m_t = jnp.maximum(lf + m_prev, ig)
        w_inter = jnp.exp(lf + m_prev - m_t)
        g_in = jnp.exp(ig - m_t)
        s = jnp.sum(q * k, axis=1, keepdims=True) * g_in
        den = w_inter * jnp.sum(n * q, axis=1, keepdims=True) + s
        denom = jnp.maximum(jnp.abs(den), jnp.exp(-m_t))
        cmat = c_ref[r, h]
        q8 = jnp.broadcast_to(q, (MXU_MIN_ROWS, MLSTM_HD)).astype(BF16)
        cq = lax.dot_general(q8, cmat.astype(BF16), (((1,), (1,)), ((), ())),
                             preferred_element_type=F32)[0:1, :]
        hm = (w_inter * cq + s * v) / denom
        gv8 = jnp.where(first_row, jnp.broadcast_to(g_in * v, (MXU_MIN_ROWS, MLSTM_HD)), 0.0).astype(BF16)
        k8 = jnp.broadcast_to(k, (MXU_MIN_ROWS, MLSTM_HD)).astype(BF16)
        outer = lax.dot_general(gv8, k8, (((0,), (0,)), ((), ())), preferred_element_type=F32)
        cn_ref[r, h] = w_inter * cmat + outer
        nn_ref[r, :, sl] = w_inter * n + g_in * k
        y = _rms(hm, gout_ref[...]) * jax.nn.sigmoid(og_ref[r, :, sl])
        y_ref[r, :, sl] = y.astype(y_ref.dtype)
        return jnp.where(lane_g == h, m_t, m_out)

    def row():
        zg = zg_ref[r] + gb_ref[...]
        m_out = jnp.zeros((1, N_GATE_PAD), F32)
        for h in range(MLSTM_HEADS):
            m_out = head(h, zg, m_out)
        mn_ref[r] = m_out

    return [row]


def _smix_state_rider(grid, c, q3, k3, n3, z3, zg3, gate_bias, m3, gout):
    nb = c.shape[0]
    n_steps, step = _grid_steps(grid)
    rows = nb // n_steps
    row = lambda width, col=0: pl.BlockSpec((rows, 1, width), lambda *g: (step(*g), 0, col))
    cblk = pl.BlockSpec((rows, MLSTM_HEADS, MLSTM_HD, MLSTM_HD), lambda *g: (step(*g), 0, 0, 0))
    return dict(
        make_units=_per_row_units(_smix_state_units, rows),
        in_specs=[cblk, row(D_MLSTM), row(D_MLSTM), row(D_MLSTM),
                  row(D_MLSTM, 3 * D_RNN // D_MLSTM), row(D_MLSTM, 4 * D_RNN // D_MLSTM),
                  row(N_GATE_PAD),
                  pl.BlockSpec((1, N_GATE_PAD), lambda *g: (0, 0)),
                  row(MLSTM_HEADS),
                  pl.BlockSpec((1, MLSTM_HD), lambda *g: (0, 0))],
        out_specs=[cblk, row(D_MLSTM), row(N_GATE_PAD), row(D_MLSTM)],
        out_shape=[jax.ShapeDtypeStruct(c.shape, F32),
                   jax.ShapeDtypeStruct((nb, 1, D_MLSTM), F32),
                   jax.ShapeDtypeStruct((nb, 1, N_GATE_PAD), F32),
                   jax.ShapeDtypeStruct((nb, 1, D_MLSTM), F32)],
        args=[c, q3, k3, n3, z3, z3, zg3, gate_bias, m3, gout])


RGLRU_CHUNK = 256
PROMPT_TM = 1024
PROMPT_W_IN_TN = 1280
SAMPLE_FFN_TF = 2048


def _layer(xp, xs, mem, rg_h, rg_conv, c, n, mst, ml_conv, ck, cv, w, g_final, final_norm):
    b, t, d = xp.shape
    nb = xs.shape[0]
    m = b * t
    tm = PROMPT_TM
    xp2 = xp.reshape(m, d)
    xs2 = xs.reshape(nb, d)
    w_in_args = dict(tn=1024, out_dtype=F32, w_extra=w['w_in_gate'], n_cols=D_MAIN, w_is_transposed=True)

    zs, zgs, w_in_bf = _norm_matmul(xs2, w['g_mix'], w['w_in_main'], tm=nb, emit_w_bf16=True, **w_in_args)
    ys_rnn, s_h, s_rconv, s_mconv, qs, ks = _smix_rows(
        zs, rg_h, rg_conv.reshape(nb, 3 * D_RNN), ml_conv.reshape(nb, 3 * D_MLSTM), w)

    z, zg = _norm_matmul(xp2, w['g_mix'], w_in_bf, tm=tm, **dict(w_in_args, tn=PROMPT_W_IN_TN))
    z3 = z.reshape(b, t, D_MAIN)
    zg3 = zg.reshape(b, t, N_GATE_PAD)
    h0 = jnp.zeros((b, 1, D_RNN), F32)
    conv0 = jnp.zeros((b, CONV_W - 1, D_RNN), F32)
    rg_grid = (b, t // RGLRU_CHUNK)
    state_rider = _smix_state_rider(
        rg_grid, c, qs.reshape(nb, 1, D_MLSTM), ks.reshape(nb, 1, D_MLSTM), n.reshape(nb, 1, D_MLSTM),
        zs.reshape(nb, 1, D_MAIN), zgs.reshape(nb, 1, N_GATE_PAD), w['gate_bias'],
        mst.reshape(nb, 1, MLSTM_HEADS), w['g_ml_out'])
    rg_rider = _merge_riders(state_rider, _cast_rider(rg_grid, [[w['w_out'], w['w_cq']], [w['w_co']]]))
    y_rnn, p_h, p_rconv, s_c, s_n, s_m, ys_ml, w_out_cq, w_co = _rglru_prompt(
        z3, h0, conv0, w, tc=RGLRU_CHUNK, rider=rg_rider)

    xs1, qx = _matmul_res_norm_matmul([ys_rnn, ys_ml.reshape(nb, D_MLSTM).astype(BF16)], w_out_cq, xs2,
                                      w['g_xattn'], tm=nb, tn=1024, out2_dtype=F32)

    c0 = jnp.zeros((b, MLSTM_HEADS, MLSTM_HD, MLSTM_HD), F32)
    n0 = jnp.zeros((b, MLSTM_HEADS, 1, MLSTM_HD), F32)
    m0 = jnp.zeros((b, MLSTM_HEADS, 1, N_GATE_PAD), F32)
    mconv0 = jnp.zeros((b, CONV_W - 1, D_MLSTM), F32)
    y_ml, p_c, p_n, p_m, p_mconv, w_up, w_down = _mlstm_prompt(
        z3, zg3, c0, n0, m0, mconv0, w,
        rider=_cast_rider((b // MLSTM_SEQS, t // CHUNK), [[w['w_up']], [w['w_down']]]))

    x1, q = _matmul_res_norm_matmul([y_rnn.reshape(m, D_RNN), y_ml.reshape(m, D_MLSTM)], w_out_cq, xp2,
                                    w['g_xattn'], tm=tm, tn=1024, out2_dtype=BF16)
    mem2 = mem.reshape(b * N_MEM, d)
    mk, mv = _norm_matmul_pair(mem2, w['g_mem'], w['w_mk'], w['w_mv'], tn=512)
    x2 = _xattn_prompt(q.reshape(b, t, d), mk.reshape(b, N_MEM, d), mv.reshape(b, N_MEM, d), x1.reshape(b, t, d),
                       w_co, tq=tm, tn=1024)
    ffn_tf = 512
    xp_out, os_packed = _ffn(x2.reshape(m, d), w['g_ffn'], w_up, w_down, g_final, tm=tm, tf=ffn_tf,
                             final_norm=final_norm,
                             rider=_xattn_sample_rider((m // tm, D_FF // ffn_tf), qx, ck, cv))

    os_ = _unpack_heads(os_packed).astype(BF16)
    xs2_ = _matmul_res([os_], w_co, xs1, tm=nb, tn=1024)
    xs_out, = _ffn(xs2_, w['g_ffn'], w_up, w_down, g_final, tm=nb, tf=SAMPLE_FFN_TF, final_norm=final_norm)

    new_p = (p_h.reshape(b, D_RNN), p_rconv, p_c, p_n.reshape(b, MLSTM_HEADS, MLSTM_HD), p_m[:, :, 0, 0], p_mconv,
             mk.reshape(b, N_MEM, X_HEADS, X_HD), mv.reshape(b, N_MEM, X_HEADS, X_HD))
    new_s = (s_h, s_rconv.reshape(nb, CONV_W - 1, D_RNN), s_c, s_n.reshape(nb, MLSTM_HEADS, MLSTM_HD),
             s_m[:, 0, :MLSTM_HEADS], s_mconv.reshape(nb, CONV_W - 1, D_MLSTM))
    return xp_out.reshape(b, t, d), xs_out.reshape(nb, 1, d), new_p, new_s


def _layer_weights(l, g_mix, w_in, conv_rnn_w, conv_rnn_b, lru_wa, lru_ba, lru_wx, lru_bx, lru_lambda,
                   g_rnn_out, conv_ml_w, conv_ml_b, ml_wq, ml_wk, ml_bi, ml_bf, g_ml_out, w_out,
                   g_xattn, g_mem, w_cq, w_mk, w_mv, w_co, g_ffn, w_up, w_down):
    n_gate = 2 * MLSTM_HEADS
    w_in_t = jnp.swapaxes(w_in[l], 0, 1)
    w_gate_t = jnp.pad(w_in_t[D_MAIN:], ((0, N_GATE_PAD - n_gate), (0, 0)))
    gate_bias = jnp.pad(jnp.concatenate([ml_bi[l], ml_bf[l]]), (0, N_GATE_PAD - n_gate))
    return dict(
        g_mix=g_mix[l], w_in_main=w_in_t, w_in_gate=w_gate_t.astype(BF16),
        conv_rnn_w=conv_rnn_w[l], conv_rnn_b=conv_rnn_b[l].reshape(1, D_RNN),
        lru_wg=jnp.concatenate([lru_wa[l], lru_wx[l]], axis=-1).astype(BF16),
        lru_bg=jnp.concatenate([lru_ba[l], lru_bx[l]], axis=-1).reshape(RNN_HEADS, 1, 2 * RNN_HD),
        lru_lambda=lru_lambda[l].reshape(1, D_RNN), g_rnn_out=g_rnn_out[l].reshape(1, D_RNN),
        conv_ml_w=conv_ml_w[l], conv_ml_b=conv_ml_b[l].reshape(1, D_MLSTM),
        ml_wqk=jnp.concatenate([ml_wq[l], ml_wk[l]], axis=-1).astype(BF16),
        gate_bias=gate_bias.reshape(1, N_GATE_PAD), g_ml_out=g_ml_out[l].reshape(1, MLSTM_HD),
        w_out=w_out[l], w_cq=w_cq[l], g_xattn=g_xattn[l], g_mem=g_mem[l], w_mk=w_mk[l], w_mv=w_mv[l],
        w_co=w_co[l], g_ffn=g_ffn[l], w_up=w_up[l], w_down=w_down[l])


def kernel(x_prompt, x_sample, mem_prompt, state_rglru_h, state_rglru_conv, state_mlstm_C, state_mlstm_n, state_mlstm_m, state_mlstm_conv, cache_mem_k, cache_mem_v, g_mix, w_in, conv_rnn_w, conv_rnn_b, lru_wa, lru_ba, lru_wx, lru_bx, lru_lambda, g_rnn_out, conv_ml_w, conv_ml_b, ml_wq, ml_wk, ml_bi, ml_bf, g_ml_out, w_out, g_xattn, g_mem, w_cq, w_mk, w_mv, w_co, g_ffn, w_up, w_down, g_final):
    depth = w_in.shape[0]
    xp, xs = x_prompt, x_sample
    p_out = [[] for _ in range(8)]
    s_out = [[] for _ in range(6)]
    for l in range(depth):
        w = _layer_weights(l, g_mix, w_in, conv_rnn_w, conv_rnn_b, lru_wa, lru_ba, lru_wx, lru_bx, lru_lambda,
                           g_rnn_out, conv_ml_w, conv_ml_b, ml_wq, ml_wk, ml_bi, ml_bf, g_ml_out, w_out,
                           g_xattn, g_mem, w_cq, w_mk, w_mv, w_co, g_ffn, w_up, w_down)
        last = l == depth - 1
        xp, xs, new_p, new_s = _layer(xp, xs, mem_prompt, state_rglru_h[l], state_rglru_conv[l], state_mlstm_C[l],
                                      state_mlstm_n[l], state_mlstm_m[l], state_mlstm_conv[l], cache_mem_k[l],
                                      cache_mem_v[l], w, g_final, last)
        for j, a in enumerate(new_p):
            p_out[j].append(a)
        for j, a in enumerate(new_s):
            s_out[j].append(a)
    P = [jnp.stack(a, axis=0) for a in p_out]
    S = [jnp.stack(a, axis=0) for a in s_out]
    return (xp, xs, P[0], P[1], P[2], P[3], P[4], P[5], P[6], P[7],
            S[0], S[1], S[2], S[3], S[4], S[5])
```

```python
import functools

import jax
import jax.numpy as jnp
from jax import lax
from jax.experimental import pallas as pl
from jax.experimental.pallas import tpu as pltpu

F32 = jnp.float32
BF16 = jnp.bfloat16

D_MODEL = 2048
D_RNN = 1024
RNN_HEADS = 8
RNN_HD = 128
CONV_W = 4
LRU_C = 8.0
D_MLSTM = 1024
MLSTM_HEADS = 4
MLSTM_HD = 256
CHUNK = 128
N_MEM = 256
X_HEADS = 4
X_HD = 512
D_FF = 8192
EPS = 1e-6
D_MAIN = 5 * 1024
N_GATE_PAD = 128

V7X_VMEM_LIMIT_BYTES = 56 * 1024 * 1024
V7X_VMEM_LIMIT_FFN_BYTES = 63 * 1024 * 1024
SUBLANES = 8


def _params(*sem, vmem_limit_bytes=V7X_VMEM_LIMIT_BYTES):
    return pltpu.CompilerParams(dimension_semantics=sem, vmem_limit_bytes=vmem_limit_bytes)


def _rms(x, g):
    ms = jnp.mean(x * x, axis=-1, keepdims=True)
    return x * lax.rsqrt(ms + EPS) * g


def _dot(a, b):
    return jnp.dot(a.astype(BF16), b.astype(BF16), preferred_element_type=F32)


def _dot_t(a, bt):
    return lax.dot_general(a.astype(BF16), bt.astype(BF16), (((1,), (1,)), ((), ())), preferred_element_type=F32)


def _with_rider(host_body, rider, n_in, n_out):
    if rider is None:
        return host_body
    r_in, r_out = len(rider['in_specs']), len(rider['out_specs'])

    def body(*refs):
        host_in, refs = refs[:n_in], refs[n_in:]
        rider_in, refs = refs[:r_in], refs[r_in:]
        host_out, refs = refs[:n_out], refs[n_out:]
        rider_out, scratch = refs[:r_out], refs[r_out:]
        host_body(*host_in, *host_out, *scratch, rider=rider['make_units'](*rider_in, *rider_out))
    return body


def _grid_steps(grid):
    n = 1
    for g in grid:
        n *= g

    def step(*idx):
        s = idx[0]
        for g, i in zip(grid[1:], idx[1:]):
            s = s * g + i
        return s
    return n, step


def _per_row_units(row_units, rows):
    return lambda *refs: [u for r in range(rows) for u in row_units(*refs, r)]


def _merge_riders(a, b):
    na_in, na_out = len(a['in_specs']), len(a['out_specs'])
    nb_in = len(b['in_specs'])

    def make_units(*refs):
        a_in, refs = refs[:na_in], refs[na_in:]
        b_in, refs = refs[:nb_in], refs[nb_in:]
        a_out, b_out = refs[:na_out], refs[na_out:]
        return a['make_units'](*a_in, *a_out) + b['make_units'](*b_in, *b_out)

    return dict(make_units=make_units, **{k: a[k] + b[k] for k in ('in_specs', 'out_specs', 'out_shape', 'args')})


def _cast_rider(grid, groups):
    n_steps, step = _grid_steps(grid)
    in_specs, out_specs, out_shape, args, widths = [], [], [], [], []
    for group in groups:
        rps = group[0].shape[0] // n_steps
        for a in group:
            in_specs.append(pl.BlockSpec((rps, a.shape[1]), lambda *g: (step(*g), 0)))
            args.append(a)
        cols = sum(a.shape[1] for a in group)
        out_specs.append(pl.BlockSpec((rps, cols), lambda *g: (step(*g), 0)))
        out_shape.append(jax.ShapeDtypeStruct((group[0].shape[0], cols), BF16))
        widths.append([a.shape[1] for a in group])

    def make_units(*refs):
        in_refs, out_refs = list(refs[:len(args)]), refs[len(args):]
        units = []
        for o_ref, ws in zip(out_refs, widths):
            srcs, in_refs = in_refs[:len(ws)], in_refs[len(ws):]

            def cast(o_ref=o_ref, srcs=srcs, ws=ws):
                c0 = 0
                for s_ref, wd in zip(srcs, ws):
                    o_ref[:, c0:c0 + wd] = s_ref[...].astype(BF16)
                    c0 += wd
            units.append(cast)
        return units

    return dict(make_units=make_units, in_specs=in_specs, out_specs=out_specs, out_shape=out_shape, args=args)


def _run_rider_share(rider, i, n):
    if rider is None:
        return
    for unit in rider[i * len(rider) // n:(i + 1) * len(rider) // n]:
        unit()


def _rider_parts(rider):
    if rider is None:
        return [], [], [], []
    return rider['in_specs'], rider['out_specs'], rider['out_shape'], rider['args']


def _norm_matmul_kernel(*refs, has_extra, w_is_transposed, emit_w_bf16, ring=0, grid=None):
    refs = list(refs)
    if ring:
        sem = refs.pop()
        wbuf = refs.pop()
    xn_ref = refs.pop()
    wb_ref = refs.pop() if emit_w_bf16 else None
    if has_extra:
        x_ref, g_ref, w_ref, we_ref, o_ref, oe_ref = refs
    else:
        x_ref, g_ref, w_ref, o_ref = refs
    dot = _dot_t if w_is_transposed else _dot

    if ring:
        ni, nj = grid
        total = ni * nj
        tn = wbuf.shape[1 if w_is_transposed else 2]
        s = pl.program_id(0) * nj + pl.program_id(1)

        def fetch(step):
            col = pl.ds(pl.multiple_of((step % nj) * tn, 128), tn)
            src = w_ref.at[col, :] if w_is_transposed else w_ref.at[:, col]
            slot = step % ring
            return pltpu.make_async_copy(src, wbuf.at[slot], sem.at[slot])

        @pl.when(s == 0)
        def _():
            for p in range(min(ring - 1, total)):
                fetch(p).start()

        @pl.when(s + (ring - 1) < total)
        def _():
            fetch(s + (ring - 1)).start()

    @pl.when(pl.program_id(1) == 0)
    def _():
        xn = _rms(x_ref[...], g_ref[...]).astype(BF16)
        xn_ref[...] = xn
        if has_extra:
            oe_ref[...] = dot(xn, we_ref[...])

    if ring:
        fetch(s).wait()
        wb = wbuf[s % ring]
    else:
        wb = w_ref[...].astype(BF16)
    if emit_w_bf16:
        wb_ref[...] = wb
    o_ref[...] = dot(xn_ref[...], wb).astype(o_ref.dtype)


def _norm_matmul(x, g, w, *, tm, tn, out_dtype, w_extra=None, n_cols=None, w_is_transposed=False,
                 emit_w_bf16=False, w_buffers=2):
    m, k = x.shape
    n = n_cols or w.shape[0 if w_is_transposed else 1]
    has_extra = w_extra is not None
    w_block, w_map = ((tn, k), lambda i, j: (j, 0)) if w_is_transposed else ((k, tn), lambda i, j: (0, j))
    w_spec = pl.BlockSpec(w_block, w_map)
    ring = w_buffers if w_buffers > 2 else 0
    assert not (ring and (emit_w_bf16 or w.dtype != BF16))
    in_specs = [pl.BlockSpec((tm, k), lambda i, j: (i, 0)),
                pl.BlockSpec((1, k), lambda i, j: (0, 0)),
                pl.BlockSpec(memory_space=pl.ANY) if ring else w_spec]
    scratch = [pltpu.VMEM((tm, k), BF16)]
    if ring:
        scratch += [pltpu.VMEM((ring,) + w_block, BF16), pltpu.SemaphoreType.DMA((ring,))]
    out_specs = [pl.BlockSpec((tm, tn), lambda i, j: (i, j))]
    out_shape = [jax.ShapeDtypeStruct((m, n), out_dtype)]
    args = [x, g.reshape(1, k), w]
    if has_extra:
        ne = w_extra.shape[0 if w_is_transposed else 1]
        in_specs.append(pl.BlockSpec(w_extra.shape, lambda i, j: (0, 0)))
        out_specs.append(pl.BlockSpec((tm, ne), lambda i, j: (i, 0)))
        out_shape.append(jax.ShapeDtypeStruct((m, ne), F32))
        args.append(w_extra)
    if emit_w_bf16:
        assert m == tm
        out_specs.append(w_spec)
        out_shape.append(jax.ShapeDtypeStruct((n, k) if w_is_transposed else (k, n), BF16))
    res = pl.pallas_call(
        functools.partial(_norm_matmul_kernel, has_extra=has_extra, w_is_transposed=w_is_transposed,
                          emit_w_bf16=emit_w_bf16, ring=ring, grid=(m // tm, n // tn)),
        grid=(m // tm, n // tn),
        in_specs=in_specs, out_specs=out_specs, out_shape=out_shape,
        scratch_shapes=scratch,
        compiler_params=_params("arbitrary" if ring else "parallel", "arbitrary"),
        name="norm_matmul",
    )(*args)
    return res if len(res) > 1 else res[0]


def _norm_matmul_pair_kernel(x_ref, g_ref, wa_ref, wb_ref, oa_ref, ob_ref, xn_ref):
    @pl.when(pl.program_id(0) == 0)
    def _():
        xn_ref[...] = _rms(x_ref[...], g_ref[...]).astype(BF16)

    oa_ref[...] = _dot(xn_ref[...], wa_ref[...])
    ob_ref[...] = _dot(xn_ref[...], wb_ref[...])


def _norm_matmul_pair(x, g, wa, wb, *, tn):
    m, k = x.shape
    n = wa.shape[1]
    cols = lambda rows: pl.BlockSpec((rows, tn), lambda j: (0, j))
    return pl.pallas_call(
        _norm_matmul_pair_kernel,
        grid=(n // tn,),
        in_specs=[pl.BlockSpec((m, k), lambda j: (0, 0)), pl.BlockSpec((1, k), lambda j: (0, 0)), cols(k), cols(k)],
        out_specs=[cols(m), cols(m)],
        out_shape=[jax.ShapeDtypeStruct((m, n), F32)] * 2,
        scratch_shapes=[pltpu.VMEM((m, k), BF16)],
        compiler_params=_params("arbitrary"),
        name="norm_matmul_pair",
    )(x, g.reshape(1, k), wa, wb)


def _matmul_res_kernel(*refs, n_parts):
    a_refs, w_refs = refs[:n_parts], refs[n_parts:2 * n_parts]
    r_ref, o_ref = refs[2 * n_parts:]
    acc = r_ref[...]
    for a_ref, w_ref in zip(a_refs, w_refs):
        acc = acc + _dot(a_ref[...], w_ref[...])
    o_ref[...] = acc


def _matmul_res(a_parts, w, res, *, tm, tn):
    n_parts = len(a_parts)
    m, kp = a_parts[0].shape
    n = w.shape[1]
    a_specs = [pl.BlockSpec((tm, kp), lambda i, j: (i, 0)) for _ in a_parts]
    w_specs = [pl.BlockSpec((kp, tn), lambda i, j, p=p: (p, j)) for p in range(n_parts)]
    return pl.pallas_call(
        functools.partial(_matmul_res_kernel, n_parts=n_parts),
        grid=(m // tm, n // tn),
        in_specs=a_specs + w_specs + [pl.BlockSpec((tm, tn), lambda i, j: (i, j))],
        out_specs=pl.BlockSpec((tm, tn), lambda i, j: (i, j)),
        out_shape=jax.ShapeDtypeStruct((m, n), F32),
        compiler_params=_params("parallel", "arbitrary"),
        name="matmul_res",
    )(*a_parts, *([w] * n_parts), res)


def _matmul_res_norm_matmul_kernel(*refs, n_parts, n1):
    a_refs = refs[:n_parts]
    w_ref, r_ref, g_ref, x1_ref, q_ref, x1_s, xn_s = refs[n_parts:]
    j = pl.program_id(1)
    tn = w_ref.shape[1]
    kp = a_refs[0].shape[1]

    @pl.when(j < n1)
    def _():
        x1 = r_ref[...]
        for p, a_ref in enumerate(a_refs):
            x1 = x1 + _dot(a_ref[...], w_ref[p * kp:(p + 1) * kp, :])
        x1_ref[...] = x1
        x1_s[j] = x1

    @pl.when(j == n1)
    def _():
        ssq = sum(jnp.sum(jnp.square(x1_s[c]), axis=-1, keepdims=True) for c in range(n1))
        scale = lax.rsqrt(ssq * (1.0 / (n1 * tn)) + EPS)
        for c in range(n1):
            xn_s[:, c * tn:(c + 1) * tn] = (x1_s[c] * scale * g_ref[:, c * tn:(c + 1) * tn]).astype(xn_s.dtype)

    @pl.when(j >= n1)
    def _():
        q_ref[...] = _dot(xn_s[...], w_ref[...]).astype(q_ref.dtype)


def _matmul_res_norm_matmul(a_parts, w12, res, g, *, tm, tn, out2_dtype):
    n_parts = len(a_parts)
    m, kp = a_parts[0].shape
    d = res.shape[1]
    n1 = d // tn
    n2 = (w12.shape[1] - d) // tn
    return pl.pallas_call(
        functools.partial(_matmul_res_norm_matmul_kernel, n_parts=n_parts, n1=n1),
        grid=(m // tm, n1 + n2),
        in_specs=[pl.BlockSpec((tm, kp), lambda i, j: (i, 0))] * n_parts + [
            pl.BlockSpec((n_parts * kp, tn), lambda i, j: (0, j)),
            pl.BlockSpec((tm, tn), lambda i, j: (i, jnp.minimum(j, n1 - 1))),
            pl.BlockSpec((1, d), lambda i, j: (0, 0))],
        out_specs=[pl.BlockSpec((tm, tn), lambda i, j: (i, jnp.minimum(j, n1 - 1))),
                   pl.BlockSpec((tm, tn), lambda i, j: (i, jnp.maximum(j - n1, 0)))],
        out_shape=[jax.ShapeDtypeStruct((m, d), F32), jax.ShapeDtypeStruct((m, n2 * tn), out2_dtype)],
        scratch_shapes=[pltpu.VMEM((n1, tm, tn), F32), pltpu.VMEM((tm, d), BF16)],
        compiler_params=_params("parallel", "arbitrary"),
        name="matmul_res_norm_matmul",
    )(*a_parts, w12, res, g.reshape(1, d))


def _ffn_kernel(x_ref, g_ref, wu_ref, wd_ref, gf_ref, o_ref, xn_ref, *, final_norm, rider=None):
    f = pl.program_id(1)

    @pl.when(f == 0)
    def _():
        x = x_ref[...]
        xn_ref[...] = _rms(x, g_ref[...]).astype(BF16)
        o_ref[...] = x

    _run_rider_share(rider, 0, 1)
    h = _dot(xn_ref[...], wu_ref[...])
    h = jnp.square(jnp.maximum(h, 0.0)).astype(BF16)
    o_ref[...] += _dot(h, wd_ref[...])

    if final_norm:
        @pl.when(f == pl.num_programs(1) - 1)
        def _():
            o_ref[...] = _rms(o_ref[...], gf_ref[...])


def _ffn(x, g, w_up, w_down, g_final, *, tm, tf, final_norm, rider=None):
    m, d = x.shape
    dff = w_up.shape[1]
    r_in, r_out, r_shape, r_args = _rider_parts(rider)
    in_specs = [pl.BlockSpec((tm, d), lambda i, f: (i, 0)),
                pl.BlockSpec((1, d), lambda i, f: (0, 0)),
                pl.BlockSpec((d, tf), lambda i, f: (0, f)),
                pl.BlockSpec((tf, d), lambda i, f: (f, 0)),
                pl.BlockSpec((1, d), lambda i, f: (0, 0))]
    out_specs = [pl.BlockSpec((tm, d), lambda i, f: (i, 0))]
    return pl.pallas_call(
        _with_rider(functools.partial(_ffn_kernel, final_norm=final_norm), rider, len(in_specs), len(out_specs)),
        grid=(m // tm, dff // tf),
        in_specs=in_specs + r_in, out_specs=out_specs + r_out,
        out_shape=[jax.ShapeDtypeStruct((m, d), F32)] + r_shape,
        scratch_shapes=[pltpu.VMEM((tm, d), BF16)],
        compiler_params=_params("parallel", "arbitrary", vmem_limit_bytes=V7X_VMEM_LIMIT_FFN_BYTES),
        name="ffn",
    )(x, g.reshape(1, d), w_up, w_down, g_final.reshape(1, d), *r_args)


def _lru_gates(xc, wg, bg, lam):
    g = _dot(xc.astype(BF16), wg) + bg
    r = jax.nn.sigmoid(g[:, :RNN_HD])
    i = jax.nn.sigmoid(g[:, RNN_HD:])
    log_a = -LRU_C * r * jax.nn.softplus(-lam)
    a = jnp.exp(log_a)
    u = jnp.sqrt(-jnp.tanh(log_a) * (a * a + 1.0)) * (i * xc)
    return a, u


def _rglru_kernel(xr_ref, gr_ref, h0_ref, c0_ref, cw_ref, cb_ref, wg_ref, bg_ref, lam_ref, gout_ref,
                  y_ref, hn_ref, cn_ref, xbuf, hc, ybuf, *, tc, rider=None):
    c = pl.program_id(1)

    @pl.when(c == 0)
    def _():
        xbuf[0:8, :] = jnp.zeros((8, D_RNN), F32)
        xbuf[5:8, :] = c0_ref[0]
        hc[...] = h0_ref[0]

    @pl.when(c > 0)
    def _():
        xbuf[0:8, :] = xbuf[tc:tc + 8, :]

    xbuf[8:8 + tc, :] = xr_ref[0]
    ng = tc // SUBLANES
    sub = lax.broadcasted_iota(jnp.int32, (ng, SUBLANES, RNN_HD), 1)
    ssq = jnp.zeros((tc, 1), F32)
    for h in range(RNN_HEADS):
        _run_rider_share(rider, h, RNN_HEADS)
        sl = slice(h * RNN_HD, (h + 1) * RNN_HD)
        xc = cb_ref[:, sl]
        for j in range(CONV_W):
            xc = xc + cw_ref[j:j + 1, sl] * xbuf[5 + j:5 + j + tc, sl]
        a, u = _lru_gates(xc, wg_ref[h], bg_ref[h], lam_ref[:, sl])
        a = a.reshape(ng, SUBLANES, RNN_HD)
        u = u.reshape(ng, SUBLANES, RNN_HD)
        d = 1
        while d < SUBLANES:
            keep = sub >= d
            a_prev = jnp.where(keep, pltpu.roll(a, d, 1), 1.0)
            u_prev = jnp.where(keep, pltpu.roll(u, d, 1), 0.0)
            u = u + a * u_prev
            a = a * a_prev
            d *= 2
        carry = hc[:, sl]
        for g in range(ng):
            hg = u[g] + a[g] * carry
            carry = hg[SUBLANES - 1:SUBLANES, :]
            ybuf[g * SUBLANES:(g + 1) * SUBLANES, sl] = hg
        hc[:, sl] = carry
        yv = ybuf[:, sl] * jax.nn.gelu(gr_ref[0, :, sl])
        ssq = ssq + jnp.sum(yv * yv, axis=-1, keepdims=True)
        ybuf[:, sl] = yv
    y = ybuf[...] * lax.rsqrt(ssq * (1.0 / D_RNN) + EPS) * gout_ref[...]
    y_ref[0] = y.astype(y_ref.dtype)

    @pl.when(c == pl.num_programs(1) - 1)
    def _():
        hn_ref[0] = hc[...]
        cn_ref[0] = xbuf[tc + 5:tc + 8, :]


def _rglru_prompt(z3, h0, conv0, w, *, tc, rider=None):
    b, t, _ = z3.shape
    full2 = lambda bi, ci: (0, 0)
    r_in, r_out, r_shape, r_args = _rider_parts(rider)
    in_specs = [pl.BlockSpec((1, tc, D_RNN), lambda bi, ci: (bi, ci, 0)),
                pl.BlockSpec((1, tc, D_RNN), lambda bi, ci: (bi, ci, 1)),
                pl.BlockSpec((1, 1, D_RNN), lambda bi, ci: (bi, 0, 0)),
                pl.BlockSpec((1, CONV_W - 1, D_RNN), lambda bi, ci: (bi, 0, 0)),
                pl.BlockSpec((CONV_W, D_RNN), full2),
                pl.BlockSpec((1, D_RNN), full2),
                pl.BlockSpec((RNN_HEADS, RNN_HD, 2 * RNN_HD), lambda bi, ci: (0, 0, 0)),
                pl.BlockSpec((RNN_HEADS, 1, 2 * RNN_HD), lambda bi, ci: (0, 0, 0)),
                pl.BlockSpec((1, D_RNN), full2),
                pl.BlockSpec((1, D_RNN), full2)]
    out_specs = [pl.BlockSpec((1, tc, D_RNN), lambda bi, ci: (bi, ci, 0)),
                 pl.BlockSpec((1, 1, D_RNN), lambda bi, ci: (bi, 0, 0)),
                 pl.BlockSpec((1, CONV_W - 1, D_RNN), lambda bi, ci: (bi, 0, 0))]
    out_shape = [jax.ShapeDtypeStruct((b, t, D_RNN), BF16),
                 jax.ShapeDtypeStruct((b, 1, D_RNN), F32),
                 jax.ShapeDtypeStruct((b, CONV_W - 1, D_RNN), F32)]
    return pl.pallas_call(
        _with_rider(functools.partial(_rglru_kernel, tc=tc), rider, len(in_specs), len(out_specs)),
        grid=(b, t // tc),
        in_specs=in_specs + r_in, out_specs=out_specs + r_out, out_shape=out_shape + r_shape,
        scratch_shapes=[pltpu.VMEM((tc + 8, D_RNN), F32),
                        pltpu.VMEM((1, D_RNN), F32),
                        pltpu.VMEM((tc, D_RNN), F32)],
        compiler_params=_params("parallel", "arbitrary"),
        name="rglru_prompt",
    )(z3, z3, h0, conv0, w['conv_rnn_w'], w['conv_rnn_b'], w['lru_wg'], w['lru_bg'],
      w['lru_lambda'], w['g_rnn_out'], *r_args)


def _mlstm_kernel(u_ref, v_ref, og_ref, zg_ref, gb_ref, c0_ref, n0_ref, m0_ref, cv0_ref,
                  cw_ref, cb_ref, wqk_ref, gout_ref,
                  y_ref, cn_ref, nn_ref, mn_ref, cvn_ref, ubuf, cst, nst, mst, rider=None):
    c = pl.program_id(1)
    L = CHUNK
    n_seq = u_ref.shape[0]

    @pl.when(c == 0)
    def _():
        ubuf[:, 0:8, :] = jnp.zeros((n_seq, 8, D_MLSTM), F32)
        ubuf[:, 5:8, :] = cv0_ref[...]
        cst[...] = c0_ref[...]
        nst[...] = n0_ref[...]
        mst[...] = m0_ref[...]

    @pl.when(c > 0)
    def _():
        ubuf[:, 0:8, :] = ubuf[:, L:L + 8, :]

    ubuf[:, 8:8 + L, :] = u_ref[...]

    ti = lax.broadcasted_iota(jnp.int32, (L, L), 0)
    si = lax.broadcasted_iota(jnp.int32, (L, L), 1)
    causal = si <= ti
    anti = ti <= si
    zgs = [zg_ref[s] + gb_ref[...] for s in range(n_seq)]
    zgts = [zg.T for zg in zgs]

    for h in range(MLSTM_HEADS):
        _run_rider_share(rider, h, MLSTM_HEADS)
        sl = slice(h * MLSTM_HD, (h + 1) * MLSTM_HD)
        for s in range(n_seq):
            zg, zgt = zgs[s], zgts[s]
            uc = cb_ref[:, sl]
            for j in range(CONV_W):
                uc = uc + cw_ref[j:j + 1, sl] * ubuf[s, 5 + j:5 + j + L, sl]
            uc = jax.nn.silu(uc)
            qk = _dot(uc.astype(BF16), wqk_ref[h])
            q = qk[:, :MLSTM_HD]
            k = qk[:, MLSTM_HD:] * (MLSTM_HD ** -0.5)
            v = v_ref[s, :, sl]
            qb, kb = q.astype(BF16), k.astype(BF16)

            icol = zg[:, h:h + 1]
            fcol = jax.nn.log_sigmoid(zg[:, MLSTM_HEADS + h:MLSTM_HEADS + h + 1])
            irow = zgt[h:h + 1, :]
            frow = jax.nn.log_sigmoid(zgt[MLSTM_HEADS + h:MLSTM_HEADS + h + 1, :])
            bcol = jnp.sum(jnp.where(causal, frow, 0.0), axis=1, keepdims=True)
            brow = jnp.sum(jnp.where(anti, fcol, 0.0), axis=0, keepdims=True)
            dmat = jnp.where(causal, irow + bcol - brow, -jnp.inf)
            m_prev = mst[s, h, :, 0:1]
            inter = bcol + m_prev
            m_t = jnp.maximum(inter, jnp.max(dmat, axis=1, keepdims=True))
            w_inter = jnp.exp(inter - m_t)
            sc = lax.dot_general(qb, kb, (((1,), (1,)), ((), ())), preferred_element_type=F32) * jnp.exp(dmat - m_t)
            cmat = cst[s, h]
            nrow = nst[s, h]
            cq = lax.dot_general(qb, cmat.astype(BF16), (((1,), (1,)), ((), ())), preferred_element_type=F32)
            num = w_inter * cq + _dot(sc.astype(BF16), v.astype(BF16))
            den = w_inter * jnp.sum(q * nrow, axis=1, keepdims=True) + jnp.sum(sc, axis=1, keepdims=True)
            hm = num / jnp.maximum(jnp.abs(den), jnp.exp(-m_t))

            m_new = m_t[L - 1:L, :]
            b_last = bcol[L - 1:L, :]
            g_state = jnp.exp(b_last + m_prev - m_new)
            g_in = jnp.exp(icol + b_last - bcol - m_new)
            cst[s, h] = g_state * cmat + lax.dot_general((g_in * v).astype(BF16), kb, (((0,), (0,)), ((), ())),
                                                         preferred_element_type=F32)
            nst[s, h] = g_state * nrow + jnp.sum(g_in * k, axis=0, keepdims=True)
            mst[s, h] = jnp.broadcast_to(m_new, (1, N_GATE_PAD))

            y = _rms(hm, gout_ref[...]) * jax.nn.sigmoid(og_ref[s, :, sl])
            y_ref[s, :, sl] = y.astype(y_ref.dtype)

    @pl.when(c == pl.num_programs(1) - 1)
    def _():
        cn_ref[...] = cst[...]
        nn_ref[...] = nst[...]
        mn_ref[...] = mst[...]
        cvn_ref[...] = ubuf[:, L + 5:L + 8, :]


MLSTM_SEQS = 1


def _mlstm_prompt(z3, zg3, c0, n0, m0, conv0, w, *, rider=None):
    b, t, _ = z3.shape
    nc = t // CHUNK
    ns = MLSTM_SEQS
    u_blk, v_blk, o_blk = 2 * D_RNN // D_MLSTM, 3 * D_RNN // D_MLSTM, 4 * D_RNN // D_MLSTM
    zcol = lambda blk: pl.BlockSpec((ns, CHUNK, D_MLSTM), lambda bi, ci: (bi, ci, blk))
    state4 = lambda *minor: pl.BlockSpec((ns, MLSTM_HEADS) + minor, lambda bi, ci: (bi, 0, 0, 0))
    conv_spec = pl.BlockSpec((ns, CONV_W - 1, D_MLSTM), lambda bi, ci: (bi, 0, 0))
    const2 = lambda shape: pl.BlockSpec(shape, lambda bi, ci: (0, 0))
    r_in, r_out, r_shape, r_args = _rider_parts(rider)
    in_specs = [zcol(u_blk), zcol(v_blk), zcol(o_blk),
                pl.BlockSpec((ns, CHUNK, N_GATE_PAD), lambda bi, ci: (bi, ci, 0)),
                const2((1, N_GATE_PAD)),
                state4(MLSTM_HD, MLSTM_HD), state4(1, MLSTM_HD), state4(1, N_GATE_PAD), conv_spec,
                const2((CONV_W, D_MLSTM)), const2((1, D_MLSTM)),
                pl.BlockSpec((MLSTM_HEADS, MLSTM_HD, 2 * MLSTM_HD), lambda bi, ci: (0, 0, 0)),
                const2((1, MLSTM_HD))]
    out_specs = [pl.BlockSpec((ns, CHUNK, D_MLSTM), lambda bi, ci: (bi, ci, 0)),
                 state4(MLSTM_HD, MLSTM_HD), state4(1, MLSTM_HD), state4(1, N_GATE_PAD), conv_spec]
    out_shape = [jax.ShapeDtypeStruct((b, t, D_MLSTM), BF16),
                 jax.ShapeDtypeStruct((b, MLSTM_HEADS, MLSTM_HD, MLSTM_HD), F32),
                 jax.ShapeDtypeStruct((b, MLSTM_HEADS, 1, MLSTM_HD), F32),
                 jax.ShapeDtypeStruct((b, MLSTM_HEADS, 1, N_GATE_PAD), F32),
                 jax.ShapeDtypeStruct((b, CONV_W - 1, D_MLSTM), F32)]
    return pl.pallas_call(
        _with_rider(_mlstm_kernel, rider, len(in_specs), len(out_specs)),
        grid=(b // ns, nc),
        in_specs=in_specs + r_in, out_specs=out_specs + r_out, out_shape=out_shape + r_shape,
        scratch_shapes=[pltpu.VMEM((ns, CHUNK + 8, D_MLSTM), F32),
                        pltpu.VMEM((ns, MLSTM_HEADS, MLSTM_HD, MLSTM_HD), F32),
                        pltpu.VMEM((ns, MLSTM_HEADS, 1, MLSTM_HD), F32),
                        pltpu.VMEM((ns, MLSTM_HEADS, 1, N_GATE_PAD), F32)],
        compiler_params=_params("parallel", "arbitrary"),
        name="mlstm_prompt",
    )(z3, z3, z3, zg3, w['gate_bias'], c0, n0, m0, conv0, w['conv_ml_w'], w['conv_ml_b'],
      w['ml_wqk'], w['g_ml_out'], *r_args)


def _xattn_kernel(q_ref, k_ref, v_ref, x_ref, wo_ref, o_ref, att, rider=None):
    @pl.when(pl.program_id(2) == 0)
    def _():
        for h in range(X_HEADS):
            sl = slice(h * X_HD, (h + 1) * X_HD)
            s = _dot_t(q_ref[0, :, sl], k_ref[0, :, sl]) * (X_HD ** -0.5)
            e = jnp.exp(s - jnp.max(s, axis=-1, keepdims=True))
            p = e / jnp.sum(e, axis=-1, keepdims=True)
            att[:, sl] = _dot(p, v_ref[0, :, sl]).astype(att.dtype)

    _run_rider_share(rider, 0, 1)
    o_ref[0] = x_ref[0] + _dot(att[...], wo_ref[...])


def _xattn_prompt(q3, mk3, mv3, x3, w_co, *, tq, tn, rider=None):
    b, t, d = q3.shape
    mem = pl.BlockSpec((1, N_MEM, d), lambda bi, ti, j: (bi, 0, 0))
    cols = pl.BlockSpec((1, tq, tn), lambda bi, ti, j: (bi, ti, j))
    r_in, r_out, r_shape, r_args = _rider_parts(rider)
    in_specs = [pl.BlockSpec((1, tq, d), lambda bi, ti, j: (bi, ti, 0)), mem, mem, cols,
                pl.BlockSpec((d, tn), lambda bi, ti, j: (0, j))]
    return pl.pallas_call(
        _with_rider(_xattn_kernel, rider, len(in_specs), 1),
        grid=(b, t // tq, d // tn),
        in_specs=in_specs + r_in, out_specs=[cols] + r_out,
        out_shape=[jax.ShapeDtypeStruct((b, t, d), F32)] + r_shape,
        scratch_shapes=[pltpu.VMEM((tq, d), BF16)],
        compiler_params=_params("parallel", "parallel", "arbitrary"),
        name="xattn_prompt",
    )(q3, mk3, mv3, x3, w_co, *r_args)


XS_SUB = 2 * X_HEADS
XS_PAIRS = X_HD // (2 * 128)


def _pack_heads(x):
    lead = x.shape[:-1]
    x = x.reshape(*lead, X_HEADS, 2 * XS_PAIRS, 128)
    x = jnp.swapaxes(x, -3, -2)
    return x.reshape(*lead, XS_PAIRS, XS_SUB, 128)


def _unpack_heads(x):
    lead = x.shape[:-3]
    x = x.reshape(*lead, 2 * XS_PAIRS, X_HEADS, 128)
    x = jnp.swapaxes(x, -3, -2)
    return x.reshape(*lead, X_HEADS * X_HD)


def _xattn_sample_units(q_ref, k_ref, v_ref, o_ref, r):
    def row():
        t = jnp.sum(k_ref[r] * q_ref[r], axis=1)
        t = t + pltpu.roll(t, X_HEADS, 1)
        s = jnp.sum(t, axis=-1, keepdims=True) * (X_HD ** -0.5)
        e = jnp.exp(s - jnp.max(s, axis=0, keepdims=True))
        p = e / jnp.sum(e, axis=0, keepdims=True)
        o_ref[r] = jnp.sum(p[:, None] * v_ref[r], axis=0)

    return [row]


def _xattn_sample_rider(grid, q, ck, cv):
    b = q.shape[0]
    n_steps, step = _grid_steps(grid)
    rows = b // n_steps
    kv_spec = pl.BlockSpec((rows, N_MEM, XS_PAIRS, XS_SUB, 128), lambda *g: (step(*g), 0, 0, 0, 0))
    q_spec = pl.BlockSpec((rows, XS_PAIRS, XS_SUB, 128), lambda *g: (step(*g), 0, 0, 0))
    return dict(make_units=_per_row_units(_xattn_sample_units, rows),
                in_specs=[q_spec, kv_spec, kv_spec], out_specs=[q_spec],
                out_shape=[jax.ShapeDtypeStruct((b, XS_PAIRS, XS_SUB, 128), F32)],
                args=[_pack_heads(q), _pack_heads(ck.reshape(b, N_MEM, X_HEADS * X_HD)),
                      _pack_heads(cv.reshape(b, N_MEM, X_HEADS * X_HD))])


def _smix_rows_kernel(z_ref, h0_ref, rc_ref, mc_ref, cwr_ref, cbr_ref, wg_ref, bg_ref, lam_ref, gout_ref,
                      cwm_ref, cbm_ref, wqk_ref,
                      yr_ref, hn_ref, rcn_ref, mcn_ref, q_ref, k_ref, ybuf):
    nb = z_ref.shape[0]
    ssq = jnp.zeros((nb, 1), F32)
    for h in range(RNN_HEADS):
        sl = slice(h * RNN_HD, (h + 1) * RNN_HD)
        xc = cbr_ref[:, sl] + cwr_ref[CONV_W - 1:CONV_W, sl] * z_ref[:, sl]
        for j in range(CONV_W - 1):
            xc = xc + cwr_ref[j:j + 1, sl] * rc_ref[:, j * D_RNN + h * RNN_HD:j * D_RNN + (h + 1) * RNN_HD]
        a, u = _lru_gates(xc, wg_ref[h], bg_ref[h], lam_ref[:, sl])
        hs = a * h0_ref[:, sl] + u
        hn_ref[:, sl] = hs
        yv = hs * jax.nn.gelu(z_ref[:, D_RNN + h * RNN_HD:D_RNN + (h + 1) * RNN_HD])
        ssq = ssq + jnp.sum(yv * yv, axis=-1, keepdims=True)
        ybuf[:, sl] = yv
    yr_ref[...] = (ybuf[...] * lax.rsqrt(ssq * (1.0 / D_RNN) + EPS) * gout_ref[...]).astype(yr_ref.dtype)
    rcn_ref[:, 0:2 * D_RNN] = rc_ref[:, D_RNN:3 * D_RNN]
    rcn_ref[:, 2 * D_RNN:3 * D_RNN] = z_ref[:, 0:D_RNN]

    for h in range(MLSTM_HEADS):
        sl = slice(h * MLSTM_HD, (h + 1) * MLSTM_HD)
        uc = cbm_ref[:, sl] + cwm_ref[CONV_W - 1:CONV_W, sl] * z_ref[:, 2 * D_RNN + h * MLSTM_HD:2 * D_RNN + (h + 1) * MLSTM_HD]
        for j in range(CONV_W - 1):
            uc = uc + cwm_ref[j:j + 1, sl] * mc_ref[:, j * D_MLSTM + h * MLSTM_HD:j * D_MLSTM + (h + 1) * MLSTM_HD]
        uc = jax.nn.silu(uc)
        qk = _dot(uc.astype(BF16), wqk_ref[h])
        q_ref[:, sl] = qk[:, :MLSTM_HD]
        k_ref[:, sl] = qk[:, MLSTM_HD:] * (MLSTM_HD ** -0.5)
    mcn_ref[:, 0:2 * D_MLSTM] = mc_ref[:, D_MLSTM:3 * D_MLSTM]
    mcn_ref[:, 2 * D_MLSTM:3 * D_MLSTM] = z_ref[:, 2 * D_RNN:2 * D_RNN + D_MLSTM]


def _smix_rows(z, h0, rconv, mconv, w):
    nb = z.shape[0]
    outs = [jax.ShapeDtypeStruct((nb, D_RNN), BF16),
            jax.ShapeDtypeStruct((nb, D_RNN), F32),
            jax.ShapeDtypeStruct((nb, 3 * D_RNN), F32),
            jax.ShapeDtypeStruct((nb, 3 * D_MLSTM), F32),
            jax.ShapeDtypeStruct((nb, D_MLSTM), F32),
            jax.ShapeDtypeStruct((nb, D_MLSTM), F32)]
    return pl.pallas_call(
        _smix_rows_kernel,
        out_shape=outs,
        scratch_shapes=[pltpu.VMEM((nb, D_RNN), F32)],
        compiler_params=pltpu.CompilerParams(vmem_limit_bytes=V7X_VMEM_LIMIT_BYTES),
        name="smix_rows",
    )(z, h0, rconv, mconv, w['conv_rnn_w'], w['conv_rnn_b'], w['lru_wg'], w['lru_bg'], w['lru_lambda'],
      w['g_rnn_out'], w['conv_ml_w'], w['conv_ml_b'], w['ml_wqk'])


MXU_MIN_ROWS = 8


def _smix_state_units(c_ref, q_ref, k_ref, n_ref, v_ref, og_ref, zg_ref, gb_ref, m_ref, gout_ref,
                      cn_ref, nn_ref, mn_ref, y_ref, r):
    lane_g = lax.broadcasted_iota(jnp.int32, (1, N_GATE_PAD), 1)
    first_row = lax.broadcasted_iota(jnp.int32, (MXU_MIN_ROWS, MLSTM_HD), 0) == 0

    def head(h, zg, m_out):
        sl = slice(h * MLSTM_HD, (h + 1) * MLSTM_HD)
        q = q_ref[r, :, sl]
        k = k_ref[r, :, sl]
        n = n_ref[r, :, sl]
        v = v_ref[r, :, sl]
        ig = zg[:, h:h + 1]
        lf = jax.nn.log_sigmoid(zg[:, MLSTM_HEADS + h:MLSTM_HEADS + h + 1])
        m_prev = m_ref[r, :, h:h + 1]
        m_t = jnp.maximum(lf + m_prev, ig)
        w_inter = jnp.exp(lf + m_prev - m_t)
        g_in = jnp.exp(ig - m_t)
        s = jnp.sum(q * k, axis=1, keepdims=True) * g_in
        den = w_inter * jnp.sum(n * q, axis=1, keepdims=True) + s
        denom = jnp.maximum(jnp.abs(den), jnp.exp(-m_t))
        cmat = c_ref[r, h]
        q8 = jnp.broadcast_to(q, (MXU_MIN_ROWS, MLSTM_HD)).astype(BF16)
        cq = lax.dot_general(q8, cmat.astype(BF16), (((1,), (1,)), ((), ())),
                             preferred_element_type=F32)[0:1, :]
        hm = (w_inter * cq + s * v) / denom
        gv8 = jnp.where(first_row, jnp.broadcast_to(g_in * v, (MXU_MIN_ROWS, MLSTM_HD)), 0.0).astype(BF16)
        k8 = jnp.broadcast_to(k, (MXU_MIN_ROWS, MLSTM_HD)).astype(BF16)
        outer = lax.dot_general(gv8, k8, (((0,), (0,)), ((), ())), preferred_element_type=F32)
        cn_ref[r, h] = w_inter * cmat + outer
        nn_ref[r, :, sl] = w_inter * n + g_in * k
        y = _rms(hm, gout_ref[...]) * jax.nn.sigmoid(og_ref[r, :, sl])
        y_ref[r, :, sl] = y.astype(y_ref.dtype)
        return jnp.where(lane_g == h, m_t, m_out)

    def row():
        zg = zg_ref[r] + gb_ref[...]
        m_out = jnp.zeros((1, N_GATE_PAD), F32)
        for h in range(MLSTM_HEADS):
            m_out = head(h, zg, m_out)
        mn_ref[r] = m_out

    return [row]


def _smix_state_rider(grid, c, q3, k3, n3, z3, zg3, gate_bias, m3, gout):
    nb = c.shape[0]
    n_steps, step = _grid_steps(grid)
    rows = nb // n_steps
    row = lambda width, col=0: pl.BlockSpec((rows, 1, width), lambda *g: (step(*g), 0, col))
    cblk = pl.BlockSpec((rows, MLSTM_HEADS, MLSTM_HD, MLSTM_HD), lambda *g: (step(*g), 0, 0, 0))
    return dict(
        make_units=_per_row_units(_smix_state_units, rows),
        in_specs=[cblk, row(D_MLSTM), row(D_MLSTM), row(D_MLSTM),
                  row(D_MLSTM, 3 * D_RNN // D_MLSTM), row(D_MLSTM, 4 * D_RNN // D_MLSTM),
                  row(N_GATE_PAD),
                  pl.BlockSpec((1, N_GATE_PAD), lambda *g: (0, 0)),
                  row(MLSTM_HEADS),
                  pl.BlockSpec((1, MLSTM_HD), lambda *g: (0, 0))],
        out_specs=[cblk, row(D_MLSTM), row(N_GATE_PAD), row(D_MLSTM)],
        out_shape=[jax.ShapeDtypeStruct(c.shape, F32),
                   jax.ShapeDtypeStruct((nb, 1, D_MLSTM), F32),
                   jax.ShapeDtypeStruct((nb, 1, N_GATE_PAD), F32),
                   jax.ShapeDtypeStruct((nb, 1, D_MLSTM), F32)],
        args=[c, q3, k3, n3, z3, z3, zg3, gate_bias, m3, gout])


RGLRU_CHUNK = 256
PROMPT_TM = 1024
PROMPT_W_IN_TN = 1280
SAMPLE_FFN_TF = 2048


def _layer(xp, xs, mem, rg_h, rg_conv, c, n, mst, ml_conv, ck, cv, w, g_final, final_norm):
    b, t, d = xp.shape
    nb = xs.shape[0]
    m = b * t
    tm = PROMPT_TM
    xp2 = xp.reshape(m, d)
    xs2 = xs.reshape(nb, d)
    w_in_args = dict(tn=1024, out_dtype=F32, w_extra=w['w_in_gate'], n_cols=D_MAIN, w_is_transposed=True)

    zs, zgs, w_in_bf = _norm_matmul(xs2, w['g_mix'], w['w_in_main'], tm=nb, emit_w_bf16=True, **w_in_args)
    ys_rnn, s_h, s_rconv, s_mconv, qs, ks = _smix_rows(
        zs, rg_h, rg_conv.reshape(nb, 3 * D_RNN), ml_conv.reshape(nb, 3 * D_MLSTM), w)

    z, zg = _norm_matmul(xp2, w['g_mix'], w_in_bf, tm=tm, w_buffers=3, **dict(w_in_args, tn=PROMPT_W_IN_TN))
    z3 = z.reshape(b, t, D_MAIN)
    zg3 = zg.reshape(b, t, N_GATE_PAD)
    h0 = jnp.zeros((b, 1, D_RNN), F32)
    conv0 = jnp.zeros((b, CONV_W - 1, D_RNN), F32)
    rg_grid = (b, t // RGLRU_CHUNK)
    state_rider = _smix_state_rider(
        rg_grid, c, qs.reshape(nb, 1, D_MLSTM), ks.reshape(nb, 1, D_MLSTM), n.reshape(nb, 1, D_MLSTM),
        zs.reshape(nb, 1, D_MAIN), zgs.reshape(nb, 1, N_GATE_PAD), w['gate_bias'],
        mst.reshape(nb, 1, MLSTM_HEADS), w['g_ml_out'])
    rg_rider = _merge_riders(state_rider, _cast_rider(rg_grid, [[w['w_out'], w['w_cq']], [w['w_co']]]))
    y_rnn, p_h, p_rconv, s_c, s_n, s_m, ys_ml, w_out_cq, w_co = _rglru_prompt(
        z3, h0, conv0, w, tc=RGLRU_CHUNK, rider=rg_rider)

    xs1, qx = _matmul_res_norm_matmul([ys_rnn, ys_ml.reshape(nb, D_MLSTM).astype(BF16)], w_out_cq, xs2,
                                      w['g_xattn'], tm=nb, tn=1024, out2_dtype=F32)

    c0 = jnp.zeros((b, MLSTM_HEADS, MLSTM_HD, MLSTM_HD), F32)
    n0 = jnp.zeros((b, MLSTM_HEADS, 1, MLSTM_HD), F32)
    m0 = jnp.zeros((b, MLSTM_HEADS, 1, N_GATE_PAD), F32)
    mconv0 = jnp.zeros((b, CONV_W - 1, D_MLSTM), F32)
    y_ml, p_c, p_n, p_m, p_mconv, w_up, w_down = _mlstm_prompt(
        z3, zg3, c0, n0, m0, mconv0, w,
        rider=_cast_rider((b // MLSTM_SEQS, t // CHUNK), [[w['w_up']], [w['w_down']]]))

    x1, q = _matmul_res_norm_matmul([y_rnn.reshape(m, D_RNN), y_ml.reshape(m, D_MLSTM)], w_out_cq, xp2,
                                    w['g_xattn'], tm=tm, tn=1024, out2_dtype=BF16)
    mem2 = mem.reshape(b * N_MEM, d)
    mk, mv = _norm_matmul_pair(mem2, w['g_mem'], w['w_mk'], w['w_mv'], tn=512)
    x2, = _xattn_prompt(q.reshape(b, t, d), mk.reshape(b, N_MEM, d), mv.reshape(b, N_MEM, d), x1.reshape(b, t, d),
                        w_co, tq=tm, tn=1024)
    ffn_tf = 512
    xp_out, os_packed = _ffn(x2.reshape(m, d), w['g_ffn'], w_up, w_down, g_final, tm=tm, tf=ffn_tf,
                             final_norm=final_norm,
                             rider=_xattn_sample_rider((m // tm, D_FF // ffn_tf), qx, ck, cv))

    os_ = _unpack_heads(os_packed).astype(BF16)
    xs2_ = _matmul_res([os_], w_co, xs1, tm=nb, tn=1024)
    xs_out, = _ffn(xs2_, w['g_ffn'], w_up, w_down, g_final, tm=nb, tf=SAMPLE_FFN_TF, final_norm=final_norm)

    new_p = (p_h.reshape(b, D_RNN), p_rconv, p_c, p_n.reshape(b, MLSTM_HEADS, MLSTM_HD), p_m[:, :, 0, 0], p_mconv,
             mk.reshape(b, N_MEM, X_HEADS, X_HD), mv.reshape(b, N_MEM, X_HEADS, X_HD))
    new_s = (s_h, s_rconv.reshape(nb, CONV_W - 1, D_RNN), s_c, s_n.reshape(nb, MLSTM_HEADS, MLSTM_HD),
             s_m[:, 0, :MLSTM_HEADS], s_mconv.reshape(nb, CONV_W - 1, D_MLSTM))
    return xp_out.reshape(b, t, d), xs_out.reshape(nb, 1, d), new_p, new_s


def _layer_weights(l, g_mix, w_in, conv_rnn_w, conv_rnn_b, lru_wa, lru_ba, lru_wx, lru_bx, lru_lambda,
                   g_rnn_out, conv_ml_w, conv_ml_b, ml_wq, ml_wk, ml_bi, ml_bf, g_ml_out, w_out,
                   g_xattn, g_mem, w_cq, w_mk, w_mv, w_co, g_ffn, w_up, w_down):
    n_gate = 2 * MLSTM_HEADS
    w_in_t = jnp.swapaxes(w_in[l], 0, 1)
    w_gate_t = jnp.pad(w_in_t[D_MAIN:], ((0, N_GATE_PAD - n_gate), (0, 0)))
    gate_bias = jnp.pad(jnp.concatenate([ml_bi[l], ml_bf[l]]), (0, N_GATE_PAD - n_gate))
    return dict(
        g_mix=g_mix[l], w_in_main=w_in_t, w_in_gate=w_gate_t.astype(BF16),
        conv_rnn_w=conv_rnn_w[l], conv_rnn_b=conv_rnn_b[l].reshape(1, D_RNN),
        lru_wg=jnp.concatenate([lru_wa[l], lru_wx[l]], axis=-1).astype(BF16),
        lru_bg=jnp.concatenate([lru_ba[l], lru_bx[l]], axis=-1).reshape(RNN_HEADS, 1, 2 * RNN_HD),
        lru_lambda=lru_lambda[l].reshape(1, D_RNN), g_rnn_out=g_rnn_out[l].reshape(1, D_RNN),
        conv_ml_w=conv_ml_w[l], conv_ml_b=conv_ml_b[l].reshape(1, D_MLSTM),
        ml_wqk=jnp.concatenate([ml_wq[l], ml_wk[l]], axis=-1).astype(BF16),
        gate_bias=gate_bias.reshape(1, N_GATE_PAD), g_ml_out=g_ml_out[l].reshape(1, MLSTM_HD),
        w_out=w_out[l], w_cq=w_cq[l], g_xattn=g_xattn[l], g_mem=g_mem[l], w_mk=w_mk[l], w_mv=w_mv[l],
        w_co=w_co[l], g_ffn=g_ffn[l], w_up=w_up[l], w_down=w_down[l])


def kernel(x_prompt, x_sample, mem_prompt, state_rglru_h, state_rglru_conv, state_mlstm_C, state_mlstm_n, state_mlstm_m, state_mlstm_conv, cache_mem_k, cache_mem_v, g_mix, w_in, conv_rnn_w, conv_rnn_b, lru_wa, lru_ba, lru_wx, lru_bx, lru_lambda, g_rnn_out, conv_ml_w, conv_ml_b, ml_wq, ml_wk, ml_bi, ml_bf, g_ml_out, w_out, g_xattn, g_mem, w_cq, w_mk, w_mv, w_co, g_ffn, w_up, w_down, g_final):
    depth = w_in.shape[0]
    xp, xs = x_prompt, x_sample
    p_out = [[] for _ in range(8)]
    s_out = [[] for _ in range(6)]
    for l in range(depth):
        w = _layer_weights(l, g_mix, w_in, conv_rnn_w, conv_rnn_b, lru_wa, lru_ba, lru_wx, lru_bx, lru_lambda,
                           g_rnn_out, conv_ml_w, conv_ml_b, ml_wq, ml_wk, ml_bi, ml_bf, g_ml_out, w_out,
                           g_xattn, g_mem, w_cq, w_mk, w_mv, w_co, g_ffn, w_up, w_down)
        last = l == depth - 1
        xp, xs, new_p, new_s = _layer(xp, xs, mem_prompt, state_rglru_h[l], state_rglru_conv[l], state_mlstm_C[l],
                                      state_mlstm_n[l], state_mlstm_m[l], state_mlstm_conv[l], cache_mem_k[l],
                                      cache_mem_v[l], w, g_final, last)
        for j, a in enumerate(new_p):
            p_out[j].append(a)
        for j, a in enumerate(new_s):
            s_out[j].append(a)
    P = [jnp.stack(a, axis=0) for a in p_out]
    S = [jnp.stack(a, axis=0) for a in s_out]
    return (xp, xs, P[0], P[1], P[2], P[3], P[4], P[5], P[6], P[7],
            S[0], S[1], S[2], S[3], S[4], S[5])
```

```python
import functools

import jax
import jax.numpy as jnp
from jax import lax
from jax.experimental import pallas as pl
from jax.experimental.pallas import tpu as pltpu

F32 = jnp.float32
BF16 = jnp.bfloat16

D_MODEL = 2048
D_RNN = 1024
RNN_HEADS = 8
RNN_HD = 128
CONV_W = 4
LRU_C = 8.0
D_MLSTM = 1024
MLSTM_HEADS = 4
MLSTM_HD = 256
CHUNK = 128
N_MEM = 256
X_HEADS = 4
X_HD = 512
D_FF = 8192
EPS = 1e-6
D_MAIN = 5 * 1024
N_GATE_PAD = 128

V7X_VMEM_LIMIT_BYTES = 56 * 1024 * 1024
V7X_VMEM_LIMIT_FFN_BYTES = 63 * 1024 * 1024
SUBLANES = 8


def _params(*sem, vmem_limit_bytes=V7X_VMEM_LIMIT_BYTES):
    return pltpu.CompilerParams(dimension_semantics=sem, vmem_limit_bytes=vmem_limit_bytes)


def _rms(x, g):
    ms = jnp.mean(x * x, axis=-1, keepdims=True)
    return x * lax.rsqrt(ms + EPS) * g


def _dot(a, b):
    return jnp.dot(a.astype(BF16), b.astype(BF16), preferred_element_type=F32)


def _dot_t(a, bt):
    return lax.dot_general(a.astype(BF16), bt.astype(BF16), (((1,), (1,)), ((), ())), preferred_element_type=F32)


def _with_rider(host_body, rider, n_in, n_out):
    if rider is None:
        return host_body
    r_in, r_out = len(rider['in_specs']), len(rider['out_specs'])

    def body(*refs):
        host_in, refs = refs[:n_in], refs[n_in:]
        rider_in, refs = refs[:r_in], refs[r_in:]
        host_out, refs = refs[:n_out], refs[n_out:]
        rider_out, scratch = refs[:r_out], refs[r_out:]
        host_body(*host_in, *host_out, *scratch, rider=rider['make_units'](*rider_in, *rider_out))
    return body


def _grid_steps(grid):
    n = 1
    for g in grid:
        n *= g

    def step(*idx):
        s = idx[0]
        for g, i in zip(grid[1:], idx[1:]):
            s = s * g + i
        return s
    return n, step


def _per_row_units(row_units, rows):
    return lambda *refs: [u for r in range(rows) for u in row_units(*refs, r)]


def _merge_riders(a, b):
    na_in, na_out = len(a['in_specs']), len(a['out_specs'])
    nb_in = len(b['in_specs'])

    def make_units(*refs):
        a_in, refs = refs[:na_in], refs[na_in:]
        b_in, refs = refs[:nb_in], refs[nb_in:]
        a_out, b_out = refs[:na_out], refs[na_out:]
        return a['make_units'](*a_in, *a_out) + b['make_units'](*b_in, *b_out)

    return dict(make_units=make_units, **{k: a[k] + b[k] for k in ('in_specs', 'out_specs', 'out_shape', 'args')})


def _cast_rider(grid, groups):
    n_steps, step = _grid_steps(grid)
    in_specs, out_specs, out_shape, args, widths = [], [], [], [], []
    for group in groups:
        rps = group[0].shape[0] // n_steps
        for a in group:
            in_specs.append(pl.BlockSpec((rps, a.shape[1]), lambda *g: (step(*g), 0)))
            args.append(a)
        cols = sum(a.shape[1] for a in group)
        out_specs.append(pl.BlockSpec((rps, cols), lambda *g: (step(*g), 0)))
        out_shape.append(jax.ShapeDtypeStruct((group[0].shape[0], cols), BF16))
        widths.append([a.shape[1] for a in group])

    def make_units(*refs):
        in_refs, out_refs = list(refs[:len(args)]), refs[len(args):]
        units = []
        for o_ref, ws in zip(out_refs, widths):
            srcs, in_refs = in_refs[:len(ws)], in_refs[len(ws):]

            def cast(o_ref=o_ref, srcs=srcs, ws=ws):
                c0 = 0
                for s_ref, wd in zip(srcs, ws):
                    o_ref[:, c0:c0 + wd] = s_ref[...].astype(BF16)
                    c0 += wd
            units.append(cast)
        return units

    return dict(make_units=make_units, in_specs=in_specs, out_specs=out_specs, out_shape=out_shape, args=args)


def _run_rider_share(rider, i, n):
    if rider is None:
        return
    for unit in rider[i * len(rider) // n:(i + 1) * len(rider) // n]:
        unit()


def _rider_parts(rider):
    if rider is None:
        return [], [], [], []
    return rider['in_specs'], rider['out_specs'], rider['out_shape'], rider['args']


def _weight_ring(w_ref, wbuf, sem, *, grid, transposed):
    ring = wbuf.shape[0]
    ni, nj = grid
    total = ni * nj
    tn = wbuf.shape[1 if transposed else 2]
    s = pl.program_id(0) * nj + pl.program_id(1)

    def fetch(step):
        col = pl.ds(pl.multiple_of((step % nj) * tn, 128), tn)
        src = w_ref.at[col, :] if transposed else w_ref.at[:, col]
        slot = step % ring
        return pltpu.make_async_copy(src, wbuf.at[slot], sem.at[slot])

    @pl.when(s == 0)
    def _():
        for p in range(min(ring - 1, total)):
            fetch(p).start()

    @pl.when(s + (ring - 1) < total)
    def _():
        fetch(s + (ring - 1)).start()

    def wait_current():
        fetch(s).wait()
        return wbuf.at[s % ring]

    return wait_current


def _norm_matmul_kernel(*refs, has_extra, w_is_transposed, emit_w_bf16, ring=0, grid=None):
    refs = list(refs)
    if ring:
        sem = refs.pop()
        wbuf = refs.pop()
    xn_ref = refs.pop()
    wb_ref = refs.pop() if emit_w_bf16 else None
    if has_extra:
        x_ref, g_ref, w_ref, we_ref, o_ref, oe_ref = refs
    else:
        x_ref, g_ref, w_ref, o_ref = refs
    dot = _dot_t if w_is_transposed else _dot

    if ring:
        wait_w = _weight_ring(w_ref, wbuf, sem, grid=grid, transposed=w_is_transposed)

    @pl.when(pl.program_id(1) == 0)
    def _():
        xn = _rms(x_ref[...], g_ref[...]).astype(BF16)
        xn_ref[...] = xn
        if has_extra:
            oe_ref[...] = dot(xn, we_ref[...])

    if ring:
        wb = wait_w()[...]
    else:
        wb = w_ref[...].astype(BF16)
    if emit_w_bf16:
        wb_ref[...] = wb
    o_ref[...] = dot(xn_ref[...], wb).astype(o_ref.dtype)


def _norm_matmul(x, g, w, *, tm, tn, out_dtype, w_extra=None, n_cols=None, w_is_transposed=False,
                 emit_w_bf16=False, w_buffers=2):
    m, k = x.shape
    n = n_cols or w.shape[0 if w_is_transposed else 1]
    has_extra = w_extra is not None
    w_block, w_map = ((tn, k), lambda i, j: (j, 0)) if w_is_transposed else ((k, tn), lambda i, j: (0, j))
    w_spec = pl.BlockSpec(w_block, w_map)
    ring = w_buffers if w_buffers > 2 else 0
    assert not (ring and (emit_w_bf16 or w.dtype != BF16))
    in_specs = [pl.BlockSpec((tm, k), lambda i, j: (i, 0)),
                pl.BlockSpec((1, k), lambda i, j: (0, 0)),
                pl.BlockSpec(memory_space=pl.ANY) if ring else w_spec]
    scratch = [pltpu.VMEM((tm, k), BF16)]
    if ring:
        scratch += [pltpu.VMEM((ring,) + w_block, BF16), pltpu.SemaphoreType.DMA((ring,))]
    out_specs = [pl.BlockSpec((tm, tn), lambda i, j: (i, j))]
    out_shape = [jax.ShapeDtypeStruct((m, n), out_dtype)]
    args = [x, g.reshape(1, k), w]
    if has_extra:
        ne = w_extra.shape[0 if w_is_transposed else 1]
        in_specs.append(pl.BlockSpec(w_extra.shape, lambda i, j: (0, 0)))
        out_specs.append(pl.BlockSpec((tm, ne), lambda i, j: (i, 0)))
        out_shape.append(jax.ShapeDtypeStruct((m, ne), F32))
        args.append(w_extra)
    if emit_w_bf16:
        assert m == tm
        out_specs.append(w_spec)
        out_shape.append(jax.ShapeDtypeStruct((n, k) if w_is_transposed else (k, n), BF16))
    res = pl.pallas_call(
        functools.partial(_norm_matmul_kernel, has_extra=has_extra, w_is_transposed=w_is_transposed,
                          emit_w_bf16=emit_w_bf16, ring=ring, grid=(m // tm, n // tn)),
        grid=(m // tm, n // tn),
        in_specs=in_specs, out_specs=out_specs, out_shape=out_shape,
        scratch_shapes=scratch,
        compiler_params=_params("arbitrary" if ring else "parallel", "arbitrary"),
        name="norm_matmul",
    )(*args)
    return res if len(res) > 1 else res[0]


def _norm_matmul_pair_kernel(x_ref, g_ref, wa_ref, wb_ref, oa_ref, ob_ref, xn_ref):
    @pl.when(pl.program_id(0) == 0)
    def _():
        xn_ref[...] = _rms(x_ref[...], g_ref[...]).astype(BF16)

    oa_ref[...] = _dot(xn_ref[...], wa_ref[...])
    ob_ref[...] = _dot(xn_ref[...], wb_ref[...])


def _norm_matmul_pair(x, g, wa, wb, *, tn):
    m, k = x.shape
    n = wa.shape[1]
    cols = lambda rows: pl.BlockSpec((rows, tn), lambda j: (0, j))
    return pl.pallas_call(
        _norm_matmul_pair_kernel,
        grid=(n // tn,),
        in_specs=[pl.BlockSpec((m, k), lambda j: (0, 0)), pl.BlockSpec((1, k), lambda j: (0, 0)), cols(k), cols(k)],
        out_specs=[cols(m), cols(m)],
        out_shape=[jax.ShapeDtypeStruct((m, n), F32)] * 2,
        scratch_shapes=[pltpu.VMEM((m, k), BF16)],
        compiler_params=_params("arbitrary"),
        name="norm_matmul_pair",
    )(x, g.reshape(1, k), wa, wb)


def _matmul_res_kernel(*refs, n_parts):
    a_refs, w_refs = refs[:n_parts], refs[n_parts:2 * n_parts]
    r_ref, o_ref = refs[2 * n_parts:]
    acc = r_ref[...]
    for a_ref, w_ref in zip(a_refs, w_refs):
        acc = acc + _dot(a_ref[...], w_ref[...])
    o_ref[...] = acc


def _matmul_res(a_parts, w, res, *, tm, tn):
    n_parts = len(a_parts)
    m, kp = a_parts[0].shape
    n = w.shape[1]
    a_specs = [pl.BlockSpec((tm, kp), lambda i, j: (i, 0)) for _ in a_parts]
    w_specs = [pl.BlockSpec((kp, tn), lambda i, j, p=p: (p, j)) for p in range(n_parts)]
    return pl.pallas_call(
        functools.partial(_matmul_res_kernel, n_parts=n_parts),
        grid=(m // tm, n // tn),
        in_specs=a_specs + w_specs + [pl.BlockSpec((tm, tn), lambda i, j: (i, j))],
        out_specs=pl.BlockSpec((tm, tn), lambda i, j: (i, j)),
        out_shape=jax.ShapeDtypeStruct((m, n), F32),
        compiler_params=_params("parallel", "arbitrary"),
        name="matmul_res",
    )(*a_parts, *([w] * n_parts), res)


def _matmul_res_norm_matmul_kernel(*refs, n_parts, n1, grid=None):
    a_refs = refs[:n_parts]
    if grid is None:
        w_ref, r_ref, g_ref, x1_ref, q_ref, x1_s, xn_s = refs[n_parts:]
    else:
        w_hbm, r_ref, g_ref, x1_ref, q_ref, x1_s, xn_s, wbuf, sem = refs[n_parts:]
        w_ref = _weight_ring(w_hbm, wbuf, sem, grid=grid, transposed=False)()
    j = pl.program_id(1)
    tn = w_ref.shape[1]
    kp = a_refs[0].shape[1]

    @pl.when(j < n1)
    def _():
        x1 = r_ref[...]
        for p, a_ref in enumerate(a_refs):
            x1 = x1 + _dot(a_ref[...], w_ref[p * kp:(p + 1) * kp, :])
        x1_ref[...] = x1
        x1_s[j] = x1

    @pl.when(j == n1)
    def _():
        ssq = sum(jnp.sum(jnp.square(x1_s[c]), axis=-1, keepdims=True) for c in range(n1))
        scale = lax.rsqrt(ssq * (1.0 / (n1 * tn)) + EPS)
        for c in range(n1):
            xn_s[:, c * tn:(c + 1) * tn] = (x1_s[c] * scale * g_ref[:, c * tn:(c + 1) * tn]).astype(xn_s.dtype)

    @pl.when(j >= n1)
    def _():
        q_ref[...] = _dot(xn_s[...], w_ref[...]).astype(q_ref.dtype)


def _matmul_res_norm_matmul(a_parts, w12, res, g, *, tm, tn, out2_dtype, w_buffers=2):
    n_parts = len(a_parts)
    m, kp = a_parts[0].shape
    d = res.shape[1]
    n1 = d // tn
    n2 = (w12.shape[1] - d) // tn
    grid = (m // tm, n1 + n2)
    ring = w_buffers if w_buffers > 2 else 0
    scratch = [pltpu.VMEM((n1, tm, tn), F32), pltpu.VMEM((tm, d), BF16)]
    if ring:
        scratch += [pltpu.VMEM((ring, n_parts * kp, tn), w12.dtype), pltpu.SemaphoreType.DMA((ring,))]
    return pl.pallas_call(
        functools.partial(_matmul_res_norm_matmul_kernel, n_parts=n_parts, n1=n1, grid=grid if ring else None),
        grid=grid,
        in_specs=[pl.BlockSpec((tm, kp), lambda i, j: (i, 0))] * n_parts + [
            pl.BlockSpec(memory_space=pl.ANY) if ring else pl.BlockSpec((n_parts * kp, tn), lambda i, j: (0, j)),
            pl.BlockSpec((tm, tn), lambda i, j: (i, jnp.minimum(j, n1 - 1))),
            pl.BlockSpec((1, d), lambda i, j: (0, 0))],
        out_specs=[pl.BlockSpec((tm, tn), lambda i, j: (i, jnp.minimum(j, n1 - 1))),
                   pl.BlockSpec((tm, tn), lambda i, j: (i, jnp.maximum(j - n1, 0)))],
        out_shape=[jax.ShapeDtypeStruct((m, d), F32), jax.ShapeDtypeStruct((m, n2 * tn), out2_dtype)],
        scratch_shapes=scratch,
        compiler_params=(_params("arbitrary", "arbitrary", vmem_limit_bytes=V7X_VMEM_LIMIT_FFN_BYTES) if ring
                         else _params("parallel", "arbitrary")),
        name="matmul_res_norm_matmul",
    )(*a_parts, w12, res, g.reshape(1, d))


def _ffn_kernel(x_ref, g_ref, wu_ref, wd_ref, gf_ref, o_ref, xn_ref, *, final_norm, rider=None):
    f = pl.program_id(1)

    @pl.when(f == 0)
    def _():
        x = x_ref[...]
        xn_ref[...] = _rms(x, g_ref[...]).astype(BF16)
        o_ref[...] = x

    _run_rider_share(rider, 0, 1)
    h = _dot(xn_ref[...], wu_ref[...])
    h = jnp.square(jnp.maximum(h, 0.0)).astype(BF16)
    o_ref[...] += _dot(h, wd_ref[...])

    if final_norm:
        @pl.when(f == pl.num_programs(1) - 1)
        def _():
            o_ref[...] = _rms(o_ref[...], gf_ref[...])


def _ffn(x, g, w_up, w_down, g_final, *, tm, tf, final_norm, rider=None):
    m, d = x.shape
    dff = w_up.shape[1]
    r_in, r_out, r_shape, r_args = _rider_parts(rider)
    in_specs = [pl.BlockSpec((tm, d), lambda i, f: (i, 0)),
                pl.BlockSpec((1, d), lambda i, f: (0, 0)),
                pl.BlockSpec((d, tf), lambda i, f: (0, f)),
                pl.BlockSpec((tf, d), lambda i, f: (f, 0)),
                pl.BlockSpec((1, d), lambda i, f: (0, 0))]
    out_specs = [pl.BlockSpec((tm, d), lambda i, f: (i, 0))]
    return pl.pallas_call(
        _with_rider(functools.partial(_ffn_kernel, final_norm=final_norm), rider, len(in_specs), len(out_specs)),
        grid=(m // tm, dff // tf),
        in_specs=in_specs + r_in, out_specs=out_specs + r_out,
        out_shape=[jax.ShapeDtypeStruct((m, d), F32)] + r_shape,
        scratch_shapes=[pltpu.VMEM((tm, d), BF16)],
        compiler_params=_params("parallel", "arbitrary", vmem_limit_bytes=V7X_VMEM_LIMIT_FFN_BYTES),
        name="ffn",
    )(x, g.reshape(1, d), w_up, w_down, g_final.reshape(1, d), *r_args)


def _lru_gates(xc, wg, bg, lam):
    g = _dot(xc.astype(BF16), wg) + bg
    r = jax.nn.sigmoid(g[:, :RNN_HD])
    i = jax.nn.sigmoid(g[:, RNN_HD:])
    log_a = -LRU_C * r * jax.nn.softplus(-lam)
    a = jnp.exp(log_a)
    u = jnp.sqrt(-jnp.tanh(log_a) * (a * a + 1.0)) * (i * xc)
    return a, u


def _rglru_kernel(xr_ref, gr_ref, h0_ref, c0_ref, cw_ref, cb_ref, wg_ref, bg_ref, lam_ref, gout_ref,
                  y_ref, hn_ref, cn_ref, xbuf, hc, ybuf, *, tc, rider=None):
    c = pl.program_id(1)

    @pl.when(c == 0)
    def _():
        xbuf[0:8, :] = jnp.zeros((8, D_RNN), F32)
        xbuf[5:8, :] = c0_ref[0]
        hc[...] = h0_ref[0]

    @pl.when(c > 0)
    def _():
        xbuf[0:8, :] = xbuf[tc:tc + 8, :]

    xbuf[8:8 + tc, :] = xr_ref[0]
    ng = tc // SUBLANES
    sub = lax.broadcasted_iota(jnp.int32, (ng, SUBLANES, RNN_HD), 1)
    ssq = jnp.zeros((tc, 1), F32)
    for h in range(RNN_HEADS):
        _run_rider_share(rider, h, RNN_HEADS)
        sl = slice(h * RNN_HD, (h + 1) * RNN_HD)
        xc = cb_ref[:, sl]
        for j in range(CONV_W):
            xc = xc + cw_ref[j:j + 1, sl] * xbuf[5 + j:5 + j + tc, sl]
        a, u = _lru_gates(xc, wg_ref[h], bg_ref[h], lam_ref[:, sl])
        a = a.reshape(ng, SUBLANES, RNN_HD)
        u = u.reshape(ng, SUBLANES, RNN_HD)
        d = 1
        while d < SUBLANES:
            keep = sub >= d
            a_prev = jnp.where(keep, pltpu.roll(a, d, 1), 1.0)
            u_prev = jnp.where(keep, pltpu.roll(u, d, 1), 0.0)
            u = u + a * u_prev
            a = a * a_prev
            d *= 2
        carry = hc[:, sl]
        for g in range(ng):
            hg = u[g] + a[g] * carry
            carry = hg[SUBLANES - 1:SUBLANES, :]
            ybuf[g * SUBLANES:(g + 1) * SUBLANES, sl] = hg
        hc[:, sl] = carry
        yv = ybuf[:, sl] * jax.nn.gelu(gr_ref[0, :, sl])
        ssq = ssq + jnp.sum(yv * yv, axis=-1, keepdims=True)
        ybuf[:, sl] = yv
    y = ybuf[...] * lax.rsqrt(ssq * (1.0 / D_RNN) + EPS) * gout_ref[...]
    y_ref[0] = y.astype(y_ref.dtype)

    @pl.when(c == pl.num_programs(1) - 1)
    def _():
        hn_ref[0] = hc[...]
        cn_ref[0] = xbuf[tc + 5:tc + 8, :]


def _rglru_prompt(z3, h0, conv0, w, *, tc, rider=None):
    b, t, _ = z3.shape
    full2 = lambda bi, ci: (0, 0)
    r_in, r_out, r_shape, r_args = _rider_parts(rider)
    in_specs = [pl.BlockSpec((1, tc, D_RNN), lambda bi, ci: (bi, ci, 0)),
                pl.BlockSpec((1, tc, D_RNN), lambda bi, ci: (bi, ci, 1)),
                pl.BlockSpec((1, 1, D_RNN), lambda bi, ci: (bi, 0, 0)),
                pl.BlockSpec((1, CONV_W - 1, D_RNN), lambda bi, ci: (bi, 0, 0)),
                pl.BlockSpec((CONV_W, D_RNN), full2),
                pl.BlockSpec((1, D_RNN), full2),
                pl.BlockSpec((RNN_HEADS, RNN_HD, 2 * RNN_HD), lambda bi, ci: (0, 0, 0)),
                pl.BlockSpec((RNN_HEADS, 1, 2 * RNN_HD), lambda bi, ci: (0, 0, 0)),
                pl.BlockSpec((1, D_RNN), full2),
                pl.BlockSpec((1, D_RNN), full2)]
    out_specs = [pl.BlockSpec((1, tc, D_RNN), lambda bi, ci: (bi, ci, 0)),
                 pl.BlockSpec((1, 1, D_RNN), lambda bi, ci: (bi, 0, 0)),
                 pl.BlockSpec((1, CONV_W - 1, D_RNN), lambda bi, ci: (bi, 0, 0))]
    out_shape = [jax.ShapeDtypeStruct((b, t, D_RNN), BF16),
                 jax.ShapeDtypeStruct((b, 1, D_RNN), F32),
                 jax.ShapeDtypeStruct((b, CONV_W - 1, D_RNN), F32)]
    return pl.pallas_call(
        _with_rider(functools.partial(_rglru_kernel, tc=tc), rider, len(in_specs), len(out_specs)),
        grid=(b, t // tc),
        in_specs=in_specs + r_in, out_specs=out_specs + r_out, out_shape=out_shape + r_shape,
        scratch_shapes=[pltpu.VMEM((tc + 8, D_RNN), F32),
                        pltpu.VMEM((1, D_RNN), F32),
                        pltpu.VMEM((tc, D_RNN), F32)],
        compiler_params=_params("parallel", "arbitrary"),
        name="rglru_prompt",
    )(z3, z3, h0, conv0, w['conv_rnn_w'], w['conv_rnn_b'], w['lru_wg'], w['lru_bg'],
      w['lru_lambda'], w['g_rnn_out'], *r_args)


def _mlstm_kernel(u_ref, v_ref, og_ref, zg_ref, gb_ref, c0_ref, n0_ref, m0_ref, cv0_ref,
                  cw_ref, cb_ref, wqk_ref, gout_ref,
                  y_ref, cn_ref, nn_ref, mn_ref, cvn_ref, ubuf, cst, nst, mst, rider=None):
    c = pl.program_id(1)
    L = CHUNK
    n_seq = u_ref.shape[0]

    @pl.when(c == 0)
    def _():
        ubuf[:, 0:8, :] = jnp.zeros((n_seq, 8, D_MLSTM), F32)
        ubuf[:, 5:8, :] = cv0_ref[...]
        cst[...] = c0_ref[...]
        nst[...] = n0_ref[...]
        mst[...] = m0_ref[...]

    @pl.when(c > 0)
    def _():
        ubuf[:, 0:8, :] = ubuf[:, L:L + 8, :]

    ubuf[:, 8:8 + L, :] = u_ref[...]

    ti = lax.broadcasted_iota(jnp.int32, (L, L), 0)
    si = lax.broadcasted_iota(jnp.int32, (L, L), 1)
    causal = si <= ti
    anti = ti <= si
    zgs = [zg_ref[s] + gb_ref[...] for s in range(n_seq)]
    zgts = [zg.T for zg in zgs]

    for h in range(MLSTM_HEADS):
        _run_rider_share(rider, h, MLSTM_HEADS)
        sl = slice(h * MLSTM_HD, (h + 1) * MLSTM_HD)
        for s in range(n_seq):
            zg, zgt = zgs[s], zgts[s]
            uc = cb_ref[:, sl]
            for j in range(CONV_W):
                uc = uc + cw_ref[j:j + 1, sl] * ubuf[s, 5 + j:5 + j + L, sl]
            uc = jax.nn.silu(uc)
            qk = _dot(uc.astype(BF16), wqk_ref[h])
            q = qk[:, :MLSTM_HD]
            k = qk[:, MLSTM_HD:] * (MLSTM_HD ** -0.5)
            v = v_ref[s, :, sl]
            qb, kb = q.astype(BF16), k.astype(BF16)

            icol = zg[:, h:h + 1]
            fcol = jax.nn.log_sigmoid(zg[:, MLSTM_HEADS + h:MLSTM_HEADS + h + 1])
            irow = zgt[h:h + 1, :]
            frow = jax.nn.log_sigmoid(zgt[MLSTM_HEADS + h:MLSTM_HEADS + h + 1, :])
            bcol = jnp.sum(jnp.where(causal, frow, 0.0), axis=1, keepdims=True)
            brow = jnp.sum(jnp.where(anti, fcol, 0.0), axis=0, keepdims=True)
            dmat = jnp.where(causal, irow + bcol - brow, -jnp.inf)
            m_prev = mst[s, h, :, 0:1]
            inter = bcol + m_prev
            m_t = jnp.maximum(inter, jnp.max(dmat, axis=1, keepdims=True))
            w_inter = jnp.exp(inter - m_t)
            sc = lax.dot_general(qb, kb, (((1,), (1,)), ((), ())), preferred_element_type=F32) * jnp.exp(dmat - m_t)
            cmat = cst[s, h]
            nrow = nst[s, h]
            cq = lax.dot_general(qb, cmat.astype(BF16), (((1,), (1,)), ((), ())), preferred_element_type=F32)
            num = w_inter * cq + _dot(sc.astype(BF16), v.astype(BF16))
            den = w_inter * jnp.sum(q * nrow, axis=1, keepdims=True) + jnp.sum(sc, axis=1, keepdims=True)
            hm = num / jnp.maximum(jnp.abs(den), jnp.exp(-m_t))

            m_new = m_t[L - 1:L, :]
            b_last = bcol[L - 1:L, :]
            g_state = jnp.exp(b_last + m_prev - m_new)
            g_in = jnp.exp(icol + b_last - bcol - m_new)
            cst[s, h] = g_state * cmat + lax.dot_general((g_in * v).astype(BF16), kb, (((0,), (0,)), ((), ())),
                                                         preferred_element_type=F32)
            nst[s, h] = g_state * nrow + jnp.sum(g_in * k, axis=0, keepdims=True)
            mst[s, h] = jnp.broadcast_to(m_new, (1, N_GATE_PAD))

            y = _rms(hm, gout_ref[...]) * jax.nn.sigmoid(og_ref[s, :, sl])
            y_ref[s, :, sl] = y.astype(y_ref.dtype)

    @pl.when(c == pl.num_programs(1) - 1)
    def _():
        cn_ref[...] = cst[...]
        nn_ref[...] = nst[...]
        mn_ref[...] = mst[...]
        cvn_ref[...] = ubuf[:, L + 5:L + 8, :]


MLSTM_SEQS = 1


def _mlstm_prompt(z3, zg3, c0, n0, m0, conv0, w, *, rider=None):
    b, t, _ = z3.shape
    nc = t // CHUNK
    ns = MLSTM_SEQS
    u_blk, v_blk, o_blk = 2 * D_RNN // D_MLSTM, 3 * D_RNN // D_MLSTM, 4 * D_RNN // D_MLSTM
    zcol = lambda blk: pl.BlockSpec((ns, CHUNK, D_MLSTM), lambda bi, ci: (bi, ci, blk))
    state4 = lambda *minor: pl.BlockSpec((ns, MLSTM_HEADS) + minor, lambda bi, ci: (bi, 0, 0, 0))
    conv_spec = pl.BlockSpec((ns, CONV_W - 1, D_MLSTM), lambda bi, ci: (bi, 0, 0))
    const2 = lambda shape: pl.BlockSpec(shape, lambda bi, ci: (0, 0))
    r_in, r_out, r_shape, r_args = _rider_parts(rider)
    in_specs = [zcol(u_blk), zcol(v_blk), zcol(o_blk),
                pl.BlockSpec((ns, CHUNK, N_GATE_PAD), lambda bi, ci: (bi, ci, 0)),
                const2((1, N_GATE_PAD)),
                state4(MLSTM_HD, MLSTM_HD), state4(1, MLSTM_HD), state4(1, N_GATE_PAD), conv_spec,
                const2((CONV_W, D_MLSTM)), const2((1, D_MLSTM)),
                pl.BlockSpec((MLSTM_HEADS, MLSTM_HD, 2 * MLSTM_HD), lambda bi, ci: (0, 0, 0)),
                const2((1, MLSTM_HD))]
    out_specs = [pl.BlockSpec((ns, CHUNK, D_MLSTM), lambda bi, ci: (bi, ci, 0)),
                 state4(MLSTM_HD, MLSTM_HD), state4(1, MLSTM_HD), state4(1, N_GATE_PAD), conv_spec]
    out_shape = [jax.ShapeDtypeStruct((b, t, D_MLSTM), BF16),
                 jax.ShapeDtypeStruct((b, MLSTM_HEADS, MLSTM_HD, MLSTM_HD), F32),
                 jax.ShapeDtypeStruct((b, MLSTM_HEADS, 1, MLSTM_HD), F32),
                 jax.ShapeDtypeStruct((b, MLSTM_HEADS, 1, N_GATE_PAD), F32),
                 jax.ShapeDtypeStruct((b, CONV_W - 1, D_MLSTM), F32)]
    return pl.pallas_call(
        _with_rider(_mlstm_kernel, rider, len(in_specs), len(out_specs)),
        grid=(b // ns, nc),
        in_specs=in_specs + r_in, out_specs=out_specs + r_out, out_shape=out_shape + r_shape,
        scratch_shapes=[pltpu.VMEM((ns, CHUNK + 8, D_MLSTM), F32),
                        pltpu.VMEM((ns, MLSTM_HEADS, MLSTM_HD, MLSTM_HD), F32),
                        pltpu.VMEM((ns, MLSTM_HEADS, 1, MLSTM_HD), F32),
                        pltpu.VMEM((ns, MLSTM_HEADS, 1, N_GATE_PAD), F32)],
        compiler_params=_params("parallel", "arbitrary"),
        name="mlstm_prompt",
    )(z3, z3, z3, zg3, w['gate_bias'], c0, n0, m0, conv0, w['conv_ml_w'], w['conv_ml_b'],
      w['ml_wqk'], w['g_ml_out'], *r_args)


def _xattn_kernel(q_ref, k_ref, v_ref, x_ref, wo_ref, o_ref, att, rider=None):
    @pl.when(pl.program_id(2) == 0)
    def _():
        for h in range(X_HEADS):
            sl = slice(h * X_HD, (h + 1) * X_HD)
            s = _dot_t(q_ref[0, :, sl], k_ref[0, :, sl]) * (X_HD ** -0.5)
            e = jnp.exp(s - jnp.max(s, axis=-1, keepdims=True))
            p = e / jnp.sum(e, axis=-1, keepdims=True)
            att[:, sl] = _dot(p, v_ref[0, :, sl]).astype(att.dtype)

    _run_rider_share(rider, 0, 1)
    o_ref[0] = x_ref[0] + _dot(att[...], wo_ref[...])


def _xattn_prompt(q3, mk3, mv3, x3, w_co, *, tq, tn, rider=None):
    b, t, d = q3.shape
    mem = pl.BlockSpec((1, N_MEM, d), lambda bi, ti, j: (bi, 0, 0))
    cols = pl.BlockSpec((1, tq, tn), lambda bi, ti, j: (bi, ti, j))
    r_in, r_out, r_shape, r_args = _rider_parts(rider)
    in_specs = [pl.BlockSpec((1, tq, d), lambda bi, ti, j: (bi, ti, 0)), mem, mem, cols,
                pl.BlockSpec((d, tn), lambda bi, ti, j: (0, j))]
    return pl.pallas_call(
        _with_rider(_xattn_kernel, rider, len(in_specs), 1),
        grid=(b, t // tq, d // tn),
        in_specs=in_specs + r_in, out_specs=[cols] + r_out,
        out_shape=[jax.ShapeDtypeStruct((b, t, d), F32)] + r_shape,
        scratch_shapes=[pltpu.VMEM((tq, d), BF16)],
        compiler_params=_params("parallel", "parallel", "arbitrary"),
        name="xattn_prompt",
    )(q3, mk3, mv3, x3, w_co, *r_args)


XS_SUB = 2 * X_HEADS
XS_PAIRS = X_HD // (2 * 128)


def _pack_heads(x):
    lead = x.shape[:-1]
    x = x.reshape(*lead, X_HEADS, 2 * XS_PAIRS, 128)
    x = jnp.swapaxes(x, -3, -2)
    return x.reshape(*lead, XS_PAIRS, XS_SUB, 128)


def _unpack_heads(x):
    lead = x.shape[:-3]
    x = x.reshape(*lead, 2 * XS_PAIRS, X_HEADS, 128)
    x = jnp.swapaxes(x, -3, -2)
    return x.reshape(*lead, X_HEADS * X_HD)


def _xattn_sample_units(q_ref, k_ref, v_ref, o_ref, r):
    def row():
        t = jnp.sum(k_ref[r] * q_ref[r], axis=1)
        t = t + pltpu.roll(t, X_HEADS, 1)
        s = jnp.sum(t, axis=-1, keepdims=True) * (X_HD ** -0.5)
        e = jnp.exp(s - jnp.max(s, axis=0, keepdims=True))
        p = e / jnp.sum(e, axis=0, keepdims=True)
        o_ref[r] = jnp.sum(p[:, None] * v_ref[r], axis=0)

    return [row]


def _xattn_sample_rider(grid, q, ck, cv):
    b = q.shape[0]
    n_steps, step = _grid_steps(grid)
    rows = b // n_steps
    kv_spec = pl.BlockSpec((rows, N_MEM, XS_PAIRS, XS_SUB, 128), lambda *g: (step(*g), 0, 0, 0, 0))
    q_spec = pl.BlockSpec((rows, XS_PAIRS, XS_SUB, 128), lambda *g: (step(*g), 0, 0, 0))
    return dict(make_units=_per_row_units(_xattn_sample_units, rows),
                in_specs=[q_spec, kv_spec, kv_spec], out_specs=[q_spec],
                out_shape=[jax.ShapeDtypeStruct((b, XS_PAIRS, XS_SUB, 128), F32)],
                args=[_pack_heads(q), _pack_heads(ck.reshape(b, N_MEM, X_HEADS * X_HD)),
                      _pack_heads(cv.reshape(b, N_MEM, X_HEADS * X_HD))])


def _smix_rows_kernel(z_ref, h0_ref, rc_ref, mc_ref, cwr_ref, cbr_ref, wg_ref, bg_ref, lam_ref, gout_ref,
                      cwm_ref, cbm_ref, wqk_ref,
                      yr_ref, hn_ref, rcn_ref, mcn_ref, q_ref, k_ref, ybuf):
    nb = z_ref.shape[0]
    ssq = jnp.zeros((nb, 1), F32)
    for h in range(RNN_HEADS):
        sl = slice(h * RNN_HD, (h + 1) * RNN_HD)
        xc = cbr_ref[:, sl] + cwr_ref[CONV_W - 1:CONV_W, sl] * z_ref[:, sl]
        for j in range(CONV_W - 1):
            xc = xc + cwr_ref[j:j + 1, sl] * rc_ref[:, j * D_RNN + h * RNN_HD:j * D_RNN + (h + 1) * RNN_HD]
        a, u = _lru_gates(xc, wg_ref[h], bg_ref[h], lam_ref[:, sl])
        hs = a * h0_ref[:, sl] + u
        hn_ref[:, sl] = hs
        yv = hs * jax.nn.gelu(z_ref[:, D_RNN + h * RNN_HD:D_RNN + (h + 1) * RNN_HD])
        ssq = ssq + jnp.sum(yv * yv, axis=-1, keepdims=True)
        ybuf[:, sl] = yv
    yr_ref[...] = (ybuf[...] * lax.rsqrt(ssq * (1.0 / D_RNN) + EPS) * gout_ref[...]).astype(yr_ref.dtype)
    rcn_ref[:, 0:2 * D_RNN] = rc_ref[:, D_RNN:3 * D_RNN]
    rcn_ref[:, 2 * D_RNN:3 * D_RNN] = z_ref[:, 0:D_RNN]

    for h in range(MLSTM_HEADS):
        sl = slice(h * MLSTM_HD, (h + 1) * MLSTM_HD)
        uc = cbm_ref[:, sl] + cwm_ref[CONV_W - 1:CONV_W, sl] * z_ref[:, 2 * D_RNN + h * MLSTM_HD:2 * D_RNN + (h + 1) * MLSTM_HD]
        for j in range(CONV_W - 1):
            uc = uc + cwm_ref[j:j + 1, sl] * mc_ref[:, j * D_MLSTM + h * MLSTM_HD:j * D_MLSTM + (h + 1) * MLSTM_HD]
        uc = jax.nn.silu(uc)
        qk = _dot(uc.astype(BF16), wqk_ref[h])
        q_ref[:, sl] = qk[:, :MLSTM_HD]
        k_ref[:, sl] = qk[:, MLSTM_HD:] * (MLSTM_HD ** -0.5)
    mcn_ref[:, 0:2 * D_MLSTM] = mc_ref[:, D_MLSTM:3 * D_MLSTM]
    mcn_ref[:, 2 * D_MLSTM:3 * D_MLSTM] = z_ref[:, 2 * D_RNN:2 * D_RNN + D_MLSTM]


def _smix_rows(z, h0, rconv, mconv, w):
    nb = z.shape[0]
    outs = [jax.ShapeDtypeStruct((nb, D_RNN), BF16),
            jax.ShapeDtypeStruct((nb, D_RNN), F32),
            jax.ShapeDtypeStruct((nb, 3 * D_RNN), F32),
            jax.ShapeDtypeStruct((nb, 3 * D_MLSTM), F32),
            jax.ShapeDtypeStruct((nb, D_MLSTM), F32),
            jax.ShapeDtypeStruct((nb, D_MLSTM), F32)]
    return pl.pallas_call(
        _smix_rows_kernel,
        out_shape=outs,
        scratch_shapes=[pltpu.VMEM((nb, D_RNN), F32)],
        compiler_params=pltpu.CompilerParams(vmem_limit_bytes=V7X_VMEM_LIMIT_BYTES),
        name="smix_rows",
    )(z, h0, rconv, mconv, w['conv_rnn_w'], w['conv_rnn_b'], w['lru_wg'], w['lru_bg'], w['lru_lambda'],
      w['g_rnn_out'], w['conv_ml_w'], w['conv_ml_b'], w['ml_wqk'])


MXU_MIN_ROWS = 8


def _smix_state_units(c_ref, q_ref, k_ref, n_ref, v_ref, og_ref, zg_ref, gb_ref, m_ref, gout_ref,
                      cn_ref, nn_ref, mn_ref, y_ref, r):
    lane_g = lax.broadcasted_iota(jnp.int32, (1, N_GATE_PAD), 1)
    first_row = lax.broadcasted_iota(jnp.int32, (MXU_MIN_ROWS, MLSTM_HD), 0) == 0

    def head(h, zg, m_out):
        sl = slice(h * MLSTM_HD, (h + 1) * MLSTM_HD)
        q = q_ref[r, :, sl]
        k = k_ref[r, :, sl]
        n = n_ref[r, :, sl]
        v = v_ref[r, :, sl]
        ig = zg[:, h:h + 1]
        lf = jax.nn.log_sigmoid(zg[:, MLSTM_HEADS + h:MLSTM_HEADS + h + 1])
        m_prev = m_ref[r, :, h:h + 1]
        m_t = jnp.maximum(lf + m_prev, ig)
        w_inter = jnp.exp(lf + m_prev - m_t)
        g_in = jnp.exp(ig - m_t)
        s = jnp.sum(q * k, axis=1, keepdims=True) * g_in
        den = w_inter * jnp.sum(n * q, axis=1, keepdims=True) + s
        denom = jnp.maximum(jnp.abs(den), jnp.exp(-m_t))
        cmat = c_ref[r, h]
        q8 = jnp.broadcast_to(q, (MXU_MIN_ROWS, MLSTM_HD)).astype(BF16)
        cq = lax.dot_general(q8, cmat.astype(BF16), (((1,), (1,)), ((), ())),
                             preferred_element_type=F32)[0:1, :]
        hm = (w_inter * cq + s * v) / denom
        gv8 = jnp.where(first_row, jnp.broadcast_to(g_in * v, (MXU_MIN_ROWS, MLSTM_HD)), 0.0).astype(BF16)
        k8 = jnp.broadcast_to(k, (MXU_MIN_ROWS, MLSTM_HD)).astype(BF16)
        outer = lax.dot_general(gv8, k8, (((0,), (0,)), ((), ())), preferred_element_type=F32)
        cn_ref[r, h] = w_inter * cmat + outer
        nn_ref[r, :, sl] = w_inter * n + g_in * k
        y = _rms(hm, gout_ref[...]) * jax.nn.sigmoid(og_ref[r, :, sl])
        y_ref[r, :, sl] = y.astype(y_ref.dtype)
        return jnp.where(lane_g == h, m_t, m_out)

    def row():
        zg = zg_ref[r] + gb_ref[...]
        m_out = jnp.zeros((1, N_GATE_PAD), F32)
        for h in range(MLSTM_HEADS):
            m_out = head(h, zg, m_out)
        mn_ref[r] = m_out

    return [row]


def _smix_state_rider(grid, c, q3, k3, n3, z3, zg3, gate_bias, m3, gout):
    nb = c.shape[0]
    n_steps, step = _grid_steps(grid)
    rows = nb // n_steps
    row = lambda width, col=0: pl.BlockSpec((rows, 1, width), lambda *g: (step(*g), 0, col))
    cblk = pl.BlockSpec((rows, MLSTM_HEADS, MLSTM_HD, MLSTM_HD), lambda *g: (step(*g), 0, 0, 0))
    return dict(
        make_units=_per_row_units(_smix_state_units, rows),
        in_specs=[cblk, row(D_MLSTM), row(D_MLSTM), row(D_MLSTM),
                  row(D_MLSTM, 3 * D_RNN // D_MLSTM), row(D_MLSTM, 4 * D_RNN // D_MLSTM),
                  row(N_GATE_PAD),
                  pl.BlockSpec((1, N_GATE_PAD), lambda *g: (0, 0)),
                  row(MLSTM_HEADS),
                  pl.BlockSpec((1, MLSTM_HD), lambda *g: (0, 0))],
        out_specs=[cblk, row(D_MLSTM), row(N_GATE_PAD), row(D_MLSTM)],
        out_shape=[jax.ShapeDtypeStruct(c.shape, F32),
                   jax.ShapeDtypeStruct((nb, 1, D_MLSTM), F32),
                   jax.ShapeDtypeStruct((nb, 1, N_GATE_PAD), F32),
                   jax.ShapeDtypeStruct((nb, 1, D_MLSTM), F32)],
        args=[c, q3, k3, n3, z3, z3, zg3, gate_bias, m3, gout])


RGLRU_CHUNK = 256
PROMPT_TM = 1024
PROMPT_W_IN_TN = 1280
SAMPLE_FFN_TF = 2048


def _layer(xp, xs, mem, rg_h, rg_conv, c, n, mst, ml_conv, ck, cv, w, g_final, final_norm):
    b, t, d = xp.shape
    nb = xs.shape[0]
    m = b * t
    tm = PROMPT_TM
    xp2 = xp.reshape(m, d)
    xs2 = xs.reshape(nb, d)
    w_in_args = dict(tn=1024, out_dtype=F32, w_extra=w['w_in_gate'], n_cols=D_MAIN, w_is_transposed=True)

    zs, zgs, w_in_bf = _norm_matmul(xs2, w['g_mix'], w['w_in_main'], tm=nb, emit_w_bf16=True, **w_in_args)
    ys_rnn, s_h, s_rconv, s_mconv, qs, ks = _smix_rows(
        zs, rg_h, rg_conv.reshape(nb, 3 * D_RNN), ml_conv.reshape(nb, 3 * D_MLSTM), w)

    z, zg = _norm_matmul(xp2, w['g_mix'], w_in_bf, tm=tm, w_buffers=3, **dict(w_in_args, tn=PROMPT_W_IN_TN))
    z3 = z.reshape(b, t, D_MAIN)
    zg3 = zg.reshape(b, t, N_GATE_PAD)
    h0 = jnp.zeros((b, 1, D_RNN), F32)
    conv0 = jnp.zeros((b, CONV_W - 1, D_RNN), F32)
    rg_grid = (b, t // RGLRU_CHUNK)
    state_rider = _smix_state_rider(
        rg_grid, c, qs.reshape(nb, 1, D_MLSTM), ks.reshape(nb, 1, D_MLSTM), n.reshape(nb, 1, D_MLSTM),
        zs.reshape(nb, 1, D_MAIN), zgs.reshape(nb, 1, N_GATE_PAD), w['gate_bias'],
        mst.reshape(nb, 1, MLSTM_HEADS), w['g_ml_out'])
    rg_rider = _merge_riders(state_rider, _cast_rider(rg_grid, [[w['w_out'], w['w_cq']], [w['w_co']]]))
    y_rnn, p_h, p_rconv, s_c, s_n, s_m, ys_ml, w_out_cq, w_co = _rglru_prompt(
        z3, h0, conv0, w, tc=RGLRU_CHUNK, rider=rg_rider)

    xs1, qx = _matmul_res_norm_matmul([ys_rnn, ys_ml.reshape(nb, D_MLSTM).astype(BF16)], w_out_cq, xs2,
                                      w['g_xattn'], tm=nb, tn=1024, out2_dtype=F32)

    c0 = jnp.zeros((b, MLSTM_HEADS, MLSTM_HD, MLSTM_HD), F32)
    n0 = jnp.zeros((b, MLSTM_HEADS, 1, MLSTM_HD), F32)
    m0 = jnp.zeros((b, MLSTM_HEADS, 1, N_GATE_PAD), F32)
    mconv0 = jnp.zeros((b, CONV_W - 1, D_MLSTM), F32)
    y_ml, p_c, p_n, p_m, p_mconv, w_up, w_down = _mlstm_prompt(
        z3, zg3, c0, n0, m0, mconv0, w,
        rider=_cast_rider((b // MLSTM_SEQS, t // CHUNK), [[w['w_up']], [w['w_down']]]))

    x1, q = _matmul_res_norm_matmul([y_rnn.reshape(m, D_RNN), y_ml.reshape(m, D_MLSTM)], w_out_cq, xp2,
                                    w['g_xattn'], tm=tm, tn=1024, out2_dtype=BF16, w_buffers=3)
    mem2 = mem.reshape(b * N_MEM, d)
    mk, mv = _norm_matmul_pair(mem2, w['g_mem'], w['w_mk'], w['w_mv'], tn=512)
    x2, = _xattn_prompt(q.reshape(b, t, d), mk.reshape(b, N_MEM, d), mv.reshape(b, N_MEM, d), x1.reshape(b, t, d),
                        w_co, tq=tm, tn=1024)
    ffn_tf = 512
    xp_out, os_packed = _ffn(x2.reshape(m, d), w['g_ffn'], w_up, w_down, g_final, tm=tm, tf=ffn_tf,
                             final_norm=final_norm,
                             rider=_xattn_sample_rider((m // tm, D_FF // ffn_tf), qx, ck, cv))

    os_ = _unpack_heads(os_packed).astype(BF16)
    xs2_ = _matmul_res([os_], w_co, xs1, tm=nb, tn=1024)
    xs_out, = _ffn(xs2_, w['g_ffn'], w_up, w_down, g_final, tm=nb, tf=SAMPLE_FFN_TF, final_norm=final_norm)

    new_p = (p_h.reshape(b, D_RNN), p_rconv, p_c, p_n.reshape(b, MLSTM_HEADS, MLSTM_HD), p_m[:, :, 0, 0], p_mconv,
             mk.reshape(b, N_MEM, X_HEADS, X_HD), mv.reshape(b, N_MEM, X_HEADS, X_HD))
    new_s = (s_h, s_rconv.reshape(nb, CONV_W - 1, D_RNN), s_c, s_n.reshape(nb, MLSTM_HEADS, MLSTM_HD),
             s_m[:, 0, :MLSTM_HEADS], s_mconv.reshape(nb, CONV_W - 1, D_MLSTM))
    return xp_out.reshape(b, t, d), xs_out.reshape(nb, 1, d), new_p, new_s


def _layer_weights(l, g_mix, w_in, conv_rnn_w, conv_rnn_b, lru_wa, lru_ba, lru_wx, lru_bx, lru_lambda,
                   g_rnn_out, conv_ml_w, conv_ml_b, ml_wq, ml_wk, ml_bi, ml_bf, g_ml_out, w_out,
                   g_xattn, g_mem, w_cq, w_mk, w_mv, w_co, g_ffn, w_up, w_down):
    n_gate = 2 * MLSTM_HEADS
    w_in_t = jnp.swapaxes(w_in[l], 0, 1)
    w_gate_t = jnp.pad(w_in_t[D_MAIN:], ((0, N_GATE_PAD - n_gate), (0, 0)))
    gate_bias = jnp.pad(jnp.concatenate([ml_bi[l], ml_bf[l]]), (0, N_GATE_PAD - n_gate))
    return dict(
        g_mix=g_mix[l], w_in_main=w_in_t, w_in_gate=w_gate_t.astype(BF16),
        conv_rnn_w=conv_rnn_w[l], conv_rnn_b=conv_rnn_b[l].reshape(1, D_RNN),
        lru_wg=jnp.concatenate([lru_wa[l], lru_wx[l]], axis=-1).astype(BF16),
        lru_bg=jnp.concatenate([lru_ba[l], lru_bx[l]], axis=-1).reshape(RNN_HEADS, 1, 2 * RNN_HD),
        lru_lambda=lru_lambda[l].reshape(1, D_RNN), g_rnn_out=g_rnn_out[l].reshape(1, D_RNN),
        conv_ml_w=conv_ml_w[l], conv_ml_b=conv_ml_b[l].reshape(1, D_MLSTM),
        ml_wqk=jnp.concatenate([ml_wq[l], ml_wk[l]], axis=-1).astype(BF16),
        gate_bias=gate_bias.reshape(1, N_GATE_PAD), g_ml_out=g_ml_out[l].reshape(1, MLSTM_HD),
        w_out=w_out[l], w_cq=w_cq[l], g_xattn=g_xattn[l], g_mem=g_mem[l], w_mk=w_mk[l], w_mv=w_mv[l],
        w_co=w_co[l], g_ffn=g_ffn[l], w_up=w_up[l], w_down=w_down[l])


def kernel(x_prompt, x_sample, mem_prompt, state_rglru_h, state_rglru_conv, state_mlstm_C, state_mlstm_n, state_mlstm_m, state_mlstm_conv, cache_mem_k, cache_mem_v, g_mix, w_in, conv_rnn_w, conv_rnn_b, lru_wa, lru_ba, lru_wx, lru_bx, lru_lambda, g_rnn_out, conv_ml_w, conv_ml_b, ml_wq, ml_wk, ml_bi, ml_bf, g_ml_out, w_out, g_xattn, g_mem, w_cq, w_mk, w_mv, w_co, g_ffn, w_up, w_down, g_final):
    depth = w_in.shape[0]
    xp, xs = x_prompt, x_sample
    p_out = [[] for _ in range(8)]
    s_out = [[] for _ in range(6)]
    for l in range(depth):
        w = _layer_weights(l, g_mix, w_in, conv_rnn_w, conv_rnn_b, lru_wa, lru_ba, lru_wx, lru_bx, lru_lambda,
                           g_rnn_out, conv_ml_w, conv_ml_b, ml_wq, ml_wk, ml_bi, ml_bf, g_ml_out, w_out,
                           g_xattn, g_mem, w_cq, w_mk, w_mv, w_co, g_ffn, w_up, w_down)
        last = l == depth - 1
        xp, xs, new_p, new_s = _layer(xp, xs, mem_prompt, state_rglru_h[l], state_rglru_conv[l], state_mlstm_C[l],
                                      state_mlstm_n[l], state_mlstm_m[l], state_mlstm_conv[l], cache_mem_k[l],
                                      cache_mem_v[l], w, g_final, last)
        for j, a in enumerate(new_p):
            p_out[j].append(a)
        for j, a in enumerate(new_s):
            s_out[j].append(a)
    P = [jnp.stack(a, axis=0) for a in p_out]
    S = [jnp.stack(a, axis=0) for a in s_out]
    return (xp, xs, P[0], P[1], P[2], P[3], P[4], P[5], P[6], P[7],
            S[0], S[1], S[2], S[3], S[4], S[5])
```
